```python
import math
import jax, jax.numpy as jnp
from jax import lax
import numpy as np

D_MODEL = 2048
BATCH = 1
SEQ = 8192
DEPTH = 2
DEC_BATCH = 128
DEC_SEQ = 1
PAST_LEN = 8192
PAGE_SIZE = 128

N_META = 16
MIX_WIDTH = D_MODEL
HEAD_DIM = 64
A_WIDTH = MIX_WIDTH // 2
A_HEADS = A_WIDTH // HEAD_DIM
A_KV_HEADS = A_HEADS // 4
GQA = A_HEADS // A_KV_HEADS
WINDOW = 128
SWA_BLOCK = 128
ROPE_DIM = HEAD_DIM // 4
ROPE_THETA = 500000.0
S5_WIDTH = MIX_WIDTH // 4
S5_CH = 16
S5_GROUPS = S5_WIDTH // S5_CH
S5_STATE = 64
GLA_WIDTH = MIX_WIDTH - A_WIDTH - S5_WIDTH
GLA_HEADS = 4
GLA_DV = GLA_WIDTH // GLA_HEADS
GLA_DK = GLA_DV // 2
GLA_LOWRANK = 16
GLA_TAU = 16.0
GLA_CHUNK = 64
D_FF = ((8 * D_MODEL // 3 + 255) // 256) * 256
CONV_W = 3
LN_EPS = 1e-5
DEEPNORM_ALPHA = (2 * DEPTH) ** 0.25
DEEPNORM_BETA = (8 * DEPTH) ** -0.25
NEG_INF = -1e30
F32 = jnp.float32
COL_SIZES = (A_WIDTH, A_KV_HEADS * HEAD_DIM, A_KV_HEADS * HEAD_DIM, S5_WIDTH,
             GLA_HEADS * GLA_DK, GLA_HEADS * GLA_DK, GLA_WIDTH, GLA_WIDTH, GLA_LOWRANK)
IN_COLS = sum(COL_SIZES)

kernel_name = 'hymba_swa_s5_gla_convffn_step'


def _layer_norm(x, g, b):
    xf = x.astype(F32)
    mu = jnp.mean(xf, -1, keepdims=True)
    var = jnp.mean(jnp.square(xf - mu), -1, keepdims=True)
    return ((xf - mu) * lax.rsqrt(var + LN_EPS) * g.astype(F32) + b.astype(F32)).astype(x.dtype)


def _split_cols(h):
    idx = np.cumsum(COL_SIZES)[:-1].tolist()
    return jnp.split(h, idx, axis=-1)


def _partial_rope(x, pos):
    half = ROPE_DIM // 2
    inv = jnp.power(ROPE_THETA, -jnp.arange(half, dtype=F32) / half)
    ang = pos.astype(F32)[:, None] * inv[None, :]
    cos, sin = jnp.cos(ang)[:, None, :], jnp.sin(ang)[:, None, :]
    xr = x[..., :ROPE_DIM].astype(F32)
    x1, x2 = xr[..., :half], xr[..., half:]
    rot = jnp.concatenate([x1 * cos - x2 * sin, x1 * sin + x2 * cos], -1).astype(x.dtype)
    return jnp.concatenate([rot, x[..., ROPE_DIM:]], -1)


def _swa_attend(q, k, v, sink, q_pos, k_pos):
    s = jnp.einsum('bnqkgd,bnskd->bnkgqs', q.astype(F32), k.astype(F32)) * HEAD_DIM ** -0.5
    diff = q_pos[:, :, None] - k_pos[:, None, :]
    ok = (diff >= 0) & (diff <= WINDOW) & (k_pos[:, None, :] >= 0)
    s = jnp.where(ok[None, :, None, None], s, NEG_INF)
    sk = sink.astype(F32).reshape(A_KV_HEADS, GQA)[None, None, :, :, None, None]
    m = jnp.maximum(jnp.max(s, -1, keepdims=True), sk)
    p = jnp.exp(s - m)
    den = jnp.sum(p, -1, keepdims=True) + jnp.exp(sk - m)
    return jnp.einsum('bnkgqs,bnskd->bnqkgd', p / den, v.astype(F32))


def _s5_discretise(lam_re, lam_im, log_dt, b_re, b_im):
    dt = jnp.exp(log_dt.astype(F32))[:, None]
    lr, li = lam_re.astype(F32), lam_im.astype(F32)
    mag = jnp.exp(lr * dt)
    ab_re, ab_im = mag * jnp.cos(li * dt), mag * jnp.sin(li * dt)
    den = lr * lr + li * li
    nr, ni = ab_re - 1.0, ab_im
    f_re = (nr * lr + ni * li) / den
    f_im = (ni * lr - nr * li) / den
    bb_re = f_re[..., None] * b_re - f_im[..., None] * b_im
    bb_im = f_re[..., None] * b_im + f_im[..., None] * b_re
    return ab_re, ab_im, bb_re, bb_im


def _ssm_combine(left, right):
    ar1, ai1, br1, bi1 = left
    ar2, ai2, br2, bi2 = right
    return (ar2 * ar1 - ai2 * ai1, ar2 * ai1 + ai2 * ar1,
            ar2 * br1 - ai2 * bi1 + br2, ar2 * bi1 + ai2 * br1 + bi2)


def _s5(u, h0_re, h0_im, lam_re, lam_im, log_dt, b_re, b_im, c_re, c_im, d, w_glu, b_glu):
    Bsz, T, _ = u.shape
    uf = u.astype(F32).reshape(Bsz, T, S5_GROUPS, S5_CH)
    ab_re, ab_im, bb_re, bb_im = _s5_discretise(lam_re, lam_im, log_dt, b_re.astype(F32), b_im.astype(F32))
    bu_re = jnp.einsum('btgh,gph->btgp', uf, bb_re)
    bu_im = jnp.einsum('btgh,gph->btgp', uf, bb_im)
    e_re = jnp.concatenate([h0_re.astype(F32)[:, None], bu_re], 1)
    e_im = jnp.concatenate([h0_im.astype(F32)[:, None], bu_im], 1)
    a_re = jnp.broadcast_to(ab_re, e_re.shape)
    a_im = jnp.broadcast_to(ab_im, e_im.shape)
    _, _, h_re, h_im = lax.associative_scan(_ssm_combine, (a_re, a_im, e_re, e_im), axis=1)
    h_re, h_im = h_re[:, 1:], h_im[:, 1:]
    y = (jnp.einsum('btgp,ghp->btgh', h_re, c_re.astype(F32))
         - jnp.einsum('btgp,ghp->btgh', h_im, c_im.astype(F32))
         + d.astype(F32) * uf)
    z = jax.nn.gelu(y).reshape(Bsz, T, S5_WIDTH)
    out = z * jax.nn.sigmoid(z @ w_glu.astype(F32) + b_glu.astype(F32))
    return out, h_re[:, -1], h_im[:, -1]


def _gla(q, k, v, logg, S0, lead):
    Bsz, T = q.shape[0], q.shape[1]
    tail = (-(lead + T)) % GLA_CHUNK
    nc = (lead + T + tail) // GLA_CHUNK

    def blocks(a):
        a = jnp.pad(a, ((0, 0), (lead, tail), (0, 0), (0, 0)))
        return a.reshape(Bsz, nc, GLA_CHUNK, GLA_HEADS, a.shape[-1]).transpose(1, 0, 3, 2, 4)

    causal = jnp.tril(jnp.ones((GLA_CHUNK, GLA_CHUNK), bool))

    def step(S, inp):
        qc, kc, vc, gc = inp
        b = jnp.cumsum(gc, axis=2)
        q_in = qc * jnp.exp(b)
        att = jnp.where(causal, jnp.einsum('bhtd,bhsd->bhts', q_in, kc * jnp.exp(-b)), 0.0)
        o = jnp.einsum('bhtd,bhde->bhte', q_in, S) + jnp.einsum('bhts,bhse->bhte', att, vc)
        b_end = b[:, :, -1]
        S_new = (jnp.exp(b_end)[..., None] * S
                 + jnp.einsum('bhsd,bhse->bhde', kc * jnp.exp(b_end[:, :, None] - b), vc))
        return S_new, o

    S_fin, o = lax.scan(step, S0, (blocks(q), blocks(k), blocks(v), blocks(logg)))
    o = o.transpose(1, 0, 3, 2, 4).reshape(Bsz, nc * GLA_CHUNK, GLA_HEADS, GLA_DV)[:, lead:lead + T]
    return o, S_fin


def _conv_ffn(x, prev, w_up, conv_w, conv_b, w_down):
    T = x.shape[1]
    u, g = jnp.split(x @ w_up, 2, axis=-1)
    ext = jnp.concatenate([prev.astype(u.dtype), u], 1)
    c = conv_b
    for j in range(CONV_W):
        c = c + conv_w[j] * ext[:, j:j + T]
    out = (jax.nn.gelu(c) * g) @ w_down
    return out, ext[:, -(CONV_W - 1):]


def _layer(x, l, prm, st, prompt):
    Bsz, T, _ = x.shape
    qa, ka, va, us, qc, kc, vc, gate_c, ac = _split_cols(x @ prm['w_in'][l])
    pos = jnp.arange(T, dtype=jnp.int32) + (0 if prompt else PAST_LEN)
    qa = _partial_rope(qa.reshape(Bsz, T, A_HEADS, HEAD_DIM), pos).reshape(Bsz, T, A_KV_HEADS, GQA, HEAD_DIM)
    ka = _partial_rope(ka.reshape(Bsz, T, A_KV_HEADS, HEAD_DIM), pos)
    va = va.reshape(Bsz, T, A_KV_HEADS, HEAD_DIM)
    sink = prm['attn_sink'][l]
    if prompt:
        lead = (-T) % SWA_BLOCK
        n = (T + lead) // SWA_BLOCK
        padt = lambda a: jnp.pad(a, [(0, 0), (lead, 0)] + [(0, 0)] * (a.ndim - 2))
        qb = padt(qa).reshape(Bsz, n, SWA_BLOCK, A_KV_HEADS, GQA, HEAD_DIM)
        kb = padt(ka).reshape(Bsz, n, SWA_BLOCK, A_KV_HEADS, HEAD_DIM)
        vb = padt(va).reshape(Bsz, n, SWA_BLOCK, A_KV_HEADS, HEAD_DIM)
        shift = [(0, 0), (1, 0), (0, 0), (0, 0), (0, 0)]
        kb2 = jnp.concatenate([jnp.pad(kb[:, :-1], shift), kb], axis=2)
        vb2 = jnp.concatenate([jnp.pad(vb[:, :-1], shift), vb], axis=2)
        qpos = (jnp.arange(n * SWA_BLOCK, dtype=jnp.int32) - lead).reshape(n, SWA_BLOCK)
        kpos = jnp.concatenate([qpos - SWA_BLOCK, qpos], axis=1)
        oa = _swa_attend(qb, kb2, vb2, sink, qpos, kpos).reshape(Bsz, n * SWA_BLOCK, A_WIDTH)[:, lead:]
        new_k, new_v = ka[:, -WINDOW:], va[:, -WINDOW:]
    else:
        k_all = jnp.concatenate([st[0].astype(ka.dtype), ka], 1)
        v_all = jnp.concatenate([st[1].astype(va.dtype), va], 1)
        kpos = PAST_LEN - WINDOW + jnp.arange(WINDOW + T, dtype=jnp.int32)
        oa = _swa_attend(qa[:, None], k_all[:, None], v_all[:, None], sink, pos[None], kpos[None]).reshape(Bsz, T, A_WIDTH)
        new_k, new_v = k_all[:, -WINDOW:], v_all[:, -WINDOW:]
    if prompt:
        h0_re = jnp.zeros((Bsz, S5_GROUPS, S5_STATE), F32)
        h0_im = jnp.zeros((Bsz, S5_GROUPS, S5_STATE), F32)
    else:
        h0_re, h0_im = st[2], st[3]
    ob, ssm_re, ssm_im = _s5(us, h0_re, h0_im, prm['s5_lam_re'][l], prm['s5_lam_im'][l], prm['s5_log_dt'][l],
                             prm['s5_b_re'][l], prm['s5_b_im'][l], prm['s5_c_re'][l], prm['s5_c_im'][l],
                             prm['s5_d'][l], prm['s5_w_glu'][l], prm['s5_b_glu'][l])
    qg = qc.astype(F32).reshape(Bsz, T, GLA_HEADS, GLA_DK) * GLA_DK ** -0.5
    kg = kc.astype(F32).reshape(Bsz, T, GLA_HEADS, GLA_DK)
    vg = vc.astype(F32).reshape(Bsz, T, GLA_HEADS, GLA_DV)
    logg = jax.nn.log_sigmoid(ac.astype(F32) @ prm['gla_w_a2'][l].astype(F32) + prm['gla_b_a'][l].astype(F32)) / GLA_TAU
    logg = logg.reshape(Bsz, T, GLA_HEADS, GLA_DK)
    if prompt:
        S0 = jnp.zeros((Bsz, GLA_HEADS, GLA_DK, GLA_DV), F32)
        g_lead = (-T) % GLA_CHUNK
    else:
        S0 = st[4].astype(F32)
        g_lead = 0
    og, S_fin = _gla(qg, kg, vg, logg, S0, g_lead)
    og = og * lax.rsqrt(jnp.mean(og * og, -1, keepdims=True) + LN_EPS)
    og = og.reshape(Bsz, T, GLA_WIDTH) * prm['gla_norm_g'][l].astype(F32) * jax.nn.silu(gate_c.astype(F32))
    mix = jnp.concatenate([oa.astype(x.dtype), ob.astype(x.dtype), og.astype(x.dtype)], -1)
    x = _layer_norm(DEEPNORM_ALPHA * x + mix @ prm['w_out'][l], prm['ln1_g'][l], prm['ln1_b'][l])
    conv_prev = jnp.zeros((Bsz, CONV_W - 1, D_FF), x.dtype) if prompt else st[5]
    f, conv_new = _conv_ffn(x, conv_prev, prm['ffn_w_up'][l], prm['ffn_conv_w'][l], prm['ffn_conv_b'][l], prm['ffn_w_down'][l])
    x = _layer_norm(DEEPNORM_ALPHA * x + f, prm['ln2_g'][l], prm['ln2_b'][l])
    return x, (new_k, new_v, ssm_re, ssm_im, S_fin, conv_new)


def setup_inputs(seed: int = 0) -> dict:
    key = jax.random.key(seed)
    ks = iter(list(jax.random.split(key, 48)))
    nrm = lambda shape, scale: scale * jax.random.normal(next(ks), shape, F32)
    L = DEPTH
    d = {}
    d['x_prompt'] = nrm((BATCH, SEQ, D_MODEL), 1.0)
    d['x_sample'] = nrm((DEC_BATCH, DEC_SEQ, D_MODEL), 1.0)
    d['cache_swa_k'] = nrm((L, DEC_BATCH, WINDOW, A_KV_HEADS, HEAD_DIM), 1.0)
    d['cache_swa_v'] = nrm((L, DEC_BATCH, WINDOW, A_KV_HEADS, HEAD_DIM), 1.0)
    d['state_ssm_re'] = nrm((L, DEC_BATCH, S5_GROUPS, S5_STATE), 1.0)
    d['state_ssm_im'] = nrm((L, DEC_BATCH, S5_GROUPS, S5_STATE), 1.0)
    d['state_gla'] = nrm((L, DEC_BATCH, GLA_HEADS, GLA_DK, GLA_DV), 1.0)
    d['state_conv'] = nrm((L, DEC_BATCH, CONV_W - 1, D_FF), 1.0)
    d['meta_tokens'] = nrm((N_META, D_MODEL), 1.0)
    d['ln_in_g'] = 1.0 + nrm((D_MODEL,), 0.01)
    d['ln_in_b'] = nrm((D_MODEL,), 0.01)
    d['w_in'] = nrm((L, D_MODEL, IN_COLS), D_MODEL ** -0.5)
    d['attn_sink'] = nrm((L, A_HEADS), 0.5)
    d['s5_lam_re'] = -0.5 + nrm((L, S5_GROUPS, S5_STATE), 0.01)
    d['s5_lam_im'] = jnp.broadcast_to(math.pi * jnp.arange(S5_STATE, dtype=F32), (L, S5_GROUPS, S5_STATE)) + nrm((L, S5_GROUPS, S5_STATE), 0.01)
    d['s5_log_dt'] = math.log(1e-3) + jax.random.uniform(next(ks), (L, S5_GROUPS), F32) * (math.log(1e-1) - math.log(1e-3))
    d['s5_b_re'] = nrm((L, S5_GROUPS, S5_STATE, S5_CH), 0.5 * S5_CH ** -0.5)
    d['s5_b_im'] = nrm((L, S5_GROUPS, S5_STATE, S5_CH), 0.5 * S5_CH ** -0.5)
    d['s5_c_re'] = nrm((L, S5_GROUPS, S5_CH, S5_STATE), (2 * S5_STATE) ** -0.5)
    d['s5_c_im'] = nrm((L, S5_GROUPS, S5_CH, S5_STATE), (2 * S5_STATE) ** -0.5)
    d['s5_d'] = nrm((L, S5_GROUPS, S5_CH), 1.0)
    d['s5_w_glu'] = nrm((L, S5_WIDTH, S5_WIDTH), S5_WIDTH ** -0.5)
    d['s5_b_glu'] = nrm((L, S5_WIDTH), 0.01)
    d['gla_w_a2'] = nrm((L, GLA_LOWRANK, GLA_HEADS * GLA_DK), GLA_LOWRANK ** -0.5)
    d['gla_b_a'] = nrm((L, GLA_HEADS * GLA_DK), 0.01)
    d['gla_norm_g'] = 1.0 + nrm((L, GLA_WIDTH), 0.01)
    d['w_out'] = nrm((L, MIX_WIDTH, D_MODEL), DEEPNORM_BETA * MIX_WIDTH ** -0.5)
    d['ln1_g'] = 1.0 + nrm((L, D_MODEL), 0.01)
    d['ln1_b'] = nrm((L, D_MODEL), 0.01)
    d['ffn_w_up'] = nrm((L, D_MODEL, 2 * D_FF), D_MODEL ** -0.5)
    d['ffn_conv_w'] = nrm((L, CONV_W, D_FF), CONV_W ** -0.5)
    d['ffn_conv_b'] = nrm((L, D_FF), 0.01)
    d['ffn_w_down'] = nrm((L, D_FF, D_MODEL), DEEPNORM_BETA * D_FF ** -0.5)
    d['ln2_g'] = 1.0 + nrm((L, D_MODEL), 0.01)
    d['ln2_b'] = nrm((L, D_MODEL), 0.01)
    return d


def reference(x_prompt, x_sample, cache_swa_k, cache_swa_v, state_ssm_re, state_ssm_im, state_gla, state_conv,
              meta_tokens, ln_in_g, ln_in_b, w_in, attn_sink, s5_lam_re, s5_lam_im, s5_log_dt, s5_b_re, s5_b_im,
              s5_c_re, s5_c_im, s5_d, s5_w_glu, s5_b_glu, gla_w_a2, gla_b_a, gla_norm_g, w_out, ln1_g, ln1_b,
              ffn_w_up, ffn_conv_w, ffn_conv_b, ffn_w_down, ln2_g, ln2_b):
    prm = dict(w_in=w_in, attn_sink=attn_sink, s5_lam_re=s5_lam_re, s5_lam_im=s5_lam_im, s5_log_dt=s5_log_dt,
               s5_b_re=s5_b_re, s5_b_im=s5_b_im, s5_c_re=s5_c_re, s5_c_im=s5_c_im, s5_d=s5_d, s5_w_glu=s5_w_glu,
               s5_b_glu=s5_b_glu, gla_w_a2=gla_w_a2, gla_b_a=gla_b_a, gla_norm_g=gla_norm_g, w_out=w_out,
               ln1_g=ln1_g, ln1_b=ln1_b, ffn_w_up=ffn_w_up, ffn_conv_w=ffn_conv_w, ffn_conv_b=ffn_conv_b,
               ffn_w_down=ffn_w_down, ln2_g=ln2_g, ln2_b=ln2_b)
    meta = jnp.broadcast_to(meta_tokens.astype(x_prompt.dtype)[None], (x_prompt.shape[0], N_META, D_MODEL))
    xp = _layer_norm(jnp.concatenate([meta, x_prompt], 1), ln_in_g, ln_in_b)
    xs = _layer_norm(x_sample, ln_in_g, ln_in_b)
    sp, ss = [], []
    for l in range(DEPTH):
        xp, st_p = _layer(xp, l, prm, None, True)
        xs, st_s = _layer(xs, l, prm, (cache_swa_k[l], cache_swa_v[l], state_ssm_re[l], state_ssm_im[l],
                                       state_gla[l], state_conv[l]), False)
        sp.append(st_p)
        ss.append(st_s)
    stk = lambda lst, i: jnp.stack([s[i] for s in lst])
    y_prompt = xp[:, N_META:]
    y_sample = xs
    return (y_prompt, y_sample, stk(sp, 0), stk(sp, 1), stk(sp, 2), stk(sp, 3), stk(sp, 4), stk(sp, 5),
            stk(ss, 0), stk(ss, 1), stk(ss, 2), stk(ss, 3), stk(ss, 4), stk(ss, 5))
```

```python
import functools
import math

import jax
import jax.numpy as jnp
from jax import lax
from jax.experimental import pallas as pl
from jax.experimental.pallas import tpu as pltpu

F32 = jnp.float32
BF16 = jnp.bfloat16

D_MODEL = 2048
SEQ = 8192
DEPTH = 2
N_SAMPLE = 128
N_META = 16
HEAD_DIM = 64
A_WIDTH = 1024
A_HEADS = 16
A_KV_HEADS = 4
GQA = 4
KV_WIDTH = A_KV_HEADS * HEAD_DIM
WINDOW = 128
ROPE_DIM = 16
ROPE_THETA = 500000.0
PAST_LEN = 8192
S5_WIDTH = 512
S5_CH = 16
S5_GROUPS = 32
S5_STATE = 64
S5_LANES = S5_GROUPS * S5_STATE
GLA_WIDTH = 512
GLA_HEADS = 4
GLA_DV = 128
GLA_DK = 64
GLA_KW = GLA_HEADS * GLA_DK
GLA_LOWRANK = 16
GLA_TAU = 16.0
GLA_CHUNK = 64
D_FF = 5632
CONV_W = 3
LN_EPS = 1e-5
ALPHA = (2 * DEPTH) ** 0.25
NEG_INF = -1e30

BLK = 128
T_PROMPT = N_META + SEQ
LEAD = (-T_PROMPT) % BLK
N_PBLK = (LEAD + T_PROMPT) // BLK
ROWS_P = N_PBLK * BLK
ROWS = ROWS_P + N_SAMPLE
SUB = 8

C_Q, C_K, C_V, C_U, C_GQ, C_GK, C_GV, C_GG, C_GA = 0, 1024, 1280, 1536, 2048, 2304, 2560, 3072, 3584
IN_COLS = 3600
IN_TILE = 1280
IN_PAD = 3 * IN_TILE
ROPE_COLS = A_WIDTH + KV_WIDTH

VMEM_LIMIT = 56 * 1024 * 1024


def _cparams(*sem):
    return pltpu.CompilerParams(dimension_semantics=sem, vmem_limit_bytes=VMEM_LIMIT)


def _layer_norm_rows(x, g, b):
    mu = jnp.mean(x, -1, keepdims=True)
    xc = x - mu
    var = jnp.mean(xc * xc, -1, keepdims=True)
    return xc * lax.rsqrt(var + LN_EPS) * g + b


def _zero_pad_rows(y, first_row):
    row = first_row + lax.broadcasted_iota(jnp.int32, y.shape, 0)
    return jnp.where(row >= LEAD, y, 0.0)


def _dot(a, b):
    return jnp.dot(a, b, preferred_element_type=F32)


def _dot_nt(a, b):
    return lax.dot_general(a, b, (((1,), (1,)), ((), ())), preferred_element_type=F32)


def _dot_tn(a, b):
    return lax.dot_general(a, b, (((0,), (0,)), ((), ())), preferred_element_type=F32)


def _log_sigmoid(x):
    return jnp.minimum(x, 0.0) - jnp.log(1.0 + jnp.exp(-jnp.abs(x)))


def _ln_in_kernel(xp_ref, meta_ref, xs_ref, g_ref, b_ref, o_ref):
    i = pl.program_id(0)
    g, b = g_ref[...], b_ref[...]

    @pl.when(i == 0)
    def _():
        o_ref[0:LEAD, :] = jnp.zeros((LEAD, D_MODEL), F32)
        o_ref[LEAD:BLK, :] = _layer_norm_rows(meta_ref[...], g, b)

    @pl.when(jnp.logical_and(i > 0, i < N_PBLK))
    def _():
        o_ref[...] = _layer_norm_rows(xp_ref[...], g, b)

    @pl.when(i == N_PBLK)
    def _():
        o_ref[...] = _layer_norm_rows(xs_ref[...], g, b)


def _ln_in(xp, meta, xs, g, b):
    return pl.pallas_call(
        _ln_in_kernel,
        grid=(N_PBLK + 1,),
        in_specs=[
            pl.BlockSpec((BLK, D_MODEL), lambda i: (jnp.clip(i - 1, 0, SEQ // BLK - 1), 0)),
            pl.BlockSpec((N_META, D_MODEL), lambda i: (0, 0)),
            pl.BlockSpec((N_SAMPLE, D_MODEL), lambda i: (0, 0)),
            pl.BlockSpec((1, D_MODEL), lambda i: (0, 0)),
            pl.BlockSpec((1, D_MODEL), lambda i: (0, 0)),
        ],
        out_specs=pl.BlockSpec((BLK, D_MODEL), lambda i: (i, 0)),
        out_shape=jax.ShapeDtypeStruct((ROWS, D_MODEL), F32),
        compiler_params=_cparams("arbitrary"),
        name="ln_in",
    )(xp, meta, xs, g, b)


IN_TM = 768


def _in_proj_kernel(x_ref, w_ref, cos_ref, sa_ref, sb_ref, o_ref, xb_ref):
    j = pl.program_id(1)

    @pl.when(j == 0)
    def _():
        xb_ref[...] = x_ref[...].astype(BF16)

    o_ref[...] = _dot(xb_ref[...], w_ref[...])

    @pl.when(j == 0)
    def _():
        cos, sa, sb = cos_ref[...], sa_ref[...], sb_ref[...]
        for c in range(ROPE_COLS // 128):
            blk = o_ref[:, c * 128:(c + 1) * 128]
            o_ref[:, c * 128:(c + 1) * 128] = (
                blk * cos + pltpu.roll(blk, 128 - ROPE_DIM // 2, 1) * sa + pltpu.roll(blk, ROPE_DIM // 2, 1) * sb)


def _in_proj(x, w, cos_t, sa_t, sb_t):
    tab = pl.BlockSpec((IN_TM, 128), lambda i, j: (i, 0))
    return pl.pallas_call(
        _in_proj_kernel,
        grid=(ROWS // IN_TM, IN_PAD // IN_TILE),
        in_specs=[
            pl.BlockSpec((IN_TM, D_MODEL), lambda i, j: (i, 0)),
            pl.BlockSpec((D_MODEL, IN_TILE), lambda i, j: (0, j)),
            tab, tab, tab,
        ],
        out_specs=pl.BlockSpec((IN_TM, IN_TILE), lambda i, j: (i, j)),
        out_shape=jax.ShapeDtypeStruct((ROWS, IN_PAD), F32),
        scratch_shapes=[pltpu.VMEM((IN_TM, D_MODEL), BF16)],
        compiler_params=_cparams("arbitrary", "arbitrary"),
        name="in_proj",
    )(x, w, cos_t, sa_t, sb_t)


def _swa_prompt_kernel(sink_ref, q_ref, kp_ref, kc_ref, vp_ref, vc_ref, o_ref):
    i = pl.program_id(0)
    k2 = jnp.concatenate([kp_ref[...], kc_ref[...]], axis=0).astype(BF16)
    v2 = jnp.concatenate([vp_ref[...], vc_ref[...]], axis=0).astype(BF16)
    row = lax.broadcasted_iota(jnp.int32, (BLK, 2 * BLK), 0)
    col = lax.broadcasted_iota(jnp.int32, (BLK, 2 * BLK), 1)
    diff = row - col + BLK
    kpos = (i - 1) * BLK + col - LEAD
    ok = (diff >= 0) & (diff <= WINDOW) & (kpos >= 0)
    for h in range(A_HEADS):
        kv = h // GQA
        qh = (q_ref[:, h * HEAD_DIM:(h + 1) * HEAD_DIM] * HEAD_DIM ** -0.5).astype(BF16)
        s = _dot_nt(qh, k2[:, kv * HEAD_DIM:(kv + 1) * HEAD_DIM])
        s = jnp.where(ok, s, NEG_INF)
        sk = sink_ref[h]
        m = jnp.maximum(jnp.max(s, -1, keepdims=True), sk)
        p = jnp.exp(s - m)
        den = jnp.sum(p, -1, keepdims=True) + jnp.exp(sk - m)
        o = _dot(p.astype(BF16), v2[:, kv * HEAD_DIM:(kv + 1) * HEAD_DIM]) / den
        o_ref[:, h * HEAD_DIM:(h + 1) * HEAD_DIM] = o.astype(BF16)


def _swa_prompt(sink, h):
    prev = lambda i: jnp.maximum(i - 1, 0)
    kvb = KV_WIDTH
    return pl.pallas_call(
        _swa_prompt_kernel,
        grid=(N_PBLK,),
        in_specs=[
            pl.BlockSpec(memory_space=pltpu.SMEM),
            pl.BlockSpec((BLK, A_WIDTH), lambda i: (i, C_Q // A_WIDTH)),
            pl.BlockSpec((BLK, kvb), lambda i: (prev(i), C_K // kvb)),
            pl.BlockSpec((BLK, kvb), lambda i: (i, C_K // kvb)),
            pl.BlockSpec((BLK, kvb), lambda i: (prev(i), C_V // kvb)),
            pl.BlockSpec((BLK, kvb), lambda i: (i, C_V // kvb)),
        ],
        out_specs=pl.BlockSpec((BLK, A_WIDTH), lambda i: (i, 0)),
        out_shape=jax.ShapeDtypeStruct((ROWS, A_WIDTH), BF16),
        compiler_params=_cparams("arbitrary"),
        name="swa_prompt",
    )(sink, h, h, h, h, h)


SWA_BT = 16


def _swa_sample_kernel(sink_ref, q_ref, kn_ref, vn_ref, ck_ref, cv_ref, o_ref, cko_ref, cvo_ref):
    bt = SWA_BT
    kn, vn = kn_ref[...], vn_ref[...]
    row = lax.broadcasted_iota(jnp.int32, (WINDOW, KV_WIDTH), 0)
    for j in range(bt):
        cko_ref[j] = jnp.where(row == WINDOW - 1, kn[j:j + 1, :], pltpu.roll(ck_ref[j], WINDOW - 1, 0))
        cvo_ref[j] = jnp.where(row == WINDOW - 1, vn[j:j + 1, :], pltpu.roll(cv_ref[j], WINDOW - 1, 0))
    k2 = ck_ref[...].reshape(bt * WINDOW, KV_WIDTH).astype(BF16)
    v2 = cv_ref[...].reshape(bt * WINDOW, KV_WIDTH).astype(BF16)
    nq = GQA * bt
    rowb = lax.broadcasted_iota(jnp.int32, (nq, bt * WINDOW), 0) % bt
    colb = lax.broadcasted_iota(jnp.int32, (nq, bt * WINDOW), 1) // WINDOW
    same = rowb == colb
    rg = lax.broadcasted_iota(jnp.int32, (nq, 1), 0) // bt
    for kv in range(A_KV_HEADS):
        sl = slice(kv * HEAD_DIM, (kv + 1) * HEAD_DIM)
        q4 = jnp.concatenate(
            [q_ref[:, (kv * GQA + g) * HEAD_DIM:(kv * GQA + g + 1) * HEAD_DIM] for g in range(GQA)], axis=0)
        q4 = q4 * HEAD_DIM ** -0.5
        kn4 = jnp.concatenate([kn[:, sl]] * GQA, axis=0)
        vn4 = jnp.concatenate([vn[:, sl]] * GQA, axis=0)
        sk = jnp.zeros((nq, 1), F32)
        for g in range(GQA):
            sk = jnp.where(rg == g, sink_ref[kv * GQA + g], sk)
        s = jnp.where(same, _dot_nt(q4.astype(BF16), k2[:, sl]), NEG_INF)
        s_self = jnp.sum(q4 * kn4, -1, keepdims=True)
        m = jnp.maximum(jnp.maximum(jnp.max(s, -1, keepdims=True), s_self), sk)
        p = jnp.exp(s - m)
        p_self = jnp.exp(s_self - m)
        den = jnp.sum(p, -1, keepdims=True) + p_self + jnp.exp(sk - m)
        o = (_dot(p.astype(BF16), v2[:, sl]) + p_self * vn4) / den
        for g in range(GQA):
            hh = kv * GQA + g
            o_ref[:, hh * HEAD_DIM:(hh + 1) * HEAD_DIM] = o[g * bt:(g + 1) * bt]


def _swa_sample(sink, h, cache_k, cache_v, l):
    bt = SWA_BT
    r0 = ROWS_P // bt
    cache = pl.BlockSpec((None, bt, WINDOW, KV_WIDTH), lambda s: (l, s, 0, 0))
    cache_o = pl.BlockSpec((bt, WINDOW, KV_WIDTH), lambda s: (s, 0, 0))
    return pl.pallas_call(
        _swa_sample_kernel,
        grid=(N_SAMPLE // bt,),
        in_specs=[
            pl.BlockSpec(memory_space=pltpu.SMEM),
            pl.BlockSpec((bt, A_WIDTH), lambda s: (r0 + s, C_Q // A_WIDTH)),
            pl.BlockSpec((bt, KV_WIDTH), lambda s: (r0 + s, C_K // KV_WIDTH)),
            pl.BlockSpec((bt, KV_WIDTH), lambda s: (r0 + s, C_V // KV_WIDTH)),
            cache, cache,
        ],
        out_specs=[pl.BlockSpec((bt, A_WIDTH), lambda s: (s, 0)), cache_o, cache_o],
        out_shape=[
            jax.ShapeDtypeStruct((N_SAMPLE, A_WIDTH), F32),
            jax.ShapeDtypeStruct((N_SAMPLE, WINDOW, KV_WIDTH), F32),
            jax.ShapeDtypeStruct((N_SAMPLE, WINDOW, KV_WIDTH), F32),
        ],
        compiler_params=_cparams("arbitrary"),
        name="swa_sample",
    )(sink, h, h, h, cache_k, cache_v)


S5_SHIFTS = (1, 2, 4)


def _s5_kernel(u_ref, bbre_ref, bbim_ref, cre_ref, cim_ref, d_ref, wg_ref, bg_ref, tab_ref, h0re_ref, h0im_ref,
               o_ref, hpre_ref, hpim_ref, hsre_ref, hsim_ref, hre, him, cre_s, cim_s):
    i = pl.program_id(0)
    u = u_ref[...]
    ub = u.astype(BF16)
    hre[...] = _dot(ub, bbre_ref[...])
    him[...] = _dot(ub, bbim_ref[...])

    def tab(k):
        return tab_ref[2 * k * SUB:(2 * k + 1) * SUB, :], tab_ref[(2 * k + 1) * SUB:(2 * k + 2) * SUB, :]

    @pl.when(i == 0)
    def _():
        cre_s[...] = jnp.zeros_like(cre_s)
        cim_s[...] = jnp.zeros_like(cim_s)

    @pl.when(i < N_PBLK)
    def _():
        cr, ci = cre_s[...], cim_s[...]
        pr, pi = tab(len(S5_SHIFTS))
        for j in range(BLK // SUB):
            rows = slice(j * SUB, (j + 1) * SUB)
            xr, xi = hre[rows, :], him[rows, :]
            for k, d in enumerate(S5_SHIFTS):
                ar, ai = tab(k)
                sr, si = pltpu.roll(xr, d, 0), pltpu.roll(xi, d, 0)
                xr, xi = xr + ar * sr - ai * si, xi + ar * si + ai * sr
            xr, xi = xr + pr * cr - pi * ci, xi + pr * ci + pi * cr
            hre[rows, :] = xr
            him[rows, :] = xi
            cr = jnp.broadcast_to(xr[SUB - 1:SUB, :], (SUB, S5_LANES))
            ci = jnp.broadcast_to(xi[SUB - 1:SUB, :], (SUB, S5_LANES))
        cre_s[...] = cr
        cim_s[...] = ci

    @pl.when(i == N_PBLK - 1)
    def _():
        hpre_ref[...] = cre_s[...]
        hpim_ref[...] = cim_s[...]

    @pl.when(i == N_PBLK)
    def _():
        pr, pi = tab(len(S5_SHIFTS))
        ar, ai = pr[0:1, :], pi[0:1, :]
        h0r, h0i = h0re_ref[...], h0im_ref[...]
        nr = ar * h0r - ai * h0i + hre[...]
        ni = ar * h0i + ai * h0r + him[...]
        hre[...] = nr
        him[...] = ni
        hsre_ref[...] = nr
        hsim_ref[...] = ni

    y = _dot(hre[...].astype(BF16), cre_ref[...]) - _dot(him[...].astype(BF16), cim_ref[...]) + d_ref[...] * u
    z = jax.nn.gelu(y)
    gate = jax.nn.sigmoid(_dot(z.astype(BF16), wg_ref[...]) + bg_ref[...])
    o_ref[...] = (z * gate).astype(BF16)


def _s5(h, prm, h0re, h0im, l):
    const = lambda shape: pl.BlockSpec(shape, lambda i: (0,) * len(shape))
    st_in = pl.BlockSpec((None, N_SAMPLE, S5_LANES), lambda i: (l, 0, 0))
    ntab = 2 * (len(S5_SHIFTS) + 1) * SUB
    return pl.pallas_call(
        _s5_kernel,
        grid=(N_PBLK + 1,),
        in_specs=[
            pl.BlockSpec((BLK, S5_WIDTH), lambda i: (i, C_U // S5_WIDTH)),
            const((S5_WIDTH, S5_LANES)), const((S5_WIDTH, S5_LANES)),
            const((S5_LANES, S5_WIDTH)), const((S5_LANES, S5_WIDTH)),
            const((1, S5_WIDTH)), const((S5_WIDTH, S5_WIDTH)), const((1, S5_WIDTH)),
            const((ntab, S5_LANES)),
            st_in, st_in,
        ],
        out_specs=[
            pl.BlockSpec((BLK, S5_WIDTH), lambda i: (i, 0)),
            const((SUB, S5_LANES)), const((SUB, S5_LANES)),
            const((N_SAMPLE, S5_LANES)), const((N_SAMPLE, S5_LANES)),
        ],
        out_shape=[
            jax.ShapeDtypeStruct((ROWS, S5_WIDTH), BF16),
            jax.ShapeDtypeStruct((SUB, S5_LANES), F32), jax.ShapeDtypeStruct((SUB, S5_LANES), F32),
            jax.ShapeDtypeStruct((N_SAMPLE, S5_LANES), F32), jax.ShapeDtypeStruct((N_SAMPLE, S5_LANES), F32),
        ],
        scratch_shapes=[pltpu.VMEM((BLK, S5_LANES), F32), pltpu.VMEM((BLK, S5_LANES), F32),
                        pltpu.VMEM((SUB, S5_LANES), F32), pltpu.VMEM((SUB, S5_LANES), F32)],
        compiler_params=_cparams("arbitrary"),
        name="s5",
    )(h, prm["bb_re"], prm["bb_im"], prm["c_re"], prm["c_im"], prm["d"], prm["w_glu"], prm["b_glu"], prm["tab"],
      h0re, h0im)


def _gla_log_gate(ac, wa_ref, ba_ref):
    return _log_sigmoid(_dot(ac.astype(BF16), wa_ref[...]) + ba_ref[...]) * (1.0 / GLA_TAU)


def _gla_finish(o_heads, gate, ng):
    outs = [o * lax.rsqrt(jnp.mean(o * o, -1, keepdims=True) + LN_EPS) for o in o_heads]
    return jnp.concatenate(outs, axis=1) * ng * (gate * jax.nn.sigmoid(gate))


def _gla_prompt_kernel(q_ref, k_ref, v_ref, gate_ref, ac_ref, wa_ref, ba_ref, ng_ref, o_ref, s_ref):
    i = pl.program_id(0)

    @pl.when(i == 0)
    def _():
        s_ref[...] = jnp.zeros_like(s_ref)

    lg = _gla_log_gate(ac_ref[...], wa_ref, ba_ref)
    L = GLA_CHUNK
    row = lax.broadcasted_iota(jnp.int32, (L, GLA_KW), 0)
    causal = lax.broadcasted_iota(jnp.int32, (L, L), 0) >= lax.broadcasted_iota(jnp.int32, (L, L), 1)
    for c in range(BLK // L):
        rows = slice(c * L, (c + 1) * L)
        b = lg[rows, :]
        d = 1
        while d < L:
            b = b + jnp.where(row >= d, pltpu.roll(b, d, 0), 0.0)
            d *= 2
        b_end = b[L - 1:L, :]
        kc = k_ref[rows, :]
        q_in = (q_ref[rows, :] * GLA_DK ** -0.5 * jnp.exp(b)).astype(BF16)
        k_out = (kc * jnp.exp(-b)).astype(BF16)
        k_dec = (kc * jnp.exp(b_end - b)).astype(BF16)
        dec = jnp.transpose(jnp.broadcast_to(jnp.exp(b_end), (GLA_DV, GLA_KW)))
        vb = v_ref[rows, :].astype(BF16)
        o_heads = []
        for hh in range(GLA_HEADS):
            ks = slice(hh * GLA_DK, (hh + 1) * GLA_DK)
            vh = vb[:, hh * GLA_DV:(hh + 1) * GLA_DV]
            att = jnp.where(causal, _dot_nt(q_in[:, ks], k_out[:, ks]), 0.0)
            s_old = s_ref[ks, :]
            o_heads.append(_dot(q_in[:, ks], s_old.astype(BF16)) + _dot(att.astype(BF16), vh))
            s_ref[ks, :] = dec[ks, :] * s_old + _dot_tn(k_dec[:, ks], vh)
        o_ref[rows, :] = _gla_finish(o_heads, gate_ref[rows, :], ng_ref[...]).astype(BF16)


def _gla_prompt(h, wa, ba, ng):
    const = lambda shape: pl.BlockSpec(shape, lambda i: (0,) * len(shape))
    return pl.pallas_call(
        _gla_prompt_kernel,
        grid=(N_PBLK,),
        in_specs=[
            pl.BlockSpec((BLK, GLA_KW), lambda i: (i, C_GQ // GLA_KW)),
            pl.BlockSpec((BLK, GLA_KW), lambda i: (i, C_GK // GLA_KW)),
            pl.BlockSpec((BLK, GLA_WIDTH), lambda i: (i, C_GV // GLA_WIDTH)),
            pl.BlockSpec((BLK, GLA_WIDTH), lambda i: (i, C_GG // GLA_WIDTH)),
            pl.BlockSpec((BLK, 128), lambda i: (i, C_GA // 128)),
            const((128, GLA_KW)), const((1, GLA_KW)), const((1, GLA_WIDTH)),
        ],
        out_specs=[pl.BlockSpec((BLK, GLA_WIDTH), lambda i: (i, 0)), const((GLA_KW, GLA_DV))],
        out_shape=[jax.ShapeDtypeStruct((ROWS, GLA_WIDTH), BF16), jax.ShapeDtypeStruct((GLA_KW, GLA_DV), F32)],
        compiler_params=_cparams("arbitrary"),
        name="gla_prompt",
    )(h, h, h, h, h, wa, ba, ng)


GLA_BT = 16


def _gla_sample_kernel(q_ref, k_ref, v_ref, gate_ref, ac_ref, wa_ref, ba_ref, ng_ref, s_ref, o_ref, so_ref):
    bt = GLA_BT
    lg = _gla_log_gate(ac_ref[...], wa_ref, ba_ref)
    v = v_ref[...]

    def cols(x):
        return jnp.transpose(jnp.concatenate([x, jnp.zeros((128 - bt, GLA_KW), F32)], axis=0))

    eg_t, k_t, q_t = cols(jnp.exp(lg)), cols(k_ref[...]), cols(q_ref[...] * GLA_DK ** -0.5)
    o_rows = []
    for j in range(bt):
        bc = lambda t: jnp.broadcast_to(t[:, j:j + 1], (GLA_KW, GLA_DV))
        vrow = jnp.concatenate(
            [jnp.broadcast_to(v[j:j + 1, hh * GLA_DV:(hh + 1) * GLA_DV], (GLA_DK, GLA_DV)) for hh in range(GLA_HEADS)],
            axis=0)
        s_new = bc(eg_t) * s_ref[j] + bc(k_t) * vrow
        so_ref[j] = s_new
        qs = bc(q_t) * s_new
        o_rows.append(jnp.concatenate(
            [jnp.sum(qs[hh * GLA_DK:(hh + 1) * GLA_DK, :], axis=0, keepdims=True) for hh in range(GLA_HEADS)], axis=1))
    o = jnp.concatenate(o_rows, axis=0)
    o_heads = [o[:, hh * GLA_DV:(hh + 1) * GLA_DV] for hh in range(GLA_HEADS)]
    o_ref[...] = _gla_finish(o_heads, gate_ref[...], ng_ref[...])


def _gla_sample(h, wa, ba, ng, state, l):
    bt = GLA_BT
    r0 = ROWS_P // bt
    const = lambda shape: pl.BlockSpec(shape, lambda s: (0,) * len(shape))
    return pl.pallas_call(
        _gla_sample_kernel,
        grid=(N_SAMPLE // bt,),
        in_specs=[
            pl.BlockSpec((bt, GLA_KW), lambda s: (r0 + s, C_GQ // GLA_KW)),
            pl.BlockSpec((bt, GLA_KW), lambda s: (r0 + s, C_GK // GLA_KW)),
            pl.BlockSpec((bt, GLA_WIDTH), lambda s: (r0 + s, C_GV // GLA_WIDTH)),
            pl.BlockSpec((bt, GLA_WIDTH), lambda s: (r0 + s, C_GG // GLA_WIDTH)),
            pl.BlockSpec((bt, 128), lambda s: (r0 + s, C_GA // 128)),
            const((128, GLA_KW)), const((1, GLA_KW)), const((1, GLA_WIDTH)),
            pl.BlockSpec((None, bt, GLA_KW, GLA_DV), lambda s: (l, s, 0, 0)),
        ],
        out_specs=[pl.BlockSpec((bt, GLA_WIDTH), lambda s: (s, 0)),
                   pl.BlockSpec((bt, GLA_KW, GLA_DV), lambda s: (s, 0, 0))],
        out_shape=[jax.ShapeDtypeStruct((N_SAMPLE, GLA_WIDTH), F32),
                   jax.ShapeDtypeStruct((N_SAMPLE, GLA_KW, GLA_DV), F32)],
        compiler_params=_cparams("arbitrary"),
        name="gla_sample",
    )(h, h, h, h, h, wa, ba, ng, state)


OUT_TM = 384


def _out_proj_kernel(oa_ref, ob_ref, og_ref, w_ref, x_ref, g_ref, b_ref, o_ref):
    i = pl.program_id(0)
    y = (_dot(oa_ref[...], w_ref[0:A_WIDTH, :])
         + _dot(ob_ref[...], w_ref[A_WIDTH:A_WIDTH + S5_WIDTH, :])
         + _dot(og_ref[...], w_ref[A_WIDTH + S5_WIDTH:, :])
         + ALPHA * x_ref[...])
    o_ref[...] = _zero_pad_rows(_layer_norm_rows(y, g_ref[...], b_ref[...]), i * OUT_TM)


def _out_proj(oa, ob, og, w, x, g, b):
    tm = OUT_TM
    const = lambda shape: pl.BlockSpec(shape, lambda i: (0,) * len(shape))
    return pl.pallas_call(
        _out_proj_kernel,
        grid=(ROWS // tm,),
        in_specs=[
            pl.BlockSpec((tm, A_WIDTH), lambda i: (i, 0)),
            pl.BlockSpec((tm, S5_WIDTH), lambda i: (i, 0)),
            pl.BlockSpec((tm, GLA_WIDTH), lambda i: (i, 0)),
            const((D_MODEL, D_MODEL)),
            pl.BlockSpec((tm, D_MODEL), lambda i: (i, 0)),
            const((1, D_MODEL)), const((1, D_MODEL)),
        ],
        out_specs=pl.BlockSpec((tm, D_MODEL), lambda i: (i, 0)),
        out_shape=jax.ShapeDtypeStruct((ROWS, D_MODEL), F32),
        compiler_params=_cparams("arbitrary"),
        name="out_proj",
    )(oa, ob, og, w, x, g, b)


FFN_TM = 768
FFN_TF = 256
FFN_HALO = 16
assert (ROWS - ROWS_P) == N_SAMPLE and ROWS % FFN_TM == 0 and FFN_TM >= N_SAMPLE + SUB


def _ffn_kernel(x_ref, xh_ref, wu_ref, wg_ref, cw_ref, cb_ref, wd_ref, p0_ref, p1_ref, g_ref, b_ref,
                o_ref, ut_ref, us_ref, xb_ref, acc_ref, um1_ref, um2_ref):
    i, f = pl.program_id(0), pl.program_id(1)
    tm = FFN_TM
    last_i = pl.num_programs(0) - 1

    @pl.when(f == 0)
    def _():
        xb_ref[...] = x_ref[...].astype(BF16)
        acc_ref[...] = jnp.zeros_like(acc_ref)

    xb = xb_ref[...]
    u = _dot(xb, wu_ref[...])
    gte = _dot(xb, wg_ref[...])
    xh = jnp.where(i > 0, xh_ref[...], 0.0).astype(BF16)
    ext = jnp.concatenate([_dot(xh, wu_ref[...]), u], axis=0)
    um1_ref[...] = ext[FFN_HALO - 1:FFN_HALO - 1 + tm, :]
    um2_ref[...] = ext[FFN_HALO - 2:FFN_HALO - 2 + tm, :]

    @pl.when(i == last_i)
    def _():
        um1_ref[tm - N_SAMPLE:, :] = p1_ref[...]
        um2_ref[tm - N_SAMPLE:, :] = p0_ref[...]

    c = cb_ref[...] + cw_ref[0:1, :] * um2_ref[...] + cw_ref[1:2, :] * um1_ref[...] + cw_ref[2:3, :] * u
    a = (jax.nn.gelu(c) * gte).astype(BF16)
    acc_ref[...] += _dot(a, wd_ref[...])
    ut_ref[...] = u[tm - N_SAMPLE - SUB:tm - N_SAMPLE, :]
    us_ref[...] = u[tm - N_SAMPLE:, :]

    @pl.when(f == pl.num_programs(1) - 1)
    def _():
        y = ALPHA * x_ref[...] + acc_ref[...]
        o_ref[...] = _zero_pad_rows(_layer_norm_rows(y, g_ref[...], b_ref[...]), i * tm)


def _ffn(x, wup, cw, cb, wd, p0, p1, g, b):
    tm, tf = FFN_TM, FFN_TF
    nf = D_FF // tf
    const = lambda shape: pl.BlockSpec(shape, lambda i, f: (0,) * len(shape))
    tail = lambda i, f: (0, jnp.where(i == ROWS // tm - 1, f, 0))
    return pl.pallas_call(
        _ffn_kernel,
        grid=(ROWS // tm, nf),
        in_specs=[
            pl.BlockSpec((tm, D_MODEL), lambda i, f: (i, 0)),
            pl.BlockSpec((FFN_HALO, D_MODEL), lambda i, f: (jnp.maximum(i * (tm // FFN_HALO) - 1, 0), 0)),
            pl.BlockSpec((D_MODEL, tf), lambda i, f: (0, f)),
            pl.BlockSpec((D_MODEL, tf), lambda i, f: (0, nf + f)),
            pl.BlockSpec((CONV_W, tf), lambda i, f: (0, f)),
            pl.BlockSpec((1, tf), lambda i, f: (0, f)),
            pl.BlockSpec((tf, D_MODEL), lambda i, f: (f, 0)),
            pl.BlockSpec((N_SAMPLE, tf), tail),
            pl.BlockSpec((N_SAMPLE, tf), tail),
            const((1, D_MODEL)), const((1, D_MODEL)),
        ],
        out_specs=[
            pl.BlockSpec((tm, D_MODEL), lambda i, f: (i, 0)),
            pl.BlockSpec((SUB, tf), tail),
            pl.BlockSpec((N_SAMPLE, tf), tail),
        ],
        out_shape=[
            jax.ShapeDtypeStruct((ROWS, D_MODEL), F32),
            jax.ShapeDtypeStruct((SUB, D_FF), F32),
            jax.ShapeDtypeStruct((N_SAMPLE, D_FF), F32),
        ],
        scratch_shapes=[pltpu.VMEM((tm, D_MODEL), BF16), pltpu.VMEM((tm, D_MODEL), F32),
                        pltpu.VMEM((tm, tf), F32), pltpu.VMEM((tm, tf), F32)],
        compiler_params=_cparams("arbitrary", "arbitrary"),
        name="conv_ffn",
    )(x, x, wup, wup, cw, cb, wd, p0, p1, g, b)


def _rope_tables():
    half = ROPE_DIM // 2
    inv = jnp.power(ROPE_THETA, -jnp.arange(half, dtype=F32) / half)
    pos = jnp.concatenate([jnp.arange(ROWS_P, dtype=jnp.int32) - LEAD,
                           jnp.full((N_SAMPLE,), PAST_LEN, jnp.int32)]).astype(F32)
    ang = pos[:, None] * inv[None, :]
    cos, sin = jnp.cos(ang), jnp.sin(ang)
    one = jnp.ones((ROWS, HEAD_DIM - ROPE_DIM), F32)
    zero = jnp.zeros((ROWS, HEAD_DIM - ROPE_DIM), F32)
    zh = jnp.zeros((ROWS, half), F32)
    cos_t = jnp.concatenate([cos, cos, one], 1)
    sa_t = jnp.concatenate([-sin, zh, zero], 1)
    sb_t = jnp.concatenate([zh, sin, zero], 1)
    rep = lambda t: jnp.concatenate([t, t], 1)
    return rep(cos_t), rep(sa_t), rep(sb_t)


def _s5_tables(lam_re, lam_im, log_dt, b_re, b_im, c_re, c_im, d, w_glu, b_glu):
    dt = jnp.exp(log_dt)[:, None]
    mag = jnp.exp(lam_re * dt)
    ab_re, ab_im = mag * jnp.cos(lam_im * dt), mag * jnp.sin(lam_im * dt)
    den = lam_re * lam_re + lam_im * lam_im
    nr, ni = ab_re - 1.0, ab_im
    f_re = (nr * lam_re + ni * lam_im) / den
    f_im = (ni * lam_re - nr * lam_im) / den
    bb_re = f_re[..., None] * b_re - f_im[..., None] * b_im
    bb_im = f_re[..., None] * b_im + f_im[..., None] * b_re
    eye = jnp.eye(S5_GROUPS, dtype=F32)
    bd_in = lambda t: jnp.einsum('gph,gk->ghkp', t, eye).reshape(S5_WIDTH, S5_LANES)
    bd_out = lambda t: jnp.einsum('ghp,gk->gpkh', t, eye).reshape(S5_LANES, S5_WIDTH)
    ar, ai = ab_re.reshape(1, S5_LANES), ab_im.reshape(1, S5_LANES)
    pows = [(ar, ai)]
    for _ in range(SUB - 1):
        pr, pi = pows[-1]
        pows.append((pr * ar - pi * ai, pr * ai + pi * ar))
    row = jnp.arange(SUB)[:, None]
    parts = []
    for sft in S5_SHIFTS:
        pr, pi = pows[sft - 1]
        parts += [jnp.where(row >= sft, pr, 0.0), jnp.where(row >= sft, pi, 0.0)]
    parts += [jnp.concatenate([p[0] for p in pows], 0), jnp.concatenate([p[1] for p in pows], 0)]
    return dict(bb_re=bd_in(bb_re).astype(BF16), bb_im=bd_in(bb_im).astype(BF16),
                c_re=bd_out(c_re).astype(BF16), c_im=bd_out(c_im).astype(BF16),
                d=d.reshape(1, S5_WIDTH), w_glu=w_glu.astype(BF16), b_glu=b_glu.reshape(1, S5_WIDTH),
                tab=jnp.concatenate(parts, 0))


@jax.jit
def kernel(x_prompt, x_sample, cache_swa_k, cache_swa_v, state_ssm_re, state_ssm_im, state_gla, state_conv,
           meta_tokens, ln_in_g, ln_in_b, w_in, attn_sink, s5_lam_re, s5_lam_im, s5_log_dt, s5_b_re, s5_b_im,
           s5_c_re, s5_c_im, s5_d, s5_w_glu, s5_b_glu, gla_w_a2, gla_b_a, gla_norm_g, w_out, ln1_g, ln1_b,
           ffn_w_up, ffn_conv_w, ffn_conv_b, ffn_w_down, ln2_g, ln2_b):
    L = DEPTH
    row2 = lambda t: t.reshape(1, -1)
    w_in_b = jnp.pad(w_in, ((0, 0), (0, 0), (0, IN_PAD - IN_COLS))).astype(BF16)
    w_out_b = w_out.astype(BF16)
    w_up_b = ffn_w_up.astype(BF16)
    w_down_b = ffn_w_down.astype(BF16)
    wa_b = jnp.pad(gla_w_a2, ((0, 0), (0, 128 - GLA_LOWRANK), (0, 0))).astype(BF16)
    cos_t, sa_t, sb_t = _rope_tables()
    ck = cache_swa_k.reshape(L, N_SAMPLE, WINDOW, KV_WIDTH)
    cv = cache_swa_v.reshape(L, N_SAMPLE, WINDOW, KV_WIDTH)
    h0re = state_ssm_re.reshape(L, N_SAMPLE, S5_LANES)
    h0im = state_ssm_im.reshape(L, N_SAMPLE, S5_LANES)
    sg = state_gla.reshape(L, N_SAMPLE, GLA_KW, GLA_DV)

    x = _ln_in(x_prompt.reshape(SEQ, D_MODEL), meta_tokens, x_sample.reshape(N_SAMPLE, D_MODEL),
               row2(ln_in_g), row2(ln_in_b))
    outs = [[] for _ in range(12)]
    for l in range(L):
        h = _in_proj(x, w_in_b[l], cos_t, sa_t, sb_t)
        oa = _swa_prompt(attn_sink[l], h)
        oa_s, ck_new, cv_new = _swa_sample(attn_sink[l], h, ck, cv, l)
        oa = lax.dynamic_update_slice(oa, oa_s.astype(BF16), (ROWS_P, 0))
        s5p = _s5_tables(s5_lam_re[l], s5_lam_im[l], s5_log_dt[l], s5_b_re[l], s5_b_im[l], s5_c_re[l], s5_c_im[l],
                         s5_d[l], s5_w_glu[l], s5_b_glu[l])
        ob, hp_re, hp_im, hs_re, hs_im = _s5(h, s5p, h0re, h0im, l)
        gla_args = (wa_b[l], row2(gla_b_a[l]), row2(gla_norm_g[l]))
        og, s_p = _gla_prompt(h, *gla_args)
        og_s, s_s = _gla_sample(h, *gla_args, sg, l)
        og = lax.dynamic_update_slice(og, og_s.astype(BF16), (ROWS_P, 0))
        x = _out_proj(oa, ob, og, w_out_b[l], x, row2(ln1_g[l]), row2(ln1_b[l]))
        p0, p1 = state_conv[l, :, 0, :], state_conv[l, :, 1, :]
        x, u_tail, u_s = _ffn(x, w_up_b[l], ffn_conv_w[l], row2(ffn_conv_b[l]), w_down_b[l], p0, p1,
                              row2(ln2_g[l]), row2(ln2_b[l]))
        kv_p = lambda c: h[ROWS_P - WINDOW:ROWS_P, c:c + KV_WIDTH].reshape(1, WINDOW, A_KV_HEADS, HEAD_DIM)
        new = (kv_p(C_K), kv_p(C_V),
               hp_re[SUB - 1].reshape(1, S5_GROUPS, S5_STATE), hp_im[SUB - 1].reshape(1, S5_GROUPS, S5_STATE),
               s_p.reshape(1, GLA_HEADS, GLA_DK, GLA_DV), u_tail[SUB - 2:SUB][None],
               ck_new.reshape(N_SAMPLE, WINDOW, A_KV_HEADS, HEAD_DIM),
               cv_new.reshape(N_SAMPLE, WINDOW, A_KV_HEADS, HEAD_DIM),
               hs_re.reshape(N_SAMPLE, S5_GROUPS, S5_STATE), hs_im.reshape(N_SAMPLE, S5_GROUPS, S5_STATE),
               s_s.reshape(N_SAMPLE, GLA_HEADS, GLA_DK, GLA_DV), jnp.stack([p1, u_s], axis=1))
        for lst, val in zip(outs, new):
            lst.append(val)
    y_prompt = x[BLK:ROWS_P].reshape(1, SEQ, D_MODEL)
    y_sample = x[ROWS_P:].reshape(N_SAMPLE, 1, D_MODEL)
    return (y_prompt, y_sample) + tuple(jnp.stack(lst) for lst in outs)
```

```python
import jax
import jax.numpy as jnp
from jax import lax
from jax.experimental import pallas as pl
from jax.experimental.pallas import tpu as pltpu

F32 = jnp.float32
BF16 = jnp.bfloat16

D_MODEL = 2048
SEQ = 8192
DEPTH = 2
N_SAMPLE = 128
N_META = 16
HEAD_DIM = 64
A_WIDTH = 1024
A_HEADS = 16
A_KV_HEADS = 4
GQA = 4
KV_WIDTH = A_KV_HEADS * HEAD_DIM
WINDOW = 128
ROPE_DIM = 16
ROPE_THETA = 500000.0
PAST_LEN = 8192
S5_WIDTH = 512
S5_CH = 16
S5_GROUPS = 32
S5_STATE = 64
S5_LANES = S5_GROUPS * S5_STATE
GLA_WIDTH = 512
GLA_HEADS = 4
GLA_DV = 128
GLA_DK = 64
GLA_KW = GLA_HEADS * GLA_DK
GLA_LOWRANK = 16
GLA_TAU = 16.0
GLA_CHUNK = 64
D_FF = 5632
CONV_W = 3
LN_EPS = 1e-5
ALPHA = (2 * DEPTH) ** 0.25
NEG_INF = -1e30

BLK = 128
T_PROMPT = N_META + SEQ
LEAD = (-T_PROMPT) % BLK
N_PBLK = (LEAD + T_PROMPT) // BLK
ROWS_P = N_PBLK * BLK
ROWS = ROWS_P + N_SAMPLE
SUB = 8

C_Q, C_K, C_V, C_U, C_GQ, C_GK, C_GV, C_GG, C_GA = 0, 1024, 1280, 1536, 2048, 2304, 2560, 3072, 3584
IN_COLS = 3600
IN_TILE = 1280
IN_PAD = 3 * IN_TILE
ROPE_COLS = A_WIDTH + KV_WIDTH
M_A, M_B, M_C = 0, A_WIDTH, A_WIDTH + S5_WIDTH

VMEM_LIMIT = 56 * 1024 * 1024


def _cparams(*sem):
    return pltpu.CompilerParams(dimension_semantics=sem, vmem_limit_bytes=VMEM_LIMIT)


def _any():
    return pl.BlockSpec(memory_space=pl.ANY)


def _drop_inputs(kern, start, count):
    def wrapped(*refs):
        return kern(*refs[:start], *refs[start + count:])
    return wrapped


def _layer_norm_rows(x, g, b):
    mu = jnp.mean(x, -1, keepdims=True)
    xc = x - mu
    var = jnp.mean(xc * xc, -1, keepdims=True)
    return xc * lax.rsqrt(var + LN_EPS) * g + b


def _zero_pad_rows(y, first_row):
    row = first_row + lax.broadcasted_iota(jnp.int32, y.shape, 0)
    return jnp.where(row >= LEAD, y, 0.0)


LN_CHUNK = 16


def _deepnorm_ln(acc_ref, x_ref, g_ref, b_ref, o_ref, first_row, rows):
    g, b = g_ref[...], b_ref[...]

    def body(c, carry):
        r = pl.ds(pl.multiple_of(c * LN_CHUNK, LN_CHUNK), LN_CHUNK)
        y = _layer_norm_rows(ALPHA * x_ref[r, :] + acc_ref[r, :], g, b)
        o_ref[r, :] = _zero_pad_rows(y, first_row + c * LN_CHUNK)
        return carry

    lax.fori_loop(0, rows // LN_CHUNK, body, 0)


def _dot(a, b):
    return jnp.dot(a, b, preferred_element_type=F32)


def _dot_nt(a, b):
    return lax.dot_general(a, b, (((1,), (1,)), ((), ())), preferred_element_type=F32)


def _dot_tn(a, b):
    return lax.dot_general(a, b, (((0,), (0,)), ((), ())), preferred_element_type=F32)


def _log_sigmoid(x):
    return jnp.minimum(x, 0.0) - jnp.log(1.0 + jnp.exp(-jnp.abs(x)))


def _ln_in_kernel(xp_ref, meta_ref, xs_ref, g_ref, b_ref, o_ref):
    i = pl.program_id(0)
    g, b = g_ref[...], b_ref[...]

    @pl.when(i == 0)
    def _():
        o_ref[0:LEAD, :] = jnp.zeros((LEAD, D_MODEL), F32)
        o_ref[LEAD:BLK, :] = _layer_norm_rows(meta_ref[...], g, b)

    @pl.when(jnp.logical_and(i > 0, i < N_PBLK))
    def _():
        o_ref[...] = _layer_norm_rows(xp_ref[...], g, b)

    @pl.when(i == N_PBLK)
    def _():
        o_ref[...] = _layer_norm_rows(xs_ref[...], g, b)


def _ln_in(xp, meta, xs, g, b):
    return pl.pallas_call(
        _ln_in_kernel,
        grid=(N_PBLK + 1,),
        in_specs=[
            pl.BlockSpec((BLK, D_MODEL), lambda i: (jnp.clip(i - 1, 0, SEQ // BLK - 1), 0)),
            pl.BlockSpec((N_META, D_MODEL), lambda i: (0, 0)),
            pl.BlockSpec((N_SAMPLE, D_MODEL), lambda i: (0, 0)),
            pl.BlockSpec((1, D_MODEL), lambda i: (0, 0)),
            pl.BlockSpec((1, D_MODEL), lambda i: (0, 0)),
        ],
        out_specs=pl.BlockSpec((BLK, D_MODEL), lambda i: (i, 0)),
        out_shape=jax.ShapeDtypeStruct((ROWS, D_MODEL), F32),
        compiler_params=_cparams("arbitrary"),
        name="ln_in",
    )(xp, meta, xs, g, b)


IN_TM = 1056


def _in_proj_kernel(x_ref, w_ref, cos_ref, sa_ref, sb_ref, o_ref, xb_ref):
    j = pl.program_id(1)

    @pl.when(j == 0)
    def _():
        xb_ref[...] = x_ref[...].astype(BF16)

    o_ref[...] = _dot(xb_ref[...], w_ref[...])

    @pl.when(j == 0)
    def _():
        cos, sa, sb = cos_ref[...], sa_ref[...], sb_ref[...]
        for c in range(ROPE_COLS // 128):
            blk = o_ref[:, c * 128:(c + 1) * 128]
            o_ref[:, c * 128:(c + 1) * 128] = (
                blk * cos + pltpu.roll(blk, 128 - ROPE_DIM // 2, 1) * sa + pltpu.roll(blk, ROPE_DIM // 2, 1) * sb)


def _in_proj(x, w, cos_t, sa_t, sb_t, l):
    tab = pl.BlockSpec((IN_TM, 128), lambda i, j: (i, 0))
    return pl.pallas_call(
        _in_proj_kernel,
        grid=(ROWS // IN_TM, IN_PAD // IN_TILE),
        in_specs=[
            pl.BlockSpec((IN_TM, D_MODEL), lambda i, j: (i, 0)),
            pl.BlockSpec((None, D_MODEL, IN_TILE), lambda i, j: (l, 0, j)),
            tab, tab, tab,
        ],
        out_specs=pl.BlockSpec((IN_TM, IN_TILE), lambda i, j: (i, j)),
        out_shape=jax.ShapeDtypeStruct((ROWS, IN_PAD), F32),
        scratch_shapes=[pltpu.VMEM((IN_TM, D_MODEL), BF16)],
        compiler_params=_cparams("arbitrary", "arbitrary"),
        name="in_proj",
    )(x, w, cos_t, sa_t, sb_t)


def _swa_prompt_kernel(sink_ref, q_ref, kp_ref, kc_ref, vp_ref, vc_ref, o_ref):
    i = pl.program_id(0)
    k2 = jnp.concatenate([kp_ref[...], kc_ref[...]], axis=0).astype(BF16)
    v2 = jnp.concatenate([vp_ref[...], vc_ref[...]], axis=0).astype(BF16)
    row = lax.broadcasted_iota(jnp.int32, (BLK, 2 * BLK), 0)
    col = lax.broadcasted_iota(jnp.int32, (BLK, 2 * BLK), 1)
    diff = row - col + BLK
    kpos = (i - 1) * BLK + col - LEAD
    ok = (diff >= 0) & (diff <= WINDOW) & (kpos >= 0)
    for h in range(A_HEADS):
        kv = h // GQA
        qh = (q_ref[:, h * HEAD_DIM:(h + 1) * HEAD_DIM] * HEAD_DIM ** -0.5).astype(BF16)
        s = _dot_nt(qh, k2[:, kv * HEAD_DIM:(kv + 1) * HEAD_DIM])
        s = jnp.where(ok, s, NEG_INF)
        sk = sink_ref[h]
        m = jnp.maximum(jnp.max(s, -1, keepdims=True), sk)
        p = jnp.exp(s - m)
        den = jnp.sum(p, -1, keepdims=True) + jnp.exp(sk - m)
        o = _dot(p.astype(BF16), v2[:, kv * HEAD_DIM:(kv + 1) * HEAD_DIM]) / den
        o_ref[:, h * HEAD_DIM:(h + 1) * HEAD_DIM] = o.astype(BF16)


def _swa_prompt(sink, h):
    prev = lambda i: jnp.maximum(i - 1, 0)
    kvb = KV_WIDTH
    return pl.pallas_call(
        _swa_prompt_kernel,
        grid=(N_PBLK,),
        in_specs=[
            pl.BlockSpec(memory_space=pltpu.SMEM),
            pl.BlockSpec((BLK, A_WIDTH), lambda i: (i, C_Q // A_WIDTH)),
            pl.BlockSpec((BLK, kvb), lambda i: (prev(i), C_K // kvb)),
            pl.BlockSpec((BLK, kvb), lambda i: (i, C_K // kvb)),
            pl.BlockSpec((BLK, kvb), lambda i: (prev(i), C_V // kvb)),
            pl.BlockSpec((BLK, kvb), lambda i: (i, C_V // kvb)),
        ],
        out_specs=pl.BlockSpec((BLK, A_WIDTH), lambda i: (i, M_A // A_WIDTH)),
        out_shape=jax.ShapeDtypeStruct((ROWS, D_MODEL), BF16),
        compiler_params=_cparams("arbitrary"),
        name="swa_prompt",
    )(sink, h, h, h, h, h)


SWA_BT = 16


def _swa_sample_kernel(sink_ref, q_ref, kn_ref, vn_ref, ck_ref, cv_ref, o_ref, cko_ref, cvo_ref):
    bt = SWA_BT
    kn, vn = kn_ref[...], vn_ref[...]
    row = lax.broadcasted_iota(jnp.int32, (WINDOW, KV_WIDTH), 0)
    for j in range(bt):
        cko_ref[j] = jnp.where(row == WINDOW - 1, kn[j:j + 1, :], pltpu.roll(ck_ref[j], WINDOW - 1, 0))
        cvo_ref[j] = jnp.where(row == WINDOW - 1, vn[j:j + 1, :], pltpu.roll(cv_ref[j], WINDOW - 1, 0))
    k2 = ck_ref[...].reshape(bt * WINDOW, KV_WIDTH).astype(BF16)
    v2 = cv_ref[...].reshape(bt * WINDOW, KV_WIDTH).astype(BF16)
    nq = GQA * bt
    rowb = lax.broadcasted_iota(jnp.int32, (nq, bt * WINDOW), 0) % bt
    colb = lax.broadcasted_iota(jnp.int32, (nq, bt * WINDOW), 1) // WINDOW
    same = rowb == colb
    rg = lax.broadcasted_iota(jnp.int32, (nq, 1), 0) // bt
    for kv in range(A_KV_HEADS):
        sl = slice(kv * HEAD_DIM, (kv + 1) * HEAD_DIM)
        q4 = jnp.concatenate(
            [q_ref[:, (kv * GQA + g) * HEAD_DIM:(kv * GQA + g + 1) * HEAD_DIM] for g in range(GQA)], axis=0)
        q4 = q4 * HEAD_DIM ** -0.5
        kn4 = jnp.concatenate([kn[:, sl]] * GQA, axis=0)
        vn4 = jnp.concatenate([vn[:, sl]] * GQA, axis=0)
        sk = jnp.zeros((nq, 1), F32)
        for g in range(GQA):
            sk = jnp.where(rg == g, sink_ref[kv * GQA + g], sk)
        s = jnp.where(same, _dot_nt(q4.astype(BF16), k2[:, sl]), NEG_INF)
        s_self = jnp.sum(q4 * kn4, -1, keepdims=True)
        m = jnp.maximum(jnp.maximum(jnp.max(s, -1, keepdims=True), s_self), sk)
        p = jnp.exp(s - m)
        p_self = jnp.exp(s_self - m)
        den = jnp.sum(p, -1, keepdims=True) + p_self + jnp.exp(sk - m)
        o = (_dot(p.astype(BF16), v2[:, sl]) + p_self * vn4) / den
        for g in range(GQA):
            hh = kv * GQA + g
            o_ref[:, hh * HEAD_DIM:(hh + 1) * HEAD_DIM] = o[g * bt:(g + 1) * bt].astype(BF16)


def _swa_sample(sink, h, cache_k, cache_v, mix, ck_all, cv_all, l):
    bt = SWA_BT
    r0 = ROWS_P // bt
    cache = pl.BlockSpec((None, bt, WINDOW, KV_WIDTH), lambda s: (l, s, 0, 0))
    n_in = 6
    aliased = [mix] if ck_all is None else [mix, ck_all, cv_all]
    cache_shape = jax.ShapeDtypeStruct((DEPTH, N_SAMPLE, WINDOW, KV_WIDTH), F32)
    return pl.pallas_call(
        _drop_inputs(_swa_sample_kernel, n_in, len(aliased)),
        grid=(N_SAMPLE // bt,),
        in_specs=[
            pl.BlockSpec(memory_space=pltpu.SMEM),
            pl.BlockSpec((bt, A_WIDTH), lambda s: (r0 + s, C_Q // A_WIDTH)),
            pl.BlockSpec((bt, KV_WIDTH), lambda s: (r0 + s, C_K // KV_WIDTH)),
            pl.BlockSpec((bt, KV_WIDTH), lambda s: (r0 + s, C_V // KV_WIDTH)),
            cache, cache,
        ] + [_any()] * len(aliased),
        out_specs=[pl.BlockSpec((bt, A_WIDTH), lambda s: (r0 + s, M_A // A_WIDTH)), cache, cache],
        out_shape=[jax.ShapeDtypeStruct((ROWS, D_MODEL), BF16), cache_shape, cache_shape],
        input_output_aliases={n_in + k: k for k in range(len(aliased))},
        compiler_params=_cparams("arbitrary"),
        name="swa_sample",
    )(sink, h, h, h, cache_k, cache_v, *aliased)


S5_SHIFTS = (1, 2, 4)


def _s5_kernel(u_ref, bbre_ref, bbim_ref, cre_ref, cim_ref, d_ref, wg_ref, bg_ref, tab_ref, h0re_ref, h0im_ref,
               o_ref, hpre_ref, hpim_ref, hsre_ref, hsim_ref, hre, him, cre_s, cim_s):
    i = pl.program_id(0)
    u = u_ref[...]
    ub = u.astype(BF16)
    hre[...] = _dot(ub, bbre_ref[...])
    him[...] = _dot(ub, bbim_ref[...])

    def tab(k):
        return tab_ref[2 * k * SUB:(2 * k + 1) * SUB, :], tab_ref[(2 * k + 1) * SUB:(2 * k + 2) * SUB, :]

    @pl.when(i == 0)
    def _():
        cre_s[...] = jnp.zeros_like(cre_s)
        cim_s[...] = jnp.zeros_like(cim_s)

    @pl.when(i < N_PBLK)
    def _():
        cr, ci = cre_s[...], cim_s[...]
        pr, pi = tab(len(S5_SHIFTS))
        for j in range(BLK // SUB):
            rows = slice(j * SUB, (j + 1) * SUB)
            xr, xi = hre[rows, :], him[rows, :]
            for k, d in enumerate(S5_SHIFTS):
                ar, ai = tab(k)
                sr, si = pltpu.roll(xr, d, 0), pltpu.roll(xi, d, 0)
                xr, xi = xr + ar * sr - ai * si, xi + ar * si + ai * sr
            xr, xi = xr + pr * cr - pi * ci, xi + pr * ci + pi * cr
            hre[rows, :] = xr
            him[rows, :] = xi
            cr = jnp.broadcast_to(xr[SUB - 1:SUB, :], (SUB, S5_LANES))
            ci = jnp.broadcast_to(xi[SUB - 1:SUB, :], (SUB, S5_LANES))
        cre_s[...] = cr
        cim_s[...] = ci

    @pl.when(i == N_PBLK - 1)
    def _():
        hpre_ref[...] = cre_s[...]
        hpim_ref[...] = cim_s[...]

    @pl.when(i == N_PBLK)
    def _():
        pr, pi = tab(len(S5_SHIFTS))
        ar, ai = pr[0:1, :], pi[0:1, :]
        h0r, h0i = h0re_ref[...], h0im_ref[...]
        nr = ar * h0r - ai * h0i + hre[...]
        ni = ar * h0i + ai * h0r + him[...]
        hre[...] = nr
        him[...] = ni
        hsre_ref[...] = nr
        hsim_ref[...] = ni

    y = _dot(hre[...].astype(BF16), cre_ref[...]) - _dot(him[...].astype(BF16), cim_ref[...]) + d_ref[...] * u
    z = jax.nn.gelu(y)
    gate = jax.nn.sigmoid(_dot(z.astype(BF16), wg_ref[...]) + bg_ref[...])
    o_ref[...] = (z * gate).astype(BF16)


def _s5(h, prm, h0re, h0im, mix, l):
    const = lambda shape: pl.BlockSpec(shape, lambda i: (0,) * len(shape))
    st_in = pl.BlockSpec((None, N_SAMPLE, S5_LANES), lambda i: (l, 0, 0))
    ntab = 2 * (len(S5_SHIFTS) + 1) * SUB
    n_in = 11
    return pl.pallas_call(
        _drop_inputs(_s5_kernel, n_in, 1),
        grid=(N_PBLK + 1,),
        in_specs=[
            pl.BlockSpec((BLK, S5_WIDTH), lambda i: (i, C_U // S5_WIDTH)),
            const((S5_WIDTH, S5_LANES)), const((S5_WIDTH, S5_LANES)),
            const((S5_LANES, S5_WIDTH)), const((S5_LANES, S5_WIDTH)),
            const((1, S5_WIDTH)), const((S5_WIDTH, S5_WIDTH)), const((1, S5_WIDTH)),
            const((ntab, S5_LANES)),
            st_in, st_in,
            _any(),
        ],
        out_specs=[
            pl.BlockSpec((BLK, S5_WIDTH), lambda i: (i, M_B // S5_WIDTH)),
            const((SUB, S5_LANES)), const((SUB, S5_LANES)),
            const((N_SAMPLE, S5_LANES)), const((N_SAMPLE, S5_LANES)),
        ],
        out_shape=[
            jax.ShapeDtypeStruct((ROWS, D_MODEL), BF16),
            jax.ShapeDtypeStruct((SUB, S5_LANES), F32), jax.ShapeDtypeStruct((SUB, S5_LANES), F32),
            jax.ShapeDtypeStruct((N_SAMPLE, S5_LANES), F32), jax.ShapeDtypeStruct((N_SAMPLE, S5_LANES), F32),
        ],
        scratch_shapes=[pltpu.VMEM((BLK, S5_LANES), F32), pltpu.VMEM((BLK, S5_LANES), F32),
                        pltpu.VMEM((SUB, S5_LANES), F32), pltpu.VMEM((SUB, S5_LANES), F32)],
        input_output_aliases={n_in: 0},
        compiler_params=_cparams("arbitrary"),
        name="s5",
    )(h, prm["bb_re"], prm["bb_im"], prm["c_re"], prm["c_im"], prm["d"], prm["w_glu"], prm["b_glu"], prm["tab"],
      h0re, h0im, mix)


def _gla_log_gate(ac, wa_ref, ba_ref):
    return _log_sigmoid(_dot(ac.astype(BF16), wa_ref[...]) + ba_ref[...]) * (1.0 / GLA_TAU)


def _gla_finish(o_heads, gate, ng):
    outs = [o * lax.rsqrt(jnp.mean(o * o, -1, keepdims=True) + LN_EPS) for o in o_heads]
    return jnp.concatenate(outs, axis=1) * ng * (gate * jax.nn.sigmoid(gate))


def _gla_prompt_kernel(q_ref, k_ref, v_ref, gate_ref, ac_ref, wa_ref, ba_ref, ng_ref, o_ref, s_ref):
    i = pl.program_id(0)

    @pl.when(i == 0)
    def _():
        s_ref[...] = jnp.zeros_like(s_ref)

    lg = _gla_log_gate(ac_ref[...], wa_ref, ba_ref)
    L = GLA_CHUNK
    row = lax.broadcasted_iota(jnp.int32, (L, GLA_KW), 0)
    causal = lax.broadcasted_iota(jnp.int32, (L, L), 0) >= lax.broadcasted_iota(jnp.int32, (L, L), 1)
    for c in range(BLK // L):
        rows = slice(c * L, (c + 1) * L)
        b = lg[rows, :]
        d = 1
        while d < L:
            b = b + jnp.where(row >= d, pltpu.roll(b, d, 0), 0.0)
            d *= 2
        b_end = b[L - 1:L, :]
        kc = k_ref[rows, :]
        q_in = (q_ref[rows, :] * GLA_DK ** -0.5 * jnp.exp(b)).astype(BF16)
        k_out = (kc * jnp.exp(-b)).astype(BF16)
        k_dec = (kc * jnp.exp(b_end - b)).astype(BF16)
        dec = jnp.transpose(jnp.broadcast_to(jnp.exp(b_end), (GLA_DV, GLA_KW)))
        vb = v_ref[rows, :].astype(BF16)
        o_heads = []
        for hh in range(GLA_HEADS):
            ks = slice(hh * GLA_DK, (hh + 1) * GLA_DK)
            vh = vb[:, hh * GLA_DV:(hh + 1) * GLA_DV]
            att = jnp.where(causal, _dot_nt(q_in[:, ks], k_out[:, ks]), 0.0)
            s_old = s_ref[ks, :]
            o_heads.append(_dot(q_in[:, ks], s_old.astype(BF16)) + _dot(att.astype(BF16), vh))
            s_ref[ks, :] = dec[ks, :] * s_old + _dot_tn(k_dec[:, ks], vh)
        o_ref[rows, :] = _gla_finish(o_heads, gate_ref[rows, :], ng_ref[...]).astype(BF16)


def _gla_prompt(h, wa, ba, ng, mix):
    const = lambda shape: pl.BlockSpec(shape, lambda i: (0,) * len(shape))
    n_in = 8
    return pl.pallas_call(
        _drop_inputs(_gla_prompt_kernel, n_in, 1),
        grid=(N_PBLK,),
        in_specs=[
            pl.BlockSpec((BLK, GLA_KW), lambda i: (i, C_GQ // GLA_KW)),
            pl.BlockSpec((BLK, GLA_KW), lambda i: (i, C_GK // GLA_KW)),
            pl.BlockSpec((BLK, GLA_WIDTH), lambda i: (i, C_GV // GLA_WIDTH)),
            pl.BlockSpec((BLK, GLA_WIDTH), lambda i: (i, C_GG // GLA_WIDTH)),
            pl.BlockSpec((BLK, 128), lambda i: (i, C_GA // 128)),
            const((128, GLA_KW)), const((1, GLA_KW)), const((1, GLA_WIDTH)),
            _any(),
        ],
        out_specs=[pl.BlockSpec((BLK, GLA_WIDTH), lambda i: (i, M_C // GLA_WIDTH)), const((GLA_KW, GLA_DV))],
        out_shape=[jax.ShapeDtypeStruct((ROWS, D_MODEL), BF16), jax.ShapeDtypeStruct((GLA_KW, GLA_DV), F32)],
        input_output_aliases={n_in: 0},
        compiler_params=_cparams("arbitrary"),
        name="gla_prompt",
    )(h, h, h, h, h, wa, ba, ng, mix)


GLA_BT = 16


def _gla_sample_kernel(q_ref, k_ref, v_ref, gate_ref, ac_ref, wa_ref, ba_ref, ng_ref, s_ref, o_ref, so_ref):
    bt = GLA_BT
    lg = _gla_log_gate(ac_ref[...], wa_ref, ba_ref)
    v = v_ref[...]

    def cols(x):
        return jnp.transpose(jnp.concatenate([x, jnp.zeros((128 - bt, GLA_KW), F32)], axis=0))

    eg_t, k_t, q_t = cols(jnp.exp(lg)), cols(k_ref[...]), cols(q_ref[...] * GLA_DK ** -0.5)
    o_rows = []
    for j in range(bt):
        bc = lambda t: jnp.broadcast_to(t[:, j:j + 1], (GLA_KW, GLA_DV))
        vrow = jnp.concatenate(
            [jnp.broadcast_to(v[j:j + 1, hh * GLA_DV:(hh + 1) * GLA_DV], (GLA_DK, GLA_DV)) for hh in range(GLA_HEADS)],
            axis=0)
        s_new = bc(eg_t) * s_ref[j] + bc(k_t) * vrow
        so_ref[j] = s_new
        qs = bc(q_t) * s_new
        o_rows.append(jnp.concatenate(
            [jnp.sum(qs[hh * GLA_DK:(hh + 1) * GLA_DK, :], axis=0, keepdims=True) for hh in range(GLA_HEADS)], axis=1))
    o = jnp.concatenate(o_rows, axis=0)
    o_heads = [o[:, hh * GLA_DV:(hh + 1) * GLA_DV] for hh in range(GLA_HEADS)]
    o_ref[...] = _gla_finish(o_heads, gate_ref[...], ng_ref[...]).astype(BF16)


def _gla_sample(h, wa, ba, ng, state, mix, s_all, l):
    bt = GLA_BT
    r0 = ROWS_P // bt
    const = lambda shape: pl.BlockSpec(shape, lambda s: (0,) * len(shape))
    st = pl.BlockSpec((None, bt, GLA_KW, GLA_DV), lambda s: (l, s, 0, 0))
    n_in = 9
    aliased = [mix] if s_all is None else [mix, s_all]
    return pl.pallas_call(
        _drop_inputs(_gla_sample_kernel, n_in, len(aliased)),
        grid=(N_SAMPLE // bt,),
        in_specs=[
            pl.BlockSpec((bt, GLA_KW), lambda s: (r0 + s, C_GQ // GLA_KW)),
            pl.BlockSpec((bt, GLA_KW), lambda s: (r0 + s, C_GK // GLA_KW)),
            pl.BlockSpec((bt, GLA_WIDTH), lambda s: (r0 + s, C_GV // GLA_WIDTH)),
            pl.BlockSpec((bt, GLA_WIDTH), lambda s: (r0 + s, C_GG // GLA_WIDTH)),
            pl.BlockSpec((bt, 128), lambda s: (r0 + s, C_GA // 128)),
            const((128, GLA_KW)), const((1, GLA_KW)), const((1, GLA_WIDTH)),
            st,
        ] + [_any()] * len(aliased),
        out_specs=[pl.BlockSpec((bt, GLA_WIDTH), lambda s: (r0 + s, M_C // GLA_WIDTH)), st],
        out_shape=[jax.ShapeDtypeStruct((ROWS, D_MODEL), BF16),
                   jax.ShapeDtypeStruct((DEPTH, N_SAMPLE, GLA_KW, GLA_DV), F32)],
        input_output_aliases={n_in + k: k for k in range(len(aliased))},
        compiler_params=_cparams("arbitrary"),
        name="gla_sample",
    )(h, h, h, h, h, wa, ba, ng, state, *aliased)


OUT_TM = 768


def _out_proj_kernel(mix_ref, w_ref, x_ref, g_ref, b_ref, o_ref):
    i = pl.program_id(0)
    o_ref[...] = _dot(mix_ref[...], w_ref[...])
    _deepnorm_ln(o_ref, x_ref, g_ref, b_ref, o_ref, i * OUT_TM, OUT_TM)


def _out_proj(mix, w, x, g, b, l):
    tm = OUT_TM
    const = lambda shape: pl.BlockSpec(shape, lambda i: (0,) * len(shape))
    return pl.pallas_call(
        _out_proj_kernel,
        grid=(ROWS // tm,),
        in_specs=[
            pl.BlockSpec((tm, D_MODEL), lambda i: (i, 0)),
            pl.BlockSpec((None, D_MODEL, D_MODEL), lambda i: (l, 0, 0), pipeline_mode=pl.Buffered(1)),
            pl.BlockSpec((tm, D_MODEL), lambda i: (i, 0)),
            const((1, D_MODEL)), const((1, D_MODEL)),
        ],
        out_specs=pl.BlockSpec((tm, D_MODEL), lambda i: (i, 0)),
        out_shape=jax.ShapeDtypeStruct((ROWS, D_MODEL), F32),
        compiler_params=_cparams("arbitrary"),
        name="out_proj",
    )(mix, w, x, g, b)


FFN_TM = 768
FFN_TF = 512
FFN_HALO = 16
assert (ROWS - ROWS_P) == N_SAMPLE and ROWS % FFN_TM == 0 and FFN_TM >= N_SAMPLE + SUB


def _ffn_kernel(x_ref, xh_ref, wu_ref, wg_ref, cw_ref, cb_ref, wd_ref, p0_ref, p1_ref, g_ref, b_ref,
                o_ref, ut_ref, us_ref, xb_ref, a_ref):
    i, f = pl.program_id(0), pl.program_id(1)
    tm, hl = FFN_TM, FFN_HALO
    last_i = pl.num_programs(0) - 1

    @pl.when(f == 0)
    def _():
        xb_ref[...] = x_ref[...].astype(BF16)
        o_ref[...] = jnp.zeros_like(o_ref)

    xb = xb_ref[...]
    u = _dot(xb, wu_ref[...])
    gte = _dot(xb, wg_ref[...])
    cw0, cw1, cw2, cb = cw_ref[0:1, :], cw_ref[1:2, :], cw_ref[2:3, :], cb_ref[...]

    def act(um2, um1, u0, g0):
        return (jax.nn.gelu(cb + cw0 * um2 + cw1 * um1 + cw2 * u0) * g0).astype(BF16)

    a_ref[...] = act(pltpu.roll(u, 2, 0), pltpu.roll(u, 1, 0), u, gte)
    xh = jnp.where(i > 0, xh_ref[...], 0.0).astype(BF16)
    ext = jnp.concatenate([_dot(xh, wu_ref[...]), u[0:hl, :]], axis=0)
    a_ref[0:hl, :] = act(ext[hl - 2:2 * hl - 2, :], ext[hl - 1:2 * hl - 1, :], u[0:hl, :], gte[0:hl, :])

    @pl.when(i == last_i)
    def _():
        a_ref[tm - N_SAMPLE:, :] = act(p0_ref[...], p1_ref[...], u[tm - N_SAMPLE:, :], gte[tm - N_SAMPLE:, :])

    o_ref[...] += _dot(a_ref[...], wd_ref[...])
    ut_ref[...] = u[tm - N_SAMPLE - SUB:tm - N_SAMPLE, :]
    us_ref[0] = p1_ref[...]
    us_ref[1] = u[tm - N_SAMPLE:, :]

    @pl.when(f == pl.num_programs(1) - 1)
    def _():
        _deepnorm_ln(o_ref, x_ref, g_ref, b_ref, o_ref, i * tm, tm)


def _ffn(x, wup, cw, cb, wd, conv_prev, g, b, l):
    tm, tf = FFN_TM, FFN_TF
    nf = D_FF // tf
    const = lambda shape: pl.BlockSpec(shape, lambda i, f: (0,) * len(shape))
    tail = lambda i, f: jnp.where(i == ROWS // tm - 1, f, 0)
    return pl.pallas_call(
        _ffn_kernel,
        grid=(ROWS // tm, nf),
        in_specs=[
            pl.BlockSpec((tm, D_MODEL), lambda i, f: (i, 0)),
            pl.BlockSpec((FFN_HALO, D_MODEL), lambda i, f: (jnp.maximum(i * (tm // FFN_HALO) - 1, 0), 0)),
            pl.BlockSpec((None, D_MODEL, tf), lambda i, f: (l, 0, f)),
            pl.BlockSpec((None, D_MODEL, tf), lambda i, f: (l, 0, nf + f)),
            pl.BlockSpec((None, CONV_W, tf), lambda i, f: (l, 0, f)),
            pl.BlockSpec((None, 1, tf), lambda i, f: (l, 0, f)),
            pl.BlockSpec((None, tf, D_MODEL), lambda i, f: (l, f, 0)),
            pl.BlockSpec((None, N_SAMPLE, tf), lambda i, f: (l, 0, tail(i, f))),
            pl.BlockSpec((None, N_SAMPLE, tf), lambda i, f: (l, 0, nf + tail(i, f))),
            const((1, D_MODEL)), const((1, D_MODEL)),
        ],
        out_specs=[
            pl.BlockSpec((tm, D_MODEL), lambda i, f: (i, 0)),
            pl.BlockSpec((SUB, tf), lambda i, f: (0, tail(i, f))),
            pl.BlockSpec((2, N_SAMPLE, tf), lambda i, f: (0, 0, tail(i, f))),
        ],
        out_shape=[
            jax.ShapeDtypeStruct((ROWS, D_MODEL), F32),
            jax.ShapeDtypeStruct((SUB, D_FF), F32),
            jax.ShapeDtypeStruct((2, N_SAMPLE, D_FF), F32),
        ],
        scratch_shapes=[pltpu.VMEM((tm, D_MODEL), BF16), pltpu.VMEM((tm, tf), BF16)],
        compiler_params=_cparams("arbitrary", "arbitrary"),
        name="conv_ffn",
    )(x, x, wup, wup, cw, cb, wd, conv_prev, conv_prev, g, b)


def _rope_tables():
    half = ROPE_DIM // 2
    inv = jnp.power(ROPE_THETA, -jnp.arange(half, dtype=F32) / half)
    pos = jnp.concatenate([jnp.arange(ROWS_P, dtype=jnp.int32) - LEAD,
                           jnp.full((N_SAMPLE,), PAST_LEN, jnp.int32)]).astype(F32)
    ang = pos[:, None] * inv[None, :]
    cos, sin = jnp.cos(ang), jnp.sin(ang)
    one = jnp.ones((ROWS, HEAD_DIM - ROPE_DIM), F32)
    zero = jnp.zeros((ROWS, HEAD_DIM - ROPE_DIM), F32)
    zh = jnp.zeros((ROWS, half), F32)
    cos_t = jnp.concatenate([cos, cos, one], 1)
    sa_t = jnp.concatenate([-sin, zh, zero], 1)
    sb_t = jnp.concatenate([zh, sin, zero], 1)
    rep = lambda t: jnp.concatenate([t, t], 1)
    return rep(cos_t), rep(sa_t), rep(sb_t)


def _s5_tables(lam_re, lam_im, log_dt, b_re, b_im, c_re, c_im, d, w_glu, b_glu):
    dt = jnp.exp(log_dt)[:, None]
    mag = jnp.exp(lam_re * dt)
    ab_re, ab_im = mag * jnp.cos(lam_im * dt), mag * jnp.sin(lam_im * dt)
    den = lam_re * lam_re + lam_im * lam_im
    nr, ni = ab_re - 1.0, ab_im
    f_re = (nr * lam_re + ni * lam_im) / den
    f_im = (ni * lam_re - nr * lam_im) / den
    bb_re = f_re[..., None] * b_re - f_im[..., None] * b_im
    bb_im = f_re[..., None] * b_im + f_im[..., None] * b_re
    eye = jnp.eye(S5_GROUPS, dtype=F32)
    bd_in = lambda t: jnp.einsum('gph,gk->ghkp', t, eye).reshape(S5_WIDTH, S5_LANES)
    bd_out = lambda t: jnp.einsum('ghp,gk->gpkh', t, eye).reshape(S5_LANES, S5_WIDTH)
    ar, ai = ab_re.reshape(1, S5_LANES), ab_im.reshape(1, S5_LANES)
    pows = [(ar, ai)]
    for _ in range(SUB - 1):
        pr, pi = pows[-1]
        pows.append((pr * ar - pi * ai, pr * ai + pi * ar))
    row = jnp.arange(SUB)[:, None]
    parts = []
    for sft in S5_SHIFTS:
        pr, pi = pows[sft - 1]
        parts += [jnp.where(row >= sft, pr, 0.0), jnp.where(row >= sft, pi, 0.0)]
    parts += [jnp.concatenate([p[0] for p in pows], 0), jnp.concatenate([p[1] for p in pows], 0)]
    return dict(bb_re=bd_in(bb_re).astype(BF16), bb_im=bd_in(bb_im).astype(BF16),
                c_re=bd_out(c_re).astype(BF16), c_im=bd_out(c_im).astype(BF16),
                d=d.reshape(1, S5_WIDTH), w_glu=w_glu.astype(BF16), b_glu=b_glu.reshape(1, S5_WIDTH),
                tab=jnp.concatenate(parts, 0))


@jax.jit
def kernel(x_prompt, x_sample, cache_swa_k, cache_swa_v, state_ssm_re, state_ssm_im, state_gla, state_conv,
           meta_tokens, ln_in_g, ln_in_b, w_in, attn_sink, s5_lam_re, s5_lam_im, s5_log_dt, s5_b_re, s5_b_im,
           s5_c_re, s5_c_im, s5_d, s5_w_glu, s5_b_glu, gla_w_a2, gla_b_a, gla_norm_g, w_out, ln1_g, ln1_b,
           ffn_w_up, ffn_conv_w, ffn_conv_b, ffn_w_down, ln2_g, ln2_b):
    L = DEPTH
    row2 = lambda t: t.reshape(1, -1)
    w_in_b = jnp.pad(w_in, ((0, 0), (0, 0), (0, IN_PAD - IN_COLS))).astype(BF16)
    w_out_b = w_out.astype(BF16)
    w_up_b = ffn_w_up.astype(BF16)
    w_down_b = ffn_w_down.astype(BF16)
    wa_b = jnp.pad(gla_w_a2, ((0, 0), (0, 128 - GLA_LOWRANK), (0, 0))).astype(BF16)
    conv_b3 = ffn_conv_b.reshape(L, 1, D_FF)
    cos_t, sa_t, sb_t = _rope_tables()
    ck = cache_swa_k.reshape(L, N_SAMPLE, WINDOW, KV_WIDTH)
    cv = cache_swa_v.reshape(L, N_SAMPLE, WINDOW, KV_WIDTH)
    h0re = state_ssm_re.reshape(L, N_SAMPLE, S5_LANES)
    h0im = state_ssm_im.reshape(L, N_SAMPLE, S5_LANES)
    sg = state_gla.reshape(L, N_SAMPLE, GLA_KW, GLA_DV)
    conv_prev = state_conv.reshape(L, N_SAMPLE, (CONV_W - 1) * D_FF)

    x = _ln_in(x_prompt.reshape(SEQ, D_MODEL), meta_tokens, x_sample.reshape(N_SAMPLE, D_MODEL),
               row2(ln_in_g), row2(ln_in_b))
    outs = [[] for _ in range(9)]
    ck_all = cv_all = s_all = None
    for l in range(L):
        h = _in_proj(x, w_in_b, cos_t, sa_t, sb_t, l)
        mix = _swa_prompt(attn_sink[l], h)
        mix, ck_all, cv_all = _swa_sample(attn_sink[l], h, ck, cv, mix, ck_all, cv_all, l)
        s5p = _s5_tables(s5_lam_re[l], s5_lam_im[l], s5_log_dt[l], s5_b_re[l], s5_b_im[l], s5_c_re[l], s5_c_im[l],
                         s5_d[l], s5_w_glu[l], s5_b_glu[l])
        mix, hp_re, hp_im, hs_re, hs_im = _s5(h, s5p, h0re, h0im, mix, l)
        gla_args = (wa_b[l], row2(gla_b_a[l]), row2(gla_norm_g[l]))
        mix, s_p = _gla_prompt(h, *gla_args, mix)
        mix, s_all = _gla_sample(h, *gla_args, sg, mix, s_all, l)
        x = _out_proj(mix, w_out_b, x, row2(ln1_g[l]), row2(ln1_b[l]), l)
        x, u_tail, u_s = _ffn(x, w_up_b, ffn_conv_w, conv_b3, w_down_b, conv_prev, row2(ln2_g[l]), row2(ln2_b[l]), l)
        kv_p = lambda c: h[ROWS_P - WINDOW:ROWS_P, c:c + KV_WIDTH].reshape(1, WINDOW, A_KV_HEADS, HEAD_DIM)
        new = (kv_p(C_K), kv_p(C_V),
               hp_re[SUB - 1].reshape(1, S5_GROUPS, S5_STATE), hp_im[SUB - 1].reshape(1, S5_GROUPS, S5_STATE),
               s_p.reshape(1, GLA_HEADS, GLA_DK, GLA_DV), u_tail[SUB - 2:SUB][None],
               hs_re.reshape(N_SAMPLE, S5_GROUPS, S5_STATE), hs_im.reshape(N_SAMPLE, S5_GROUPS, S5_STATE),
               jnp.transpose(u_s, (1, 0, 2)))
        for lst, val in zip(outs, new):
            lst.append(val)
    y_prompt = x[BLK:ROWS_P].reshape(1, SEQ, D_MODEL)
    y_sample = x[ROWS_P:].reshape(N_SAMPLE, 1, D_MODEL)
    st = [jnp.stack(lst) for lst in outs]
    return (y_prompt, y_sample, st[0], st[1], st[2], st[3], st[4], st[5],
            ck_all.reshape(L, N_SAMPLE, WINDOW, A_KV_HEADS, HEAD_DIM),
            cv_all.reshape(L, N_SAMPLE, WINDOW, A_KV_HEADS, HEAD_DIM),
            st[6], st[7], s_all.reshape(L, N_SAMPLE, GLA_HEADS, GLA_DK, GLA_DV), st[8])
```

```python
import jax
import jax.numpy as jnp
from jax import lax
from jax.experimental import pallas as pl
from jax.experimental.pallas import tpu as pltpu

F32 = jnp.float32
BF16 = jnp.bfloat16

D_MODEL = 2048
SEQ = 8192
DEPTH = 2
N_SAMPLE = 128
N_META = 16
HEAD_DIM = 64
A_WIDTH = 1024
A_HEADS = 16
A_KV_HEADS = 4
GQA = 4
KV_WIDTH = A_KV_HEADS * HEAD_DIM
WINDOW = 128
ROPE_DIM = 16
ROPE_THETA = 500000.0
PAST_LEN = 8192
S5_WIDTH = 512
S5_CH = 16
S5_GROUPS = 32
S5_STATE = 64
S5_LANES = S5_GROUPS * S5_STATE
GLA_WIDTH = 512
GLA_HEADS = 4
GLA_DV = 128
GLA_DK = 64
GLA_KW = GLA_HEADS * GLA_DK
GLA_LOWRANK = 16
GLA_TAU = 16.0
GLA_CHUNK = 64
D_FF = 5632
CONV_W = 3
LN_EPS = 1e-5
ALPHA = (2 * DEPTH) ** 0.25
NEG_INF = -1e30

BLK = 128
T_PROMPT = N_META + SEQ
LEAD = (-T_PROMPT) % BLK
N_PBLK = (LEAD + T_PROMPT) // BLK
ROWS_P = N_PBLK * BLK
ROWS = ROWS_P + N_SAMPLE
SUB = 8

C_Q, C_K, C_V, C_U, C_GQ, C_GK, C_GV, C_GG, C_GA = 0, 1024, 1280, 1536, 2048, 2304, 2560, 3072, 3584
IN_COLS = 3600
IN_TILE = 1280
IN_PAD = 3 * IN_TILE
ROPE_COLS = A_WIDTH + KV_WIDTH
M_A, M_B, M_C = 0, A_WIDTH, A_WIDTH + S5_WIDTH

VMEM_LIMIT = 56 * 1024 * 1024


def _cparams(*sem):
    return pltpu.CompilerParams(dimension_semantics=sem, vmem_limit_bytes=VMEM_LIMIT)


def _any():
    return pl.BlockSpec(memory_space=pl.ANY)


def _drop_inputs(kern, start, count):
    def wrapped(*refs):
        return kern(*refs[:start], *refs[start + count:])
    return wrapped


def _layer_norm_rows(x, g, b):
    mu = jnp.mean(x, -1, keepdims=True)
    xc = x - mu
    var = jnp.mean(xc * xc, -1, keepdims=True)
    return xc * lax.rsqrt(var + LN_EPS) * g + b


def _zero_pad_rows(y, first_row):
    row = first_row + lax.broadcasted_iota(jnp.int32, y.shape, 0)
    return jnp.where(row >= LEAD, y, 0.0)


def _deepnorm_ln(acc_ref, x_ref, g_ref, b_ref, o_ref, first_row, rows):
    g, b = g_ref[...], b_ref[...]
    for c in range(rows // BLK):
        r = slice(c * BLK, (c + 1) * BLK)
        y = _layer_norm_rows(ALPHA * x_ref[r, :] + acc_ref[r, :], g, b)
        o_ref[r, :] = _zero_pad_rows(y, first_row + c * BLK)


def _dot(a, b):
    return jnp.dot(a, b, preferred_element_type=F32)


def _dot_nt(a, b):
    return lax.dot_general(a, b, (((1,), (1,)), ((), ())), preferred_element_type=F32)


def _dot_tn(a, b):
    return lax.dot_general(a, b, (((0,), (0,)), ((), ())), preferred_element_type=F32)


def _log_sigmoid(x):
    return jnp.minimum(x, 0.0) - jnp.log(1.0 + jnp.exp(-jnp.abs(x)))


def _ln_in_kernel(xp_ref, meta_ref, xs_ref, g_ref, b_ref, o_ref):
    i = pl.program_id(0)
    g, b = g_ref[...], b_ref[...]

    @pl.when(i == 0)
    def _():
        o_ref[0:LEAD, :] = jnp.zeros((LEAD, D_MODEL), F32)
        o_ref[LEAD:BLK, :] = _layer_norm_rows(meta_ref[...], g, b)

    @pl.when(jnp.logical_and(i > 0, i < N_PBLK))
    def _():
        o_ref[...] = _layer_norm_rows(xp_ref[...], g, b)

    @pl.when(i == N_PBLK)
    def _():
        o_ref[...] = _layer_norm_rows(xs_ref[...], g, b)


def _ln_in(xp, meta, xs, g, b):
    return pl.pallas_call(
        _ln_in_kernel,
        grid=(N_PBLK + 1,),
        in_specs=[
            pl.BlockSpec((BLK, D_MODEL), lambda i: (jnp.clip(i - 1, 0, SEQ // BLK - 1), 0)),
            pl.BlockSpec((N_META, D_MODEL), lambda i: (0, 0)),
            pl.BlockSpec((N_SAMPLE, D_MODEL), lambda i: (0, 0)),
            pl.BlockSpec((1, D_MODEL), lambda i: (0, 0)),
            pl.BlockSpec((1, D_MODEL), lambda i: (0, 0)),
        ],
        out_specs=pl.BlockSpec((BLK, D_MODEL), lambda i: (i, 0)),
        out_shape=jax.ShapeDtypeStruct((ROWS, D_MODEL), F32),
        compiler_params=_cparams("arbitrary"),
        name="ln_in",
    )(xp, meta, xs, g, b)


IN_TM = 1056


def _in_proj_kernel(x_ref, w_ref, cos_ref, sa_ref, sb_ref, o_ref, xb_ref):
    j = pl.program_id(1)

    @pl.when(j == 0)
    def _():
        xb_ref[...] = x_ref[...].astype(BF16)

    o_ref[...] = _dot(xb_ref[...], w_ref[...])

    @pl.when(j == 0)
    def _():
        cos, sa, sb = cos_ref[...], sa_ref[...], sb_ref[...]
        for c in range(ROPE_COLS // 128):
            blk = o_ref[:, c * 128:(c + 1) * 128]
            o_ref[:, c * 128:(c + 1) * 128] = (
                blk * cos + pltpu.roll(blk, 128 - ROPE_DIM // 2, 1) * sa + pltpu.roll(blk, ROPE_DIM // 2, 1) * sb)

    @pl.when(j == IN_PAD // IN_TILE - 1)
    def _():
        edge = (IN_COLS % IN_TILE) // 128 * 128
        col = edge + lax.broadcasted_iota(jnp.int32, (IN_TM, 128), 1)
        o_ref[:, edge:edge + 128] = jnp.where(col < IN_COLS % IN_TILE, o_ref[:, edge:edge + 128], 0.0)
        o_ref[:, edge + 128:] = jnp.zeros((IN_TM, IN_TILE - edge - 128), F32)


def _in_proj(x, w, cos_t, sa_t, sb_t, l):
    tab = pl.BlockSpec((IN_TM, 128), lambda i, j: (i, 0))
    return pl.pallas_call(
        _in_proj_kernel,
        grid=(ROWS // IN_TM, IN_PAD // IN_TILE),
        in_specs=[
            pl.BlockSpec((IN_TM, D_MODEL), lambda i, j: (i, 0)),
            pl.BlockSpec((None, D_MODEL, IN_TILE), lambda i, j: (l, 0, j)),
            tab, tab, tab,
        ],
        out_specs=pl.BlockSpec((IN_TM, IN_TILE), lambda i, j: (i, j)),
        out_shape=jax.ShapeDtypeStruct((ROWS, IN_PAD), F32),
        scratch_shapes=[pltpu.VMEM((IN_TM, D_MODEL), BF16)],
        compiler_params=_cparams("arbitrary", "arbitrary"),
        name="in_proj",
    )(x, w, cos_t, sa_t, sb_t)


def _swa_prompt_kernel(sink_ref, q_ref, kp_ref, kc_ref, vp_ref, vc_ref, o_ref):
    i = pl.program_id(0)
    k2 = jnp.concatenate([kp_ref[...], kc_ref[...]], axis=0).astype(BF16)
    v2 = jnp.concatenate([vp_ref[...], vc_ref[...]], axis=0).astype(BF16)
    row = lax.broadcasted_iota(jnp.int32, (BLK, 2 * BLK), 0)
    col = lax.broadcasted_iota(jnp.int32, (BLK, 2 * BLK), 1)
    diff = row - col + BLK
    kpos = (i - 1) * BLK + col - LEAD
    ok = (diff >= 0) & (diff <= WINDOW) & (kpos >= 0)
    for h in range(A_HEADS):
        kv = h // GQA
        qh = (q_ref[:, h * HEAD_DIM:(h + 1) * HEAD_DIM] * HEAD_DIM ** -0.5).astype(BF16)
        s = _dot_nt(qh, k2[:, kv * HEAD_DIM:(kv + 1) * HEAD_DIM])
        s = jnp.where(ok, s, NEG_INF)
        sk = sink_ref[h]
        m = jnp.maximum(jnp.max(s, -1, keepdims=True), sk)
        p = jnp.exp(s - m)
        den = jnp.sum(p, -1, keepdims=True) + jnp.exp(sk - m)
        o = _dot(p.astype(BF16), v2[:, kv * HEAD_DIM:(kv + 1) * HEAD_DIM]) / den
        o_ref[:, h * HEAD_DIM:(h + 1) * HEAD_DIM] = o.astype(BF16)


def _swa_prompt(sink, h):
    prev = lambda i: jnp.maximum(i - 1, 0)
    kvb = KV_WIDTH
    return pl.pallas_call(
        _swa_prompt_kernel,
        grid=(N_PBLK,),
        in_specs=[
            pl.BlockSpec(memory_space=pltpu.SMEM),
            pl.BlockSpec((BLK, A_WIDTH), lambda i: (i, C_Q // A_WIDTH)),
            pl.BlockSpec((BLK, kvb), lambda i: (prev(i), C_K // kvb)),
            pl.BlockSpec((BLK, kvb), lambda i: (i, C_K // kvb)),
            pl.BlockSpec((BLK, kvb), lambda i: (prev(i), C_V // kvb)),
            pl.BlockSpec((BLK, kvb), lambda i: (i, C_V // kvb)),
        ],
        out_specs=pl.BlockSpec((BLK, A_WIDTH), lambda i: (i, M_A // A_WIDTH)),
        out_shape=jax.ShapeDtypeStruct((ROWS, D_MODEL), BF16),
        compiler_params=_cparams("arbitrary"),
        name="swa_prompt",
    )(sink, h, h, h, h, h)


SWA_BT = 16


def _swa_sample_kernel(sink_ref, q_ref, kn_ref, vn_ref, ck_ref, cv_ref, o_ref, cko_ref, cvo_ref):
    bt = SWA_BT
    kn, vn = kn_ref[...], vn_ref[...]
    row = lax.broadcasted_iota(jnp.int32, (WINDOW, KV_WIDTH), 0)
    for j in range(bt):
        cko_ref[j] = jnp.where(row == WINDOW - 1, kn[j:j + 1, :], pltpu.roll(ck_ref[j], WINDOW - 1, 0))
        cvo_ref[j] = jnp.where(row == WINDOW - 1, vn[j:j + 1, :], pltpu.roll(cv_ref[j], WINDOW - 1, 0))
    k2 = ck_ref[...].reshape(bt * WINDOW, KV_WIDTH).astype(BF16)
    v2 = cv_ref[...].reshape(bt * WINDOW, KV_WIDTH).astype(BF16)
    nq = GQA * bt
    rowb = lax.broadcasted_iota(jnp.int32, (nq, bt * WINDOW), 0) % bt
    colb = lax.broadcasted_iota(jnp.int32, (nq, bt * WINDOW), 1) // WINDOW
    same = rowb == colb
    rg = lax.broadcasted_iota(jnp.int32, (nq, 1), 0) // bt
    for kv in range(A_KV_HEADS):
        sl = slice(kv * HEAD_DIM, (kv + 1) * HEAD_DIM)
        q4 = jnp.concatenate(
            [q_ref[:, (kv * GQA + g) * HEAD_DIM:(kv * GQA + g + 1) * HEAD_DIM] for g in range(GQA)], axis=0)
        q4 = q4 * HEAD_DIM ** -0.5
        kn4 = jnp.concatenate([kn[:, sl]] * GQA, axis=0)
        vn4 = jnp.concatenate([vn[:, sl]] * GQA, axis=0)
        sk = jnp.zeros((nq, 1), F32)
        for g in range(GQA):
            sk = jnp.where(rg == g, sink_ref[kv * GQA + g], sk)
        s = jnp.where(same, _dot_nt(q4.astype(BF16), k2[:, sl]), NEG_INF)
        s_self = jnp.sum(q4 * kn4, -1, keepdims=True)
        m = jnp.maximum(jnp.maximum(jnp.max(s, -1, keepdims=True), s_self), sk)
        p = jnp.exp(s - m)
        p_self = jnp.exp(s_self - m)
        den = jnp.sum(p, -1, keepdims=True) + p_self + jnp.exp(sk - m)
        o = (_dot(p.astype(BF16), v2[:, sl]) + p_self * vn4) / den
        for g in range(GQA):
            hh = kv * GQA + g
            o_ref[:, hh * HEAD_DIM:(hh + 1) * HEAD_DIM] = o[g * bt:(g + 1) * bt].astype(BF16)


def _swa_sample(sink, h, cache_k, cache_v, mix, ck_all, cv_all, l):
    bt = SWA_BT
    r0 = ROWS_P // bt
    cache = pl.BlockSpec((None, bt, WINDOW, KV_WIDTH), lambda s: (l, s, 0, 0))
    n_in = 6
    aliased = [mix] if ck_all is None else [mix, ck_all, cv_all]
    cache_shape = jax.ShapeDtypeStruct((DEPTH, N_SAMPLE, WINDOW, KV_WIDTH), F32)
    return pl.pallas_call(
        _drop_inputs(_swa_sample_kernel, n_in, len(aliased)),
        grid=(N_SAMPLE // bt,),
        in_specs=[
            pl.BlockSpec(memory_space=pltpu.SMEM),
            pl.BlockSpec((bt, A_WIDTH), lambda s: (r0 + s, C_Q // A_WIDTH)),
            pl.BlockSpec((bt, KV_WIDTH), lambda s: (r0 + s, C_K // KV_WIDTH)),
            pl.BlockSpec((bt, KV_WIDTH), lambda s: (r0 + s, C_V // KV_WIDTH)),
            cache, cache,
        ] + [_any()] * len(aliased),
        out_specs=[pl.BlockSpec((bt, A_WIDTH), lambda s: (r0 + s, M_A // A_WIDTH)), cache, cache],
        out_shape=[jax.ShapeDtypeStruct((ROWS, D_MODEL), BF16), cache_shape, cache_shape],
        input_output_aliases={n_in + k: k for k in range(len(aliased))},
        compiler_params=_cparams("arbitrary"),
        name="swa_sample",
    )(sink, h, h, h, cache_k, cache_v, *aliased)


S5_SHIFTS = (1, 2, 4)


S5_NTAB = 2 * (len(S5_SHIFTS) + 1) * SUB
S5_IN_PIECES = 4
S5_IN_K = S5_WIDTH // S5_IN_PIECES
S5_IN_N = S5_LANES // S5_IN_PIECES
S5_OUT_PIECES = 2
S5_OUT_K = S5_LANES // S5_OUT_PIECES
S5_OUT_N = S5_WIDTH // S5_OUT_PIECES


def _s5_tab(tab_ref, k, lanes):
    return (tab_ref[2 * k * SUB:(2 * k + 1) * SUB, lanes], tab_ref[(2 * k + 1) * SUB:(2 * k + 2) * SUB, lanes])


def _s5_drive(ub, bb_ref, p):
    return _dot(ub[:, p * S5_IN_K:(p + 1) * S5_IN_K], bb_ref[p])


def _s5_readout(hre, him, u, cre_ref, cim_ref, d_ref, wg_ref, bg_ref):
    ys = []
    for j in range(S5_OUT_PIECES):
        lanes = slice(j * S5_OUT_K, (j + 1) * S5_OUT_K)
        ys.append(_dot(hre[:, lanes].astype(BF16), cre_ref[j]) - _dot(him[:, lanes].astype(BF16), cim_ref[j]))
    y = jnp.concatenate(ys, axis=1) + d_ref[...] * u
    z = jax.nn.gelu(y)
    gate = jax.nn.sigmoid(_dot(z.astype(BF16), wg_ref[...]) + bg_ref[...])
    return (z * gate).astype(BF16)


def _s5_prompt_kernel(u_ref, bb_ref, cre_ref, cim_ref, d_ref, wg_ref, bg_ref, tab_ref,
                      o_ref, hpre_ref, hpim_ref, hre, him):
    i = pl.program_id(0)

    @pl.when(i == 0)
    def _():
        hpre_ref[...] = jnp.zeros_like(hpre_ref)
        hpim_ref[...] = jnp.zeros_like(hpim_ref)

    u = u_ref[...]
    ub = u.astype(BF16)
    for p in range(S5_IN_PIECES):
        lanes = slice(p * S5_IN_N, (p + 1) * S5_IN_N)
        bu = _s5_drive(ub, bb_ref, p)
        cr, ci = hpre_ref[:, lanes], hpim_ref[:, lanes]
        pr, pi = _s5_tab(tab_ref, len(S5_SHIFTS), lanes)
        for j in range(BLK // SUB):
            rows = slice(j * SUB, (j + 1) * SUB)
            xr, xi = bu[rows, 0:S5_IN_N], bu[rows, S5_IN_N:]
            for k, d in enumerate(S5_SHIFTS):
                ar, ai = _s5_tab(tab_ref, k, lanes)
                sr, si = pltpu.roll(xr, d, 0), pltpu.roll(xi, d, 0)
                xr, xi = xr + ar * sr - ai * si, xi + ar * si + ai * sr
            xr, xi = xr + pr * cr - pi * ci, xi + pr * ci + pi * cr
            hre[rows, lanes] = xr
            him[rows, lanes] = xi
            cr = jnp.broadcast_to(xr[SUB - 1:SUB, :], (SUB, S5_IN_N))
            ci = jnp.broadcast_to(xi[SUB - 1:SUB, :], (SUB, S5_IN_N))
        hpre_ref[:, lanes] = cr
        hpim_ref[:, lanes] = ci
    o_ref[...] = _s5_readout(hre, him, u, cre_ref, cim_ref, d_ref, wg_ref, bg_ref)


def _s5_sample_kernel(u_ref, bb_ref, cre_ref, cim_ref, d_ref, wg_ref, bg_ref, tab_ref, h0re_ref, h0im_ref,
                      o_ref, hsre_ref, hsim_ref):
    u = u_ref[...]
    ub = u.astype(BF16)
    for p in range(S5_IN_PIECES):
        lanes = slice(p * S5_IN_N, (p + 1) * S5_IN_N)
        bu = _s5_drive(ub, bb_ref, p)
        pr, pi = _s5_tab(tab_ref, len(S5_SHIFTS), lanes)
        ar, ai = pr[0:1, :], pi[0:1, :]
        h0r, h0i = h0re_ref[:, lanes], h0im_ref[:, lanes]
        hsre_ref[:, lanes] = ar * h0r - ai * h0i + bu[:, 0:S5_IN_N]
        hsim_ref[:, lanes] = ar * h0i + ai * h0r + bu[:, S5_IN_N:]
    o_ref[...] = _s5_readout(hsre_ref, hsim_ref, u, cre_ref, cim_ref, d_ref, wg_ref, bg_ref)


def _s5_param_specs():
    const = lambda shape: pl.BlockSpec(shape, lambda i: (0,) * len(shape))
    return [
        const((S5_IN_PIECES, S5_IN_K, 2 * S5_IN_N)),
        const((S5_OUT_PIECES, S5_OUT_K, S5_OUT_N)), const((S5_OUT_PIECES, S5_OUT_K, S5_OUT_N)),
        const((1, S5_WIDTH)), const((S5_WIDTH, S5_WIDTH)), const((1, S5_WIDTH)),
        const((S5_NTAB, S5_LANES)),
    ]


def _s5_params(prm):
    return (prm["bb"], prm["c_re"], prm["c_im"], prm["d"], prm["w_glu"], prm["b_glu"], prm["tab"])


def _s5_prompt(h, prm, mix):
    const = lambda shape: pl.BlockSpec(shape, lambda i: (0,) * len(shape))
    n_in = 8
    return pl.pallas_call(
        _drop_inputs(_s5_prompt_kernel, n_in, 1),
        grid=(N_PBLK,),
        in_specs=[pl.BlockSpec((BLK, S5_WIDTH), lambda i: (i, C_U // S5_WIDTH))] + _s5_param_specs() + [_any()],
        out_specs=[
            pl.BlockSpec((BLK, S5_WIDTH), lambda i: (i, M_B // S5_WIDTH)),
            const((SUB, S5_LANES)), const((SUB, S5_LANES)),
        ],
        out_shape=[
            jax.ShapeDtypeStruct((ROWS, D_MODEL), BF16),
            jax.ShapeDtypeStruct((SUB, S5_LANES), F32), jax.ShapeDtypeStruct((SUB, S5_LANES), F32),
        ],
        scratch_shapes=[pltpu.VMEM((BLK, S5_LANES), F32), pltpu.VMEM((BLK, S5_LANES), F32)],
        input_output_aliases={n_in: 0},
        compiler_params=_cparams("arbitrary"),
        name="s5_prompt",
    )(h, *_s5_params(prm), mix)


def _s5_sample(h, prm, h0re, h0im, mix, l):
    const = lambda shape: pl.BlockSpec(shape, lambda i: (0,) * len(shape))
    st_in = pl.BlockSpec((None, N_SAMPLE, S5_LANES), lambda i: (l, 0, 0))
    n_in = 10
    return pl.pallas_call(
        _drop_inputs(_s5_sample_kernel, n_in, 1),
        grid=(1,),
        in_specs=([pl.BlockSpec((N_SAMPLE, S5_WIDTH), lambda i: (N_PBLK, C_U // S5_WIDTH))] + _s5_param_specs()
                  + [st_in, st_in, _any()]),
        out_specs=[
            pl.BlockSpec((N_SAMPLE, S5_WIDTH), lambda i: (N_PBLK, M_B // S5_WIDTH)),
            const((N_SAMPLE, S5_LANES)), const((N_SAMPLE, S5_LANES)),
        ],
        out_shape=[
            jax.ShapeDtypeStruct((ROWS, D_MODEL), BF16),
            jax.ShapeDtypeStruct((N_SAMPLE, S5_LANES), F32), jax.ShapeDtypeStruct((N_SAMPLE, S5_LANES), F32),
        ],
        input_output_aliases={n_in: 0},
        compiler_params=_cparams("arbitrary"),
        name="s5_sample",
    )(h, *_s5_params(prm), h0re, h0im, mix)


def _gla_log_gate(ac, wa_ref, ba_ref):
    return _log_sigmoid(_dot(ac.astype(BF16), wa_ref[...]) + ba_ref[...]) * (1.0 / GLA_TAU)


def _gla_finish(o_heads, gate, ng):
    outs = [o * lax.rsqrt(jnp.mean(o * o, -1, keepdims=True) + LN_EPS) for o in o_heads]
    return jnp.concatenate(outs, axis=1) * ng * (gate * jax.nn.sigmoid(gate))


def _gla_prompt_kernel(q_ref, k_ref, v_ref, gate_ref, ac_ref, wa_ref, ba_ref, ng_ref, o_ref, s_ref):
    i = pl.program_id(0)

    @pl.when(i == 0)
    def _():
        s_ref[...] = jnp.zeros_like(s_ref)

    lg = _gla_log_gate(ac_ref[...], wa_ref, ba_ref)
    L = GLA_CHUNK
    row = lax.broadcasted_iota(jnp.int32, (L, GLA_KW), 0)
    causal = lax.broadcasted_iota(jnp.int32, (L, L), 0) >= lax.broadcasted_iota(jnp.int32, (L, L), 1)
    for c in range(BLK // L):
        rows = slice(c * L, (c + 1) * L)
        b = lg[rows, :]
        d = 1
        while d < L:
            b = b + jnp.where(row >= d, pltpu.roll(b, d, 0), 0.0)
            d *= 2
        b_end = b[L - 1:L, :]
        kc = k_ref[rows, :]
        q_in = (q_ref[rows, :] * GLA_DK ** -0.5 * jnp.exp(b)).astype(BF16)
        k_out = (kc * jnp.exp(-b)).astype(BF16)
        k_dec = (kc * jnp.exp(b_end - b)).astype(BF16)
        dec = jnp.transpose(jnp.broadcast_to(jnp.exp(b_end), (GLA_DV, GLA_KW)))
        vb = v_ref[rows, :].astype(BF16)
        o_heads = []
        for hh in range(GLA_HEADS):
            ks = slice(hh * GLA_DK, (hh + 1) * GLA_DK)
            vh = vb[:, hh * GLA_DV:(hh + 1) * GLA_DV]
            att = jnp.where(causal, _dot_nt(q_in[:, ks], k_out[:, ks]), 0.0)
            s_old = s_ref[ks, :]
            o_heads.append(_dot(q_in[:, ks], s_old.astype(BF16)) + _dot(att.astype(BF16), vh))
            s_ref[ks, :] = dec[ks, :] * s_old + _dot_tn(k_dec[:, ks], vh)
        o_ref[rows, :] = _gla_finish(o_heads, gate_ref[rows, :], ng_ref[...]).astype(BF16)


def _gla_prompt(h, wa, ba, ng, mix):
    const = lambda shape: pl.BlockSpec(shape, lambda i: (0,) * len(shape))
    n_in = 8
    return pl.pallas_call(
        _drop_inputs(_gla_prompt_kernel, n_in, 1),
        grid=(N_PBLK,),
        in_specs=[
            pl.BlockSpec((BLK, GLA_KW), lambda i: (i, C_GQ // GLA_KW)),
            pl.BlockSpec((BLK, GLA_KW), lambda i: (i, C_GK // GLA_KW)),
            pl.BlockSpec((BLK, GLA_WIDTH), lambda i: (i, C_GV // GLA_WIDTH)),
            pl.BlockSpec((BLK, GLA_WIDTH), lambda i: (i, C_GG // GLA_WIDTH)),
            pl.BlockSpec((BLK, 128), lambda i: (i, C_GA // 128)),
            const((128, GLA_KW)), const((1, GLA_KW)), const((1, GLA_WIDTH)),
            _any(),
        ],
        out_specs=[pl.BlockSpec((BLK, GLA_WIDTH), lambda i: (i, M_C // GLA_WIDTH)), const((GLA_KW, GLA_DV))],
        out_shape=[jax.ShapeDtypeStruct((ROWS, D_MODEL), BF16), jax.ShapeDtypeStruct((GLA_KW, GLA_DV), F32)],
        input_output_aliases={n_in: 0},
        compiler_params=_cparams("arbitrary"),
        name="gla_prompt",
    )(h, h, h, h, h, wa, ba, ng, mix)


GLA_BT = 16


def _gla_sample_kernel(q_ref, k_ref, v_ref, gate_ref, ac_ref, wa_ref, ba_ref, ng_ref, s_ref, o_ref, so_ref):
    bt = GLA_BT
    lg = _gla_log_gate(ac_ref[...], wa_ref, ba_ref)
    v = v_ref[...]

    def cols(x):
        return jnp.transpose(jnp.concatenate([x, jnp.zeros((128 - bt, GLA_KW), F32)], axis=0))

    eg_t, k_t, q_t = cols(jnp.exp(lg)), cols(k_ref[...]), cols(q_ref[...] * GLA_DK ** -0.5)
    o_rows = []
    for j in range(bt):
        bc = lambda t: jnp.broadcast_to(t[:, j:j + 1], (GLA_KW, GLA_DV))
        vrow = jnp.concatenate(
            [jnp.broadcast_to(v[j:j + 1, hh * GLA_DV:(hh + 1) * GLA_DV], (GLA_DK, GLA_DV)) for hh in range(GLA_HEADS)],
            axis=0)
        s_new = bc(eg_t) * s_ref[j] + bc(k_t) * vrow
        so_ref[j] = s_new
        qs = bc(q_t) * s_new
        o_rows.append(jnp.concatenate(
            [jnp.sum(qs[hh * GLA_DK:(hh + 1) * GLA_DK, :], axis=0, keepdims=True) for hh in range(GLA_HEADS)], axis=1))
    o = jnp.concatenate(o_rows, axis=0)
    o_heads = [o[:, hh * GLA_DV:(hh + 1) * GLA_DV] for hh in range(GLA_HEADS)]
    o_ref[...] = _gla_finish(o_heads, gate_ref[...], ng_ref[...]).astype(BF16)


def _gla_sample(h, wa, ba, ng, state, mix, s_all, l):
    bt = GLA_BT
    r0 = ROWS_P // bt
    const = lambda shape: pl.BlockSpec(shape, lambda s: (0,) * len(shape))
    st = pl.BlockSpec((None, bt, GLA_KW, GLA_DV), lambda s: (l, s, 0, 0))
    n_in = 9
    aliased = [mix] if s_all is None else [mix, s_all]
    return pl.pallas_call(
        _drop_inputs(_gla_sample_kernel, n_in, len(aliased)),
        grid=(N_SAMPLE // bt,),
        in_specs=[
            pl.BlockSpec((bt, GLA_KW), lambda s: (r0 + s, C_GQ // GLA_KW)),
            pl.BlockSpec((bt, GLA_KW), lambda s: (r0 + s, C_GK // GLA_KW)),
            pl.BlockSpec((bt, GLA_WIDTH), lambda s: (r0 + s, C_GV // GLA_WIDTH)),
            pl.BlockSpec((bt, GLA_WIDTH), lambda s: (r0 + s, C_GG // GLA_WIDTH)),
            pl.BlockSpec((bt, 128), lambda s: (r0 + s, C_GA // 128)),
            const((128, GLA_KW)), const((1, GLA_KW)), const((1, GLA_WIDTH)),
            st,
        ] + [_any()] * len(aliased),
        out_specs=[pl.BlockSpec((bt, GLA_WIDTH), lambda s: (r0 + s, M_C // GLA_WIDTH)), st],
        out_shape=[jax.ShapeDtypeStruct((ROWS, D_MODEL), BF16),
                   jax.ShapeDtypeStruct((DEPTH, N_SAMPLE, GLA_KW, GLA_DV), F32)],
        input_output_aliases={n_in + k: k for k in range(len(aliased))},
        compiler_params=_cparams("arbitrary"),
        name="gla_sample",
    )(h, h, h, h, h, wa, ba, ng, state, *aliased)


OUT_TM = 768


def _out_proj_kernel(mix_ref, w_ref, x_ref, g_ref, b_ref, o_ref):
    i = pl.program_id(0)
    o_ref[...] = _dot(mix_ref[...], w_ref[...])
    _deepnorm_ln(o_ref, x_ref, g_ref, b_ref, o_ref, i * OUT_TM, OUT_TM)


def _out_proj(mix, w, x, g, b, l):
    tm = OUT_TM
    const = lambda shape: pl.BlockSpec(shape, lambda i: (0,) * len(shape))
    return pl.pallas_call(
        _out_proj_kernel,
        grid=(ROWS // tm,),
        in_specs=[
            pl.BlockSpec((tm, D_MODEL), lambda i: (i, 0)),
            pl.BlockSpec((None, D_MODEL, D_MODEL), lambda i: (l, 0, 0), pipeline_mode=pl.Buffered(1)),
            pl.BlockSpec((tm, D_MODEL), lambda i: (i, 0)),
            const((1, D_MODEL)), const((1, D_MODEL)),
        ],
        out_specs=pl.BlockSpec((tm, D_MODEL), lambda i: (i, 0)),
        out_shape=jax.ShapeDtypeStruct((ROWS, D_MODEL), F32),
        compiler_params=_cparams("arbitrary"),
        name="out_proj",
    )(mix, w, x, g, b)


FFN_TM = 768
FFN_TF = 512
FFN_HALO = 16
assert (ROWS - ROWS_P) == N_SAMPLE and ROWS % FFN_TM == 0 and FFN_TM >= N_SAMPLE + SUB


def _ffn_kernel(x_ref, xh_ref, wu_ref, wg_ref, cw_ref, cb_ref, wd_ref, p0_ref, p1_ref, g_ref, b_ref,
                o_ref, ut_ref, us_ref, xb_ref, a_ref):
    i, f = pl.program_id(0), pl.program_id(1)
    tm, hl = FFN_TM, FFN_HALO
    last_i = pl.num_programs(0) - 1

    @pl.when(f == 0)
    def _():
        xb_ref[...] = x_ref[...].astype(BF16)
        o_ref[...] = jnp.zeros_like(o_ref)

    xb = xb_ref[...]
    u = _dot(xb, wu_ref[...])
    gte = _dot(xb, wg_ref[...])
    cw0, cw1, cw2, cb = cw_ref[0:1, :], cw_ref[1:2, :], cw_ref[2:3, :], cb_ref[...]

    def act(um2, um1, u0, g0):
        return (jax.nn.gelu(cb + cw0 * um2 + cw1 * um1 + cw2 * u0) * g0).astype(BF16)

    a_ref[...] = act(pltpu.roll(u, 2, 0), pltpu.roll(u, 1, 0), u, gte)
    xh = jnp.where(i > 0, xh_ref[...], 0.0).astype(BF16)
    ext = jnp.concatenate([_dot(xh, wu_ref[...]), u[0:hl, :]], axis=0)
    a_ref[0:hl, :] = act(ext[hl - 2:2 * hl - 2, :], ext[hl - 1:2 * hl - 1, :], u[0:hl, :], gte[0:hl, :])

    @pl.when(i == last_i)
    def _():
        a_ref[tm - N_SAMPLE:, :] = act(p0_ref[...], p1_ref[...], u[tm - N_SAMPLE:, :], gte[tm - N_SAMPLE:, :])

    o_ref[...] += _dot(a_ref[...], wd_ref[...])
    ut_ref[...] = u[tm - N_SAMPLE - SUB:tm - N_SAMPLE, :]
    us_ref[0] = p1_ref[...]
    us_ref[1] = u[tm - N_SAMPLE:, :]

    @pl.when(f == pl.num_programs(1) - 1)
    def _():
        _deepnorm_ln(o_ref, x_ref, g_ref, b_ref, o_ref, i * tm, tm)


def _ffn(x, wup, cw, cb, wd, conv_p0, conv_p1, g, b, l):
    tm, tf = FFN_TM, FFN_TF
    nf = D_FF // tf
    const = lambda shape: pl.BlockSpec(shape, lambda i, f: (0,) * len(shape))
    tail = lambda i, f: jnp.where(i == ROWS // tm - 1, f, 0)
    return pl.pallas_call(
        _ffn_kernel,
        grid=(ROWS // tm, nf),
        in_specs=[
            pl.BlockSpec((tm, D_MODEL), lambda i, f: (i, 0)),
            pl.BlockSpec((FFN_HALO, D_MODEL), lambda i, f: (jnp.maximum(i * (tm // FFN_HALO) - 1, 0), 0)),
            pl.BlockSpec((None, D_MODEL, tf), lambda i, f: (l, 0, f)),
            pl.BlockSpec((None, D_MODEL, tf), lambda i, f: (l, 0, nf + f)),
            pl.BlockSpec((None, CONV_W, tf), lambda i, f: (l, 0, f)),
            pl.BlockSpec((None, 1, tf), lambda i, f: (l, 0, f)),
            pl.BlockSpec((None, tf, D_MODEL), lambda i, f: (l, f, 0)),
            pl.BlockSpec((None, N_SAMPLE, tf), lambda i, f: (l, 0, tail(i, f))),
            pl.BlockSpec((None, N_SAMPLE, tf), lambda i, f: (l, 0, tail(i, f))),
            const((1, D_MODEL)), const((1, D_MODEL)),
        ],
        out_specs=[
            pl.BlockSpec((tm, D_MODEL), lambda i, f: (i, 0)),
            pl.BlockSpec((SUB, tf), lambda i, f: (0, tail(i, f))),
            pl.BlockSpec((2, N_SAMPLE, tf), lambda i, f: (0, 0, tail(i, f))),
        ],
        out_shape=[
            jax.ShapeDtypeStruct((ROWS, D_MODEL), F32),
            jax.ShapeDtypeStruct((SUB, D_FF), F32),
            jax.ShapeDtypeStruct((2, N_SAMPLE, D_FF), F32),
        ],
        scratch_shapes=[pltpu.VMEM((tm, D_MODEL), BF16), pltpu.VMEM((tm, tf), BF16)],
        compiler_params=_cparams("arbitrary", "arbitrary"),
        name="conv_ffn",
    )(x, x, wup, wup, cw, cb, wd, conv_p0, conv_p1, g, b)


def _rope_tables():
    half = ROPE_DIM // 2
    inv = jnp.power(ROPE_THETA, -jnp.arange(half, dtype=F32) / half)
    pos = jnp.concatenate([jnp.arange(ROWS_P, dtype=jnp.int32) - LEAD,
                           jnp.full((N_SAMPLE,), PAST_LEN, jnp.int32)]).astype(F32)
    ang = pos[:, None] * inv[None, :]
    cos, sin = jnp.cos(ang), jnp.sin(ang)
    one = jnp.ones((ROWS, HEAD_DIM - ROPE_DIM), F32)
    zero = jnp.zeros((ROWS, HEAD_DIM - ROPE_DIM), F32)
    zh = jnp.zeros((ROWS, half), F32)
    cos_t = jnp.concatenate([cos, cos, one], 1)
    sa_t = jnp.concatenate([-sin, zh, zero], 1)
    sb_t = jnp.concatenate([zh, sin, zero], 1)
    rep = lambda t: jnp.concatenate([t, t], 1)
    return rep(cos_t), rep(sa_t), rep(sb_t)


def _s5_tables(lam_re, lam_im, log_dt, b_re, b_im, c_re, c_im, d, w_glu, b_glu):
    dt = jnp.exp(log_dt)[:, None]
    mag = jnp.exp(lam_re * dt)
    ab_re, ab_im = mag * jnp.cos(lam_im * dt), mag * jnp.sin(lam_im * dt)
    den = lam_re * lam_re + lam_im * lam_im
    nr, ni = ab_re - 1.0, ab_im
    f_re = (nr * lam_re + ni * lam_im) / den
    f_im = (ni * lam_re - nr * lam_im) / den
    bb_re = f_re[..., None] * b_re - f_im[..., None] * b_im
    bb_im = f_re[..., None] * b_im + f_im[..., None] * b_re
    gi, go = S5_GROUPS // S5_IN_PIECES, S5_GROUPS // S5_OUT_PIECES

    def bd_in(t):
        t = t.reshape(S5_IN_PIECES, gi, S5_STATE, S5_CH)
        return jnp.einsum('qgph,gk->qghkp', t, jnp.eye(gi, dtype=F32)).reshape(S5_IN_PIECES, S5_IN_K, S5_IN_N)

    def bd_out(t):
        t = t.reshape(S5_OUT_PIECES, go, S5_CH, S5_STATE)
        return jnp.einsum('qghp,gk->qgpkh', t, jnp.eye(go, dtype=F32)).reshape(S5_OUT_PIECES, S5_OUT_K, S5_OUT_N)
    ar, ai = ab_re.reshape(1, S5_LANES), ab_im.reshape(1, S5_LANES)
    pows = [(ar, ai)]
    for _ in range(SUB - 1):
        pr, pi = pows[-1]
        pows.append((pr * ar - pi * ai, pr * ai + pi * ar))
    row = jnp.arange(SUB)[:, None]
    parts = []
    for sft in S5_SHIFTS:
        pr, pi = pows[sft - 1]
        parts += [jnp.where(row >= sft, pr, 0.0), jnp.where(row >= sft, pi, 0.0)]
    parts += [jnp.concatenate([p[0] for p in pows], 0), jnp.concatenate([p[1] for p in pows], 0)]
    return dict(bb=jnp.concatenate([bd_in(bb_re), bd_in(bb_im)], axis=2).astype(BF16),
                c_re=bd_out(c_re).astype(BF16), c_im=bd_out(c_im).astype(BF16),
                d=d.reshape(1, S5_WIDTH), w_glu=w_glu.astype(BF16), b_glu=b_glu.reshape(1, S5_WIDTH),
                tab=jnp.concatenate(parts, 0))


@jax.jit
def kernel(x_prompt, x_sample, cache_swa_k, cache_swa_v, state_ssm_re, state_ssm_im, state_gla, state_conv,
           meta_tokens, ln_in_g, ln_in_b, w_in, attn_sink, s5_lam_re, s5_lam_im, s5_log_dt, s5_b_re, s5_b_im,
           s5_c_re, s5_c_im, s5_d, s5_w_glu, s5_b_glu, gla_w_a2, gla_b_a, gla_norm_g, w_out, ln1_g, ln1_b,
           ffn_w_up, ffn_conv_w, ffn_conv_b, ffn_w_down, ln2_g, ln2_b):
    L = DEPTH
    row2 = lambda t: t.reshape(1, -1)
    w_in_b = w_in.astype(BF16)
    w_out_b = w_out.astype(BF16)
    w_up_b = ffn_w_up.astype(BF16)
    w_down_b = ffn_w_down.astype(BF16)
    wa_b = jnp.pad(gla_w_a2, ((0, 0), (0, 128 - GLA_LOWRANK), (0, 0))).astype(BF16)
    conv_b3 = ffn_conv_b.reshape(L, 1, D_FF)
    cos_t, sa_t, sb_t = _rope_tables()
    ck = cache_swa_k.reshape(L, N_SAMPLE, WINDOW, KV_WIDTH)
    cv = cache_swa_v.reshape(L, N_SAMPLE, WINDOW, KV_WIDTH)
    h0re = state_ssm_re.reshape(L, N_SAMPLE, S5_LANES)
    h0im = state_ssm_im.reshape(L, N_SAMPLE, S5_LANES)
    sg = state_gla.reshape(L, N_SAMPLE, GLA_KW, GLA_DV)
    conv_p0, conv_p1 = state_conv[:, :, 0, :], state_conv[:, :, 1, :]

    x = _ln_in(x_prompt.reshape(SEQ, D_MODEL), meta_tokens, x_sample.reshape(N_SAMPLE, D_MODEL),
               row2(ln_in_g), row2(ln_in_b))
    outs = [[] for _ in range(9)]
    ck_all = cv_all = s_all = None
    for l in range(L):
        h = _in_proj(x, w_in_b, cos_t, sa_t, sb_t, l)
        mix = _swa_prompt(attn_sink[l], h)
        mix, ck_all, cv_all = _swa_sample(attn_sink[l], h, ck, cv, mix, ck_all, cv_all, l)
        s5p = _s5_tables(s5_lam_re[l], s5_lam_im[l], s5_log_dt[l], s5_b_re[l], s5_b_im[l], s5_c_re[l], s5_c_im[l],
                         s5_d[l], s5_w_glu[l], s5_b_glu[l])
        mix, hp_re, hp_im = _s5_prompt(h, s5p, mix)
        mix, hs_re, hs_im = _s5_sample(h, s5p, h0re, h0im, mix, l)
        gla_args = (wa_b[l], row2(gla_b_a[l]), row2(gla_norm_g[l]))
        mix, s_p = _gla_prompt(h, *gla_args, mix)
        mix, s_all = _gla_sample(h, *gla_args, sg, mix, s_all, l)
        x = _out_proj(mix, w_out_b, x, row2(ln1_g[l]), row2(ln1_b[l]), l)
        x, u_tail, u_s = _ffn(x, w_up_b, ffn_conv_w, conv_b3, w_down_b, conv_p0, conv_p1,
                              row2(ln2_g[l]), row2(ln2_b[l]), l)
        kv_p = lambda c: h[ROWS_P - WINDOW:ROWS_P, c:c + KV_WIDTH].reshape(1, WINDOW, A_KV_HEADS, HEAD_DIM)
        new = (kv_p(C_K), kv_p(C_V),
               hp_re[SUB - 1].reshape(1, S5_GROUPS, S5_STATE), hp_im[SUB - 1].reshape(1, S5_GROUPS, S5_STATE),
               s_p.reshape(1, GLA_HEADS, GLA_DK, GLA_DV), u_tail[SUB - 2:SUB][None],
               hs_re.reshape(N_SAMPLE, S5_GROUPS, S5_STATE), hs_im.reshape(N_SAMPLE, S5_GROUPS, S5_STATE),
               jnp.transpose(u_s, (1, 0, 2)))
        for lst, val in zip(outs, new):
            lst.append(val)
    y_prompt = x[BLK:ROWS_P].reshape(1, SEQ, D_MODEL)
    y_sample = x[ROWS_P:].reshape(N_SAMPLE, 1, D_MODEL)
    st = [jnp.stack(lst) for lst in outs]
    return (y_prompt, y_sample, st[0], st[1], st[2], st[3], st[4], st[5],
            ck_all.reshape(L, N_SAMPLE, WINDOW, A_KV_HEADS, HEAD_DIM),
            cv_all.reshape(L, N_SAMPLE, WINDOW, A_KV_HEADS, HEAD_DIM),
            st[6], st[7], s_all.reshape(L, N_SAMPLE, GLA_HEADS, GLA_DK, GLA_DV), st[8])
```

```python
import functools

import jax
import jax.numpy as jnp
from jax import lax
from jax.experimental import pallas as pl
from jax.experimental.pallas import tpu as pltpu

F32 = jnp.float32
BF16 = jnp.bfloat16

D_MODEL = 2048
SEQ = 8192
DEPTH = 2
N_SAMPLE = 128
N_META = 16
HEAD_DIM = 64
A_WIDTH = 1024
A_HEADS = 16
A_KV_HEADS = 4
GQA = 4
KV_WIDTH = A_KV_HEADS * HEAD_DIM
WINDOW = 128
ROPE_DIM = 16
ROPE_HALF = ROPE_DIM // 2
ROPE_THETA = 500000.0
PAST_LEN = 8192
S5_WIDTH = 512
S5_CH = 16
S5_GROUPS = 32
S5_STATE = 64
S5_LANES = S5_GROUPS * S5_STATE
GLA_WIDTH = 512
GLA_HEADS = 4
GLA_DV = 128
GLA_DK = 64
GLA_KW = GLA_HEADS * GLA_DK
GLA_LOWRANK = 16
GLA_TAU = 16.0
GLA_CHUNK = 64
D_FF = 5632
CONV_W = 3
LN_EPS = 1e-5
ALPHA = (2 * DEPTH) ** 0.25
NEG_INF = -1e30

BLK = 128
T_PROMPT = N_META + SEQ
LEAD = (-T_PROMPT) % BLK
N_PBLK = (LEAD + T_PROMPT) // BLK
ROWS_P = N_PBLK * BLK
ROWS = ROWS_P + N_SAMPLE
META_ROW = SEQ + LEAD
SUB = 8

C_Q, C_K, C_V, C_U, C_GQ, C_GK, C_GV, C_GG, C_GA = 0, 1024, 1280, 1536, 2048, 2304, 2560, 3072, 3584
IN_COLS = 3600
IN_TILE = 1280
IN_PAD = 3 * IN_TILE
ROPE_COLS = A_WIDTH + KV_WIDTH
M_A, M_B, M_C = 0, A_WIDTH, A_WIDTH + S5_WIDTH

VMEM_LIMIT = 56 * 1024 * 1024


def _cparams(*sem):
    return pltpu.CompilerParams(dimension_semantics=sem, vmem_limit_bytes=VMEM_LIMIT)


def _any():
    return pl.BlockSpec(memory_space=pl.ANY)


def _drop_inputs(kern, start, count):
    def wrapped(*refs):
        return kern(*refs[:start], *refs[start + count:])
    return wrapped


def _seq_block(s):
    return jnp.where(s == 0, N_PBLK - 1, s - 1)


def _layer_spec(shape, l):
    return pl.BlockSpec((None,) + shape, lambda *_: (l,) + (0,) * len(shape))


def _layer_norm_rows(x, g, b):
    mu = jnp.mean(x, -1, keepdims=True)
    xc = x - mu
    var = jnp.mean(xc * xc, -1, keepdims=True)
    return xc * lax.rsqrt(var + LN_EPS) * g + b


def _zero_pad_rows(y, first_row):
    row = first_row + lax.broadcasted_iota(jnp.int32, y.shape, 0)
    return jnp.where((row >= SEQ) & (row < META_ROW), 0.0, y)


def _deepnorm_ln(acc_ref, x_ref, g_ref, b_ref, o_ref, first_row, rows):
    g, b = g_ref[...], b_ref[...]
    for c in range(rows // BLK):
        r = slice(c * BLK, (c + 1) * BLK)
        y = _layer_norm_rows(ALPHA * x_ref[r, :] + acc_ref[r, :], g, b)
        o_ref[r, :] = _zero_pad_rows(y, first_row + c * BLK)


def _dot(a, b):
    return jnp.dot(a, b, preferred_element_type=F32)


def _dot_nt(a, b):
    return lax.dot_general(a, b, (((1,), (1,)), ((), ())), preferred_element_type=F32)


def _dot_tn(a, b):
    return lax.dot_general(a, b, (((0,), (0,)), ((), ())), preferred_element_type=F32)


def _log_sigmoid(x):
    return jnp.minimum(x, 0.0) - jnp.log(1.0 + jnp.exp(-jnp.abs(x)))


LN_IN_TM = 256
assert ROWS - SEQ == LN_IN_TM and SEQ % LN_IN_TM == 0


def _ln_in_kernel(xp_ref, meta_ref, xs_ref, g_ref, b_ref, o_ref):
    i = pl.program_id(0)
    g, b = g_ref[...], b_ref[...]

    @pl.when(i < SEQ // LN_IN_TM)
    def _():
        o_ref[...] = _layer_norm_rows(xp_ref[...], g, b)

    @pl.when(i == SEQ // LN_IN_TM)
    def _():
        o_ref[0:LEAD, :] = jnp.zeros((LEAD, D_MODEL), F32)
        o_ref[LEAD:BLK, :] = _layer_norm_rows(meta_ref[...], g, b)
        o_ref[BLK:, :] = _layer_norm_rows(xs_ref[...], g, b)


def _ln_in(xp, meta, xs, g, b):
    tm = LN_IN_TM
    const = lambda shape: pl.BlockSpec(shape, lambda i: (0,) * len(shape))
    return pl.pallas_call(
        _ln_in_kernel,
        grid=(ROWS // tm,),
        in_specs=[
            pl.BlockSpec((tm, D_MODEL), lambda i: (jnp.minimum(i, SEQ // tm - 1), 0)),
            const((N_META, D_MODEL)), const((N_SAMPLE, D_MODEL)), const((1, D_MODEL)), const((1, D_MODEL)),
        ],
        out_specs=pl.BlockSpec((tm, D_MODEL), lambda i: (i, 0)),
        out_shape=jax.ShapeDtypeStruct((ROWS, D_MODEL), F32),
        compiler_params=_cparams("arbitrary"),
        name="ln_in",
    )(xp, meta, xs, g, b)


IN_TM = 1056


def _in_proj_kernel(x_ref, w_ref, cs_ref, o_ref, xb_ref):
    j = pl.program_id(1)

    @pl.when(j == 0)
    def _():
        xb_ref[...] = x_ref[...].astype(BF16)

    o_ref[...] = _dot(xb_ref[...], w_ref[...])

    @pl.when(j == 0)
    def _():
        c8, s8 = cs_ref[:, 0:ROPE_HALF], cs_ref[:, ROPE_HALF:ROPE_DIM]
        rest = HEAD_DIM - ROPE_DIM
        one, zero = jnp.ones((IN_TM, rest), F32), jnp.zeros((IN_TM, rest), F32)
        z8 = jnp.zeros((IN_TM, ROPE_HALF), F32)
        cos = jnp.concatenate([c8, c8, one] * 2, axis=1)
        sa = jnp.concatenate([-s8, z8, zero] * 2, axis=1)
        sb = jnp.concatenate([z8, s8, zero] * 2, axis=1)
        for c in range(ROPE_COLS // 128):
            blk = o_ref[:, c * 128:(c + 1) * 128]
            o_ref[:, c * 128:(c + 1) * 128] = (
                blk * cos + pltpu.roll(blk, 128 - ROPE_HALF, 1) * sa + pltpu.roll(blk, ROPE_HALF, 1) * sb)

    @pl.when(j == IN_PAD // IN_TILE - 1)
    def _():
        edge = (IN_COLS % IN_TILE) // 128 * 128
        col = edge + lax.broadcasted_iota(jnp.int32, (IN_TM, 128), 1)
        o_ref[:, edge:edge + 128] = jnp.where(col < IN_COLS % IN_TILE, o_ref[:, edge:edge + 128], 0.0)
        o_ref[:, edge + 128:] = jnp.zeros((IN_TM, IN_TILE - edge - 128), F32)


def _in_proj(x, w, cs, l):
    return pl.pallas_call(
        _in_proj_kernel,
        grid=(ROWS // IN_TM, IN_PAD // IN_TILE),
        in_specs=[
            pl.BlockSpec((IN_TM, D_MODEL), lambda i, j: (i, 0)),
            pl.BlockSpec((None, D_MODEL, IN_TILE), lambda i, j: (l, 0, j)),
            pl.BlockSpec((IN_TM, ROPE_DIM), lambda i, j: (i, 0)),
        ],
        out_specs=pl.BlockSpec((IN_TM, IN_TILE), lambda i, j: (i, j)),
        out_shape=jax.ShapeDtypeStruct((ROWS, IN_PAD), F32),
        scratch_shapes=[pltpu.VMEM((IN_TM, D_MODEL), BF16)],
        compiler_params=_cparams("arbitrary", "arbitrary"),
        name="in_proj",
    )(x, w, cs)


def _swa_prompt_kernel(l, sink_ref, q_ref, kp_ref, kc_ref, vp_ref, vc_ref, o_ref):
    s_blk = pl.program_id(0)
    k2 = jnp.concatenate([kp_ref[...], kc_ref[...]], axis=0).astype(BF16)
    v2 = jnp.concatenate([vp_ref[...], vc_ref[...]], axis=0).astype(BF16)
    row = lax.broadcasted_iota(jnp.int32, (BLK, 2 * BLK), 0)
    col = lax.broadcasted_iota(jnp.int32, (BLK, 2 * BLK), 1)
    diff = row - col + BLK
    kpos = (s_blk - 1) * BLK + col - LEAD
    ok = (diff >= 0) & (diff <= WINDOW) & (kpos >= 0)
    for h in range(A_HEADS):
        kv = h // GQA
        qh = (q_ref[:, h * HEAD_DIM:(h + 1) * HEAD_DIM] * HEAD_DIM ** -0.5).astype(BF16)
        s = _dot_nt(qh, k2[:, kv * HEAD_DIM:(kv + 1) * HEAD_DIM])
        s = jnp.where(ok, s, NEG_INF)
        sk = sink_ref[l, h]
        m = jnp.maximum(jnp.max(s, -1, keepdims=True), sk)
        p = jnp.exp(s - m)
        den = jnp.sum(p, -1, keepdims=True) + jnp.exp(sk - m)
        o = _dot(p.astype(BF16), v2[:, kv * HEAD_DIM:(kv + 1) * HEAD_DIM]) / den
        o_ref[:, h * HEAD_DIM:(h + 1) * HEAD_DIM] = o.astype(BF16)


def _swa_prompt(sink, h, mix, l):
    cur = _seq_block
    prev = lambda s: _seq_block(jnp.maximum(s - 1, 0))
    kvb = KV_WIDTH
    n_in = 6
    return pl.pallas_call(
        _drop_inputs(functools.partial(_swa_prompt_kernel, l), n_in, 1),
        grid=(N_PBLK,),
        in_specs=[
            pl.BlockSpec(memory_space=pltpu.SMEM),
            pl.BlockSpec((BLK, A_WIDTH), lambda s: (cur(s), C_Q // A_WIDTH)),
            pl.BlockSpec((BLK, kvb), lambda s: (prev(s), C_K // kvb)),
            pl.BlockSpec((BLK, kvb), lambda s: (cur(s), C_K // kvb)),
            pl.BlockSpec((BLK, kvb), lambda s: (prev(s), C_V // kvb)),
            pl.BlockSpec((BLK, kvb), lambda s: (cur(s), C_V // kvb)),
            _any(),
        ],
        out_specs=pl.BlockSpec((BLK, A_WIDTH), lambda s: (cur(s), M_A // A_WIDTH)),
        out_shape=jax.ShapeDtypeStruct((ROWS, D_MODEL), BF16),
        input_output_aliases={n_in: 0},
        compiler_params=_cparams("arbitrary"),
        name="swa_prompt",
    )(sink, h, h, h, h, h, mix)


SWA_BT = 16


def _swa_sample_kernel(l, sink_ref, q_ref, kn_ref, vn_ref, ck_ref, cv_ref, o_ref, cko_ref, cvo_ref):
    bt = SWA_BT
    kn, vn = kn_ref[...], vn_ref[...]
    k2 = ck_ref[...].reshape(bt * WINDOW, KV_WIDTH).astype(BF16)
    v2 = cv_ref[...].reshape(bt * WINDOW, KV_WIDTH).astype(BF16)
    row = lax.broadcasted_iota(jnp.int32, (WINDOW, KV_WIDTH), 0)
    for j in range(bt):
        cko_ref[j] = jnp.where(row == WINDOW - 1, kn[j:j + 1, :], pltpu.roll(ck_ref[j], WINDOW - 1, 0))
        cvo_ref[j] = jnp.where(row == WINDOW - 1, vn[j:j + 1, :], pltpu.roll(cv_ref[j], WINDOW - 1, 0))
    nq = GQA * bt
    rowb = lax.broadcasted_iota(jnp.int32, (nq, bt * WINDOW), 0) % bt
    colb = lax.broadcasted_iota(jnp.int32, (nq, bt * WINDOW), 1) // WINDOW
    same = rowb == colb
    rg = lax.broadcasted_iota(jnp.int32, (nq, 1), 0) // bt
    for kv in range(A_KV_HEADS):
        sl = slice(kv * HEAD_DIM, (kv + 1) * HEAD_DIM)
        q4 = jnp.concatenate(
            [q_ref[:, (kv * GQA + g) * HEAD_DIM:(kv * GQA + g + 1) * HEAD_DIM] for g in range(GQA)], axis=0)
        q4 = q4 * HEAD_DIM ** -0.5
        kn4 = jnp.concatenate([kn[:, sl]] * GQA, axis=0)
        vn4 = jnp.concatenate([vn[:, sl]] * GQA, axis=0)
        sk = jnp.zeros((nq, 1), F32)
        for g in range(GQA):
            sk = jnp.where(rg == g, sink_ref[l, kv * GQA + g], sk)
        s = jnp.where(same, _dot_nt(q4.astype(BF16), k2[:, sl]), NEG_INF)
        s_self = jnp.sum(q4 * kn4, -1, keepdims=True)
        m = jnp.maximum(jnp.maximum(jnp.max(s, -1, keepdims=True), s_self), sk)
        p = jnp.exp(s - m)
        p_self = jnp.exp(s_self - m)
        den = jnp.sum(p, -1, keepdims=True) + p_self + jnp.exp(sk - m)
        o = (_dot(p.astype(BF16), v2[:, sl]) + p_self * vn4) / den
        for g in range(GQA):
            hh = kv * GQA + g
            o_ref[:, hh * HEAD_DIM:(hh + 1) * HEAD_DIM] = o[g * bt:(g + 1) * bt].astype(BF16)


def _swa_sample(sink, h, cache_k, cache_v, mix, l):
    bt = SWA_BT
    r0 = ROWS_P // bt
    cache = pl.BlockSpec((None, bt, WINDOW, KV_WIDTH), lambda s: (l, s, 0, 0))
    n_in = 6
    return pl.pallas_call(
        _drop_inputs(functools.partial(_swa_sample_kernel, l), n_in, 1),
        grid=(N_SAMPLE // bt,),
        in_specs=[
            pl.BlockSpec(memory_space=pltpu.SMEM),
            pl.BlockSpec((bt, A_WIDTH), lambda s: (r0 + s, C_Q // A_WIDTH)),
            pl.BlockSpec((bt, KV_WIDTH), lambda s: (r0 + s, C_K // KV_WIDTH)),
            pl.BlockSpec((bt, KV_WIDTH), lambda s: (r0 + s, C_V // KV_WIDTH)),
            cache, cache,
            _any(),
        ],
        out_specs=[pl.BlockSpec((bt, A_WIDTH), lambda s: (r0 + s, M_A // A_WIDTH)), cache, cache],
        out_shape=[jax.ShapeDtypeStruct((ROWS, D_MODEL), BF16),
                   jax.ShapeDtypeStruct(cache_k.shape, F32), jax.ShapeDtypeStruct(cache_v.shape, F32)],
        input_output_aliases={n_in: 0, 4: 1, 5: 2},
        compiler_params=_cparams("arbitrary"),
        name="swa_sample",
    )(sink, h, h, h, cache_k, cache_v, mix)


S5_SHIFTS = (1, 2, 4)
S5_NTAB = 2 * (len(S5_SHIFTS) + 1) * SUB
S5_IN_PIECES = 4
S5_IN_K = S5_WIDTH // S5_IN_PIECES
S5_IN_N = S5_LANES // S5_IN_PIECES
S5_OUT_PIECES = 2
S5_OUT_K = S5_LANES // S5_OUT_PIECES
S5_OUT_N = S5_WIDTH // S5_OUT_PIECES


def _s5_tab(tab_ref, k, lanes):
    return (tab_ref[2 * k * SUB:(2 * k + 1) * SUB, lanes], tab_ref[(2 * k + 1) * SUB:(2 * k + 2) * SUB, lanes])


def _s5_drive(ub, bb_ref, p):
    return _dot(ub[:, p * S5_IN_K:(p + 1) * S5_IN_K], bb_ref[p])


def _s5_readout(hre, him, u, cre_ref, cim_ref, d_ref, wg_ref, bg_ref):
    ys = []
    for j in range(S5_OUT_PIECES):
        lanes = slice(j * S5_OUT_K, (j + 1) * S5_OUT_K)
        ys.append(_dot(hre[:, lanes].astype(BF16), cre_ref[j]) - _dot(him[:, lanes].astype(BF16), cim_ref[j]))
    y = jnp.concatenate(ys, axis=1) + d_ref[...] * u
    z = jax.nn.gelu(y)
    gate = jax.nn.sigmoid(_dot(z.astype(BF16), wg_ref[...]) + bg_ref[...])
    return (z * gate).astype(BF16)


def _s5_prompt_kernel(u_ref, bb_ref, cre_ref, cim_ref, d_ref, wg_ref, bg_ref, tab_ref,
                      o_ref, hpre_ref, hpim_ref, hre, him):
    i = pl.program_id(0)

    @pl.when(i == 0)
    def _():
        hpre_ref[...] = jnp.zeros_like(hpre_ref)
        hpim_ref[...] = jnp.zeros_like(hpim_ref)

    u = u_ref[...]
    ub = u.astype(BF16)
    for p in range(S5_IN_PIECES):
        lanes = slice(p * S5_IN_N, (p + 1) * S5_IN_N)
        bu = _s5_drive(ub, bb_ref, p)
        cr, ci = hpre_ref[:, lanes], hpim_ref[:, lanes]
        pr, pi = _s5_tab(tab_ref, len(S5_SHIFTS), lanes)
        for j in range(BLK // SUB):
            rows = slice(j * SUB, (j + 1) * SUB)
            xr, xi = bu[rows, 0:S5_IN_N], bu[rows, S5_IN_N:]
            for k, d in enumerate(S5_SHIFTS):
                ar, ai = _s5_tab(tab_ref, k, lanes)
                sr, si = pltpu.roll(xr, d, 0), pltpu.roll(xi, d, 0)
                xr, xi = xr + ar * sr - ai * si, xi + ar * si + ai * sr
            xr, xi = xr + pr * cr - pi * ci, xi + pr * ci + pi * cr
            hre[rows, lanes] = xr
            him[rows, lanes] = xi
            cr = jnp.broadcast_to(xr[SUB - 1:SUB, :], (SUB, S5_IN_N))
            ci = jnp.broadcast_to(xi[SUB - 1:SUB, :], (SUB, S5_IN_N))
        hpre_ref[:, lanes] = cr
        hpim_ref[:, lanes] = ci
    o_ref[...] = _s5_readout(hre, him, u, cre_ref, cim_ref, d_ref, wg_ref, bg_ref)


def _s5_sample_kernel(u_ref, bb_ref, cre_ref, cim_ref, d_ref, wg_ref, bg_ref, tab_ref, h0re_ref, h0im_ref,
                      o_ref, hsre_ref, hsim_ref):
    u = u_ref[...]
    ub = u.astype(BF16)
    for p in range(S5_IN_PIECES):
        lanes = slice(p * S5_IN_N, (p + 1) * S5_IN_N)
        bu = _s5_drive(ub, bb_ref, p)
        pr, pi = _s5_tab(tab_ref, len(S5_SHIFTS), lanes)
        ar, ai = pr[0:1, :], pi[0:1, :]
        h0r, h0i = h0re_ref[:, lanes], h0im_ref[:, lanes]
        hsre_ref[:, lanes] = ar * h0r - ai * h0i + bu[:, 0:S5_IN_N]
        hsim_ref[:, lanes] = ar * h0i + ai * h0r + bu[:, S5_IN_N:]
    o_ref[...] = _s5_readout(hsre_ref, hsim_ref, u, cre_ref, cim_ref, d_ref, wg_ref, bg_ref)


def _s5_param_specs(l):
    return [
        _layer_spec((S5_IN_PIECES, S5_IN_K, 2 * S5_IN_N), l),
        _layer_spec((S5_OUT_PIECES, S5_OUT_K, S5_OUT_N), l), _layer_spec((S5_OUT_PIECES, S5_OUT_K, S5_OUT_N), l),
        _layer_spec((1, S5_WIDTH), l), _layer_spec((S5_WIDTH, S5_WIDTH), l), _layer_spec((1, S5_WIDTH), l),
        _layer_spec((S5_NTAB, S5_LANES), l),
    ]


def _s5_params(prm):
    return (prm["bb"], prm["c_re"], prm["c_im"], prm["d"], prm["w_glu"], prm["b_glu"], prm["tab"])


def _s5_prompt(h, prm, mix, l):
    const = lambda shape: pl.BlockSpec(shape, lambda i: (0,) * len(shape))
    n_in = 8
    return pl.pallas_call(
        _drop_inputs(_s5_prompt_kernel, n_in, 1),
        grid=(N_PBLK,),
        in_specs=([pl.BlockSpec((BLK, S5_WIDTH), lambda s: (_seq_block(s), C_U // S5_WIDTH))]
                  + _s5_param_specs(l) + [_any()]),
        out_specs=[
            pl.BlockSpec((BLK, S5_WIDTH), lambda s: (_seq_block(s), M_B // S5_WIDTH)),
            const((SUB, S5_LANES)), const((SUB, S5_LANES)),
        ],
        out_shape=[
            jax.ShapeDtypeStruct((ROWS, D_MODEL), BF16),
            jax.ShapeDtypeStruct((SUB, S5_LANES), F32), jax.ShapeDtypeStruct((SUB, S5_LANES), F32),
        ],
        scratch_shapes=[pltpu.VMEM((BLK, S5_LANES), F32), pltpu.VMEM((BLK, S5_LANES), F32)],
        input_output_aliases={n_in: 0},
        compiler_params=_cparams("arbitrary"),
        name="s5_prompt",
    )(h, *_s5_params(prm), mix)


def _s5_sample(h, prm, h0re, h0im, mix, l):
    const = lambda shape: pl.BlockSpec(shape, lambda i: (0,) * len(shape))
    n_in = 10
    return pl.pallas_call(
        _drop_inputs(_s5_sample_kernel, n_in, 1),
        grid=(1,),
        in_specs=([pl.BlockSpec((N_SAMPLE, S5_WIDTH), lambda i: (N_PBLK, C_U // S5_WIDTH))] + _s5_param_specs(l)
                  + [_layer_spec((N_SAMPLE, S5_LANES), l), _layer_spec((N_SAMPLE, S5_LANES), l), _any()]),
        out_specs=[
            pl.BlockSpec((N_SAMPLE, S5_WIDTH), lambda i: (N_PBLK, M_B // S5_WIDTH)),
            const((N_SAMPLE, S5_LANES)), const((N_SAMPLE, S5_LANES)),
        ],
        out_shape=[
            jax.ShapeDtypeStruct((ROWS, D_MODEL), BF16),
            jax.ShapeDtypeStruct((N_SAMPLE, S5_LANES), F32), jax.ShapeDtypeStruct((N_SAMPLE, S5_LANES), F32),
        ],
        input_output_aliases={n_in: 0},
        compiler_params=_cparams("arbitrary"),
        name="s5_sample",
    )(h, *_s5_params(prm), h0re, h0im, mix)


def _gla_log_gate(ac, wa_ref, ba_ref):
    return _log_sigmoid(_dot(ac.astype(BF16), wa_ref[...]) + ba_ref[...]) * (1.0 / GLA_TAU)


def _gla_finish(o_heads, gate, ng):
    outs = [o * lax.rsqrt(jnp.mean(o * o, -1, keepdims=True) + LN_EPS) for o in o_heads]
    return jnp.concatenate(outs, axis=1) * ng * (gate * jax.nn.sigmoid(gate))


def _gla_prompt_kernel(q_ref, k_ref, v_ref, gate_ref, ac_ref, wa_ref, ba_ref, ng_ref, o_ref, s_ref):
    i = pl.program_id(0)

    @pl.when(i == 0)
    def _():
        s_ref[...] = jnp.zeros_like(s_ref)

    lg = _gla_log_gate(ac_ref[...], wa_ref, ba_ref)
    L = GLA_CHUNK
    row = lax.broadcasted_iota(jnp.int32, (L, GLA_KW), 0)
    causal = lax.broadcasted_iota(jnp.int32, (L, L), 0) >= lax.broadcasted_iota(jnp.int32, (L, L), 1)
    for c in range(BLK // L):
        rows = slice(c * L, (c + 1) * L)
        b = lg[rows, :]
        d = 1
        while d < L:
            b = b + jnp.where(row >= d, pltpu.roll(b, d, 0), 0.0)
            d *= 2
        b_end = b[L - 1:L, :]
        kc = k_ref[rows, :]
        q_in = (q_ref[rows, :] * GLA_DK ** -0.5 * jnp.exp(b)).astype(BF16)
        k_out = (kc * jnp.exp(-b)).astype(BF16)
        k_dec = (kc * jnp.exp(b_end - b)).astype(BF16)
        dec = jnp.transpose(jnp.broadcast_to(jnp.exp(b_end), (GLA_DV, GLA_KW)))
        vb = v_ref[rows, :].astype(BF16)
        o_heads = []
        for hh in range(GLA_HEADS):
            ks = slice(hh * GLA_DK, (hh + 1) * GLA_DK)
            vh = vb[:, hh * GLA_DV:(hh + 1) * GLA_DV]
            att = jnp.where(causal, _dot_nt(q_in[:, ks], k_out[:, ks]), 0.0)
            s_old = s_ref[ks, :]
            o_heads.append(_dot(q_in[:, ks], s_old.astype(BF16)) + _dot(att.astype(BF16), vh))
            s_ref[ks, :] = dec[ks, :] * s_old + _dot_tn(k_dec[:, ks], vh)
        o_ref[rows, :] = _gla_finish(o_heads, gate_ref[rows, :], ng_ref[...]).astype(BF16)


def _gla_param_specs(l):
    return [_layer_spec((128, GLA_KW), l), _layer_spec((1, GLA_KW), l), _layer_spec((1, GLA_WIDTH), l)]


def _gla_prompt(h, wa, ba, ng, mix, l):
    blk = lambda c, w: pl.BlockSpec((BLK, w), lambda s: (_seq_block(s), c // w))
    n_in = 8
    return pl.pallas_call(
        _drop_inputs(_gla_prompt_kernel, n_in, 1),
        grid=(N_PBLK,),
        in_specs=([blk(C_GQ, GLA_KW), blk(C_GK, GLA_KW), blk(C_GV, GLA_WIDTH), blk(C_GG, GLA_WIDTH), blk(C_GA, 128)]
                  + _gla_param_specs(l) + [_any()]),
        out_specs=[blk(M_C, GLA_WIDTH), pl.BlockSpec((GLA_KW, GLA_DV), lambda s: (0, 0))],
        out_shape=[jax.ShapeDtypeStruct((ROWS, D_MODEL), BF16), jax.ShapeDtypeStruct((GLA_KW, GLA_DV), F32)],
        input_output_aliases={n_in: 0},
        compiler_params=_cparams("arbitrary"),
        name="gla_prompt",
    )(h, h, h, h, h, wa, ba, ng, mix)


GLA_BT = 16


def _gla_sample_kernel(q_ref, k_ref, v_ref, gate_ref, ac_ref, wa_ref, ba_ref, ng_ref, s_ref, o_ref, so_ref):
    bt = GLA_BT
    lg = _gla_log_gate(ac_ref[...], wa_ref, ba_ref)
    v = v_ref[...]

    def cols(x):
        return jnp.transpose(jnp.concatenate([x, jnp.zeros((128 - bt, GLA_KW), F32)], axis=0))

    eg_t, k_t, q_t = cols(jnp.exp(lg)), cols(k_ref[...]), cols(q_ref[...] * GLA_DK ** -0.5)
    o_rows = []
    for j in range(bt):
        bc = lambda t: jnp.broadcast_to(t[:, j:j + 1], (GLA_KW, GLA_DV))
        vrow = jnp.concatenate(
            [jnp.broadcast_to(v[j:j + 1, hh * GLA_DV:(hh + 1) * GLA_DV], (GLA_DK, GLA_DV)) for hh in range(GLA_HEADS)],
            axis=0)
        s_new = bc(eg_t) * s_ref[j] + bc(k_t) * vrow
        so_ref[j] = s_new
        qs = bc(q_t) * s_new
        o_rows.append(jnp.concatenate(
            [jnp.sum(qs[hh * GLA_DK:(hh + 1) * GLA_DK, :], axis=0, keepdims=True) for hh in range(GLA_HEADS)], axis=1))
    o = jnp.concatenate(o_rows, axis=0)
    o_heads = [o[:, hh * GLA_DV:(hh + 1) * GLA_DV] for hh in range(GLA_HEADS)]
    o_ref[...] = _gla_finish(o_heads, gate_ref[...], ng_ref[...]).astype(BF16)


def _gla_sample(h, wa, ba, ng, state, mix, s_all, l):
    bt = GLA_BT
    r0 = ROWS_P // bt
    blk = lambda c, w: pl.BlockSpec((bt, w), lambda s: (r0 + s, c // w))
    st = pl.BlockSpec((None, bt, GLA_KW, GLA_DV), lambda s: (l, s, 0, 0))
    n_in = 9
    return pl.pallas_call(
        _drop_inputs(_gla_sample_kernel, n_in, 2),
        grid=(N_SAMPLE // bt,),
        in_specs=([blk(C_GQ, GLA_KW), blk(C_GK, GLA_KW), blk(C_GV, GLA_WIDTH), blk(C_GG, GLA_WIDTH), blk(C_GA, 128)]
                  + _gla_param_specs(l) + [st, _any(), _any()]),
        out_specs=[blk(M_C, GLA_WIDTH), st],
        out_shape=[jax.ShapeDtypeStruct((ROWS, D_MODEL), BF16), jax.ShapeDtypeStruct(s_all.shape, F32)],
        input_output_aliases={n_in: 0, n_in + 1: 1},
        compiler_params=_cparams("arbitrary"),
        name="gla_sample",
    )(h, h, h, h, h, wa, ba, ng, state, mix, s_all)


OUT_TM = 768


def _out_proj_kernel(mix_ref, w_ref, x_ref, g_ref, b_ref, o_ref):
    i = pl.program_id(0)
    o_ref[...] = _dot(mix_ref[...], w_ref[...])
    _deepnorm_ln(o_ref, x_ref, g_ref, b_ref, o_ref, i * OUT_TM, OUT_TM)


def _out_proj(mix, w, x, g, b, l):
    tm = OUT_TM
    return pl.pallas_call(
        _out_proj_kernel,
        grid=(ROWS // tm,),
        in_specs=[
            pl.BlockSpec((tm, D_MODEL), lambda i: (i, 0)),
            pl.BlockSpec((None, D_MODEL, D_MODEL), lambda i: (l, 0, 0), pipeline_mode=pl.Buffered(1)),
            pl.BlockSpec((tm, D_MODEL), lambda i: (i, 0)),
            _layer_spec((1, D_MODEL), l), _layer_spec((1, D_MODEL), l),
        ],
        out_specs=pl.BlockSpec((tm, D_MODEL), lambda i: (i, 0)),
        out_shape=jax.ShapeDtypeStruct((ROWS, D_MODEL), F32),
        compiler_params=_cparams("arbitrary"),
        name="out_proj",
    )(mix, w, x, g, b)


FFN_TM = 768
FFN_TF = 512
FFN_HALO = 16
FFN_NI = ROWS // FFN_TM
assert ROWS % FFN_TM == 0 and FFN_TM >= N_SAMPLE + BLK + SUB and META_ROW % FFN_HALO == 0


def _ffn_kernel(final, x_ref, xh_ref, wu_ref, wg_ref, cw_ref, cb_ref, wd_ref, p0_ref, p1_ref, g_ref, b_ref,
                o_ref, *rest):
    if final:
        ys_ref, ut_ref, us_ref, xb_ref, a_ref = rest
    else:
        ut_ref, us_ref, xb_ref, a_ref = rest
    i, f = pl.program_id(0), pl.program_id(1)
    tm, hl = FFN_TM, FFN_HALO
    last_i = FFN_NI - 1

    @pl.when(f == 0)
    def _():
        xb_ref[...] = x_ref[...].astype(BF16)
        o_ref[...] = jnp.zeros_like(o_ref)

    xb = xb_ref[...]
    u = _dot(xb, wu_ref[...])
    gte = _dot(xb, wg_ref[...])
    cw0, cw1, cw2, cb = cw_ref[0:1, :], cw_ref[1:2, :], cw_ref[2:3, :], cb_ref[...]

    def act(um2, um1, u0, g0):
        return (jax.nn.gelu(cb + cw0 * um2 + cw1 * um1 + cw2 * u0) * g0).astype(BF16)

    a_ref[...] = act(pltpu.roll(u, 2, 0), pltpu.roll(u, 1, 0), u, gte)
    ext = jnp.concatenate([_dot(xh_ref[...].astype(BF16), wu_ref[...]), u[0:hl, :]], axis=0)
    a_ref[0:hl, :] = act(ext[hl - 2:2 * hl - 2, :], ext[hl - 1:2 * hl - 1, :], u[0:hl, :], gte[0:hl, :])

    @pl.when(i == last_i)
    def _():
        a_ref[tm - N_SAMPLE:, :] = act(p0_ref[...], p1_ref[...], u[tm - N_SAMPLE:, :], gte[tm - N_SAMPLE:, :])

    o_ref[...] += _dot(a_ref[...], wd_ref[...])
    ut_ref[...] = u[tm - N_SAMPLE - BLK - SUB:tm - N_SAMPLE - BLK, :]
    us_ref[0] = p1_ref[...]
    us_ref[1] = u[tm - N_SAMPLE:, :]

    @pl.when(f == pl.num_programs(1) - 1)
    def _():
        _deepnorm_ln(o_ref, x_ref, g_ref, b_ref, o_ref, i * tm, tm)
        if final:
            ys_ref[...] = o_ref[tm - N_SAMPLE:, :]


def _ffn(x, wup, cw, cb, wd, conv_p0, conv_p1, g, b, l, final):
    tm, tf = FFN_TM, FFN_TF
    nf = D_FF // tf
    tail = lambda i, f: jnp.where(i == FFN_NI - 1, f, 0)
    halo = lambda i, f: (jnp.where(i == 0, ROWS_P // FFN_HALO, i * (tm // FFN_HALO)) - 1, 0)
    y_spec = pl.BlockSpec((tm, D_MODEL), lambda i, f: (i, 0))
    y_specs = [y_spec, pl.BlockSpec((N_SAMPLE, D_MODEL), lambda i, f: (0, 0))] if final else [y_spec]
    y_shapes = ([jax.ShapeDtypeStruct((SEQ, D_MODEL), F32), jax.ShapeDtypeStruct((N_SAMPLE, D_MODEL), F32)]
                if final else [jax.ShapeDtypeStruct((ROWS, D_MODEL), F32)])
    return pl.pallas_call(
        functools.partial(_ffn_kernel, final),
        grid=(FFN_NI, nf),
        in_specs=[
            pl.BlockSpec((tm, D_MODEL), lambda i, f: (i, 0)),
            pl.BlockSpec((FFN_HALO, D_MODEL), halo),
            pl.BlockSpec((None, D_MODEL, tf), lambda i, f: (l, 0, f)),
            pl.BlockSpec((None, D_MODEL, tf), lambda i, f: (l, 0, nf + f)),
            pl.BlockSpec((None, CONV_W, tf), lambda i, f: (l, 0, f)),
            pl.BlockSpec((None, 1, tf), lambda i, f: (l, 0, f)),
            pl.BlockSpec((None, tf, D_MODEL), lambda i, f: (l, f, 0)),
            pl.BlockSpec((None, N_SAMPLE, tf), lambda i, f: (l, 0, tail(i, f))),
            pl.BlockSpec((None, N_SAMPLE, tf), lambda i, f: (l, 0, tail(i, f))),
            _layer_spec((1, D_MODEL), l), _layer_spec((1, D_MODEL), l),
        ],
        out_specs=y_specs + [
            pl.BlockSpec((SUB, tf), lambda i, f: (0, tail(i, f))),
            pl.BlockSpec((2, N_SAMPLE, tf), lambda i, f: (0, 0, tail(i, f))),
        ],
        out_shape=y_shapes + [
            jax.ShapeDtypeStruct((SUB, D_FF), F32),
            jax.ShapeDtypeStruct((2, N_SAMPLE, D_FF), F32),
        ],
        scratch_shapes=[pltpu.VMEM((tm, D_MODEL), BF16), pltpu.VMEM((tm, tf), BF16)],
        compiler_params=_cparams("arbitrary", "arbitrary"),
        name="conv_ffn",
    )(x, x, wup, wup, cw, cb, wd, conv_p0, conv_p1, g, b)


def _rope_table():
    inv = jnp.power(ROPE_THETA, -jnp.arange(ROPE_HALF, dtype=F32) / ROPE_HALF)
    pos = jnp.concatenate([jnp.arange(SEQ, dtype=jnp.int32) + N_META, jnp.zeros((LEAD,), jnp.int32),
                           jnp.arange(N_META, dtype=jnp.int32), jnp.full((N_SAMPLE,), PAST_LEN, jnp.int32)])
    ang = pos.astype(F32)[:, None] * inv[None, :]
    return jnp.concatenate([jnp.cos(ang), jnp.sin(ang)], axis=1)


def _s5_tables(lam_re, lam_im, log_dt, b_re, b_im, c_re, c_im, d, w_glu, b_glu):
    L = lam_re.shape[0]
    dt = jnp.exp(log_dt)[..., None]
    mag = jnp.exp(lam_re * dt)
    ab_re, ab_im = mag * jnp.cos(lam_im * dt), mag * jnp.sin(lam_im * dt)
    den = lam_re * lam_re + lam_im * lam_im
    nr, ni = ab_re - 1.0, ab_im
    f_re = (nr * lam_re + ni * lam_im) / den
    f_im = (ni * lam_re - nr * lam_im) / den
    bb_re = f_re[..., None] * b_re - f_im[..., None] * b_im
    bb_im = f_re[..., None] * b_im + f_im[..., None] * b_re
    gi, go = S5_GROUPS // S5_IN_PIECES, S5_GROUPS // S5_OUT_PIECES

    def bd_in(t):
        t = t.reshape(L, S5_IN_PIECES, gi, S5_STATE, S5_CH)
        return jnp.einsum('lqgph,gk->lqghkp', t, jnp.eye(gi, dtype=F32)).reshape(L, S5_IN_PIECES, S5_IN_K, S5_IN_N)

    def bd_out(t):
        t = t.reshape(L, S5_OUT_PIECES, go, S5_CH, S5_STATE)
        return jnp.einsum('lqghp,gk->lqgpkh', t, jnp.eye(go, dtype=F32)).reshape(
            L, S5_OUT_PIECES, S5_OUT_K, S5_OUT_N)

    ar, ai = ab_re.reshape(L, 1, S5_LANES), ab_im.reshape(L, 1, S5_LANES)
    pows = [(ar, ai)]
    for _ in range(SUB - 1):
        pr, pi = pows[-1]
        pows.append((pr * ar - pi * ai, pr * ai + pi * ar))
    row = jnp.arange(SUB)[None, :, None]
    parts = []
    for sft in S5_SHIFTS:
        pr, pi = pows[sft - 1]
        parts += [jnp.where(row >= sft, pr, 0.0), jnp.where(row >= sft, pi, 0.0)]
    parts += [jnp.concatenate([p[0] for p in pows], 1), jnp.concatenate([p[1] for p in pows], 1)]
    return dict(bb=jnp.concatenate([bd_in(bb_re), bd_in(bb_im)], axis=3).astype(BF16),
                c_re=bd_out(c_re).astype(BF16), c_im=bd_out(c_im).astype(BF16),
                d=d.reshape(L, 1, S5_WIDTH), w_glu=w_glu.astype(BF16), b_glu=b_glu.reshape(L, 1, S5_WIDTH),
                tab=jnp.concatenate(parts, 1))


@jax.jit
def kernel(x_prompt, x_sample, cache_swa_k, cache_swa_v, state_ssm_re, state_ssm_im, state_gla, state_conv,
           meta_tokens, ln_in_g, ln_in_b, w_in, attn_sink, s5_lam_re, s5_lam_im, s5_log_dt, s5_b_re, s5_b_im,
           s5_c_re, s5_c_im, s5_d, s5_w_glu, s5_b_glu, gla_w_a2, gla_b_a, gla_norm_g, w_out, ln1_g, ln1_b,
           ffn_w_up, ffn_conv_w, ffn_conv_b, ffn_w_down, ln2_g, ln2_b):
    L = DEPTH
    row3 = lambda t: t.reshape(L, 1, -1)
    w_in_b = w_in.astype(BF16)
    w_out_b = w_out.astype(BF16)
    w_up_b = ffn_w_up.astype(BF16)
    w_down_b = ffn_w_down.astype(BF16)
    wa_b = jnp.pad(gla_w_a2, ((0, 0), (0, 128 - GLA_LOWRANK), (0, 0))).astype(BF16)
    gla_args = (wa_b, row3(gla_b_a), row3(gla_norm_g))
    ln1, ln2 = (row3(ln1_g), row3(ln1_b)), (row3(ln2_g), row3(ln2_b))
    conv_b3 = row3(ffn_conv_b)
    cs = _rope_table()
    s5p = _s5_tables(s5_lam_re, s5_lam_im, s5_log_dt, s5_b_re, s5_b_im, s5_c_re, s5_c_im, s5_d, s5_w_glu, s5_b_glu)
    ck = cache_swa_k.reshape(L, N_SAMPLE, WINDOW, KV_WIDTH)
    cv = cache_swa_v.reshape(L, N_SAMPLE, WINDOW, KV_WIDTH)
    h0re = state_ssm_re.reshape(L, N_SAMPLE, S5_LANES)
    h0im = state_ssm_im.reshape(L, N_SAMPLE, S5_LANES)
    sg = state_gla.reshape(L, N_SAMPLE, GLA_KW, GLA_DV)
    conv_p0, conv_p1 = state_conv[:, :, 0, :], state_conv[:, :, 1, :]
    mix = jnp.zeros((ROWS, D_MODEL), BF16)
    s_all = jnp.zeros((L, N_SAMPLE, GLA_KW, GLA_DV), F32)

    x = _ln_in(x_prompt.reshape(SEQ, D_MODEL), meta_tokens, x_sample.reshape(N_SAMPLE, D_MODEL),
               ln_in_g.reshape(1, D_MODEL), ln_in_b.reshape(1, D_MODEL))
    outs = [[] for _ in range(9)]
    for l in range(L):
        h = _in_proj(x, w_in_b, cs, l)
        mix = _swa_prompt(attn_sink, h, mix, l)
        mix, ck, cv = _swa_sample(attn_sink, h, ck, cv, mix, l)
        mix, hp_re, hp_im = _s5_prompt(h, s5p, mix, l)
        mix, hs_re, hs_im = _s5_sample(h, s5p, h0re, h0im, mix, l)
        mix, s_p = _gla_prompt(h, *gla_args, mix, l)
        mix, s_all = _gla_sample(h, *gla_args, sg, mix, s_all, l)
        x = _out_proj(mix, w_out_b, x, *ln1, l)
        res = _ffn(x, w_up_b, ffn_conv_w, conv_b3, w_down_b, conv_p0, conv_p1, *ln2, l, l == L - 1)
        if l == L - 1:
            y_prompt, y_sample, u_tail, u_s = res
        else:
            x, u_tail, u_s = res
        kv_p = lambda c: h[SEQ - WINDOW:SEQ, c:c + KV_WIDTH].reshape(1, WINDOW, A_KV_HEADS, HEAD_DIM)
        new = (kv_p(C_K), kv_p(C_V),
               hp_re[SUB - 1].reshape(1, S5_GROUPS, S5_STATE), hp_im[SUB - 1].reshape(1, S5_GROUPS, S5_STATE),
               s_p.reshape(1, GLA_HEADS, GLA_DK, GLA_DV), u_tail[SUB - 2:SUB][None],
               hs_re.reshape(N_SAMPLE, S5_GROUPS, S5_STATE), hs_im.reshape(N_SAMPLE, S5_GROUPS, S5_STATE),
               jnp.transpose(u_s, (1, 0, 2)))
        for lst, val in zip(outs, new):
            lst.append(val)
    st = [jnp.stack(lst) for lst in outs]
    return (y_prompt.reshape(1, SEQ, D_MODEL), y_sample.reshape(N_SAMPLE, 1, D_MODEL),
            st[0], st[1], st[2], st[3], st[4], st[5],
            ck.reshape(L, N_SAMPLE, WINDOW, A_KV_HEADS, HEAD_DIM),
            cv.reshape(L, N_SAMPLE, WINDOW, A_KV_HEADS, HEAD_DIM),
            st[6], st[7], s_all.reshape(L, N_SAMPLE, GLA_HEADS, GLA_DK, GLA_DV), st[8])
```

```python
import functools

import jax
import jax.numpy as jnp
from jax import lax
from jax.experimental import pallas as pl
from jax.experimental.pallas import tpu as pltpu

F32 = jnp.float32
BF16 = jnp.bfloat16

D_MODEL = 2048
SEQ = 8192
DEPTH = 2
N_SAMPLE = 128
N_META = 16
HEAD_DIM = 64
A_WIDTH = 1024
A_HEADS = 16
A_KV_HEADS = 4
GQA = 4
KV_WIDTH = A_KV_HEADS * HEAD_DIM
WINDOW = 128
ROPE_DIM = 16
ROPE_HALF = ROPE_DIM // 2
ROPE_THETA = 500000.0
PAST_LEN = 8192
S5_WIDTH = 512
S5_CH = 16
S5_GROUPS = 32
S5_STATE = 64
S5_LANES = S5_GROUPS * S5_STATE
GLA_WIDTH = 512
GLA_HEADS = 4
GLA_DV = 128
GLA_DK = 64
GLA_KW = GLA_HEADS * GLA_DK
GLA_LOWRANK = 16
GLA_TAU = 16.0
GLA_CHUNK = 64
D_FF = 5632
CONV_W = 3
LN_EPS = 1e-5
ALPHA = (2 * DEPTH) ** 0.25
NEG_INF = -1e30

BLK = 128
T_PROMPT = N_META + SEQ
LEAD = (-T_PROMPT) % BLK
N_PBLK = (LEAD + T_PROMPT) // BLK
ROWS_P = N_PBLK * BLK
ROWS = ROWS_P + N_SAMPLE
META_ROW = SEQ + LEAD
SUB = 8

C_Q, C_K, C_V, C_U, C_GQ, C_GK, C_GV, C_GG, C_GA = 0, 1024, 1280, 1536, 2048, 2304, 2560, 3072, 3584
IN_COLS = 3600
IN_TILE = 1280
IN_PAD = 3 * IN_TILE
ROPE_COLS = A_WIDTH + KV_WIDTH
M_A, M_B, M_C = 0, A_WIDTH, A_WIDTH + S5_WIDTH

VMEM_LIMIT = 56 * 1024 * 1024


def _cparams(*sem):
    return pltpu.CompilerParams(dimension_semantics=sem, vmem_limit_bytes=VMEM_LIMIT)


def _any():
    return pl.BlockSpec(memory_space=pl.ANY)


def _drop_inputs(kern, start, count):
    def wrapped(*refs):
        return kern(*refs[:start], *refs[start + count:])
    return wrapped


def _seq_block(s):
    return jnp.where(s == 0, N_PBLK - 1, s - 1)


def _layer_spec(shape, l):
    return pl.BlockSpec((None,) + shape, lambda *_: (l,) + (0,) * len(shape))


def _cast_specs(w, rows, l):
    n, cols = w.shape[1] // rows, w.shape[2]
    chunk = lambda s: jnp.minimum(s, n - 1)
    return (pl.BlockSpec((None, rows, cols), lambda s: (l, chunk(s), 0)),
            pl.BlockSpec((rows, cols), lambda s: (chunk(s), 0)),
            jax.ShapeDtypeStruct(w.shape[1:], BF16))


def _layer_norm_rows(x, g, b):
    mu = jnp.mean(x, -1, keepdims=True)
    xc = x - mu
    var = jnp.mean(xc * xc, -1, keepdims=True)
    return xc * lax.rsqrt(var + LN_EPS) * g + b


def _zero_pad_rows(y, first_row):
    row = first_row + lax.broadcasted_iota(jnp.int32, y.shape, 0)
    return jnp.where((row >= SEQ) & (row < META_ROW), 0.0, y)


def _deepnorm_ln(acc_ref, x_ref, g_ref, b_ref, o_ref, first_row, rows):
    g, b = g_ref[...], b_ref[...]
    for c in range(rows // BLK):
        r = slice(c * BLK, (c + 1) * BLK)
        y = _layer_norm_rows(ALPHA * x_ref[r, :] + acc_ref[r, :], g, b)
        o_ref[r, :] = _zero_pad_rows(y, first_row + c * BLK)


def _dot(a, b):
    return jnp.dot(a, b, preferred_element_type=F32)


def _dot_nt(a, b):
    return lax.dot_general(a, b, (((1,), (1,)), ((), ())), preferred_element_type=F32)


def _dot_tn(a, b):
    return lax.dot_general(a, b, (((0,), (0,)), ((), ())), preferred_element_type=F32)


def _log_sigmoid(x):
    return jnp.minimum(x, 0.0) - jnp.log(1.0 + jnp.exp(-jnp.abs(x)))


LN_IN_TM = 256
assert ROWS - SEQ == LN_IN_TM and SEQ % LN_IN_TM == 0


def _ln_in_kernel(xp_ref, meta_ref, xs_ref, g_ref, b_ref, o_ref):
    i = pl.program_id(0)
    g, b = g_ref[...], b_ref[...]

    @pl.when(i < SEQ // LN_IN_TM)
    def _():
        o_ref[...] = _layer_norm_rows(xp_ref[...], g, b)

    @pl.when(i == SEQ // LN_IN_TM)
    def _():
        o_ref[0:LEAD, :] = jnp.zeros((LEAD, D_MODEL), F32)
        o_ref[LEAD:BLK, :] = _layer_norm_rows(meta_ref[...], g, b)
        o_ref[BLK:, :] = _layer_norm_rows(xs_ref[...], g, b)


def _ln_in(xp, meta, xs, g, b):
    tm = LN_IN_TM
    const = lambda shape: pl.BlockSpec(shape, lambda i: (0,) * len(shape))
    return pl.pallas_call(
        _ln_in_kernel,
        grid=(ROWS // tm,),
        in_specs=[
            pl.BlockSpec((tm, D_MODEL), lambda i: (jnp.minimum(i, SEQ // tm - 1), 0)),
            const((N_META, D_MODEL)), const((N_SAMPLE, D_MODEL)), const((1, D_MODEL)), const((1, D_MODEL)),
        ],
        out_specs=pl.BlockSpec((tm, D_MODEL), lambda i: (i, 0)),
        out_shape=jax.ShapeDtypeStruct((ROWS, D_MODEL), F32),
        compiler_params=_cparams("arbitrary"),
        name="ln_in",
    )(xp, meta, xs, g, b)


IN_TM = 1056


def _in_proj_kernel(x_ref, w_ref, cs_ref, o_ref, xb_ref):
    j = pl.program_id(1)

    @pl.when(j == 0)
    def _():
        xb_ref[...] = x_ref[...].astype(BF16)

    o_ref[...] = _dot(xb_ref[...], w_ref[...])

    @pl.when(j == 0)
    def _():
        c8, s8 = cs_ref[:, 0:ROPE_HALF], cs_ref[:, ROPE_HALF:ROPE_DIM]
        rest = HEAD_DIM - ROPE_DIM
        one, zero = jnp.ones((IN_TM, rest), F32), jnp.zeros((IN_TM, rest), F32)
        z8 = jnp.zeros((IN_TM, ROPE_HALF), F32)
        cos = jnp.concatenate([c8, c8, one] * 2, axis=1)
        sa = jnp.concatenate([-s8, z8, zero] * 2, axis=1)
        sb = jnp.concatenate([z8, s8, zero] * 2, axis=1)
        for c in range(ROPE_COLS // 128):
            blk = o_ref[:, c * 128:(c + 1) * 128]
            o_ref[:, c * 128:(c + 1) * 128] = (
                blk * cos + pltpu.roll(blk, 128 - ROPE_HALF, 1) * sa + pltpu.roll(blk, ROPE_HALF, 1) * sb)

    @pl.when(j == IN_PAD // IN_TILE - 1)
    def _():
        edge = (IN_COLS % IN_TILE) // 128 * 128
        col = edge + lax.broadcasted_iota(jnp.int32, (IN_TM, 128), 1)
        o_ref[:, edge:edge + 128] = jnp.where(col < IN_COLS % IN_TILE, o_ref[:, edge:edge + 128], 0.0)
        o_ref[:, edge + 128:] = jnp.zeros((IN_TM, IN_TILE - edge - 128), F32)


def _in_proj(x, w, cs, l):
    return pl.pallas_call(
        _in_proj_kernel,
        grid=(ROWS // IN_TM, IN_PAD // IN_TILE),
        in_specs=[
            pl.BlockSpec((IN_TM, D_MODEL), lambda i, j: (i, 0)),
            pl.BlockSpec((None, D_MODEL, IN_TILE), lambda i, j: (l, 0, j)),
            pl.BlockSpec((IN_TM, ROPE_DIM), lambda i, j: (i, 0)),
        ],
        out_specs=pl.BlockSpec((IN_TM, IN_TILE), lambda i, j: (i, j)),
        out_shape=jax.ShapeDtypeStruct((ROWS, IN_PAD), F32),
        scratch_shapes=[pltpu.VMEM((IN_TM, D_MODEL), BF16)],
        compiler_params=_cparams("arbitrary", "arbitrary"),
        name="in_proj",
    )(x, w, cs)


def _swa_prompt_kernel(l, sink_ref, q_ref, kp_ref, kc_ref, vp_ref, vc_ref, wsrc_ref, o_ref, wdst_ref):
    s_blk = pl.program_id(0)
    wdst_ref[...] = wsrc_ref[...].astype(BF16)
    k2 = jnp.concatenate([kp_ref[...], kc_ref[...]], axis=0).astype(BF16)
    v2 = jnp.concatenate([vp_ref[...], vc_ref[...]], axis=0).astype(BF16)
    row = lax.broadcasted_iota(jnp.int32, (BLK, 2 * BLK), 0)
    col = lax.broadcasted_iota(jnp.int32, (BLK, 2 * BLK), 1)
    diff = row - col + BLK
    kpos = (s_blk - 1) * BLK + col - LEAD
    ok = (diff >= 0) & (diff <= WINDOW) & (kpos >= 0)
    for h in range(A_HEADS):
        kv = h // GQA
        qh = (q_ref[:, h * HEAD_DIM:(h + 1) * HEAD_DIM] * HEAD_DIM ** -0.5).astype(BF16)
        s = _dot_nt(qh, k2[:, kv * HEAD_DIM:(kv + 1) * HEAD_DIM])
        s = jnp.where(ok, s, NEG_INF)
        sk = sink_ref[l, h]
        m = jnp.maximum(jnp.max(s, -1, keepdims=True), sk)
        p = jnp.exp(s - m)
        den = jnp.sum(p, -1, keepdims=True) + jnp.exp(sk - m)
        o = _dot(p.astype(BF16), v2[:, kv * HEAD_DIM:(kv + 1) * HEAD_DIM]) / den
        o_ref[:, h * HEAD_DIM:(h + 1) * HEAD_DIM] = o.astype(BF16)


def _swa_prompt(sink, h, mix, w_cast, l):
    cur = _seq_block
    prev = lambda s: _seq_block(jnp.maximum(s - 1, 0))
    kvb = KV_WIDTH
    w_in_spec, w_out_spec, w_shape = _cast_specs(w_cast, 32, l)
    n_in = 7
    return pl.pallas_call(
        _drop_inputs(functools.partial(_swa_prompt_kernel, l), n_in, 1),
        grid=(N_PBLK,),
        in_specs=[
            pl.BlockSpec(memory_space=pltpu.SMEM),
            pl.BlockSpec((BLK, A_WIDTH), lambda s: (cur(s), C_Q // A_WIDTH)),
            pl.BlockSpec((BLK, kvb), lambda s: (prev(s), C_K // kvb)),
            pl.BlockSpec((BLK, kvb), lambda s: (cur(s), C_K // kvb)),
            pl.BlockSpec((BLK, kvb), lambda s: (prev(s), C_V // kvb)),
            pl.BlockSpec((BLK, kvb), lambda s: (cur(s), C_V // kvb)),
            w_in_spec,
            _any(),
        ],
        out_specs=[pl.BlockSpec((BLK, A_WIDTH), lambda s: (cur(s), M_A // A_WIDTH)), w_out_spec],
        out_shape=[jax.ShapeDtypeStruct((ROWS, D_MODEL), BF16), w_shape],
        input_output_aliases={n_in: 0},
        compiler_params=_cparams("arbitrary"),
        name="swa_prompt",
    )(sink, h, h, h, h, h, w_cast, mix)


SWA_BT = 16


def _swa_sample_kernel(l, sink_ref, q_ref, kn_ref, vn_ref, ck_ref, cv_ref, o_ref, cko_ref, cvo_ref):
    bt = SWA_BT
    kn, vn = kn_ref[...], vn_ref[...]
    k2 = ck_ref[...].reshape(bt * WINDOW, KV_WIDTH).astype(BF16)
    v2 = cv_ref[...].reshape(bt * WINDOW, KV_WIDTH).astype(BF16)
    row = lax.broadcasted_iota(jnp.int32, (WINDOW, KV_WIDTH), 0)
    for j in range(bt):
        cko_ref[j] = jnp.where(row == WINDOW - 1, kn[j:j + 1, :], pltpu.roll(ck_ref[j], WINDOW - 1, 0))
        cvo_ref[j] = jnp.where(row == WINDOW - 1, vn[j:j + 1, :], pltpu.roll(cv_ref[j], WINDOW - 1, 0))
    nq = GQA * bt
    rowb = lax.broadcasted_iota(jnp.int32, (nq, bt * WINDOW), 0) % bt
    colb = lax.broadcasted_iota(jnp.int32, (nq, bt * WINDOW), 1) // WINDOW
    same = rowb == colb
    rg = lax.broadcasted_iota(jnp.int32, (nq, 1), 0) // bt
    for kv in range(A_KV_HEADS):
        sl = slice(kv * HEAD_DIM, (kv + 1) * HEAD_DIM)
        q4 = jnp.concatenate(
            [q_ref[:, (kv * GQA + g) * HEAD_DIM:(kv * GQA + g + 1) * HEAD_DIM] for g in range(GQA)], axis=0)
        q4 = q4 * HEAD_DIM ** -0.5
        kn4 = jnp.concatenate([kn[:, sl]] * GQA, axis=0)
        vn4 = jnp.concatenate([vn[:, sl]] * GQA, axis=0)
        sk = jnp.zeros((nq, 1), F32)
        for g in range(GQA):
            sk = jnp.where(rg == g, sink_ref[l, kv * GQA + g], sk)
        s = jnp.where(same, _dot_nt(q4.astype(BF16), k2[:, sl]), NEG_INF)
        s_self = jnp.sum(q4 * kn4, -1, keepdims=True)
        m = jnp.maximum(jnp.maximum(jnp.max(s, -1, keepdims=True), s_self), sk)
        p = jnp.exp(s - m)
        p_self = jnp.exp(s_self - m)
        den = jnp.sum(p, -1, keepdims=True) + p_self + jnp.exp(sk - m)
        o = (_dot(p.astype(BF16), v2[:, sl]) + p_self * vn4) / den
        for g in range(GQA):
            hh = kv * GQA + g
            o_ref[:, hh * HEAD_DIM:(hh + 1) * HEAD_DIM] = o[g * bt:(g + 1) * bt].astype(BF16)


def _swa_sample(sink, h, cache_k, cache_v, mix, l):
    bt = SWA_BT
    r0 = ROWS_P // bt
    cache = pl.BlockSpec((None, bt, WINDOW, KV_WIDTH), lambda s: (l, s, 0, 0))
    n_in = 6
    return pl.pallas_call(
        _drop_inputs(functools.partial(_swa_sample_kernel, l), n_in, 1),
        grid=(N_SAMPLE // bt,),
        in_specs=[
            pl.BlockSpec(memory_space=pltpu.SMEM),
            pl.BlockSpec((bt, A_WIDTH), lambda s: (r0 + s, C_Q // A_WIDTH)),
            pl.BlockSpec((bt, KV_WIDTH), lambda s: (r0 + s, C_K // KV_WIDTH)),
            pl.BlockSpec((bt, KV_WIDTH), lambda s: (r0 + s, C_V // KV_WIDTH)),
            cache, cache,
            _any(),
        ],
        out_specs=[pl.BlockSpec((bt, A_WIDTH), lambda s: (r0 + s, M_A // A_WIDTH)), cache, cache],
        out_shape=[jax.ShapeDtypeStruct((ROWS, D_MODEL), BF16),
                   jax.ShapeDtypeStruct(cache_k.shape, F32), jax.ShapeDtypeStruct(cache_v.shape, F32)],
        input_output_aliases={n_in: 0, 4: 1, 5: 2},
        compiler_params=_cparams("arbitrary"),
        name="swa_sample",
    )(sink, h, h, h, cache_k, cache_v, mix)


S5_SHIFTS = (1, 2, 4)
S5_NTAB = 2 * (len(S5_SHIFTS) + 1) * SUB
S5_IN_PIECES = 4
S5_IN_K = S5_WIDTH // S5_IN_PIECES
S5_IN_N = S5_LANES // S5_IN_PIECES
S5_OUT_PIECES = 2
S5_OUT_K = S5_LANES // S5_OUT_PIECES
S5_OUT_N = S5_WIDTH // S5_OUT_PIECES


def _s5_tab(tab_ref, k, lanes):
    return (tab_ref[2 * k * SUB:(2 * k + 1) * SUB, lanes], tab_ref[(2 * k + 1) * SUB:(2 * k + 2) * SUB, lanes])


def _s5_drive(ub, bb_ref, p):
    return _dot(ub[:, p * S5_IN_K:(p + 1) * S5_IN_K], bb_ref[p])


def _s5_readout(hre, him, u, cre_ref, cim_ref, d_ref, wg_ref, bg_ref):
    ys = []
    for j in range(S5_OUT_PIECES):
        lanes = slice(j * S5_OUT_K, (j + 1) * S5_OUT_K)
        ys.append(_dot(hre[:, lanes].astype(BF16), cre_ref[j]) - _dot(him[:, lanes].astype(BF16), cim_ref[j]))
    y = jnp.concatenate(ys, axis=1) + d_ref[...] * u
    z = jax.nn.gelu(y)
    gate = jax.nn.sigmoid(_dot(z.astype(BF16), wg_ref[...]) + bg_ref[...])
    return (z * gate).astype(BF16)


def _s5_prompt_kernel(u_ref, bb_ref, cre_ref, cim_ref, d_ref, wg_ref, bg_ref, tab_ref, wsrc_ref,
                      o_ref, hpre_ref, hpim_ref, wdst_ref, hre, him):
    i = pl.program_id(0)
    wdst_ref[...] = wsrc_ref[...].astype(BF16)

    @pl.when(i == 0)
    def _():
        hpre_ref[...] = jnp.zeros_like(hpre_ref)
        hpim_ref[...] = jnp.zeros_like(hpim_ref)

    u = u_ref[...]
    ub = u.astype(BF16)
    for p in range(S5_IN_PIECES):
        lanes = slice(p * S5_IN_N, (p + 1) * S5_IN_N)
        bu = _s5_drive(ub, bb_ref, p)
        cr, ci = hpre_ref[:, lanes], hpim_ref[:, lanes]
        pr, pi = _s5_tab(tab_ref, len(S5_SHIFTS), lanes)
        for j in range(BLK // SUB):
            rows = slice(j * SUB, (j + 1) * SUB)
            xr, xi = bu[rows, 0:S5_IN_N], bu[rows, S5_IN_N:]
            for k, d in enumerate(S5_SHIFTS):
                ar, ai = _s5_tab(tab_ref, k, lanes)
                sr, si = pltpu.roll(xr, d, 0), pltpu.roll(xi, d, 0)
                xr, xi = xr + ar * sr - ai * si, xi + ar * si + ai * sr
            xr, xi = xr + pr * cr - pi * ci, xi + pr * ci + pi * cr
            hre[rows, lanes] = xr
            him[rows, lanes] = xi
            cr = jnp.broadcast_to(xr[SUB - 1:SUB, :], (SUB, S5_IN_N))
            ci = jnp.broadcast_to(xi[SUB - 1:SUB, :], (SUB, S5_IN_N))
        hpre_ref[:, lanes] = cr
        hpim_ref[:, lanes] = ci
    o_ref[...] = _s5_readout(hre, him, u, cre_ref, cim_ref, d_ref, wg_ref, bg_ref)


def _s5_sample_kernel(u_ref, bb_ref, cre_ref, cim_ref, d_ref, wg_ref, bg_ref, tab_ref, h0re_ref, h0im_ref,
                      o_ref, hsre_ref, hsim_ref):
    u = u_ref[...]
    ub = u.astype(BF16)
    for p in range(S5_IN_PIECES):
        lanes = slice(p * S5_IN_N, (p + 1) * S5_IN_N)
        bu = _s5_drive(ub, bb_ref, p)
        pr, pi = _s5_tab(tab_ref, len(S5_SHIFTS), lanes)
        ar, ai = pr[0:1, :], pi[0:1, :]
        h0r, h0i = h0re_ref[:, lanes], h0im_ref[:, lanes]
        hsre_ref[:, lanes] = ar * h0r - ai * h0i + bu[:, 0:S5_IN_N]
        hsim_ref[:, lanes] = ar * h0i + ai * h0r + bu[:, S5_IN_N:]
    o_ref[...] = _s5_readout(hsre_ref, hsim_ref, u, cre_ref, cim_ref, d_ref, wg_ref, bg_ref)


def _s5_param_specs(l):
    return [
        _layer_spec((S5_IN_PIECES, S5_IN_K, 2 * S5_IN_N), l),
        _layer_spec((S5_OUT_PIECES, S5_OUT_K, S5_OUT_N), l), _layer_spec((S5_OUT_PIECES, S5_OUT_K, S5_OUT_N), l),
        _layer_spec((1, S5_WIDTH), l), _layer_spec((S5_WIDTH, S5_WIDTH), l), _layer_spec((1, S5_WIDTH), l),
        _layer_spec((S5_NTAB, S5_LANES), l),
    ]


def _s5_params(prm):
    return (prm["bb"], prm["c_re"], prm["c_im"], prm["d"], prm["w_glu"], prm["b_glu"], prm["tab"])


def _s5_prompt(h, prm, mix, w_cast, l):
    const = lambda shape: pl.BlockSpec(shape, lambda i: (0,) * len(shape))
    w_in_spec, w_out_spec, w_shape = _cast_specs(w_cast, 176, l)
    n_in = 9
    return pl.pallas_call(
        _drop_inputs(_s5_prompt_kernel, n_in, 1),
        grid=(N_PBLK,),
        in_specs=([pl.BlockSpec((BLK, S5_WIDTH), lambda s: (_seq_block(s), C_U // S5_WIDTH))]
                  + _s5_param_specs(l) + [w_in_spec, _any()]),
        out_specs=[
            pl.BlockSpec((BLK, S5_WIDTH), lambda s: (_seq_block(s), M_B // S5_WIDTH)),
            const((SUB, S5_LANES)), const((SUB, S5_LANES)), w_out_spec,
        ],
        out_shape=[
            jax.ShapeDtypeStruct((ROWS, D_MODEL), BF16),
            jax.ShapeDtypeStruct((SUB, S5_LANES), F32), jax.ShapeDtypeStruct((SUB, S5_LANES), F32), w_shape,
        ],
        scratch_shapes=[pltpu.VMEM((BLK, S5_LANES), F32), pltpu.VMEM((BLK, S5_LANES), F32)],
        input_output_aliases={n_in: 0},
        compiler_params=_cparams("arbitrary"),
        name="s5_prompt",
    )(h, *_s5_params(prm), w_cast, mix)


def _s5_sample(h, prm, h0re, h0im, mix, l):
    const = lambda shape: pl.BlockSpec(shape, lambda i: (0,) * len(shape))
    n_in = 10
    return pl.pallas_call(
        _drop_inputs(_s5_sample_kernel, n_in, 1),
        grid=(1,),
        in_specs=([pl.BlockSpec((N_SAMPLE, S5_WIDTH), lambda i: (N_PBLK, C_U // S5_WIDTH))] + _s5_param_specs(l)
                  + [_layer_spec((N_SAMPLE, S5_LANES), l), _layer_spec((N_SAMPLE, S5_LANES), l), _any()]),
        out_specs=[
            pl.BlockSpec((N_SAMPLE, S5_WIDTH), lambda i: (N_PBLK, M_B // S5_WIDTH)),
            const((N_SAMPLE, S5_LANES)), const((N_SAMPLE, S5_LANES)),
        ],
        out_shape=[
            jax.ShapeDtypeStruct((ROWS, D_MODEL), BF16),
            jax.ShapeDtypeStruct((N_SAMPLE, S5_LANES), F32), jax.ShapeDtypeStruct((N_SAMPLE, S5_LANES), F32),
        ],
        input_output_aliases={n_in: 0},
        compiler_params=_cparams("arbitrary"),
        name="s5_sample",
    )(h, *_s5_params(prm), h0re, h0im, mix)


def _gla_log_gate(ac, wa_ref, ba_ref):
    return _log_sigmoid(_dot(ac.astype(BF16), wa_ref[...]) + ba_ref[...]) * (1.0 / GLA_TAU)


def _gla_finish(o_heads, gate, ng):
    outs = [o * lax.rsqrt(jnp.mean(o * o, -1, keepdims=True) + LN_EPS) for o in o_heads]
    return jnp.concatenate(outs, axis=1) * ng * (gate * jax.nn.sigmoid(gate))


def _gla_prompt_kernel(q_ref, k_ref, v_ref, gate_ref, ac_ref, wa_ref, ba_ref, ng_ref, wsrc_ref,
                       o_ref, s_ref, wdst_ref):
    i = pl.program_id(0)
    wdst_ref[...] = wsrc_ref[...].astype(BF16)

    @pl.when(i == 0)
    def _():
        s_ref[...] = jnp.zeros_like(s_ref)

    lg = _gla_log_gate(ac_ref[...], wa_ref, ba_ref)
    L = GLA_CHUNK
    row = lax.broadcasted_iota(jnp.int32, (L, GLA_KW), 0)
    causal = lax.broadcasted_iota(jnp.int32, (L, L), 0) >= lax.broadcasted_iota(jnp.int32, (L, L), 1)
    for c in range(BLK // L):
        rows = slice(c * L, (c + 1) * L)
        b = lg[rows, :]
        d = 1
        while d < L:
            b = b + jnp.where(row >= d, pltpu.roll(b, d, 0), 0.0)
            d *= 2
        b_end = b[L - 1:L, :]
        kc = k_ref[rows, :]
        q_in = (q_ref[rows, :] * GLA_DK ** -0.5 * jnp.exp(b)).astype(BF16)
        k_out = (kc * jnp.exp(-b)).astype(BF16)
        k_dec = (kc * jnp.exp(b_end - b)).astype(BF16)
        dec = jnp.transpose(jnp.broadcast_to(jnp.exp(b_end), (GLA_DV, GLA_KW)))
        vb = v_ref[rows, :].astype(BF16)
        o_heads = []
        for hh in range(GLA_HEADS):
            ks = slice(hh * GLA_DK, (hh + 1) * GLA_DK)
            vh = vb[:, hh * GLA_DV:(hh + 1) * GLA_DV]
            att = jnp.where(causal, _dot_nt(q_in[:, ks], k_out[:, ks]), 0.0)
            s_old = s_ref[ks, :]
            o_heads.append(_dot(q_in[:, ks], s_old.astype(BF16)) + _dot(att.astype(BF16), vh))
            s_ref[ks, :] = dec[ks, :] * s_old + _dot_tn(k_dec[:, ks], vh)
        o_ref[rows, :] = _gla_finish(o_heads, gate_ref[rows, :], ng_ref[...]).astype(BF16)


def _gla_param_specs(l):
    return [_layer_spec((128, GLA_KW), l), _layer_spec((1, GLA_KW), l), _layer_spec((1, GLA_WIDTH), l)]


def _gla_prompt(h, wa, ba, ng, mix, w_cast, l):
    blk = lambda c, w: pl.BlockSpec((BLK, w), lambda s: (_seq_block(s), c // w))
    w_in_spec, w_out_spec, w_shape = _cast_specs(w_cast, 32, l)
    n_in = 9
    return pl.pallas_call(
        _drop_inputs(_gla_prompt_kernel, n_in, 1),
        grid=(N_PBLK,),
        in_specs=([blk(C_GQ, GLA_KW), blk(C_GK, GLA_KW), blk(C_GV, GLA_WIDTH), blk(C_GG, GLA_WIDTH), blk(C_GA, 128)]
                  + _gla_param_specs(l) + [w_in_spec, _any()]),
        out_specs=[blk(M_C, GLA_WIDTH), pl.BlockSpec((GLA_KW, GLA_DV), lambda s: (0, 0)), w_out_spec],
        out_shape=[jax.ShapeDtypeStruct((ROWS, D_MODEL), BF16), jax.ShapeDtypeStruct((GLA_KW, GLA_DV), F32),
                   w_shape],
        input_output_aliases={n_in: 0},
        compiler_params=_cparams("arbitrary"),
        name="gla_prompt",
    )(h, h, h, h, h, wa, ba, ng, w_cast, mix)


GLA_BT = 16


def _gla_sample_kernel(q_ref, k_ref, v_ref, gate_ref, ac_ref, wa_ref, ba_ref, ng_ref, s_ref, o_ref, so_ref):
    bt = GLA_BT
    lg = _gla_log_gate(ac_ref[...], wa_ref, ba_ref)
    v = v_ref[...]

    def cols(x):
        return jnp.transpose(jnp.concatenate([x, jnp.zeros((128 - bt, GLA_KW), F32)], axis=0))

    eg_t, k_t, q_t = cols(jnp.exp(lg)), cols(k_ref[...]), cols(q_ref[...] * GLA_DK ** -0.5)
    o_rows = []
    for j in range(bt):
        bc = lambda t: jnp.broadcast_to(t[:, j:j + 1], (GLA_KW, GLA_DV))
        vrow = jnp.concatenate(
            [jnp.broadcast_to(v[j:j + 1, hh * GLA_DV:(hh + 1) * GLA_DV], (GLA_DK, GLA_DV)) for hh in range(GLA_HEADS)],
            axis=0)
        s_new = bc(eg_t) * s_ref[j] + bc(k_t) * vrow
        so_ref[j] = s_new
        qs = bc(q_t) * s_new
        o_rows.append(jnp.concatenate(
            [jnp.sum(qs[hh * GLA_DK:(hh + 1) * GLA_DK, :], axis=0, keepdims=True) for hh in range(GLA_HEADS)], axis=1))
    o = jnp.concatenate(o_rows, axis=0)
    o_heads = [o[:, hh * GLA_DV:(hh + 1) * GLA_DV] for hh in range(GLA_HEADS)]
    o_ref[...] = _gla_finish(o_heads, gate_ref[...], ng_ref[...]).astype(BF16)


def _gla_sample(h, wa, ba, ng, state, mix, s_all, l):
    bt = GLA_BT
    r0 = ROWS_P // bt
    blk = lambda c, w: pl.BlockSpec((bt, w), lambda s: (r0 + s, c // w))
    st = pl.BlockSpec((None, bt, GLA_KW, GLA_DV), lambda s: (l, s, 0, 0))
    n_in = 9
    return pl.pallas_call(
        _drop_inputs(_gla_sample_kernel, n_in, 2),
        grid=(N_SAMPLE // bt,),
        in_specs=([blk(C_GQ, GLA_KW), blk(C_GK, GLA_KW), blk(C_GV, GLA_WIDTH), blk(C_GG, GLA_WIDTH), blk(C_GA, 128)]
                  + _gla_param_specs(l) + [st, _any(), _any()]),
        out_specs=[blk(M_C, GLA_WIDTH), st],
        out_shape=[jax.ShapeDtypeStruct((ROWS, D_MODEL), BF16), jax.ShapeDtypeStruct(s_all.shape, F32)],
        input_output_aliases={n_in: 0, n_in + 1: 1},
        compiler_params=_cparams("arbitrary"),
        name="gla_sample",
    )(h, h, h, h, h, wa, ba, ng, state, mix, s_all)


OUT_TM = 768


def _out_proj_kernel(mix_ref, w_ref, x_ref, g_ref, b_ref, o_ref):
    i = pl.program_id(0)
    o_ref[...] = _dot(mix_ref[...], w_ref[...])
    _deepnorm_ln(o_ref, x_ref, g_ref, b_ref, o_ref, i * OUT_TM, OUT_TM)


def _out_proj(mix, w, x, g, b, l):
    tm = OUT_TM
    return pl.pallas_call(
        _out_proj_kernel,
        grid=(ROWS // tm,),
        in_specs=[
            pl.BlockSpec((tm, D_MODEL), lambda i: (i, 0)),
            pl.BlockSpec((D_MODEL, D_MODEL), lambda i: (0, 0), pipeline_mode=pl.Buffered(1)),
            pl.BlockSpec((tm, D_MODEL), lambda i: (i, 0)),
            _layer_spec((1, D_MODEL), l), _layer_spec((1, D_MODEL), l),
        ],
        out_specs=pl.BlockSpec((tm, D_MODEL), lambda i: (i, 0)),
        out_shape=jax.ShapeDtypeStruct((ROWS, D_MODEL), F32),
        compiler_params=_cparams("arbitrary"),
        name="out_proj",
    )(mix, w, x, g, b)


FFN_TM = 768
FFN_TF = 512
FFN_HALO = 16
FFN_NI = ROWS // FFN_TM
assert ROWS % FFN_TM == 0 and FFN_TM >= N_SAMPLE + BLK + SUB and META_ROW % FFN_HALO == 0


def _ffn_kernel(final, x_ref, xh_ref, wu_ref, wg_ref, cw_ref, cb_ref, wd_ref, p0_ref, p1_ref, g_ref, b_ref,
                o_ref, *rest):
    if final:
        ys_ref, ut_ref, us_ref, xb_ref, a_ref = rest
    else:
        ut_ref, us_ref, xb_ref, a_ref = rest
    i, f = pl.program_id(0), pl.program_id(1)
    tm, hl = FFN_TM, FFN_HALO
    last_i = FFN_NI - 1

    @pl.when(f == 0)
    def _():
        xb_ref[...] = x_ref[...].astype(BF16)
        o_ref[...] = jnp.zeros_like(o_ref)

    xb = xb_ref[...]
    u = _dot(xb, wu_ref[...])
    gte = _dot(xb, wg_ref[...])
    cw0, cw1, cw2, cb = cw_ref[0:1, :], cw_ref[1:2, :], cw_ref[2:3, :], cb_ref[...]

    def act(um2, um1, u0, g0):
        return (jax.nn.gelu(cb + cw0 * um2 + cw1 * um1 + cw2 * u0) * g0).astype(BF16)

    a_ref[...] = act(pltpu.roll(u, 2, 0), pltpu.roll(u, 1, 0), u, gte)
    ext = jnp.concatenate([_dot(xh_ref[...].astype(BF16), wu_ref[...]), u[0:hl, :]], axis=0)
    a_ref[0:hl, :] = act(ext[hl - 2:2 * hl - 2, :], ext[hl - 1:2 * hl - 1, :], u[0:hl, :], gte[0:hl, :])

    @pl.when(i == last_i)
    def _():
        a_ref[tm - N_SAMPLE:, :] = act(p0_ref[...], p1_ref[...], u[tm - N_SAMPLE:, :], gte[tm - N_SAMPLE:, :])

    o_ref[...] += _dot(a_ref[...], wd_ref[...])
    ut_ref[...] = u[tm - N_SAMPLE - BLK - SUB:tm - N_SAMPLE - BLK, :]
    us_ref[0] = p1_ref[...]
    us_ref[1] = u[tm - N_SAMPLE:, :]

    @pl.when(f == pl.num_programs(1) - 1)
    def _():
        _deepnorm_ln(o_ref, x_ref, g_ref, b_ref, o_ref, i * tm, tm)
        if final:
            ys_ref[...] = o_ref[tm - N_SAMPLE:, :]


def _ffn(x, wup, cw, cb, wd, conv_p0, conv_p1, g, b, l, final):
    tm, tf = FFN_TM, FFN_TF
    nf = D_FF // tf
    tail = lambda i, f: jnp.where(i == FFN_NI - 1, f, 0)
    halo = lambda i, f: (jnp.where(i == 0, ROWS_P // FFN_HALO, i * (tm // FFN_HALO)) - 1, 0)
    y_spec = pl.BlockSpec((tm, D_MODEL), lambda i, f: (i, 0))
    y_specs = [y_spec, pl.BlockSpec((N_SAMPLE, D_MODEL), lambda i, f: (0, 0))] if final else [y_spec]
    y_shapes = ([jax.ShapeDtypeStruct((SEQ, D_MODEL), F32), jax.ShapeDtypeStruct((N_SAMPLE, D_MODEL), F32)]
                if final else [jax.ShapeDtypeStruct((ROWS, D_MODEL), F32)])
    return pl.pallas_call(
        functools.partial(_ffn_kernel, final),
        grid=(FFN_NI, nf),
        in_specs=[
            pl.BlockSpec((tm, D_MODEL), lambda i, f: (i, 0)),
            pl.BlockSpec((FFN_HALO, D_MODEL), halo),
            pl.BlockSpec((D_MODEL, tf), lambda i, f: (0, f)),
            pl.BlockSpec((D_MODEL, tf), lambda i, f: (0, nf + f)),
            pl.BlockSpec((None, CONV_W, tf), lambda i, f: (l, 0, f)),
            pl.BlockSpec((None, 1, tf), lambda i, f: (l, 0, f)),
            pl.BlockSpec((tf, D_MODEL), lambda i, f: (f, 0)),
            pl.BlockSpec((None, N_SAMPLE, tf), lambda i, f: (l, 0, tail(i, f))),
            pl.BlockSpec((None, N_SAMPLE, tf), lambda i, f: (l, 0, tail(i, f))),
            _layer_spec((1, D_MODEL), l), _layer_spec((1, D_MODEL), l),
        ],
        out_specs=y_specs + [
            pl.BlockSpec((SUB, tf), lambda i, f: (0, tail(i, f))),
            pl.BlockSpec((2, N_SAMPLE, tf), lambda i, f: (0, 0, tail(i, f))),
        ],
        out_shape=y_shapes + [
            jax.ShapeDtypeStruct((SUB, D_FF), F32),
            jax.ShapeDtypeStruct((2, N_SAMPLE, D_FF), F32),
        ],
        scratch_shapes=[pltpu.VMEM((tm, D_MODEL), BF16), pltpu.VMEM((tm, tf), BF16)],
        compiler_params=_cparams("arbitrary", "arbitrary"),
        name="conv_ffn",
    )(x, x, wup, wup, cw, cb, wd, conv_p0, conv_p1, g, b)


def _rope_table():
    inv = jnp.power(ROPE_THETA, -jnp.arange(ROPE_HALF, dtype=F32) / ROPE_HALF)
    pos = jnp.concatenate([jnp.arange(SEQ, dtype=jnp.int32) + N_META, jnp.zeros((LEAD,), jnp.int32),
                           jnp.arange(N_META, dtype=jnp.int32), jnp.full((N_SAMPLE,), PAST_LEN, jnp.int32)])
    ang = pos.astype(F32)[:, None] * inv[None, :]
    return jnp.concatenate([jnp.cos(ang), jnp.sin(ang)], axis=1)


def _s5_tables(lam_re, lam_im, log_dt, b_re, b_im, c_re, c_im, d, w_glu, b_glu):
    L = lam_re.shape[0]
    dt = jnp.exp(log_dt)[..., None]
    mag = jnp.exp(lam_re * dt)
    ab_re, ab_im = mag * jnp.cos(lam_im * dt), mag * jnp.sin(lam_im * dt)
    den = lam_re * lam_re + lam_im * lam_im
    nr, ni = ab_re - 1.0, ab_im
    f_re = (nr * lam_re + ni * lam_im) / den
    f_im = (ni * lam_re - nr * lam_im) / den
    bb_re = f_re[..., None] * b_re - f_im[..., None] * b_im
    bb_im = f_re[..., None] * b_im + f_im[..., None] * b_re
    gi, go = S5_GROUPS // S5_IN_PIECES, S5_GROUPS // S5_OUT_PIECES

    def bd_in(t):
        t = t.reshape(L, S5_IN_PIECES, gi, S5_STATE, S5_CH)
        return jnp.einsum('lqgph,gk->lqghkp', t, jnp.eye(gi, dtype=F32)).reshape(L, S5_IN_PIECES, S5_IN_K, S5_IN_N)

    def bd_out(t):
        t = t.reshape(L, S5_OUT_PIECES, go, S5_CH, S5_STATE)
        return jnp.einsum('lqghp,gk->lqgpkh', t, jnp.eye(go, dtype=F32)).reshape(
            L, S5_OUT_PIECES, S5_OUT_K, S5_OUT_N)

    ar, ai = ab_re.reshape(L, 1, S5_LANES), ab_im.reshape(L, 1, S5_LANES)
    pows = [(ar, ai)]
    for _ in range(SUB - 1):
        pr, pi = pows[-1]
        pows.append((pr * ar - pi * ai, pr * ai + pi * ar))
    row = jnp.arange(SUB)[None, :, None]
    parts = []
    for sft in S5_SHIFTS:
        pr, pi = pows[sft - 1]
        parts += [jnp.where(row >= sft, pr, 0.0), jnp.where(row >= sft, pi, 0.0)]
    parts += [jnp.concatenate([p[0] for p in pows], 1), jnp.concatenate([p[1] for p in pows], 1)]
    return dict(bb=jnp.concatenate([bd_in(bb_re), bd_in(bb_im)], axis=3).astype(BF16),
                c_re=bd_out(c_re).astype(BF16), c_im=bd_out(c_im).astype(BF16),
                d=d.reshape(L, 1, S5_WIDTH), w_glu=w_glu.astype(BF16), b_glu=b_glu.reshape(L, 1, S5_WIDTH),
                tab=jnp.concatenate(parts, 1))


@jax.jit
def kernel(x_prompt, x_sample, cache_swa_k, cache_swa_v, state_ssm_re, state_ssm_im, state_gla, state_conv,
           meta_tokens, ln_in_g, ln_in_b, w_in, attn_sink, s5_lam_re, s5_lam_im, s5_log_dt, s5_b_re, s5_b_im,
           s5_c_re, s5_c_im, s5_d, s5_w_glu, s5_b_glu, gla_w_a2, gla_b_a, gla_norm_g, w_out, ln1_g, ln1_b,
           ffn_w_up, ffn_conv_w, ffn_conv_b, ffn_w_down, ln2_g, ln2_b):
    L = DEPTH
    row3 = lambda t: t.reshape(L, 1, -1)
    w_in_b = w_in.astype(BF16)
    wa_b = jnp.pad(gla_w_a2, ((0, 0), (0, 128 - GLA_LOWRANK), (0, 0))).astype(BF16)
    gla_args = (wa_b, row3(gla_b_a), row3(gla_norm_g))
    ln1, ln2 = (row3(ln1_g), row3(ln1_b)), (row3(ln2_g), row3(ln2_b))
    conv_b3 = row3(ffn_conv_b)
    cs = _rope_table()
    s5p = _s5_tables(s5_lam_re, s5_lam_im, s5_log_dt, s5_b_re, s5_b_im, s5_c_re, s5_c_im, s5_d, s5_w_glu, s5_b_glu)
    ck = cache_swa_k.reshape(L, N_SAMPLE, WINDOW, KV_WIDTH)
    cv = cache_swa_v.reshape(L, N_SAMPLE, WINDOW, KV_WIDTH)
    h0re = state_ssm_re.reshape(L, N_SAMPLE, S5_LANES)
    h0im = state_ssm_im.reshape(L, N_SAMPLE, S5_LANES)
    sg = state_gla.reshape(L, N_SAMPLE, GLA_KW, GLA_DV)
    conv_p0, conv_p1 = state_conv[:, :, 0, :], state_conv[:, :, 1, :]
    mix = jnp.zeros((ROWS, D_MODEL), BF16)
    s_all = jnp.zeros((L, N_SAMPLE, GLA_KW, GLA_DV), F32)

    x = _ln_in(x_prompt.reshape(SEQ, D_MODEL), meta_tokens, x_sample.reshape(N_SAMPLE, D_MODEL),
               ln_in_g.reshape(1, D_MODEL), ln_in_b.reshape(1, D_MODEL))
    outs = [[] for _ in range(9)]
    for l in range(L):
        h = _in_proj(x, w_in_b, cs, l)
        mix, w_up_b = _swa_prompt(attn_sink, h, mix, ffn_w_up, l)
        mix, ck, cv = _swa_sample(attn_sink, h, ck, cv, mix, l)
        mix, hp_re, hp_im, w_down_b = _s5_prompt(h, s5p, mix, ffn_w_down, l)
        mix, hs_re, hs_im = _s5_sample(h, s5p, h0re, h0im, mix, l)
        mix, s_p, w_out_b = _gla_prompt(h, *gla_args, mix, w_out, l)
        mix, s_all = _gla_sample(h, *gla_args, sg, mix, s_all, l)
        x = _out_proj(mix, w_out_b, x, *ln1, l)
        res = _ffn(x, w_up_b, ffn_conv_w, conv_b3, w_down_b, conv_p0, conv_p1, *ln2, l, l == L - 1)
        if l == L - 1:
            y_prompt, y_sample, u_tail, u_s = res
        else:
            x, u_tail, u_s = res
        kv_p = lambda c: h[SEQ - WINDOW:SEQ, c:c + KV_WIDTH].reshape(1, WINDOW, A_KV_HEADS, HEAD_DIM)
        new = (kv_p(C_K), kv_p(C_V),
               hp_re[SUB - 1].reshape(1, S5_GROUPS, S5_STATE), hp_im[SUB - 1].reshape(1, S5_GROUPS, S5_STATE),
               s_p.reshape(1, GLA_HEADS, GLA_DK, GLA_DV), u_tail[SUB - 2:SUB][None],
               hs_re.reshape(N_SAMPLE, S5_GROUPS, S5_STATE), hs_im.reshape(N_SAMPLE, S5_GROUPS, S5_STATE),
               jnp.transpose(u_s, (1, 0, 2)))
        for lst, val in zip(outs, new):
            lst.append(val)
    st = [jnp.stack(lst) for lst in outs]
    return (y_prompt.reshape(1, SEQ, D_MODEL), y_sample.reshape(N_SAMPLE, 1, D_MODEL),
            st[0], st[1], st[2], st[3], st[4], st[5],
            ck.reshape(L, N_SAMPLE, WINDOW, A_KV_HEADS, HEAD_DIM),
            cv.reshape(L, N_SAMPLE, WINDOW, A_KV_HEADS, HEAD_DIM),
            st[6], st[7], s_all.reshape(L, N_SAMPLE, GLA_HEADS, GLA_DK, GLA_DV), st[8])
```

```python
import functools

import jax
import jax.numpy as jnp
from jax import lax
from jax.experimental import pallas as pl
from jax.experimental.pallas import tpu as pltpu

F32 = jnp.float32
BF16 = jnp.bfloat16

D_MODEL = 2048
SEQ = 8192
DEPTH = 2
N_SAMPLE = 128
N_META = 16
HEAD_DIM = 64
A_WIDTH = 1024
A_HEADS = 16
A_KV_HEADS = 4
GQA = 4
KV_WIDTH = A_KV_HEADS * HEAD_DIM
WINDOW = 128
ROPE_DIM = 16
ROPE_HALF = ROPE_DIM // 2
ROPE_THETA = 500000.0
PAST_LEN = 8192
S5_WIDTH = 512
S5_CH = 16
S5_GROUPS = 32
S5_STATE = 64
S5_LANES = S5_GROUPS * S5_STATE
GLA_WIDTH = 512
GLA_HEADS = 4
GLA_DV = 128
GLA_DK = 64
GLA_KW = GLA_HEADS * GLA_DK
GLA_LOWRANK = 16
GLA_TAU = 16.0
GLA_CHUNK = 64
D_FF = 5632
CONV_W = 3
LN_EPS = 1e-5
ALPHA = (2 * DEPTH) ** 0.25
NEG_INF = -1e30

BLK = 128
T_PROMPT = N_META + SEQ
LEAD = (-T_PROMPT) % BLK
N_PBLK = (LEAD + T_PROMPT) // BLK
ROWS_P = N_PBLK * BLK
ROWS = ROWS_P + N_SAMPLE
META_ROW = SEQ + LEAD
SUB = 8

C_Q, C_K, C_V, C_U, C_GQ, C_GK, C_GV, C_GG, C_GA = 0, 1024, 1280, 1536, 2048, 2304, 2560, 3072, 3584
IN_COLS = 3600
IN_TILE = 1280
IN_PAD = 3 * IN_TILE
ROPE_COLS = A_WIDTH + KV_WIDTH
M_A, M_B, M_C = 0, A_WIDTH, A_WIDTH + S5_WIDTH

VMEM_LIMIT = 56 * 1024 * 1024


def _cparams(*sem):
    return pltpu.CompilerParams(dimension_semantics=sem, vmem_limit_bytes=VMEM_LIMIT)


def _any():
    return pl.BlockSpec(memory_space=pl.ANY)


def _drop_inputs(kern, start, count):
    def wrapped(*refs):
        return kern(*refs[:start], *refs[start + count:])
    return wrapped


def _seq_block(s):
    return jnp.where(s == 0, N_PBLK - 1, s - 1)


def _layer_spec(shape, l):
    return pl.BlockSpec((None,) + shape, lambda *_: (l,) + (0,) * len(shape))


def _cast_specs(w, rows, l):
    n, cols = w.shape[1] // rows, w.shape[2]
    chunk = lambda s: jnp.minimum(s, n - 1)
    return (pl.BlockSpec((None, rows, cols), lambda s: (l, chunk(s), 0)),
            pl.BlockSpec((rows, cols), lambda s: (chunk(s), 0)),
            jax.ShapeDtypeStruct(w.shape[1:], BF16))


def _layer_norm_rows(x, g, b):
    mu = jnp.mean(x, -1, keepdims=True)
    xc = x - mu
    var = jnp.mean(xc * xc, -1, keepdims=True)
    return xc * lax.rsqrt(var + LN_EPS) * g + b


def _zero_pad_rows(y, first_row):
    row = first_row + lax.broadcasted_iota(jnp.int32, y.shape, 0)
    return jnp.where((row >= SEQ) & (row < META_ROW), 0.0, y)


def _deepnorm_ln(acc_ref, x_ref, g_ref, b_ref, o_ref, first_row, rows):
    g, b = g_ref[...], b_ref[...]
    for c in range(rows // BLK):
        r = slice(c * BLK, (c + 1) * BLK)
        y = _layer_norm_rows(ALPHA * x_ref[r, :] + acc_ref[r, :], g, b)
        o_ref[r, :] = _zero_pad_rows(y, first_row + c * BLK)


def _dot(a, b):
    return jnp.dot(a, b, preferred_element_type=F32)


def _dot_nt(a, b):
    return lax.dot_general(a, b, (((1,), (1,)), ((), ())), preferred_element_type=F32)


def _dot_tn(a, b):
    return lax.dot_general(a, b, (((0,), (0,)), ((), ())), preferred_element_type=F32)


def _log_sigmoid(x):
    return jnp.minimum(x, 0.0) - jnp.log(1.0 + jnp.exp(-jnp.abs(x)))


LN_IN_TM = 256
assert ROWS - SEQ == LN_IN_TM and SEQ % LN_IN_TM == 0


def _ln_in_kernel(xp_ref, meta_ref, xs_ref, g_ref, b_ref, o_ref):
    i = pl.program_id(0)
    g, b = g_ref[...], b_ref[...]

    @pl.when(i < SEQ // LN_IN_TM)
    def _():
        o_ref[...] = _layer_norm_rows(xp_ref[...], g, b)

    @pl.when(i == SEQ // LN_IN_TM)
    def _():
        o_ref[0:LEAD, :] = jnp.zeros((LEAD, D_MODEL), F32)
        o_ref[LEAD:BLK, :] = _layer_norm_rows(meta_ref[...], g, b)
        o_ref[BLK:, :] = _layer_norm_rows(xs_ref[...], g, b)


def _ln_in(xp, meta, xs, g, b):
    tm = LN_IN_TM
    const = lambda shape: pl.BlockSpec(shape, lambda i: (0,) * len(shape))
    return pl.pallas_call(
        _ln_in_kernel,
        grid=(ROWS // tm,),
        in_specs=[
            pl.BlockSpec((tm, D_MODEL), lambda i: (jnp.minimum(i, SEQ // tm - 1), 0)),
            const((N_META, D_MODEL)), const((N_SAMPLE, D_MODEL)), const((1, D_MODEL)), const((1, D_MODEL)),
        ],
        out_specs=pl.BlockSpec((tm, D_MODEL), lambda i: (i, 0)),
        out_shape=jax.ShapeDtypeStruct((ROWS, D_MODEL), F32),
        compiler_params=_cparams("arbitrary"),
        name="ln_in",
    )(xp, meta, xs, g, b)


IN_TM = 1056


def _in_proj_kernel(x_ref, w_ref, cs_ref, o_ref, xb_ref):
    j = pl.program_id(1)

    @pl.when(j == 0)
    def _():
        xb_ref[...] = x_ref[...].astype(BF16)

    o_ref[...] = _dot(xb_ref[...], w_ref[...])

    @pl.when(j == 0)
    def _():
        c8, s8 = cs_ref[:, 0:ROPE_HALF], cs_ref[:, ROPE_HALF:ROPE_DIM]
        rest = HEAD_DIM - ROPE_DIM
        one, zero = jnp.ones((IN_TM, rest), F32), jnp.zeros((IN_TM, rest), F32)
        z8 = jnp.zeros((IN_TM, ROPE_HALF), F32)
        cos = jnp.concatenate([c8, c8, one] * 2, axis=1)
        sa = jnp.concatenate([-s8, z8, zero] * 2, axis=1)
        sb = jnp.concatenate([z8, s8, zero] * 2, axis=1)
        for c in range(ROPE_COLS // 128):
            blk = o_ref[:, c * 128:(c + 1) * 128]
            o_ref[:, c * 128:(c + 1) * 128] = (
                blk * cos + pltpu.roll(blk, 128 - ROPE_HALF, 1) * sa + pltpu.roll(blk, ROPE_HALF, 1) * sb)

    @pl.when(j == IN_PAD // IN_TILE - 1)
    def _():
        edge = (IN_COLS % IN_TILE) // 128 * 128
        col = edge + lax.broadcasted_iota(jnp.int32, (IN_TM, 128), 1)
        o_ref[:, edge:edge + 128] = jnp.where(col < IN_COLS % IN_TILE, o_ref[:, edge:edge + 128], 0.0)
        o_ref[:, edge + 128:] = jnp.zeros((IN_TM, IN_TILE - edge - 128), F32)


def _in_proj(x, w, cs, l):
    return pl.pallas_call(
        _in_proj_kernel,
        grid=(ROWS // IN_TM, IN_PAD // IN_TILE),
        in_specs=[
            pl.BlockSpec((IN_TM, D_MODEL), lambda i, j: (i, 0)),
            pl.BlockSpec((None, D_MODEL, IN_TILE), lambda i, j: (l, 0, j)),
            pl.BlockSpec((IN_TM, ROPE_DIM), lambda i, j: (i, 0)),
        ],
        out_specs=pl.BlockSpec((IN_TM, IN_TILE), lambda i, j: (i, j)),
        out_shape=jax.ShapeDtypeStruct((ROWS, IN_PAD), F32),
        scratch_shapes=[pltpu.VMEM((IN_TM, D_MODEL), BF16)],
        compiler_params=_cparams("arbitrary", "arbitrary"),
        name="in_proj",
    )(x, w, cs)


def _swa_prompt_kernel(l, sink_ref, q_ref, kp_ref, kc_ref, vp_ref, vc_ref, wsrc_ref, o_ref, wdst_ref):
    s_blk = pl.program_id(0)
    wdst_ref[...] = wsrc_ref[...].astype(BF16)
    k2 = jnp.concatenate([kp_ref[...], kc_ref[...]], axis=0).astype(BF16)
    v2 = jnp.concatenate([vp_ref[...], vc_ref[...]], axis=0).astype(BF16)
    nq = GQA * BLK
    row = lax.broadcasted_iota(jnp.int32, (nq, 2 * BLK), 0) % BLK
    col = lax.broadcasted_iota(jnp.int32, (nq, 2 * BLK), 1)
    diff = row - col + BLK
    kpos = (s_blk - 1) * BLK + col - LEAD
    ok = (diff >= 0) & (diff <= WINDOW) & (kpos >= 0)
    low = lax.broadcasted_iota(jnp.int32, (BLK, 2 * HEAD_DIM), 1) < HEAD_DIM
    g_of_row = lax.broadcasted_iota(jnp.int32, (nq, 1), 0) // BLK
    ones = jnp.ones((2 * BLK, 2 * HEAD_DIM), BF16)
    scores, sinks = [], []
    for kv in range(A_KV_HEADS):
        kh = k2[:, kv * HEAD_DIM:(kv + 1) * HEAD_DIM]
        kk = jnp.concatenate([kh, kh], axis=1)
        parts = []
        for pair in range(GQA // 2):
            c0 = (kv * GQA + 2 * pair) * HEAD_DIM
            qp = q_ref[:, c0:c0 + 2 * HEAD_DIM] * HEAD_DIM ** -0.5
            parts += [jnp.where(low, qp, 0.0).astype(BF16), jnp.where(low, 0.0, qp).astype(BF16)]
        q4 = jnp.concatenate(parts, axis=0)
        sk = jnp.zeros((nq, 1), F32)
        for g in range(GQA):
            sk = jnp.where(g_of_row == g, sink_ref[l, kv * GQA + g], sk)
        scores.append(jnp.where(ok, _dot_nt(q4, kk), NEG_INF))
        sinks.append(sk)
    probs, sink_terms = [], []
    for s, sk in zip(scores, sinks):
        m = jnp.maximum(jnp.max(s, -1, keepdims=True), sk)
        probs.append(jnp.exp(s - m).astype(BF16))
        sink_terms.append(jnp.exp(sk - m))
    for kv in range(A_KV_HEADS):
        vh = v2[:, kv * HEAD_DIM:(kv + 1) * HEAD_DIM]
        v3 = jnp.concatenate([vh, vh, ones], axis=1)
        oa = _dot(probs[kv], v3)
        on = oa[:, 0:2 * HEAD_DIM] / (oa[:, 2 * HEAD_DIM:] + sink_terms[kv])
        for pair in range(GQA // 2):
            c0 = (kv * GQA + 2 * pair) * HEAD_DIM
            lo, hi = on[2 * pair * BLK:(2 * pair + 1) * BLK], on[(2 * pair + 1) * BLK:(2 * pair + 2) * BLK]
            o_ref[:, c0:c0 + 2 * HEAD_DIM] = jnp.where(low, lo, hi).astype(BF16)


def _swa_prompt(sink, h, mix, w_cast, l):
    cur = _seq_block
    prev = lambda s: _seq_block(jnp.maximum(s - 1, 0))
    kvb = KV_WIDTH
    w_in_spec, w_out_spec, w_shape = _cast_specs(w_cast, 32, l)
    n_in = 7
    return pl.pallas_call(
        _drop_inputs(functools.partial(_swa_prompt_kernel, l), n_in, 1),
        grid=(N_PBLK,),
        in_specs=[
            pl.BlockSpec(memory_space=pltpu.SMEM),
            pl.BlockSpec((BLK, A_WIDTH), lambda s: (cur(s), C_Q // A_WIDTH)),
            pl.BlockSpec((BLK, kvb), lambda s: (prev(s), C_K // kvb)),
            pl.BlockSpec((BLK, kvb), lambda s: (cur(s), C_K // kvb)),
            pl.BlockSpec((BLK, kvb), lambda s: (prev(s), C_V // kvb)),
            pl.BlockSpec((BLK, kvb), lambda s: (cur(s), C_V // kvb)),
            w_in_spec,
            _any(),
        ],
        out_specs=[pl.BlockSpec((BLK, A_WIDTH), lambda s: (cur(s), M_A // A_WIDTH)), w_out_spec],
        out_shape=[jax.ShapeDtypeStruct((ROWS, D_MODEL), BF16), w_shape],
        input_output_aliases={n_in: 0},
        compiler_params=_cparams("arbitrary"),
        name="swa_prompt",
    )(sink, h, h, h, h, h, w_cast, mix)


SWA_BT = 16


def _swa_sample_kernel(l, sink_ref, q_ref, kn_ref, vn_ref, ck_ref, cv_ref, o_ref, cko_ref, cvo_ref):
    bt = SWA_BT
    kn, vn = kn_ref[...], vn_ref[...]
    k2 = ck_ref[...].reshape(bt * WINDOW, KV_WIDTH).astype(BF16)
    v2 = cv_ref[...].reshape(bt * WINDOW, KV_WIDTH).astype(BF16)
    row = lax.broadcasted_iota(jnp.int32, (WINDOW, KV_WIDTH), 0)
    for j in range(bt):
        cko_ref[j] = jnp.where(row == WINDOW - 1, kn[j:j + 1, :], pltpu.roll(ck_ref[j], WINDOW - 1, 0))
        cvo_ref[j] = jnp.where(row == WINDOW - 1, vn[j:j + 1, :], pltpu.roll(cv_ref[j], WINDOW - 1, 0))
    nq = GQA * bt
    rowb = lax.broadcasted_iota(jnp.int32, (nq, bt * WINDOW), 0) % bt
    colb = lax.broadcasted_iota(jnp.int32, (nq, bt * WINDOW), 1) // WINDOW
    same = rowb == colb
    rg = lax.broadcasted_iota(jnp.int32, (nq, 1), 0) // bt
    for kv in range(A_KV_HEADS):
        sl = slice(kv * HEAD_DIM, (kv + 1) * HEAD_DIM)
        q4 = jnp.concatenate(
            [q_ref[:, (kv * GQA + g) * HEAD_DIM:(kv * GQA + g + 1) * HEAD_DIM] for g in range(GQA)], axis=0)
        q4 = q4 * HEAD_DIM ** -0.5
        kn4 = jnp.concatenate([kn[:, sl]] * GQA, axis=0)
        vn4 = jnp.concatenate([vn[:, sl]] * GQA, axis=0)
        sk = jnp.zeros((nq, 1), F32)
        for g in range(GQA):
            sk = jnp.where(rg == g, sink_ref[l, kv * GQA + g], sk)
        s = jnp.where(same, _dot_nt(q4.astype(BF16), k2[:, sl]), NEG_INF)
        s_self = jnp.sum(q4 * kn4, -1, keepdims=True)
        m = jnp.maximum(jnp.maximum(jnp.max(s, -1, keepdims=True), s_self), sk)
        p = jnp.exp(s - m)
        p_self = jnp.exp(s_self - m)
        den = jnp.sum(p, -1, keepdims=True) + p_self + jnp.exp(sk - m)
        o = (_dot(p.astype(BF16), v2[:, sl]) + p_self * vn4) / den
        for g in range(GQA):
            hh = kv * GQA + g
            o_ref[:, hh * HEAD_DIM:(hh + 1) * HEAD_DIM] = o[g * bt:(g + 1) * bt].astype(BF16)


def _swa_sample(sink, h, cache_k, cache_v, mix, l):
    bt = SWA_BT
    r0 = ROWS_P // bt
    cache = pl.BlockSpec((None, bt, WINDOW, KV_WIDTH), lambda s: (l, s, 0, 0))
    n_in = 6
    return pl.pallas_call(
        _drop_inputs(functools.partial(_swa_sample_kernel, l), n_in, 1),
        grid=(N_SAMPLE // bt,),
        in_specs=[
            pl.BlockSpec(memory_space=pltpu.SMEM),
            pl.BlockSpec((bt, A_WIDTH), lambda s: (r0 + s, C_Q // A_WIDTH)),
            pl.BlockSpec((bt, KV_WIDTH), lambda s: (r0 + s, C_K // KV_WIDTH)),
            pl.BlockSpec((bt, KV_WIDTH), lambda s: (r0 + s, C_V // KV_WIDTH)),
            cache, cache,
            _any(),
        ],
        out_specs=[pl.BlockSpec((bt, A_WIDTH), lambda s: (r0 + s, M_A // A_WIDTH)), cache, cache],
        out_shape=[jax.ShapeDtypeStruct((ROWS, D_MODEL), BF16),
                   jax.ShapeDtypeStruct(cache_k.shape, F32), jax.ShapeDtypeStruct(cache_v.shape, F32)],
        input_output_aliases={n_in: 0, 4: 1, 5: 2},
        compiler_params=_cparams("arbitrary"),
        name="swa_sample",
    )(sink, h, h, h, cache_k, cache_v, mix)


S5_SHIFTS = (1, 2, 4)
S5_NTAB = 2 * (len(S5_SHIFTS) + 1) * SUB
S5_IN_PIECES = 4
S5_IN_K = S5_WIDTH // S5_IN_PIECES
S5_IN_N = S5_LANES // S5_IN_PIECES
S5_OUT_PIECES = 2
S5_OUT_K = S5_LANES // S5_OUT_PIECES
S5_OUT_N = S5_WIDTH // S5_OUT_PIECES


def _s5_tab(tab_ref, k, lanes):
    return (tab_ref[2 * k * SUB:(2 * k + 1) * SUB, lanes], tab_ref[(2 * k + 1) * SUB:(2 * k + 2) * SUB, lanes])


def _s5_drive(ub, bb_ref, p):
    return _dot(ub[:, p * S5_IN_K:(p + 1) * S5_IN_K], bb_ref[p])


def _s5_readout(hre, him, u, cre_ref, cim_ref, d_ref, wg_ref, bg_ref):
    ys = []
    for j in range(S5_OUT_PIECES):
        lanes = slice(j * S5_OUT_K, (j + 1) * S5_OUT_K)
        ys.append(_dot(hre[:, lanes].astype(BF16), cre_ref[j]) - _dot(him[:, lanes].astype(BF16), cim_ref[j]))
    y = jnp.concatenate(ys, axis=1) + d_ref[...] * u
    z = jax.nn.gelu(y)
    gate = jax.nn.sigmoid(_dot(z.astype(BF16), wg_ref[...]) + bg_ref[...])
    return (z * gate).astype(BF16)


def _s5_prompt_kernel(u_ref, bb_ref, cre_ref, cim_ref, d_ref, wg_ref, bg_ref, tab_ref, wsrc_ref,
                      o_ref, hpre_ref, hpim_ref, wdst_ref, hre, him):
    i = pl.program_id(0)
    wdst_ref[...] = wsrc_ref[...].astype(BF16)

    @pl.when(i == 0)
    def _():
        hpre_ref[...] = jnp.zeros_like(hpre_ref)
        hpim_ref[...] = jnp.zeros_like(hpim_ref)

    u = u_ref[...]
    ub = u.astype(BF16)
    for p in range(S5_IN_PIECES):
        lanes = slice(p * S5_IN_N, (p + 1) * S5_IN_N)
        bu = _s5_drive(ub, bb_ref, p)
        cr, ci = hpre_ref[:, lanes], hpim_ref[:, lanes]
        pr, pi = _s5_tab(tab_ref, len(S5_SHIFTS), lanes)
        for j in range(BLK // SUB):
            rows = slice(j * SUB, (j + 1) * SUB)
            xr, xi = bu[rows, 0:S5_IN_N], bu[rows, S5_IN_N:]
            for k, d in enumerate(S5_SHIFTS):
                ar, ai = _s5_tab(tab_ref, k, lanes)
                sr, si = pltpu.roll(xr, d, 0), pltpu.roll(xi, d, 0)
                xr, xi = xr + ar * sr - ai * si, xi + ar * si + ai * sr
            xr, xi = xr + pr * cr - pi * ci, xi + pr * ci + pi * cr
            hre[rows, lanes] = xr
            him[rows, lanes] = xi
            cr = jnp.broadcast_to(xr[SUB - 1:SUB, :], (SUB, S5_IN_N))
            ci = jnp.broadcast_to(xi[SUB - 1:SUB, :], (SUB, S5_IN_N))
        hpre_ref[:, lanes] = cr
        hpim_ref[:, lanes] = ci
    o_ref[...] = _s5_readout(hre, him, u, cre_ref, cim_ref, d_ref, wg_ref, bg_ref)


def _s5_sample_kernel(u_ref, bb_ref, cre_ref, cim_ref, d_ref, wg_ref, bg_ref, tab_ref, h0re_ref, h0im_ref,
                      o_ref, hsre_ref, hsim_ref):
    u = u_ref[...]
    ub = u.astype(BF16)
    for p in range(S5_IN_PIECES):
        lanes = slice(p * S5_IN_N, (p + 1) * S5_IN_N)
        bu = _s5_drive(ub, bb_ref, p)
        pr, pi = _s5_tab(tab_ref, len(S5_SHIFTS), lanes)
        ar, ai = pr[0:1, :], pi[0:1, :]
        h0r, h0i = h0re_ref[:, lanes], h0im_ref[:, lanes]
        hsre_ref[:, lanes] = ar * h0r - ai * h0i + bu[:, 0:S5_IN_N]
        hsim_ref[:, lanes] = ar * h0i + ai * h0r + bu[:, S5_IN_N:]
    o_ref[...] = _s5_readout(hsre_ref, hsim_ref, u, cre_ref, cim_ref, d_ref, wg_ref, bg_ref)


def _s5_param_specs(l):
    return [
        _layer_spec((S5_IN_PIECES, S5_IN_K, 2 * S5_IN_N), l),
        _layer_spec((S5_OUT_PIECES, S5_OUT_K, S5_OUT_N), l), _layer_spec((S5_OUT_PIECES, S5_OUT_K, S5_OUT_N), l),
        _layer_spec((1, S5_WIDTH), l), _layer_spec((S5_WIDTH, S5_WIDTH), l), _layer_spec((1, S5_WIDTH), l),
        _layer_spec((S5_NTAB, S5_LANES), l),
    ]


def _s5_params(prm):
    return (prm["bb"], prm["c_re"], prm["c_im"], prm["d"], prm["w_glu"], prm["b_glu"], prm["tab"])


def _s5_prompt(h, prm, mix, w_cast, l):
    const = lambda shape: pl.BlockSpec(shape, lambda i: (0,) * len(shape))
    w_in_spec, w_out_spec, w_shape = _cast_specs(w_cast, 176, l)
    n_in = 9
    return pl.pallas_call(
        _drop_inputs(_s5_prompt_kernel, n_in, 1),
        grid=(N_PBLK,),
        in_specs=([pl.BlockSpec((BLK, S5_WIDTH), lambda s: (_seq_block(s), C_U // S5_WIDTH))]
                  + _s5_param_specs(l) + [w_in_spec, _any()]),
        out_specs=[
            pl.BlockSpec((BLK, S5_WIDTH), lambda s: (_seq_block(s), M_B // S5_WIDTH)),
            const((SUB, S5_LANES)), const((SUB, S5_LANES)), w_out_spec,
        ],
        out_shape=[
            jax.ShapeDtypeStruct((ROWS, D_MODEL), BF16),
            jax.ShapeDtypeStruct((SUB, S5_LANES), F32), jax.ShapeDtypeStruct((SUB, S5_LANES), F32), w_shape,
        ],
        scratch_shapes=[pltpu.VMEM((BLK, S5_LANES), F32), pltpu.VMEM((BLK, S5_LANES), F32)],
        input_output_aliases={n_in: 0},
        compiler_params=_cparams("arbitrary"),
        name="s5_prompt",
    )(h, *_s5_params(prm), w_cast, mix)


def _s5_sample(h, prm, h0re, h0im, mix, l):
    const = lambda shape: pl.BlockSpec(shape, lambda i: (0,) * len(shape))
    n_in = 10
    return pl.pallas_call(
        _drop_inputs(_s5_sample_kernel, n_in, 1),
        grid=(1,),
        in_specs=([pl.BlockSpec((N_SAMPLE, S5_WIDTH), lambda i: (N_PBLK, C_U // S5_WIDTH))] + _s5_param_specs(l)
                  + [_layer_spec((N_SAMPLE, S5_LANES), l), _layer_spec((N_SAMPLE, S5_LANES), l), _any()]),
        out_specs=[
            pl.BlockSpec((N_SAMPLE, S5_WIDTH), lambda i: (N_PBLK, M_B // S5_WIDTH)),
            const((N_SAMPLE, S5_LANES)), const((N_SAMPLE, S5_LANES)),
        ],
        out_shape=[
            jax.ShapeDtypeStruct((ROWS, D_MODEL), BF16),
            jax.ShapeDtypeStruct((N_SAMPLE, S5_LANES), F32), jax.ShapeDtypeStruct((N_SAMPLE, S5_LANES), F32),
        ],
        input_output_aliases={n_in: 0},
        compiler_params=_cparams("arbitrary"),
        name="s5_sample",
    )(h, *_s5_params(prm), h0re, h0im, mix)


def _gla_log_gate(ac, wa_ref, ba_ref):
    return _log_sigmoid(_dot(ac.astype(BF16), wa_ref[...]) + ba_ref[...]) * (1.0 / GLA_TAU)


def _gla_finish(o_heads, gate, ng):
    outs = [o * lax.rsqrt(jnp.mean(o * o, -1, keepdims=True) + LN_EPS) for o in o_heads]
    return jnp.concatenate(outs, axis=1) * ng * (gate * jax.nn.sigmoid(gate))


def _gla_prompt_kernel(q_ref, k_ref, v_ref, gate_ref, ac_ref, wa_ref, ba_ref, ng_ref, wsrc_ref,
                       o_ref, s_ref, wdst_ref):
    i = pl.program_id(0)
    wdst_ref[...] = wsrc_ref[...].astype(BF16)

    @pl.when(i == 0)
    def _():
        s_ref[...] = jnp.zeros_like(s_ref)

    lg = _gla_log_gate(ac_ref[...], wa_ref, ba_ref)
    L = GLA_CHUNK
    row = lax.broadcasted_iota(jnp.int32, (L, GLA_KW), 0)
    causal = lax.broadcasted_iota(jnp.int32, (L, L), 0) >= lax.broadcasted_iota(jnp.int32, (L, L), 1)
    chunks = range(BLK // L)
    heads = range(GLA_HEADS)
    ks = lambda hh: slice(hh * GLA_DK, (hh + 1) * GLA_DK)
    vs = lambda hh: slice(hh * GLA_DV, (hh + 1) * GLA_DV)
    q_in, k_out, k_dec, dec, vb = [], [], [], [], []
    for c in chunks:
        rows = slice(c * L, (c + 1) * L)
        b = lg[rows, :]
        d = 1
        while d < L:
            b = b + jnp.where(row >= d, pltpu.roll(b, d, 0), 0.0)
            d *= 2
        b_end = b[L - 1:L, :]
        kc = k_ref[rows, :]
        q_in.append((q_ref[rows, :] * GLA_DK ** -0.5 * jnp.exp(b)).astype(BF16))
        k_out.append((kc * jnp.exp(-b)).astype(BF16))
        k_dec.append((kc * jnp.exp(b_end - b)).astype(BF16))
        dec.append(jnp.transpose(jnp.broadcast_to(jnp.exp(b_end), (GLA_DV, GLA_KW))))
        vb.append(v_ref[rows, :].astype(BF16))
    att = [[jnp.where(causal, _dot_nt(q_in[c][:, ks(hh)], k_out[c][:, ks(hh)]), 0.0).astype(BF16) for hh in heads]
           for c in chunks]
    kv = [[_dot_tn(k_dec[c][:, ks(hh)], vb[c][:, vs(hh)]) for hh in heads] for c in chunks]
    state = [s_ref[...]]
    for c in chunks:
        state.append(dec[c] * state[c] + jnp.concatenate(kv[c], axis=0))
    s_ref[...] = state[-1]
    for c in chunks:
        sb = state[c].astype(BF16)
        o_heads = [_dot(q_in[c][:, ks(hh)], sb[ks(hh), :]) + _dot(att[c][hh], vb[c][:, vs(hh)]) for hh in heads]
        o_ref[c * L:(c + 1) * L, :] = _gla_finish(o_heads, gate_ref[c * L:(c + 1) * L, :], ng_ref[...]).astype(BF16)


def _gla_param_specs(l):
    return [_layer_spec((128, GLA_KW), l), _layer_spec((1, GLA_KW), l), _layer_spec((1, GLA_WIDTH), l)]


def _gla_prompt(h, wa, ba, ng, mix, w_cast, l):
    blk = lambda c, w: pl.BlockSpec((BLK, w), lambda s: (_seq_block(s), c // w))
    w_in_spec, w_out_spec, w_shape = _cast_specs(w_cast, 32, l)
    n_in = 9
    return pl.pallas_call(
        _drop_inputs(_gla_prompt_kernel, n_in, 1),
        grid=(N_PBLK,),
        in_specs=([blk(C_GQ, GLA_KW), blk(C_GK, GLA_KW), blk(C_GV, GLA_WIDTH), blk(C_GG, GLA_WIDTH), blk(C_GA, 128)]
                  + _gla_param_specs(l) + [w_in_spec, _any()]),
        out_specs=[blk(M_C, GLA_WIDTH), pl.BlockSpec((GLA_KW, GLA_DV), lambda s: (0, 0)), w_out_spec],
        out_shape=[jax.ShapeDtypeStruct((ROWS, D_MODEL), BF16), jax.ShapeDtypeStruct((GLA_KW, GLA_DV), F32),
                   w_shape],
        input_output_aliases={n_in: 0},
        compiler_params=_cparams("arbitrary"),
        name="gla_prompt",
    )(h, h, h, h, h, wa, ba, ng, w_cast, mix)


GLA_BT = 16


def _gla_sample_kernel(q_ref, k_ref, v_ref, gate_ref, ac_ref, wa_ref, ba_ref, ng_ref, s_ref, o_ref, so_ref):
    bt = GLA_BT
    lg = _gla_log_gate(ac_ref[...], wa_ref, ba_ref)
    v = v_ref[...]

    def cols(x):
        return jnp.transpose(jnp.concatenate([x, jnp.zeros((128 - bt, GLA_KW), F32)], axis=0))

    eg_t, k_t, q_t = cols(jnp.exp(lg)), cols(k_ref[...]), cols(q_ref[...] * GLA_DK ** -0.5)
    o_rows = []
    for j in range(bt):
        bc = lambda t: jnp.broadcast_to(t[:, j:j + 1], (GLA_KW, GLA_DV))
        vrow = jnp.concatenate(
            [jnp.broadcast_to(v[j:j + 1, hh * GLA_DV:(hh + 1) * GLA_DV], (GLA_DK, GLA_DV)) for hh in range(GLA_HEADS)],
            axis=0)
        s_new = bc(eg_t) * s_ref[j] + bc(k_t) * vrow
        so_ref[j] = s_new
        qs = bc(q_t) * s_new
        o_rows.append(jnp.concatenate(
            [jnp.sum(qs[hh * GLA_DK:(hh + 1) * GLA_DK, :], axis=0, keepdims=True) for hh in range(GLA_HEADS)], axis=1))
    o = jnp.concatenate(o_rows, axis=0)
    o_heads = [o[:, hh * GLA_DV:(hh + 1) * GLA_DV] for hh in range(GLA_HEADS)]
    o_ref[...] = _gla_finish(o_heads, gate_ref[...], ng_ref[...]).astype(BF16)


def _gla_sample(h, wa, ba, ng, state, mix, s_all, l):
    bt = GLA_BT
    r0 = ROWS_P // bt
    blk = lambda c, w: pl.BlockSpec((bt, w), lambda s: (r0 + s, c // w))
    st = pl.BlockSpec((None, bt, GLA_KW, GLA_DV), lambda s: (l, s, 0, 0))
    n_in = 9
    return pl.pallas_call(
        _drop_inputs(_gla_sample_kernel, n_in, 2),
        grid=(N_SAMPLE // bt,),
        in_specs=([blk(C_GQ, GLA_KW), blk(C_GK, GLA_KW), blk(C_GV, GLA_WIDTH), blk(C_GG, GLA_WIDTH), blk(C_GA, 128)]
                  + _gla_param_specs(l) + [st, _any(), _any()]),
        out_specs=[blk(M_C, GLA_WIDTH), st],
        out_shape=[jax.ShapeDtypeStruct((ROWS, D_MODEL), BF16), jax.ShapeDtypeStruct(s_all.shape, F32)],
        input_output_aliases={n_in: 0, n_in + 1: 1},
        compiler_params=_cparams("arbitrary"),
        name="gla_sample",
    )(h, h, h, h, h, wa, ba, ng, state, mix, s_all)


OUT_TM = 768


def _out_proj_kernel(mix_ref, w_ref, x_ref, g_ref, b_ref, o_ref):
    i = pl.program_id(0)
    o_ref[...] = _dot(mix_ref[...], w_ref[...])
    _deepnorm_ln(o_ref, x_ref, g_ref, b_ref, o_ref, i * OUT_TM, OUT_TM)


def _out_proj(mix, w, x, g, b, l):
    tm = OUT_TM
    return pl.pallas_call(
        _out_proj_kernel,
        grid=(ROWS // tm,),
        in_specs=[
            pl.BlockSpec((tm, D_MODEL), lambda i: (i, 0)),
            pl.BlockSpec((D_MODEL, D_MODEL), lambda i: (0, 0), pipeline_mode=pl.Buffered(1)),
            pl.BlockSpec((tm, D_MODEL), lambda i: (i, 0)),
            _layer_spec((1, D_MODEL), l), _layer_spec((1, D_MODEL), l),
        ],
        out_specs=pl.BlockSpec((tm, D_MODEL), lambda i: (i, 0)),
        out_shape=jax.ShapeDtypeStruct((ROWS, D_MODEL), F32),
        compiler_params=_cparams("arbitrary"),
        name="out_proj",
    )(mix, w, x, g, b)


FFN_TM = 768
FFN_TF = 512
FFN_HALO = 16
FFN_NI = ROWS // FFN_TM
assert ROWS % FFN_TM == 0 and FFN_TM >= N_SAMPLE + BLK + SUB and META_ROW % FFN_HALO == 0


def _ffn_kernel(final, x_ref, xh_ref, wu_ref, wg_ref, cw_ref, cb_ref, wd_ref, p0_ref, p1_ref, g_ref, b_ref,
                o_ref, *rest):
    if final:
        ys_ref, ut_ref, us_ref, xb_ref, a_ref = rest
    else:
        ut_ref, us_ref, xb_ref, a_ref = rest
    i, f = pl.program_id(0), pl.program_id(1)
    tm, hl = FFN_TM, FFN_HALO
    last_i = FFN_NI - 1

    @pl.when(f == 0)
    def _():
        xb_ref[...] = x_ref[...].astype(BF16)
        o_ref[...] = jnp.zeros_like(o_ref)

    xb = xb_ref[...]
    u = _dot(xb, wu_ref[...])
    gte = _dot(xb, wg_ref[...])
    cw0, cw1, cw2, cb = cw_ref[0:1, :], cw_ref[1:2, :], cw_ref[2:3, :], cb_ref[...]

    def act(um2, um1, u0, g0):
        return (jax.nn.gelu(cb + cw0 * um2 + cw1 * um1 + cw2 * u0) * g0).astype(BF16)

    a_ref[...] = act(pltpu.roll(u, 2, 0), pltpu.roll(u, 1, 0), u, gte)
    ext = jnp.concatenate([_dot(xh_ref[...].astype(BF16), wu_ref[...]), u[0:hl, :]], axis=0)
    a_ref[0:hl, :] = act(ext[hl - 2:2 * hl - 2, :], ext[hl - 1:2 * hl - 1, :], u[0:hl, :], gte[0:hl, :])

    @pl.when(i == last_i)
    def _():
        a_ref[tm - N_SAMPLE:, :] = act(p0_ref[...], p1_ref[...], u[tm - N_SAMPLE:, :], gte[tm - N_SAMPLE:, :])

    o_ref[...] += _dot(a_ref[...], wd_ref[...])
    ut_ref[...] = u[tm - N_SAMPLE - BLK - SUB:tm - N_SAMPLE - BLK, :]
    us_ref[0] = p1_ref[...]
    us_ref[1] = u[tm - N_SAMPLE:, :]

    @pl.when(f == pl.num_programs(1) - 1)
    def _():
        _deepnorm_ln(o_ref, x_ref, g_ref, b_ref, o_ref, i * tm, tm)
        if final:
            ys_ref[...] = o_ref[tm - N_SAMPLE:, :]


def _ffn(x, wup, cw, cb, wd, conv_p0, conv_p1, g, b, l, final):
    tm, tf = FFN_TM, FFN_TF
    nf = D_FF // tf
    tail = lambda i, f: jnp.where(i == FFN_NI - 1, f, 0)
    halo = lambda i, f: (jnp.where(i == 0, ROWS_P // FFN_HALO, i * (tm // FFN_HALO)) - 1, 0)
    y_spec = pl.BlockSpec((tm, D_MODEL), lambda i, f: (i, 0))
    y_specs = [y_spec, pl.BlockSpec((N_SAMPLE, D_MODEL), lambda i, f: (0, 0))] if final else [y_spec]
    y_shapes = ([jax.ShapeDtypeStruct((SEQ, D_MODEL), F32), jax.ShapeDtypeStruct((N_SAMPLE, D_MODEL), F32)]
                if final else [jax.ShapeDtypeStruct((ROWS, D_MODEL), F32)])
    return pl.pallas_call(
        functools.partial(_ffn_kernel, final),
        grid=(FFN_NI, nf),
        in_specs=[
            pl.BlockSpec((tm, D_MODEL), lambda i, f: (i, 0)),
            pl.BlockSpec((FFN_HALO, D_MODEL), halo),
            pl.BlockSpec((D_MODEL, tf), lambda i, f: (0, f)),
            pl.BlockSpec((D_MODEL, tf), lambda i, f: (0, nf + f)),
            pl.BlockSpec((None, CONV_W, tf), lambda i, f: (l, 0, f)),
            pl.BlockSpec((None, 1, tf), lambda i, f: (l, 0, f)),
            pl.BlockSpec((tf, D_MODEL), lambda i, f: (f, 0)),
            pl.BlockSpec((None, N_SAMPLE, tf), lambda i, f: (l, 0, tail(i, f))),
            pl.BlockSpec((None, N_SAMPLE, tf), lambda i, f: (l, 0, tail(i, f))),
            _layer_spec((1, D_MODEL), l), _layer_spec((1, D_MODEL), l),
        ],
        out_specs=y_specs + [
            pl.BlockSpec((SUB, tf), lambda i, f: (0, tail(i, f))),
            pl.BlockSpec((2, N_SAMPLE, tf), lambda i, f: (0, 0, tail(i, f))),
        ],
        out_shape=y_shapes + [
            jax.ShapeDtypeStruct((SUB, D_FF), F32),
            jax.ShapeDtypeStruct((2, N_SAMPLE, D_FF), F32),
        ],
        scratch_shapes=[pltpu.VMEM((tm, D_MODEL), BF16), pltpu.VMEM((tm, tf), BF16)],
        compiler_params=_cparams("arbitrary", "arbitrary"),
        name="conv_ffn",
    )(x, x, wup, wup, cw, cb, wd, conv_p0, conv_p1, g, b)


def _rope_table():
    inv = jnp.power(ROPE_THETA, -jnp.arange(ROPE_HALF, dtype=F32) / ROPE_HALF)
    pos = jnp.concatenate([jnp.arange(SEQ, dtype=jnp.int32) + N_META, jnp.zeros((LEAD,), jnp.int32),
                           jnp.arange(N_META, dtype=jnp.int32), jnp.full((N_SAMPLE,), PAST_LEN, jnp.int32)])
    ang = pos.astype(F32)[:, None] * inv[None, :]
    return jnp.concatenate([jnp.cos(ang), jnp.sin(ang)], axis=1)


def _s5_tables(lam_re, lam_im, log_dt, b_re, b_im, c_re, c_im, d, w_glu, b_glu):
    L = lam_re.shape[0]
    dt = jnp.exp(log_dt)[..., None]
    mag = jnp.exp(lam_re * dt)
    ab_re, ab_im = mag * jnp.cos(lam_im * dt), mag * jnp.sin(lam_im * dt)
    den = lam_re * lam_re + lam_im * lam_im
    nr, ni = ab_re - 1.0, ab_im
    f_re = (nr * lam_re + ni * lam_im) / den
    f_im = (ni * lam_re - nr * lam_im) / den
    bb_re = f_re[..., None] * b_re - f_im[..., None] * b_im
    bb_im = f_re[..., None] * b_im + f_im[..., None] * b_re
    gi, go = S5_GROUPS // S5_IN_PIECES, S5_GROUPS // S5_OUT_PIECES

    def bd_in(t):
        t = t.reshape(L, S5_IN_PIECES, gi, S5_STATE, S5_CH)
        return jnp.einsum('lqgph,gk->lqghkp', t, jnp.eye(gi, dtype=F32)).reshape(L, S5_IN_PIECES, S5_IN_K, S5_IN_N)

    def bd_out(t):
        t = t.reshape(L, S5_OUT_PIECES, go, S5_CH, S5_STATE)
        return jnp.einsum('lqghp,gk->lqgpkh', t, jnp.eye(go, dtype=F32)).reshape(
            L, S5_OUT_PIECES, S5_OUT_K, S5_OUT_N)

    ar, ai = ab_re.reshape(L, 1, S5_LANES), ab_im.reshape(L, 1, S5_LANES)
    pows = [(ar, ai)]
    for _ in range(SUB - 1):
        pr, pi = pows[-1]
        pows.append((pr * ar - pi * ai, pr * ai + pi * ar))
    row = jnp.arange(SUB)[None, :, None]
    parts = []
    for sft in S5_SHIFTS:
        pr, pi = pows[sft - 1]
        parts += [jnp.where(row >= sft, pr, 0.0), jnp.where(row >= sft, pi, 0.0)]
    parts += [jnp.concatenate([p[0] for p in pows], 1), jnp.concatenate([p[1] for p in pows], 1)]
    return dict(bb=jnp.concatenate([bd_in(bb_re), bd_in(bb_im)], axis=3).astype(BF16),
                c_re=bd_out(c_re).astype(BF16), c_im=bd_out(c_im).astype(BF16),
                d=d.reshape(L, 1, S5_WIDTH), w_glu=w_glu.astype(BF16), b_glu=b_glu.reshape(L, 1, S5_WIDTH),
                tab=jnp.concatenate(parts, 1))


@jax.jit
def kernel(x_prompt, x_sample, cache_swa_k, cache_swa_v, state_ssm_re, state_ssm_im, state_gla, state_conv,
           meta_tokens, ln_in_g, ln_in_b, w_in, attn_sink, s5_lam_re, s5_lam_im, s5_log_dt, s5_b_re, s5_b_im,
           s5_c_re, s5_c_im, s5_d, s5_w_glu, s5_b_glu, gla_w_a2, gla_b_a, gla_norm_g, w_out, ln1_g, ln1_b,
           ffn_w_up, ffn_conv_w, ffn_conv_b, ffn_w_down, ln2_g, ln2_b):
    L = DEPTH
    row3 = lambda t: t.reshape(L, 1, -1)
    w_in_b = w_in.astype(BF16)
    wa_b = jnp.pad(gla_w_a2, ((0, 0), (0, 128 - GLA_LOWRANK), (0, 0))).astype(BF16)
    gla_args = (wa_b, row3(gla_b_a), row3(gla_norm_g))
    ln1, ln2 = (row3(ln1_g), row3(ln1_b)), (row3(ln2_g), row3(ln2_b))
    conv_b3 = row3(ffn_conv_b)
    cs = _rope_table()
    s5p = _s5_tables(s5_lam_re, s5_lam_im, s5_log_dt, s5_b_re, s5_b_im, s5_c_re, s5_c_im, s5_d, s5_w_glu, s5_b_glu)
    ck = cache_swa_k.reshape(L, N_SAMPLE, WINDOW, KV_WIDTH)
    cv = cache_swa_v.reshape(L, N_SAMPLE, WINDOW, KV_WIDTH)
    h0re = state_ssm_re.reshape(L, N_SAMPLE, S5_LANES)
    h0im = state_ssm_im.reshape(L, N_SAMPLE, S5_LANES)
    sg = state_gla.reshape(L, N_SAMPLE, GLA_KW, GLA_DV)
    conv_p0, conv_p1 = state_conv[:, :, 0, :], state_conv[:, :, 1, :]
    mix = jnp.zeros((ROWS, D_MODEL), BF16)
    s_all = jnp.zeros((L, N_SAMPLE, GLA_KW, GLA_DV), F32)

    x = _ln_in(x_prompt.reshape(SEQ, D_MODEL), meta_tokens, x_sample.reshape(N_SAMPLE, D_MODEL),
               ln_in_g.reshape(1, D_MODEL), ln_in_b.reshape(1, D_MODEL))
    outs = [[] for _ in range(9)]
    for l in range(L):
        h = _in_proj(x, w_in_b, cs, l)
        mix, w_up_b = _swa_prompt(attn_sink, h, mix, ffn_w_up, l)
        mix, ck, cv = _swa_sample(attn_sink, h, ck, cv, mix, l)
        mix, hp_re, hp_im, w_down_b = _s5_prompt(h, s5p, mix, ffn_w_down, l)
        mix, hs_re, hs_im = _s5_sample(h, s5p, h0re, h0im, mix, l)
        mix, s_p, w_out_b = _gla_prompt(h, *gla_args, mix, w_out, l)
        mix, s_all = _gla_sample(h, *gla_args, sg, mix, s_all, l)
        x = _out_proj(mix, w_out_b, x, *ln1, l)
        res = _ffn(x, w_up_b, ffn_conv_w, conv_b3, w_down_b, conv_p0, conv_p1, *ln2, l, l == L - 1)
        if l == L - 1:
            y_prompt, y_sample, u_tail, u_s = res
        else:
            x, u_tail, u_s = res
        kv_p = lambda c: h[SEQ - WINDOW:SEQ, c:c + KV_WIDTH].reshape(1, WINDOW, A_KV_HEADS, HEAD_DIM)
        new = (kv_p(C_K), kv_p(C_V),
               hp_re[SUB - 1].reshape(1, S5_GROUPS, S5_STATE), hp_im[SUB - 1].reshape(1, S5_GROUPS, S5_STATE),
               s_p.reshape(1, GLA_HEADS, GLA_DK, GLA_DV), u_tail[SUB - 2:SUB][None],
               hs_re.reshape(N_SAMPLE, S5_GROUPS, S5_STATE), hs_im.reshape(N_SAMPLE, S5_GROUPS, S5_STATE),
               jnp.transpose(u_s, (1, 0, 2)))
        for lst, val in zip(outs, new):
            lst.append(val)
    st = [jnp.stack(lst) for lst in outs]
    return (y_prompt.reshape(1, SEQ, D_MODEL), y_sample.reshape(N_SAMPLE, 1, D_MODEL),
            st[0], st[1], st[2], st[3], st[4], st[5],
            ck.reshape(L, N_SAMPLE, WINDOW, A_KV_HEADS, HEAD_DIM),
            cv.reshape(L, N_SAMPLE, WINDOW, A_KV_HEADS, HEAD_DIM),
            st[6], st[7], s_all.reshape(L, N_SAMPLE, GLA_HEADS, GLA_DK, GLA_DV), st[8])
```

```python
import functools

import jax
import jax.numpy as jnp
from jax import lax
from jax.experimental import pallas as pl
from jax.experimental.pallas import tpu as pltpu

F32 = jnp.float32
BF16 = jnp.bfloat16

D_MODEL = 2048
SEQ = 8192
DEPTH = 2
N_SAMPLE = 128
N_META = 16
HEAD_DIM = 64
A_WIDTH = 1024
A_HEADS = 16
A_KV_HEADS = 4
GQA = 4
KV_WIDTH = A_KV_HEADS * HEAD_DIM
WINDOW = 128
ROPE_DIM = 16
ROPE_HALF = ROPE_DIM // 2
ROPE_THETA = 500000.0
PAST_LEN = 8192
S5_WIDTH = 512
S5_CH = 16
S5_GROUPS = 32
S5_STATE = 64
S5_LANES = S5_GROUPS * S5_STATE
GLA_WIDTH = 512
GLA_HEADS = 4
GLA_DV = 128
GLA_DK = 64
GLA_KW = GLA_HEADS * GLA_DK
GLA_LOWRANK = 16
GLA_TAU = 16.0
GLA_CHUNK = 64
D_FF = 5632
CONV_W = 3
LN_EPS = 1e-5
ALPHA = (2 * DEPTH) ** 0.25
NEG_INF = -1e30

BLK = 128
T_PROMPT = N_META + SEQ
LEAD = (-T_PROMPT) % BLK
N_PBLK = (LEAD + T_PROMPT) // BLK
ROWS_P = N_PBLK * BLK
ROWS = ROWS_P + N_SAMPLE
META_ROW = SEQ + LEAD
SUB = 8

C_Q, C_K, C_V, C_U, C_GQ, C_GK, C_GV, C_GG, C_GA = 0, 1024, 1280, 1536, 2048, 2304, 2560, 3072, 3584
IN_COLS = 3600
IN_TILE = 1280
IN_PAD = 3 * IN_TILE
ROPE_COLS = A_WIDTH + KV_WIDTH
M_A, M_B, M_C = 0, A_WIDTH, A_WIDTH + S5_WIDTH

VMEM_LIMIT = 56 * 1024 * 1024


def _cparams(*sem):
    return pltpu.CompilerParams(dimension_semantics=sem, vmem_limit_bytes=VMEM_LIMIT)


def _any():
    return pl.BlockSpec(memory_space=pl.ANY)


def _drop_inputs(kern, start, count):
    def wrapped(*refs):
        return kern(*refs[:start], *refs[start + count:])
    return wrapped


def _seq_block(s):
    return jnp.where(s == 0, N_PBLK - 1, s - 1)


def _layer_spec(shape, l):
    return pl.BlockSpec((None,) + shape, lambda *_: (l,) + (0,) * len(shape))


def _cast_specs(w, rows, l):
    n, cols = w.shape[1] // rows, w.shape[2]
    chunk = lambda s: jnp.minimum(s, n - 1)
    return (pl.BlockSpec((None, rows, cols), lambda s: (l, chunk(s), 0)),
            pl.BlockSpec((rows, cols), lambda s: (chunk(s), 0)),
            jax.ShapeDtypeStruct(w.shape[1:], BF16))


def _layer_norm_rows(x, g, b):
    mu = jnp.mean(x, -1, keepdims=True)
    xc = x - mu
    var = jnp.mean(xc * xc, -1, keepdims=True)
    return xc * lax.rsqrt(var + LN_EPS) * g + b


def _zero_pad_rows(y, first_row):
    row = first_row + lax.broadcasted_iota(jnp.int32, y.shape, 0)
    return jnp.where((row >= SEQ) & (row < META_ROW), 0.0, y)


def _deepnorm_ln(acc_ref, x_ref, g_ref, b_ref, o_ref, first_row, rows):
    g, b = g_ref[...], b_ref[...]
    for c in range(rows // BLK):
        r = slice(c * BLK, (c + 1) * BLK)
        y = _layer_norm_rows(ALPHA * x_ref[r, :] + acc_ref[r, :], g, b)
        o_ref[r, :] = _zero_pad_rows(y, first_row + c * BLK)


def _dot(a, b):
    return jnp.dot(a, b, preferred_element_type=F32)


def _dot_nt(a, b):
    return lax.dot_general(a, b, (((1,), (1,)), ((), ())), preferred_element_type=F32)


def _dot_tn(a, b):
    return lax.dot_general(a, b, (((0,), (0,)), ((), ())), preferred_element_type=F32)


def _log_sigmoid(x):
    return jnp.minimum(x, 0.0) - jnp.log(1.0 + jnp.exp(-jnp.abs(x)))


LN_IN_TM = 256
assert ROWS - SEQ == LN_IN_TM and SEQ % LN_IN_TM == 0


def _ln_in_kernel(xp_ref, meta_ref, xs_ref, g_ref, b_ref, o_ref):
    i = pl.program_id(0)
    g, b = g_ref[...], b_ref[...]

    @pl.when(i < SEQ // LN_IN_TM)
    def _():
        o_ref[...] = _layer_norm_rows(xp_ref[...], g, b)

    @pl.when(i == SEQ // LN_IN_TM)
    def _():
        o_ref[0:LEAD, :] = jnp.zeros((LEAD, D_MODEL), F32)
        o_ref[LEAD:BLK, :] = _layer_norm_rows(meta_ref[...], g, b)
        o_ref[BLK:, :] = _layer_norm_rows(xs_ref[...], g, b)


def _ln_in(xp, meta, xs, g, b):
    tm = LN_IN_TM
    const = lambda shape: pl.BlockSpec(shape, lambda i: (0,) * len(shape))
    return pl.pallas_call(
        _ln_in_kernel,
        grid=(ROWS // tm,),
        in_specs=[
            pl.BlockSpec((tm, D_MODEL), lambda i: (jnp.minimum(i, SEQ // tm - 1), 0)),
            const((N_META, D_MODEL)), const((N_SAMPLE, D_MODEL)), const((1, D_MODEL)), const((1, D_MODEL)),
        ],
        out_specs=pl.BlockSpec((tm, D_MODEL), lambda i: (i, 0)),
        out_shape=jax.ShapeDtypeStruct((ROWS, D_MODEL), F32),
        compiler_params=_cparams("arbitrary"),
        name="ln_in",
    )(xp, meta, xs, g, b)


IN_TM = 1056


def _in_proj_kernel(x_ref, w_ref, cs_ref, o_ref, xb_ref):
    j = pl.program_id(1)

    @pl.when(j == 0)
    def _():
        xb_ref[...] = x_ref[...].astype(BF16)

    o_ref[...] = _dot(xb_ref[...], w_ref[...])

    @pl.when(j == 0)
    def _():
        c8, s8 = cs_ref[:, 0:ROPE_HALF], cs_ref[:, ROPE_HALF:ROPE_DIM]
        rest = HEAD_DIM - ROPE_DIM
        one, zero = jnp.ones((IN_TM, rest), F32), jnp.zeros((IN_TM, rest), F32)
        z8 = jnp.zeros((IN_TM, ROPE_HALF), F32)
        cos = jnp.concatenate([c8, c8, one] * 2, axis=1)
        sa = jnp.concatenate([-s8, z8, zero] * 2, axis=1)
        sb = jnp.concatenate([z8, s8, zero] * 2, axis=1)
        for c in range(ROPE_COLS // 128):
            blk = o_ref[:, c * 128:(c + 1) * 128]
            o_ref[:, c * 128:(c + 1) * 128] = (
                blk * cos + pltpu.roll(blk, 128 - ROPE_HALF, 1) * sa + pltpu.roll(blk, ROPE_HALF, 1) * sb)

    @pl.when(j == IN_PAD // IN_TILE - 1)
    def _():
        edge = (IN_COLS % IN_TILE) // 128 * 128
        col = edge + lax.broadcasted_iota(jnp.int32, (IN_TM, 128), 1)
        o_ref[:, edge:edge + 128] = jnp.where(col < IN_COLS % IN_TILE, o_ref[:, edge:edge + 128], 0.0)
        o_ref[:, edge + 128:] = jnp.zeros((IN_TM, IN_TILE - edge - 128), F32)


def _in_proj(x, w, cs, l):
    return pl.pallas_call(
        _in_proj_kernel,
        grid=(ROWS // IN_TM, IN_PAD // IN_TILE),
        in_specs=[
            pl.BlockSpec((IN_TM, D_MODEL), lambda i, j: (i, 0)),
            pl.BlockSpec((None, D_MODEL, IN_TILE), lambda i, j: (l, 0, j)),
            pl.BlockSpec((IN_TM, ROPE_DIM), lambda i, j: (i, 0)),
        ],
        out_specs=pl.BlockSpec((IN_TM, IN_TILE), lambda i, j: (i, j)),
        out_shape=jax.ShapeDtypeStruct((ROWS, IN_PAD), F32),
        scratch_shapes=[pltpu.VMEM((IN_TM, D_MODEL), BF16)],
        compiler_params=_cparams("arbitrary", "arbitrary"),
        name="in_proj",
    )(x, w, cs)


def _swa_block(l, s_blk, sink_ref, q_ref, kp_ref, kc_ref, vp_ref, vc_ref, o_ref):
    k2 = jnp.concatenate([kp_ref[...], kc_ref[...]], axis=0).astype(BF16)
    v2 = jnp.concatenate([vp_ref[...], vc_ref[...]], axis=0).astype(BF16)
    nq = GQA * BLK
    row = lax.broadcasted_iota(jnp.int32, (nq, 2 * BLK), 0) % BLK
    col = lax.broadcasted_iota(jnp.int32, (nq, 2 * BLK), 1)
    diff = row - col + BLK
    kpos = (s_blk - 1) * BLK + col - LEAD
    ok = (diff >= 0) & (diff <= WINDOW) & (kpos >= 0)
    low = lax.broadcasted_iota(jnp.int32, (BLK, 2 * HEAD_DIM), 1) < HEAD_DIM
    g_of_row = lax.broadcasted_iota(jnp.int32, (nq, 1), 0) // BLK
    ones = jnp.ones((2 * BLK, 2 * HEAD_DIM), BF16)
    scores, sinks = [], []
    for kv in range(A_KV_HEADS):
        kh = k2[:, kv * HEAD_DIM:(kv + 1) * HEAD_DIM]
        kk = jnp.concatenate([kh, kh], axis=1)
        parts = []
        for pair in range(GQA // 2):
            c0 = (kv * GQA + 2 * pair) * HEAD_DIM
            qp = q_ref[:, c0:c0 + 2 * HEAD_DIM] * HEAD_DIM ** -0.5
            parts += [jnp.where(low, qp, 0.0).astype(BF16), jnp.where(low, 0.0, qp).astype(BF16)]
        q4 = jnp.concatenate(parts, axis=0)
        sk = jnp.zeros((nq, 1), F32)
        for g in range(GQA):
            sk = jnp.where(g_of_row == g, sink_ref[l, kv * GQA + g], sk)
        scores.append(jnp.where(ok, _dot_nt(q4, kk), NEG_INF))
        sinks.append(sk)
    probs, sink_terms = [], []
    for s, sk in zip(scores, sinks):
        m = jnp.maximum(jnp.max(s, -1, keepdims=True), sk)
        probs.append(jnp.exp(s - m).astype(BF16))
        sink_terms.append(jnp.exp(sk - m))
    for kv in range(A_KV_HEADS):
        vh = v2[:, kv * HEAD_DIM:(kv + 1) * HEAD_DIM]
        v3 = jnp.concatenate([vh, vh, ones], axis=1)
        oa = _dot(probs[kv], v3)
        on = oa[:, 0:2 * HEAD_DIM] / (oa[:, 2 * HEAD_DIM:] + sink_terms[kv])
        for pair in range(GQA // 2):
            c0 = (kv * GQA + 2 * pair) * HEAD_DIM
            lo, hi = on[2 * pair * BLK:(2 * pair + 1) * BLK], on[(2 * pair + 1) * BLK:(2 * pair + 2) * BLK]
            o_ref[:, M_A + c0:M_A + c0 + 2 * HEAD_DIM] = jnp.where(low, lo, hi).astype(BF16)


SWA_BT = 16


def _swa_sample_kernel(l, sink_ref, q_ref, kn_ref, vn_ref, ck_ref, cv_ref, o_ref, cko_ref, cvo_ref):
    bt = SWA_BT
    kn, vn = kn_ref[...], vn_ref[...]
    k2 = ck_ref[...].reshape(bt * WINDOW, KV_WIDTH).astype(BF16)
    v2 = cv_ref[...].reshape(bt * WINDOW, KV_WIDTH).astype(BF16)
    row = lax.broadcasted_iota(jnp.int32, (WINDOW, KV_WIDTH), 0)
    for j in range(bt):
        cko_ref[j] = jnp.where(row == WINDOW - 1, kn[j:j + 1, :], pltpu.roll(ck_ref[j], WINDOW - 1, 0))
        cvo_ref[j] = jnp.where(row == WINDOW - 1, vn[j:j + 1, :], pltpu.roll(cv_ref[j], WINDOW - 1, 0))
    nq = GQA * bt
    rowb = lax.broadcasted_iota(jnp.int32, (nq, bt * WINDOW), 0) % bt
    colb = lax.broadcasted_iota(jnp.int32, (nq, bt * WINDOW), 1) // WINDOW
    same = rowb == colb
    rg = lax.broadcasted_iota(jnp.int32, (nq, 1), 0) // bt
    for kv in range(A_KV_HEADS):
        sl = slice(kv * HEAD_DIM, (kv + 1) * HEAD_DIM)
        q4 = jnp.concatenate(
            [q_ref[:, (kv * GQA + g) * HEAD_DIM:(kv * GQA + g + 1) * HEAD_DIM] for g in range(GQA)], axis=0)
        q4 = q4 * HEAD_DIM ** -0.5
        kn4 = jnp.concatenate([kn[:, sl]] * GQA, axis=0)
        vn4 = jnp.concatenate([vn[:, sl]] * GQA, axis=0)
        sk = jnp.zeros((nq, 1), F32)
        for g in range(GQA):
            sk = jnp.where(rg == g, sink_ref[l, kv * GQA + g], sk)
        s = jnp.where(same, _dot_nt(q4.astype(BF16), k2[:, sl]), NEG_INF)
        s_self = jnp.sum(q4 * kn4, -1, keepdims=True)
        m = jnp.maximum(jnp.maximum(jnp.max(s, -1, keepdims=True), s_self), sk)
        p = jnp.exp(s - m)
        p_self = jnp.exp(s_self - m)
        den = jnp.sum(p, -1, keepdims=True) + p_self + jnp.exp(sk - m)
        o = (_dot(p.astype(BF16), v2[:, sl]) + p_self * vn4) / den
        for g in range(GQA):
            hh = kv * GQA + g
            o_ref[:, hh * HEAD_DIM:(hh + 1) * HEAD_DIM] = o[g * bt:(g + 1) * bt].astype(BF16)


def _swa_sample(sink, h, cache_k, cache_v, mix, l):
    bt = SWA_BT
    r0 = ROWS_P // bt
    cache = pl.BlockSpec((None, bt, WINDOW, KV_WIDTH), lambda s: (l, s, 0, 0))
    n_in = 6
    return pl.pallas_call(
        _drop_inputs(functools.partial(_swa_sample_kernel, l), n_in, 1),
        grid=(N_SAMPLE // bt,),
        in_specs=[
            pl.BlockSpec(memory_space=pltpu.SMEM),
            pl.BlockSpec((bt, A_WIDTH), lambda s: (r0 + s, C_Q // A_WIDTH)),
            pl.BlockSpec((bt, KV_WIDTH), lambda s: (r0 + s, C_K // KV_WIDTH)),
            pl.BlockSpec((bt, KV_WIDTH), lambda s: (r0 + s, C_V // KV_WIDTH)),
            cache, cache,
            _any(),
        ],
        out_specs=[pl.BlockSpec((bt, A_WIDTH), lambda s: (r0 + s, M_A // A_WIDTH)), cache, cache],
        out_shape=[jax.ShapeDtypeStruct((ROWS, D_MODEL), BF16),
                   jax.ShapeDtypeStruct(cache_k.shape, F32), jax.ShapeDtypeStruct(cache_v.shape, F32)],
        input_output_aliases={n_in: 0, 4: 1, 5: 2},
        compiler_params=_cparams("arbitrary"),
        name="swa_sample",
    )(sink, h, h, h, cache_k, cache_v, mix)


S5_SHIFTS = (1, 2, 4)
S5_NTAB = 2 * (len(S5_SHIFTS) + 1) * SUB
S5_IN_PIECES = 4
S5_IN_K = S5_WIDTH // S5_IN_PIECES
S5_IN_N = S5_LANES // S5_IN_PIECES
S5_OUT_PIECES = 2
S5_OUT_K = S5_LANES // S5_OUT_PIECES
S5_OUT_N = S5_WIDTH // S5_OUT_PIECES


def _s5_tab(tab_ref, k, lanes):
    return (tab_ref[2 * k * SUB:(2 * k + 1) * SUB, lanes], tab_ref[(2 * k + 1) * SUB:(2 * k + 2) * SUB, lanes])


def _s5_drive(ub, bb_ref, p):
    return _dot(ub[:, p * S5_IN_K:(p + 1) * S5_IN_K], bb_ref[p])


def _s5_readout(hre, him, u, cre_ref, cim_ref, d_ref, wg_ref, bg_ref):
    ys = []
    for j in range(S5_OUT_PIECES):
        lanes = slice(j * S5_OUT_K, (j + 1) * S5_OUT_K)
        ys.append(_dot(hre[:, lanes].astype(BF16), cre_ref[j]) - _dot(him[:, lanes].astype(BF16), cim_ref[j]))
    y = jnp.concatenate(ys, axis=1) + d_ref[...] * u
    z = jax.nn.gelu(y)
    gate = jax.nn.sigmoid(_dot(z.astype(BF16), wg_ref[...]) + bg_ref[...])
    return (z * gate).astype(BF16)


def _s5_block(u_ref, bb_ref, cre_ref, cim_ref, d_ref, wg_ref, bg_ref, tab_ref, o_ref, hpre_ref, hpim_ref, hre, him):
    u = u_ref[...]
    ub = u.astype(BF16)
    for p in range(S5_IN_PIECES):
        lanes = slice(p * S5_IN_N, (p + 1) * S5_IN_N)
        bu = _s5_drive(ub, bb_ref, p)
        cr, ci = hpre_ref[:, lanes], hpim_ref[:, lanes]
        pr, pi = _s5_tab(tab_ref, len(S5_SHIFTS), lanes)
        for j in range(BLK // SUB):
            rows = slice(j * SUB, (j + 1) * SUB)
            xr, xi = bu[rows, 0:S5_IN_N], bu[rows, S5_IN_N:]
            for k, d in enumerate(S5_SHIFTS):
                ar, ai = _s5_tab(tab_ref, k, lanes)
                sr, si = pltpu.roll(xr, d, 0), pltpu.roll(xi, d, 0)
                xr, xi = xr + ar * sr - ai * si, xi + ar * si + ai * sr
            xr, xi = xr + pr * cr - pi * ci, xi + pr * ci + pi * cr
            hre[rows, lanes] = xr
            him[rows, lanes] = xi
            cr = jnp.broadcast_to(xr[SUB - 1:SUB, :], (SUB, S5_IN_N))
            ci = jnp.broadcast_to(xi[SUB - 1:SUB, :], (SUB, S5_IN_N))
        hpre_ref[:, lanes] = cr
        hpim_ref[:, lanes] = ci
    o_ref[:, M_B:M_B + S5_WIDTH] = _s5_readout(hre, him, u, cre_ref, cim_ref, d_ref, wg_ref, bg_ref)


def _s5_sample_kernel(u_ref, bb_ref, cre_ref, cim_ref, d_ref, wg_ref, bg_ref, tab_ref, h0re_ref, h0im_ref,
                      o_ref, hsre_ref, hsim_ref):
    u = u_ref[...]
    ub = u.astype(BF16)
    for p in range(S5_IN_PIECES):
        lanes = slice(p * S5_IN_N, (p + 1) * S5_IN_N)
        bu = _s5_drive(ub, bb_ref, p)
        pr, pi = _s5_tab(tab_ref, len(S5_SHIFTS), lanes)
        ar, ai = pr[0:1, :], pi[0:1, :]
        h0r, h0i = h0re_ref[:, lanes], h0im_ref[:, lanes]
        hsre_ref[:, lanes] = ar * h0r - ai * h0i + bu[:, 0:S5_IN_N]
        hsim_ref[:, lanes] = ar * h0i + ai * h0r + bu[:, S5_IN_N:]
    o_ref[...] = _s5_readout(hsre_ref, hsim_ref, u, cre_ref, cim_ref, d_ref, wg_ref, bg_ref)


def _s5_param_specs(l):
    return [
        _layer_spec((S5_IN_PIECES, S5_IN_K, 2 * S5_IN_N), l),
        _layer_spec((S5_OUT_PIECES, S5_OUT_K, S5_OUT_N), l), _layer_spec((S5_OUT_PIECES, S5_OUT_K, S5_OUT_N), l),
        _layer_spec((1, S5_WIDTH), l), _layer_spec((S5_WIDTH, S5_WIDTH), l), _layer_spec((1, S5_WIDTH), l),
        _layer_spec((S5_NTAB, S5_LANES), l),
    ]


def _s5_params(prm):
    return (prm["bb"], prm["c_re"], prm["c_im"], prm["d"], prm["w_glu"], prm["b_glu"], prm["tab"])


def _s5_sample(h, prm, h0re, h0im, mix, l):
    const = lambda shape: pl.BlockSpec(shape, lambda i: (0,) * len(shape))
    n_in = 10
    return pl.pallas_call(
        _drop_inputs(_s5_sample_kernel, n_in, 1),
        grid=(1,),
        in_specs=([pl.BlockSpec((N_SAMPLE, S5_WIDTH), lambda i: (N_PBLK, C_U // S5_WIDTH))] + _s5_param_specs(l)
                  + [_layer_spec((N_SAMPLE, S5_LANES), l), _layer_spec((N_SAMPLE, S5_LANES), l), _any()]),
        out_specs=[
            pl.BlockSpec((N_SAMPLE, S5_WIDTH), lambda i: (N_PBLK, M_B // S5_WIDTH)),
            const((N_SAMPLE, S5_LANES)), const((N_SAMPLE, S5_LANES)),
        ],
        out_shape=[
            jax.ShapeDtypeStruct((ROWS, D_MODEL), BF16),
            jax.ShapeDtypeStruct((N_SAMPLE, S5_LANES), F32), jax.ShapeDtypeStruct((N_SAMPLE, S5_LANES), F32),
        ],
        input_output_aliases={n_in: 0},
        compiler_params=_cparams("arbitrary"),
        name="s5_sample",
    )(h, *_s5_params(prm), h0re, h0im, mix)


def _gla_log_gate(ac, wa_ref, ba_ref):
    return _log_sigmoid(_dot(ac.astype(BF16), wa_ref[...]) + ba_ref[...]) * (1.0 / GLA_TAU)


def _gla_finish(o_heads, gate, ng):
    outs = [o * lax.rsqrt(jnp.mean(o * o, -1, keepdims=True) + LN_EPS) for o in o_heads]
    return jnp.concatenate(outs, axis=1) * ng * (gate * jax.nn.sigmoid(gate))


def _gla_block(q_ref, k_ref, v_ref, gate_ref, ac_ref, wa_ref, ba_ref, ng_ref, o_ref, s_ref):
    lg = _gla_log_gate(ac_ref[...], wa_ref, ba_ref)
    L = GLA_CHUNK
    row = lax.broadcasted_iota(jnp.int32, (L, GLA_KW), 0)
    causal = lax.broadcasted_iota(jnp.int32, (L, L), 0) >= lax.broadcasted_iota(jnp.int32, (L, L), 1)
    chunks = range(BLK // L)
    heads = range(GLA_HEADS)
    ks = lambda hh: slice(hh * GLA_DK, (hh + 1) * GLA_DK)
    vs = lambda hh: slice(hh * GLA_DV, (hh + 1) * GLA_DV)
    q_in, k_out, k_dec, dec, vb = [], [], [], [], []
    for c in chunks:
        rows = slice(c * L, (c + 1) * L)
        b = lg[rows, :]
        d = 1
        while d < L:
            b = b + jnp.where(row >= d, pltpu.roll(b, d, 0), 0.0)
            d *= 2
        b_end = b[L - 1:L, :]
        kc = k_ref[rows, :]
        q_in.append((q_ref[rows, :] * GLA_DK ** -0.5 * jnp.exp(b)).astype(BF16))
        k_out.append((kc * jnp.exp(-b)).astype(BF16))
        k_dec.append((kc * jnp.exp(b_end - b)).astype(BF16))
        dec.append(jnp.transpose(jnp.broadcast_to(jnp.exp(b_end), (GLA_DV, GLA_KW))))
        vb.append(v_ref[rows, :].astype(BF16))
    att = [[jnp.where(causal, _dot_nt(q_in[c][:, ks(hh)], k_out[c][:, ks(hh)]), 0.0).astype(BF16) for hh in heads]
           for c in chunks]
    kv = [[_dot_tn(k_dec[c][:, ks(hh)], vb[c][:, vs(hh)]) for hh in heads] for c in chunks]
    state = [s_ref[...]]
    for c in chunks:
        state.append(dec[c] * state[c] + jnp.concatenate(kv[c], axis=0))
    s_ref[...] = state[-1]
    for c in chunks:
        sb = state[c].astype(BF16)
        o_heads = [_dot(q_in[c][:, ks(hh)], sb[ks(hh), :]) + _dot(att[c][hh], vb[c][:, vs(hh)]) for hh in heads]
        o_ref[c * L:(c + 1) * L, M_C:M_C + GLA_WIDTH] = _gla_finish(
            o_heads, gate_ref[c * L:(c + 1) * L, :], ng_ref[...]).astype(BF16)


def _gla_param_specs(l):
    return [_layer_spec((128, GLA_KW), l), _layer_spec((1, GLA_KW), l), _layer_spec((1, GLA_WIDTH), l)]


GLA_BT = 16


def _gla_sample_kernel(q_ref, k_ref, v_ref, gate_ref, ac_ref, wa_ref, ba_ref, ng_ref, s_ref, o_ref, so_ref):
    bt = GLA_BT
    lg = _gla_log_gate(ac_ref[...], wa_ref, ba_ref)
    v = v_ref[...]

    def cols(x):
        return jnp.transpose(jnp.concatenate([x, jnp.zeros((128 - bt, GLA_KW), F32)], axis=0))

    eg_t, k_t, q_t = cols(jnp.exp(lg)), cols(k_ref[...]), cols(q_ref[...] * GLA_DK ** -0.5)
    o_rows = []
    for j in range(bt):
        bc = lambda t: jnp.broadcast_to(t[:, j:j + 1], (GLA_KW, GLA_DV))
        vrow = jnp.concatenate(
            [jnp.broadcast_to(v[j:j + 1, hh * GLA_DV:(hh + 1) * GLA_DV], (GLA_DK, GLA_DV)) for hh in range(GLA_HEADS)],
            axis=0)
        s_new = bc(eg_t) * s_ref[j] + bc(k_t) * vrow
        so_ref[j] = s_new
        qs = bc(q_t) * s_new
        o_rows.append(jnp.concatenate(
            [jnp.sum(qs[hh * GLA_DK:(hh + 1) * GLA_DK, :], axis=0, keepdims=True) for hh in range(GLA_HEADS)], axis=1))
    o = jnp.concatenate(o_rows, axis=0)
    o_heads = [o[:, hh * GLA_DV:(hh + 1) * GLA_DV] for hh in range(GLA_HEADS)]
    o_ref[...] = _gla_finish(o_heads, gate_ref[...], ng_ref[...]).astype(BF16)


def _gla_sample(h, wa, ba, ng, state, mix, s_all, l):
    bt = GLA_BT
    r0 = ROWS_P // bt
    blk = lambda c, w: pl.BlockSpec((bt, w), lambda s: (r0 + s, c // w))
    st = pl.BlockSpec((None, bt, GLA_KW, GLA_DV), lambda s: (l, s, 0, 0))
    n_in = 9
    return pl.pallas_call(
        _drop_inputs(_gla_sample_kernel, n_in, 2),
        grid=(N_SAMPLE // bt,),
        in_specs=([blk(C_GQ, GLA_KW), blk(C_GK, GLA_KW), blk(C_GV, GLA_WIDTH), blk(C_GG, GLA_WIDTH), blk(C_GA, 128)]
                  + _gla_param_specs(l) + [st, _any(), _any()]),
        out_specs=[blk(M_C, GLA_WIDTH), st],
        out_shape=[jax.ShapeDtypeStruct((ROWS, D_MODEL), BF16), jax.ShapeDtypeStruct(s_all.shape, F32)],
        input_output_aliases={n_in: 0, n_in + 1: 1},
        compiler_params=_cparams("arbitrary"),
        name="gla_sample",
    )(h, h, h, h, h, wa, ba, ng, state, mix, s_all)


def _mix_prompt_kernel(l, sink_ref, q_ref, kp_ref, kc_ref, vp_ref, vc_ref, u_ref, gq_ref, gk_ref, gv_ref, gg_ref, ga_ref,
                       bb_ref, cre_ref, cim_ref, d_ref, wg_ref, bg_ref, tab_ref, wa_ref, ba_ref, ng_ref,
                       w1_ref, w2_ref, w3_ref,
                       o_ref, hpre_ref, hpim_ref, s_ref, c1_ref, c2_ref, c3_ref, hre, him):
    s_blk = pl.program_id(0)

    @pl.when(s_blk == 0)
    def _():
        hpre_ref[...] = jnp.zeros_like(hpre_ref)
        hpim_ref[...] = jnp.zeros_like(hpim_ref)
        s_ref[...] = jnp.zeros_like(s_ref)

    c1_ref[...] = w1_ref[...].astype(BF16)
    c2_ref[...] = w2_ref[...].astype(BF16)
    c3_ref[...] = w3_ref[...].astype(BF16)
    _s5_block(u_ref, bb_ref, cre_ref, cim_ref, d_ref, wg_ref, bg_ref, tab_ref, o_ref, hpre_ref, hpim_ref, hre, him)
    _swa_block(l, s_blk, sink_ref, q_ref, kp_ref, kc_ref, vp_ref, vc_ref, o_ref)
    _gla_block(gq_ref, gk_ref, gv_ref, gg_ref, ga_ref, wa_ref, ba_ref, ng_ref, o_ref, s_ref)


def _mix_prompt(sink, h, s5_prm, gla_prm, w_casts, mix, l):
    cur = _seq_block
    prev = lambda s: _seq_block(jnp.maximum(s - 1, 0))
    blk = lambda c, w, at=cur: pl.BlockSpec((BLK, w), lambda s: (at(s), c // w))
    const = lambda shape: pl.BlockSpec(shape, lambda s: (0,) * len(shape))
    casts = [_cast_specs(w, rows, l) for w, rows in zip(w_casts, (32, 176, 32))]
    in_specs = ([pl.BlockSpec(memory_space=pltpu.SMEM), blk(C_Q, A_WIDTH),
                 blk(C_K, KV_WIDTH, prev), blk(C_K, KV_WIDTH), blk(C_V, KV_WIDTH, prev), blk(C_V, KV_WIDTH),
                 blk(C_U, S5_WIDTH),
                 blk(C_GQ, GLA_KW), blk(C_GK, GLA_KW), blk(C_GV, GLA_WIDTH), blk(C_GG, GLA_WIDTH), blk(C_GA, 128)]
                + _s5_param_specs(l) + _gla_param_specs(l) + [c[0] for c in casts])
    n_in = len(in_specs)
    return pl.pallas_call(
        _drop_inputs(functools.partial(_mix_prompt_kernel, l), n_in, 1),
        grid=(N_PBLK,),
        in_specs=in_specs + [_any()],
        out_specs=[pl.BlockSpec((BLK, D_MODEL), lambda s: (cur(s), 0)),
                   const((SUB, S5_LANES)), const((SUB, S5_LANES)), const((GLA_KW, GLA_DV))] + [c[1] for c in casts],
        out_shape=[jax.ShapeDtypeStruct((ROWS, D_MODEL), BF16),
                   jax.ShapeDtypeStruct((SUB, S5_LANES), F32), jax.ShapeDtypeStruct((SUB, S5_LANES), F32),
                   jax.ShapeDtypeStruct((GLA_KW, GLA_DV), F32)] + [c[2] for c in casts],
        scratch_shapes=[pltpu.VMEM((BLK, S5_LANES), F32), pltpu.VMEM((BLK, S5_LANES), F32)],
        input_output_aliases={n_in: 0},
        compiler_params=_cparams("arbitrary"),
        name="mix_prompt",
    )(sink, *([h] * 11), *_s5_params(s5_prm), *gla_prm, *w_casts, mix)


OUT_TM = 768


def _out_proj_kernel(mix_ref, w_ref, x_ref, g_ref, b_ref, o_ref):
    i = pl.program_id(0)
    o_ref[...] = _dot(mix_ref[...], w_ref[...])
    _deepnorm_ln(o_ref, x_ref, g_ref, b_ref, o_ref, i * OUT_TM, OUT_TM)


def _out_proj(mix, w, x, g, b, l):
    tm = OUT_TM
    return pl.pallas_call(
        _out_proj_kernel,
        grid=(ROWS // tm,),
        in_specs=[
            pl.BlockSpec((tm, D_MODEL), lambda i: (i, 0)),
            pl.BlockSpec((D_MODEL, D_MODEL), lambda i: (0, 0), pipeline_mode=pl.Buffered(1)),
            pl.BlockSpec((tm, D_MODEL), lambda i: (i, 0)),
            _layer_spec((1, D_MODEL), l), _layer_spec((1, D_MODEL), l),
        ],
        out_specs=pl.BlockSpec((tm, D_MODEL), lambda i: (i, 0)),
        out_shape=jax.ShapeDtypeStruct((ROWS, D_MODEL), F32),
        compiler_params=_cparams("arbitrary"),
        name="out_proj",
    )(mix, w, x, g, b)


FFN_TM = 768
FFN_TF = 512
FFN_HALO = 16
FFN_NI = ROWS // FFN_TM
assert ROWS % FFN_TM == 0 and FFN_TM >= N_SAMPLE + BLK + SUB and META_ROW % FFN_HALO == 0


def _ffn_kernel(final, x_ref, xh_ref, wu_ref, wg_ref, cw_ref, cb_ref, wd_ref, p0_ref, p1_ref, g_ref, b_ref,
                o_ref, *rest):
    if final:
        ys_ref, ut_ref, us_ref, xb_ref, a_ref = rest
    else:
        ut_ref, us_ref, xb_ref, a_ref = rest
    i, f = pl.program_id(0), pl.program_id(1)
    tm, hl = FFN_TM, FFN_HALO
    last_i = FFN_NI - 1

    @pl.when(f == 0)
    def _():
        xb_ref[...] = x_ref[...].astype(BF16)
        o_ref[...] = jnp.zeros_like(o_ref)

    xb = xb_ref[...]
    u = _dot(xb, wu_ref[...])
    gte = _dot(xb, wg_ref[...])
    cw0, cw1, cw2, cb = cw_ref[0:1, :], cw_ref[1:2, :], cw_ref[2:3, :], cb_ref[...]

    def act(um2, um1, u0, g0):
        return (jax.nn.gelu(cb + cw0 * um2 + cw1 * um1 + cw2 * u0) * g0).astype(BF16)

    a_ref[...] = act(pltpu.roll(u, 2, 0), pltpu.roll(u, 1, 0), u, gte)
    ext = jnp.concatenate([_dot(xh_ref[...].astype(BF16), wu_ref[...]), u[0:hl, :]], axis=0)
    a_ref[0:hl, :] = act(ext[hl - 2:2 * hl - 2, :], ext[hl - 1:2 * hl - 1, :], u[0:hl, :], gte[0:hl, :])

    @pl.when(i == last_i)
    def _():
        a_ref[tm - N_SAMPLE:, :] = act(p0_ref[...], p1_ref[...], u[tm - N_SAMPLE:, :], gte[tm - N_SAMPLE:, :])

    o_ref[...] += _dot(a_ref[...], wd_ref[...])
    ut_ref[...] = u[tm - N_SAMPLE - BLK - SUB:tm - N_SAMPLE - BLK, :]
    us_ref[0] = p1_ref[...]
    us_ref[1] = u[tm - N_SAMPLE:, :]

    @pl.when(f == pl.num_programs(1) - 1)
    def _():
        _deepnorm_ln(o_ref, x_ref, g_ref, b_ref, o_ref, i * tm, tm)
        if final:
            ys_ref[...] = o_ref[tm - N_SAMPLE:, :]


def _ffn(x, wup, cw, cb, wd, conv_p0, conv_p1, g, b, l, final):
    tm, tf = FFN_TM, FFN_TF
    nf = D_FF // tf
    tail = lambda i, f: jnp.where(i == FFN_NI - 1, f, 0)
    halo = lambda i, f: (jnp.where(i == 0, ROWS_P // FFN_HALO, i * (tm // FFN_HALO)) - 1, 0)
    y_spec = pl.BlockSpec((tm, D_MODEL), lambda i, f: (i, 0))
    y_specs = [y_spec, pl.BlockSpec((N_SAMPLE, D_MODEL), lambda i, f: (0, 0))] if final else [y_spec]
    y_shapes = ([jax.ShapeDtypeStruct((SEQ, D_MODEL), F32), jax.ShapeDtypeStruct((N_SAMPLE, D_MODEL), F32)]
                if final else [jax.ShapeDtypeStruct((ROWS, D_MODEL), F32)])
    return pl.pallas_call(
        functools.partial(_ffn_kernel, final),
        grid=(FFN_NI, nf),
        in_specs=[
            pl.BlockSpec((tm, D_MODEL), lambda i, f: (i, 0)),
            pl.BlockSpec((FFN_HALO, D_MODEL), halo),
            pl.BlockSpec((D_MODEL, tf), lambda i, f: (0, f)),
            pl.BlockSpec((D_MODEL, tf), lambda i, f: (0, nf + f)),
            pl.BlockSpec((None, CONV_W, tf), lambda i, f: (l, 0, f)),
            pl.BlockSpec((None, 1, tf), lambda i, f: (l, 0, f)),
            pl.BlockSpec((tf, D_MODEL), lambda i, f: (f, 0)),
            pl.BlockSpec((None, N_SAMPLE, tf), lambda i, f: (l, 0, tail(i, f))),
            pl.BlockSpec((None, N_SAMPLE, tf), lambda i, f: (l, 0, tail(i, f))),
            _layer_spec((1, D_MODEL), l), _layer_spec((1, D_MODEL), l),
        ],
        out_specs=y_specs + [
            pl.BlockSpec((SUB, tf), lambda i, f: (0, tail(i, f))),
            pl.BlockSpec((2, N_SAMPLE, tf), lambda i, f: (0, 0, tail(i, f))),
        ],
        out_shape=y_shapes + [
            jax.ShapeDtypeStruct((SUB, D_FF), F32),
            jax.ShapeDtypeStruct((2, N_SAMPLE, D_FF), F32),
        ],
        scratch_shapes=[pltpu.VMEM((tm, D_MODEL), BF16), pltpu.VMEM((tm, tf), BF16)],
        compiler_params=_cparams("arbitrary", "arbitrary"),
        name="conv_ffn",
    )(x, x, wup, wup, cw, cb, wd, conv_p0, conv_p1, g, b)


def _rope_table():
    inv = jnp.power(ROPE_THETA, -jnp.arange(ROPE_HALF, dtype=F32) / ROPE_HALF)
    pos = jnp.concatenate([jnp.arange(SEQ, dtype=jnp.int32) + N_META, jnp.zeros((LEAD,), jnp.int32),
                           jnp.arange(N_META, dtype=jnp.int32), jnp.full((N_SAMPLE,), PAST_LEN, jnp.int32)])
    ang = pos.astype(F32)[:, None] * inv[None, :]
    return jnp.concatenate([jnp.cos(ang), jnp.sin(ang)], axis=1)


def _s5_tables(lam_re, lam_im, log_dt, b_re, b_im, c_re, c_im, d, w_glu, b_glu):
    L = lam_re.shape[0]
    dt = jnp.exp(log_dt)[..., None]
    mag = jnp.exp(lam_re * dt)
    ab_re, ab_im = mag * jnp.cos(lam_im * dt), mag * jnp.sin(lam_im * dt)
    den = lam_re * lam_re + lam_im * lam_im
    nr, ni = ab_re - 1.0, ab_im
    f_re = (nr * lam_re + ni * lam_im) / den
    f_im = (ni * lam_re - nr * lam_im) / den
    bb_re = f_re[..., None] * b_re - f_im[..., None] * b_im
    bb_im = f_re[..., None] * b_im + f_im[..., None] * b_re
    gi, go = S5_GROUPS // S5_IN_PIECES, S5_GROUPS // S5_OUT_PIECES

    def bd_in(t):
        t = t.reshape(L, S5_IN_PIECES, gi, S5_STATE, S5_CH)
        return jnp.einsum('lqgph,gk->lqghkp', t, jnp.eye(gi, dtype=F32)).reshape(L, S5_IN_PIECES, S5_IN_K, S5_IN_N)

    def bd_out(t):
        t = t.reshape(L, S5_OUT_PIECES, go, S5_CH, S5_STATE)
        return jnp.einsum('lqghp,gk->lqgpkh', t, jnp.eye(go, dtype=F32)).reshape(
            L, S5_OUT_PIECES, S5_OUT_K, S5_OUT_N)

    ar, ai = ab_re.reshape(L, 1, S5_LANES), ab_im.reshape(L, 1, S5_LANES)
    pows = [(ar, ai)]
    for _ in range(SUB - 1):
        pr, pi = pows[-1]
        pows.append((pr * ar - pi * ai, pr * ai + pi * ar))
    row = jnp.arange(SUB)[None, :, None]
    parts = []
    for sft in S5_SHIFTS:
        pr, pi = pows[sft - 1]
        parts += [jnp.where(row >= sft, pr, 0.0), jnp.where(row >= sft, pi, 0.0)]
    parts += [jnp.concatenate([p[0] for p in pows], 1), jnp.concatenate([p[1] for p in pows], 1)]
    return dict(bb=jnp.concatenate([bd_in(bb_re), bd_in(bb_im)], axis=3).astype(BF16),
                c_re=bd_out(c_re).astype(BF16), c_im=bd_out(c_im).astype(BF16),
                d=d.reshape(L, 1, S5_WIDTH), w_glu=w_glu.astype(BF16), b_glu=b_glu.reshape(L, 1, S5_WIDTH),
                tab=jnp.concatenate(parts, 1))


@jax.jit
def kernel(x_prompt, x_sample, cache_swa_k, cache_swa_v, state_ssm_re, state_ssm_im, state_gla, state_conv,
           meta_tokens, ln_in_g, ln_in_b, w_in, attn_sink, s5_lam_re, s5_lam_im, s5_log_dt, s5_b_re, s5_b_im,
           s5_c_re, s5_c_im, s5_d, s5_w_glu, s5_b_glu, gla_w_a2, gla_b_a, gla_norm_g, w_out, ln1_g, ln1_b,
           ffn_w_up, ffn_conv_w, ffn_conv_b, ffn_w_down, ln2_g, ln2_b):
    L = DEPTH
    row3 = lambda t: t.reshape(L, 1, -1)
    w_in_b = w_in.astype(BF16)
    wa_b = jnp.pad(gla_w_a2, ((0, 0), (0, 128 - GLA_LOWRANK), (0, 0))).astype(BF16)
    gla_args = (wa_b, row3(gla_b_a), row3(gla_norm_g))
    ln1, ln2 = (row3(ln1_g), row3(ln1_b)), (row3(ln2_g), row3(ln2_b))
    conv_b3 = row3(ffn_conv_b)
    cs = _rope_table()
    s5p = _s5_tables(s5_lam_re, s5_lam_im, s5_log_dt, s5_b_re, s5_b_im, s5_c_re, s5_c_im, s5_d, s5_w_glu, s5_b_glu)
    ck = cache_swa_k.reshape(L, N_SAMPLE, WINDOW, KV_WIDTH)
    cv = cache_swa_v.reshape(L, N_SAMPLE, WINDOW, KV_WIDTH)
    h0re = state_ssm_re.reshape(L, N_SAMPLE, S5_LANES)
    h0im = state_ssm_im.reshape(L, N_SAMPLE, S5_LANES)
    sg = state_gla.reshape(L, N_SAMPLE, GLA_KW, GLA_DV)
    conv_p0, conv_p1 = state_conv[:, :, 0, :], state_conv[:, :, 1, :]
    mix = jnp.zeros((ROWS, D_MODEL), BF16)
    s_all = jnp.zeros((L, N_SAMPLE, GLA_KW, GLA_DV), F32)

    x = _ln_in(x_prompt.reshape(SEQ, D_MODEL), meta_tokens, x_sample.reshape(N_SAMPLE, D_MODEL),
               ln_in_g.reshape(1, D_MODEL), ln_in_b.reshape(1, D_MODEL))
    outs = [[] for _ in range(9)]
    for l in range(L):
        h = _in_proj(x, w_in_b, cs, l)
        mix, hp_re, hp_im, s_p, w_up_b, w_down_b, w_out_b = _mix_prompt(
            attn_sink, h, s5p, gla_args, (ffn_w_up, ffn_w_down, w_out), mix, l)
        mix, ck, cv = _swa_sample(attn_sink, h, ck, cv, mix, l)
        mix, hs_re, hs_im = _s5_sample(h, s5p, h0re, h0im, mix, l)
        mix, s_all = _gla_sample(h, *gla_args, sg, mix, s_all, l)
        x = _out_proj(mix, w_out_b, x, *ln1, l)
        res = _ffn(x, w_up_b, ffn_conv_w, conv_b3, w_down_b, conv_p0, conv_p1, *ln2, l, l == L - 1)
        if l == L - 1:
            y_prompt, y_sample, u_tail, u_s = res
        else:
            x, u_tail, u_s = res
        kv_p = lambda c: h[SEQ - WINDOW:SEQ, c:c + KV_WIDTH].reshape(1, WINDOW, A_KV_HEADS, HEAD_DIM)
        new = (kv_p(C_K), kv_p(C_V),
               hp_re[SUB - 1].reshape(1, S5_GROUPS, S5_STATE), hp_im[SUB - 1].reshape(1, S5_GROUPS, S5_STATE),
               s_p.reshape(1, GLA_HEADS, GLA_DK, GLA_DV), u_tail[SUB - 2:SUB][None],
               hs_re.reshape(N_SAMPLE, S5_GROUPS, S5_STATE), hs_im.reshape(N_SAMPLE, S5_GROUPS, S5_STATE),
               jnp.transpose(u_s, (1, 0, 2)))
        for lst, val in zip(outs, new):
            lst.append(val)
    st = [jnp.stack(lst) for lst in outs]
    return (y_prompt.reshape(1, SEQ, D_MODEL), y_sample.reshape(N_SAMPLE, 1, D_MODEL),
            st[0], st[1], st[2], st[3], st[4], st[5],
            ck.reshape(L, N_SAMPLE, WINDOW, A_KV_HEADS, HEAD_DIM),
            cv.reshape(L, N_SAMPLE, WINDOW, A_KV_HEADS, HEAD_DIM),
            st[6], st[7], s_all.reshape(L, N_SAMPLE, GLA_HEADS, GLA_DK, GLA_DV), st[8])
```

```python
import functools

import jax
import jax.numpy as jnp
from jax import lax
from jax.experimental import pallas as pl
from jax.experimental.pallas import tpu as pltpu

F32 = jnp.float32
BF16 = jnp.bfloat16

D_MODEL = 2048
SEQ = 8192
DEPTH = 2
N_SAMPLE = 128
N_META = 16
HEAD_DIM = 64
A_WIDTH = 1024
A_HEADS = 16
A_KV_HEADS = 4
GQA = 4
KV_WIDTH = A_KV_HEADS * HEAD_DIM
WINDOW = 128
ROPE_DIM = 16
ROPE_HALF = ROPE_DIM // 2
ROPE_THETA = 500000.0
PAST_LEN = 8192
S5_WIDTH = 512
S5_CH = 16
S5_GROUPS = 32
S5_STATE = 64
S5_LANES = S5_GROUPS * S5_STATE
GLA_WIDTH = 512
GLA_HEADS = 4
GLA_DV = 128
GLA_DK = 64
GLA_KW = GLA_HEADS * GLA_DK
GLA_LOWRANK = 16
GLA_TAU = 16.0
GLA_CHUNK = 64
D_FF = 5632
CONV_W = 3
LN_EPS = 1e-5
ALPHA = (2 * DEPTH) ** 0.25
NEG_INF = -1e30

BLK = 128
T_PROMPT = N_META + SEQ
LEAD = (-T_PROMPT) % BLK
N_PBLK = (LEAD + T_PROMPT) // BLK
ROWS_P = N_PBLK * BLK
ROWS = ROWS_P + N_SAMPLE
META_ROW = SEQ + LEAD
SUB = 8

C_Q, C_K, C_V, C_U, C_GQ, C_GK, C_GV, C_GG, C_GA = 0, 1024, 1280, 1536, 2048, 2304, 2560, 3072, 3584
IN_COLS = 3600
IN_TILE = 1280
IN_PAD = 3 * IN_TILE
ROPE_COLS = A_WIDTH + KV_WIDTH
M_A, M_B, M_C = 0, A_WIDTH, A_WIDTH + S5_WIDTH

VMEM_LIMIT = 56 * 1024 * 1024


def _cparams(*sem):
    return pltpu.CompilerParams(dimension_semantics=sem, vmem_limit_bytes=VMEM_LIMIT)


def _any():
    return pl.BlockSpec(memory_space=pl.ANY)


def _drop_inputs(kern, start, count):
    def wrapped(*refs):
        return kern(*refs[:start], *refs[start + count:])
    return wrapped


def _seq_block(s):
    return jnp.where(s == 0, N_PBLK - 1, s - 1)


def _layer_spec(shape, l):
    return pl.BlockSpec((None,) + shape, lambda *_: (l,) + (0,) * len(shape))


def _cast_specs(w, rows, l):
    n, cols = w.shape[1] // rows, w.shape[2]
    chunk = lambda s: jnp.minimum(s, n - 1)
    return (pl.BlockSpec((None, rows, cols), lambda s: (l, chunk(s), 0)),
            pl.BlockSpec((rows, cols), lambda s: (chunk(s), 0)),
            jax.ShapeDtypeStruct(w.shape[1:], BF16))


def _layer_norm_rows(x, g, b):
    mu = jnp.mean(x, -1, keepdims=True)
    xc = x - mu
    var = jnp.mean(xc * xc, -1, keepdims=True)
    return xc * lax.rsqrt(var + LN_EPS) * g + b


def _zero_pad_rows(y, first_row):
    row = first_row + lax.broadcasted_iota(jnp.int32, y.shape, 0)
    return jnp.where((row >= SEQ) & (row < META_ROW), 0.0, y)


def _deepnorm_ln(acc_ref, x_ref, g_ref, b_ref, o_ref, first_row, rows):
    g, b = g_ref[...], b_ref[...]
    for c in range(rows // BLK):
        r = slice(c * BLK, (c + 1) * BLK)
        y = _layer_norm_rows(ALPHA * x_ref[r, :] + acc_ref[r, :], g, b)
        o_ref[r, :] = _zero_pad_rows(y, first_row + c * BLK)


def _dot(a, b):
    return jnp.dot(a, b, preferred_element_type=F32)


def _dot_nt(a, b):
    return lax.dot_general(a, b, (((1,), (1,)), ((), ())), preferred_element_type=F32)


def _dot_tn(a, b):
    return lax.dot_general(a, b, (((0,), (0,)), ((), ())), preferred_element_type=F32)


def _log_sigmoid(x):
    return jnp.minimum(x, 0.0) - jnp.log(1.0 + jnp.exp(-jnp.abs(x)))


LN_IN_TM = 256
assert ROWS - SEQ == LN_IN_TM and SEQ % LN_IN_TM == 0


def _ln_in_kernel(xp_ref, meta_ref, xs_ref, g_ref, b_ref, o_ref):
    i = pl.program_id(0)
    g, b = g_ref[...], b_ref[...]

    @pl.when(i < SEQ // LN_IN_TM)
    def _():
        o_ref[...] = _layer_norm_rows(xp_ref[...], g, b)

    @pl.when(i == SEQ // LN_IN_TM)
    def _():
        o_ref[0:LEAD, :] = jnp.zeros((LEAD, D_MODEL), F32)
        o_ref[LEAD:BLK, :] = _layer_norm_rows(meta_ref[...], g, b)
        o_ref[BLK:, :] = _layer_norm_rows(xs_ref[...], g, b)


def _ln_in(xp, meta, xs, g, b):
    tm = LN_IN_TM
    const = lambda shape: pl.BlockSpec(shape, lambda i: (0,) * len(shape))
    return pl.pallas_call(
        _ln_in_kernel,
        grid=(ROWS // tm,),
        in_specs=[
            pl.BlockSpec((tm, D_MODEL), lambda i: (jnp.minimum(i, SEQ // tm - 1), 0)),
            const((N_META, D_MODEL)), const((N_SAMPLE, D_MODEL)), const((1, D_MODEL)), const((1, D_MODEL)),
        ],
        out_specs=pl.BlockSpec((tm, D_MODEL), lambda i: (i, 0)),
        out_shape=jax.ShapeDtypeStruct((ROWS, D_MODEL), F32),
        compiler_params=_cparams("arbitrary"),
        name="ln_in",
    )(xp, meta, xs, g, b)


IN_TM = 1056


def _in_proj_kernel(x_ref, w_ref, cs_ref, o_ref, xb_ref):
    j = pl.program_id(1)

    last_j = IN_PAD // IN_TILE - 1

    @pl.when(j == 0)
    def _():
        xb = x_ref[...].astype(BF16)
        xb_ref[...] = xb
        c8, s8 = cs_ref[:, 0:ROPE_HALF], cs_ref[:, ROPE_HALF:ROPE_DIM]
        rest = HEAD_DIM - ROPE_DIM
        one, zero = jnp.ones((IN_TM, rest), F32), jnp.zeros((IN_TM, rest), F32)
        z8 = jnp.zeros((IN_TM, ROPE_HALF), F32)
        cos = jnp.concatenate([c8, c8, one] * 2, axis=1)
        sa = jnp.concatenate([-s8, z8, zero] * 2, axis=1)
        sb = jnp.concatenate([z8, s8, zero] * 2, axis=1)
        for c in range(ROPE_COLS // 256):
            acc = _dot(xb, w_ref[:, c * 256:(c + 1) * 256])
            for hlf in range(2):
                blk = acc[:, hlf * 128:(hlf + 1) * 128]
                o_ref[:, c * 256 + hlf * 128:c * 256 + (hlf + 1) * 128] = (
                    blk * cos + pltpu.roll(blk, 128 - ROPE_HALF, 1) * sa + pltpu.roll(blk, ROPE_HALF, 1) * sb)

    @pl.when((j > 0) & (j < last_j))
    def _():
        o_ref[...] = _dot(xb_ref[...], w_ref[...])

    @pl.when(j == last_j)
    def _():
        o_ref[...] = _dot(xb_ref[...], w_ref[...])
        edge = (IN_COLS % IN_TILE) // 128 * 128
        col = edge + lax.broadcasted_iota(jnp.int32, (IN_TM, 128), 1)
        o_ref[:, edge:edge + 128] = jnp.where(col < IN_COLS % IN_TILE, o_ref[:, edge:edge + 128], 0.0)
        o_ref[:, edge + 128:] = jnp.zeros((IN_TM, IN_TILE - edge - 128), F32)


def _in_proj(x, w, cs, l):
    return pl.pallas_call(
        _in_proj_kernel,
        grid=(ROWS // IN_TM, IN_PAD // IN_TILE),
        in_specs=[
            pl.BlockSpec((IN_TM, D_MODEL), lambda i, j: (i, 0)),
            pl.BlockSpec((None, D_MODEL, IN_TILE), lambda i, j: (l, 0, j)),
            pl.BlockSpec((IN_TM, ROPE_DIM), lambda i, j: (i, 0)),
        ],
        out_specs=pl.BlockSpec((IN_TM, IN_TILE), lambda i, j: (i, j)),
        out_shape=jax.ShapeDtypeStruct((ROWS, IN_PAD), F32),
        scratch_shapes=[pltpu.VMEM((IN_TM, D_MODEL), BF16)],
        compiler_params=_cparams("arbitrary", "arbitrary"),
        name="in_proj",
    )(x, w, cs)


def _swa_block(l, s_blk, sink_ref, q_ref, kp_ref, kc_ref, vp_ref, vc_ref, o_ref):
    k2 = jnp.concatenate([kp_ref[...], kc_ref[...]], axis=0).astype(BF16)
    v2 = jnp.concatenate([vp_ref[...], vc_ref[...]], axis=0).astype(BF16)
    nq = GQA * BLK
    row = lax.broadcasted_iota(jnp.int32, (nq, 2 * BLK), 0) % BLK
    col = lax.broadcasted_iota(jnp.int32, (nq, 2 * BLK), 1)
    diff = row - col + BLK
    kpos = (s_blk - 1) * BLK + col - LEAD
    ok = (diff >= 0) & (diff <= WINDOW) & (kpos >= 0)
    low = lax.broadcasted_iota(jnp.int32, (BLK, 2 * HEAD_DIM), 1) < HEAD_DIM
    g_of_row = lax.broadcasted_iota(jnp.int32, (nq, 1), 0) // BLK
    ones = jnp.ones((2 * BLK, 2 * HEAD_DIM), BF16)
    scores, sinks = [], []
    for kv in range(A_KV_HEADS):
        kh = k2[:, kv * HEAD_DIM:(kv + 1) * HEAD_DIM]
        kk = jnp.concatenate([kh, kh], axis=1)
        parts = []
        for pair in range(GQA // 2):
            c0 = (kv * GQA + 2 * pair) * HEAD_DIM
            qp = q_ref[:, c0:c0 + 2 * HEAD_DIM] * HEAD_DIM ** -0.5
            parts += [jnp.where(low, qp, 0.0).astype(BF16), jnp.where(low, 0.0, qp).astype(BF16)]
        q4 = jnp.concatenate(parts, axis=0)
        sk = jnp.zeros((nq, 1), F32)
        for g in range(GQA):
            sk = jnp.where(g_of_row == g, sink_ref[l, kv * GQA + g], sk)
        scores.append(jnp.where(ok, _dot_nt(q4, kk), NEG_INF))
        sinks.append(sk)
    probs, sink_terms = [], []
    for s, sk in zip(scores, sinks):
        m = jnp.maximum(jnp.max(s, -1, keepdims=True), sk)
        probs.append(jnp.exp(s - m).astype(BF16))
        sink_terms.append(jnp.exp(sk - m))
    for kv in range(A_KV_HEADS):
        vh = v2[:, kv * HEAD_DIM:(kv + 1) * HEAD_DIM]
        v3 = jnp.concatenate([vh, vh, ones], axis=1)
        oa = _dot(probs[kv], v3)
        on = oa[:, 0:2 * HEAD_DIM] / (oa[:, 2 * HEAD_DIM:] + sink_terms[kv])
        for pair in range(GQA // 2):
            c0 = (kv * GQA + 2 * pair) * HEAD_DIM
            lo, hi = on[2 * pair * BLK:(2 * pair + 1) * BLK], on[(2 * pair + 1) * BLK:(2 * pair + 2) * BLK]
            o_ref[:, M_A + c0:M_A + c0 + 2 * HEAD_DIM] = jnp.where(low, lo, hi).astype(BF16)


SWA_BT = 16


def _swa_sample_kernel(l, sink_ref, q_ref, kn_ref, vn_ref, ck_ref, cv_ref, o_ref, cko_ref, cvo_ref):
    bt = SWA_BT
    kn, vn = kn_ref[...], vn_ref[...]
    k2 = ck_ref[...].reshape(bt * WINDOW, KV_WIDTH).astype(BF16)
    v2 = cv_ref[...].reshape(bt * WINDOW, KV_WIDTH).astype(BF16)
    row = lax.broadcasted_iota(jnp.int32, (WINDOW, KV_WIDTH), 0)
    for j in range(bt):
        cko_ref[j] = jnp.where(row == WINDOW - 1, kn[j:j + 1, :], pltpu.roll(ck_ref[j], WINDOW - 1, 0))
        cvo_ref[j] = jnp.where(row == WINDOW - 1, vn[j:j + 1, :], pltpu.roll(cv_ref[j], WINDOW - 1, 0))
    nq = GQA * bt
    rowb = lax.broadcasted_iota(jnp.int32, (nq, bt * WINDOW), 0) % bt
    colb = lax.broadcasted_iota(jnp.int32, (nq, bt * WINDOW), 1) // WINDOW
    same = rowb == colb
    rg = lax.broadcasted_iota(jnp.int32, (nq, 1), 0) // bt
    for kv in range(A_KV_HEADS):
        sl = slice(kv * HEAD_DIM, (kv + 1) * HEAD_DIM)
        q4 = jnp.concatenate(
            [q_ref[:, (kv * GQA + g) * HEAD_DIM:(kv * GQA + g + 1) * HEAD_DIM] for g in range(GQA)], axis=0)
        q4 = q4 * HEAD_DIM ** -0.5
        kn4 = jnp.concatenate([kn[:, sl]] * GQA, axis=0)
        vn4 = jnp.concatenate([vn[:, sl]] * GQA, axis=0)
        sk = jnp.zeros((nq, 1), F32)
        for g in range(GQA):
            sk = jnp.where(rg == g, sink_ref[l, kv * GQA + g], sk)
        s = jnp.where(same, _dot_nt(q4.astype(BF16), k2[:, sl]), NEG_INF)
        s_self = jnp.sum(q4 * kn4, -1, keepdims=True)
        m = jnp.maximum(jnp.maximum(jnp.max(s, -1, keepdims=True), s_self), sk)
        p = jnp.exp(s - m)
        p_self = jnp.exp(s_self - m)
        den = jnp.sum(p, -1, keepdims=True) + p_self + jnp.exp(sk - m)
        o = (_dot(p.astype(BF16), v2[:, sl]) + p_self * vn4) / den
        for g in range(GQA):
            hh = kv * GQA + g
            o_ref[:, hh * HEAD_DIM:(hh + 1) * HEAD_DIM] = o[g * bt:(g + 1) * bt].astype(BF16)


def _swa_sample(sink, h, cache_k, cache_v, mix, l):
    bt = SWA_BT
    r0 = ROWS_P // bt
    cache = pl.BlockSpec((None, bt, WINDOW, KV_WIDTH), lambda s: (l, s, 0, 0))
    n_in = 6
    return pl.pallas_call(
        _drop_inputs(functools.partial(_swa_sample_kernel, l), n_in, 1),
        grid=(N_SAMPLE // bt,),
        in_specs=[
            pl.BlockSpec(memory_space=pltpu.SMEM),
            pl.BlockSpec((bt, A_WIDTH), lambda s: (r0 + s, C_Q // A_WIDTH)),
            pl.BlockSpec((bt, KV_WIDTH), lambda s: (r0 + s, C_K // KV_WIDTH)),
            pl.BlockSpec((bt, KV_WIDTH), lambda s: (r0 + s, C_V // KV_WIDTH)),
            cache, cache,
            _any(),
        ],
        out_specs=[pl.BlockSpec((bt, A_WIDTH), lambda s: (r0 + s, M_A // A_WIDTH)), cache, cache],
        out_shape=[jax.ShapeDtypeStruct((ROWS, D_MODEL), BF16),
                   jax.ShapeDtypeStruct(cache_k.shape, F32), jax.ShapeDtypeStruct(cache_v.shape, F32)],
        input_output_aliases={n_in: 0, 4: 1, 5: 2},
        compiler_params=_cparams("arbitrary"),
        name="swa_sample",
    )(sink, h, h, h, cache_k, cache_v, mix)


S5_SHIFTS = (1, 2, 4)
S5_NTAB = 2 * (len(S5_SHIFTS) + 1) * SUB
S5_IN_PIECES = 4
S5_IN_K = S5_WIDTH // S5_IN_PIECES
S5_IN_N = S5_LANES // S5_IN_PIECES
S5_OUT_PIECES = 2
S5_OUT_K = S5_LANES // S5_OUT_PIECES
S5_OUT_N = S5_WIDTH // S5_OUT_PIECES


def _s5_tab(tab_ref, k, lanes):
    return (tab_ref[2 * k * SUB:(2 * k + 1) * SUB, lanes], tab_ref[(2 * k + 1) * SUB:(2 * k + 2) * SUB, lanes])


def _s5_drive(ub, bb_ref, p):
    return _dot(ub[:, p * S5_IN_K:(p + 1) * S5_IN_K], bb_ref[p])


def _s5_readout(hre, him, u, cre_ref, cim_ref, d_ref, wg_ref, bg_ref):
    ys = []
    for j in range(S5_OUT_PIECES):
        lanes = slice(j * S5_OUT_K, (j + 1) * S5_OUT_K)
        ys.append(_dot(hre[:, lanes].astype(BF16), cre_ref[j]) - _dot(him[:, lanes].astype(BF16), cim_ref[j]))
    y = jnp.concatenate(ys, axis=1) + d_ref[...] * u
    z = jax.nn.gelu(y)
    gate = jax.nn.sigmoid(_dot(z.astype(BF16), wg_ref[...]) + bg_ref[...])
    return (z * gate).astype(BF16)


def _s5_block(u_ref, bb_ref, cre_ref, cim_ref, d_ref, wg_ref, bg_ref, tab_ref, o_ref, hpre_ref, hpim_ref, hre, him):
    u = u_ref[...]
    ub = u.astype(BF16)
    for p in range(S5_IN_PIECES):
        lanes = slice(p * S5_IN_N, (p + 1) * S5_IN_N)
        bu = _s5_drive(ub, bb_ref, p)
        cr, ci = hpre_ref[:, lanes], hpim_ref[:, lanes]
        pr, pi = _s5_tab(tab_ref, len(S5_SHIFTS), lanes)
        for j in range(BLK // SUB):
            rows = slice(j * SUB, (j + 1) * SUB)
            xr, xi = bu[rows, 0:S5_IN_N], bu[rows, S5_IN_N:]
            for k, d in enumerate(S5_SHIFTS):
                ar, ai = _s5_tab(tab_ref, k, lanes)
                sr, si = pltpu.roll(xr, d, 0), pltpu.roll(xi, d, 0)
                xr, xi = xr + ar * sr - ai * si, xi + ar * si + ai * sr
            xr, xi = xr + pr * cr - pi * ci, xi + pr * ci + pi * cr
            hre[rows, lanes] = xr
            him[rows, lanes] = xi
            cr = jnp.broadcast_to(xr[SUB - 1:SUB, :], (SUB, S5_IN_N))
            ci = jnp.broadcast_to(xi[SUB - 1:SUB, :], (SUB, S5_IN_N))
        hpre_ref[:, lanes] = cr
        hpim_ref[:, lanes] = ci
    o_ref[:, M_B:M_B + S5_WIDTH] = _s5_readout(hre, him, u, cre_ref, cim_ref, d_ref, wg_ref, bg_ref)


def _s5_sample_kernel(u_ref, bb_ref, cre_ref, cim_ref, d_ref, wg_ref, bg_ref, tab_ref, h0re_ref, h0im_ref,
                      o_ref, hsre_ref, hsim_ref):
    u = u_ref[...]
    ub = u.astype(BF16)
    for p in range(S5_IN_PIECES):
        lanes = slice(p * S5_IN_N, (p + 1) * S5_IN_N)
        bu = _s5_drive(ub, bb_ref, p)
        pr, pi = _s5_tab(tab_ref, len(S5_SHIFTS), lanes)
        ar, ai = pr[0:1, :], pi[0:1, :]
        h0r, h0i = h0re_ref[:, lanes], h0im_ref[:, lanes]
        hsre_ref[:, lanes] = ar * h0r - ai * h0i + bu[:, 0:S5_IN_N]
        hsim_ref[:, lanes] = ar * h0i + ai * h0r + bu[:, S5_IN_N:]
    o_ref[...] = _s5_readout(hsre_ref, hsim_ref, u, cre_ref, cim_ref, d_ref, wg_ref, bg_ref)


def _s5_param_specs(l):
    return [
        _layer_spec((S5_IN_PIECES, S5_IN_K, 2 * S5_IN_N), l),
        _layer_spec((S5_OUT_PIECES, S5_OUT_K, S5_OUT_N), l), _layer_spec((S5_OUT_PIECES, S5_OUT_K, S5_OUT_N), l),
        _layer_spec((1, S5_WIDTH), l), _layer_spec((S5_WIDTH, S5_WIDTH), l), _layer_spec((1, S5_WIDTH), l),
        _layer_spec((S5_NTAB, S5_LANES), l),
    ]


def _s5_params(prm):
    return (prm["bb"], prm["c_re"], prm["c_im"], prm["d"], prm["w_glu"], prm["b_glu"], prm["tab"])


def _s5_sample(h, prm, h0re, h0im, mix, l):
    const = lambda shape: pl.BlockSpec(shape, lambda i: (0,) * len(shape))
    n_in = 10
    return pl.pallas_call(
        _drop_inputs(_s5_sample_kernel, n_in, 1),
        grid=(1,),
        in_specs=([pl.BlockSpec((N_SAMPLE, S5_WIDTH), lambda i: (N_PBLK, C_U // S5_WIDTH))] + _s5_param_specs(l)
                  + [_layer_spec((N_SAMPLE, S5_LANES), l), _layer_spec((N_SAMPLE, S5_LANES), l), _any()]),
        out_specs=[
            pl.BlockSpec((N_SAMPLE, S5_WIDTH), lambda i: (N_PBLK, M_B // S5_WIDTH)),
            const((N_SAMPLE, S5_LANES)), const((N_SAMPLE, S5_LANES)),
        ],
        out_shape=[
            jax.ShapeDtypeStruct((ROWS, D_MODEL), BF16),
            jax.ShapeDtypeStruct((N_SAMPLE, S5_LANES), F32), jax.ShapeDtypeStruct((N_SAMPLE, S5_LANES), F32),
        ],
        input_output_aliases={n_in: 0},
        compiler_params=_cparams("arbitrary"),
        name="s5_sample",
    )(h, *_s5_params(prm), h0re, h0im, mix)


def _gla_log_gate(ac, wa_ref, ba_ref):
    return _log_sigmoid(_dot(ac.astype(BF16), wa_ref[...]) + ba_ref[...]) * (1.0 / GLA_TAU)


def _gla_finish(o_heads, gate, ng):
    outs = [o * lax.rsqrt(jnp.mean(o * o, -1, keepdims=True) + LN_EPS) for o in o_heads]
    return jnp.concatenate(outs, axis=1) * ng * (gate * jax.nn.sigmoid(gate))


def _gla_block(q_ref, k_ref, v_ref, gate_ref, ac_ref, wa_ref, ba_ref, ng_ref, o_ref, s_ref):
    lg = _gla_log_gate(ac_ref[...], wa_ref, ba_ref)
    L = GLA_CHUNK
    row = lax.broadcasted_iota(jnp.int32, (L, GLA_KW), 0)
    causal = lax.broadcasted_iota(jnp.int32, (L, L), 0) >= lax.broadcasted_iota(jnp.int32, (L, L), 1)
    chunks = range(BLK // L)
    heads = range(GLA_HEADS)
    ks = lambda hh: slice(hh * GLA_DK, (hh + 1) * GLA_DK)
    vs = lambda hh: slice(hh * GLA_DV, (hh + 1) * GLA_DV)
    q_in, k_out, k_dec, dec, vb = [], [], [], [], []
    for c in chunks:
        rows = slice(c * L, (c + 1) * L)
        b = lg[rows, :]
        d = 1
        while d < L:
            b = b + jnp.where(row >= d, pltpu.roll(b, d, 0), 0.0)
            d *= 2
        b_end = b[L - 1:L, :]
        kc = k_ref[rows, :]
        q_in.append((q_ref[rows, :] * GLA_DK ** -0.5 * jnp.exp(b)).astype(BF16))
        k_out.append((kc * jnp.exp(-b)).astype(BF16))
        k_dec.append((kc * jnp.exp(b_end - b)).astype(BF16))
        dec.append(jnp.transpose(jnp.broadcast_to(jnp.exp(b_end), (GLA_DV, GLA_KW))))
        vb.append(v_ref[rows, :].astype(BF16))
    att = [[jnp.where(causal, _dot_nt(q_in[c][:, ks(hh)], k_out[c][:, ks(hh)]), 0.0).astype(BF16) for hh in heads]
           for c in chunks]
    kv = [[_dot_tn(k_dec[c][:, ks(hh)], vb[c][:, vs(hh)]) for hh in heads] for c in chunks]
    state = [s_ref[...]]
    for c in chunks:
        state.append(dec[c] * state[c] + jnp.concatenate(kv[c], axis=0))
    s_ref[...] = state[-1]
    for c in chunks:
        sb = state[c].astype(BF16)
        o_heads = [_dot(q_in[c][:, ks(hh)], sb[ks(hh), :]) + _dot(att[c][hh], vb[c][:, vs(hh)]) for hh in heads]
        o_ref[c * L:(c + 1) * L, M_C:M_C + GLA_WIDTH] = _gla_finish(
            o_heads, gate_ref[c * L:(c + 1) * L, :], ng_ref[...]).astype(BF16)


def _gla_param_specs(l):
    return [_layer_spec((128, GLA_KW), l), _layer_spec((1, GLA_KW), l), _layer_spec((1, GLA_WIDTH), l)]


GLA_BT = 16


def _gla_sample_kernel(q_ref, k_ref, v_ref, gate_ref, ac_ref, wa_ref, ba_ref, ng_ref, s_ref, o_ref, so_ref):
    bt = GLA_BT
    lg = _gla_log_gate(ac_ref[...], wa_ref, ba_ref)
    v = v_ref[...]

    def cols(x):
        return jnp.transpose(jnp.concatenate([x, jnp.zeros((128 - bt, GLA_KW), F32)], axis=0))

    eg_t, k_t, q_t = cols(jnp.exp(lg)), cols(k_ref[...]), cols(q_ref[...] * GLA_DK ** -0.5)
    o_rows = []
    for j in range(bt):
        bc = lambda t: jnp.broadcast_to(t[:, j:j + 1], (GLA_KW, GLA_DV))
        vrow = jnp.concatenate(
            [jnp.broadcast_to(v[j:j + 1, hh * GLA_DV:(hh + 1) * GLA_DV], (GLA_DK, GLA_DV)) for hh in range(GLA_HEADS)],
            axis=0)
        s_new = bc(eg_t) * s_ref[j] + bc(k_t) * vrow
        so_ref[j] = s_new
        qs = bc(q_t) * s_new
        o_rows.append(jnp.concatenate(
            [jnp.sum(qs[hh * GLA_DK:(hh + 1) * GLA_DK, :], axis=0, keepdims=True) for hh in range(GLA_HEADS)], axis=1))
    o = jnp.concatenate(o_rows, axis=0)
    o_heads = [o[:, hh * GLA_DV:(hh + 1) * GLA_DV] for hh in range(GLA_HEADS)]
    o_ref[...] = _gla_finish(o_heads, gate_ref[...], ng_ref[...]).astype(BF16)


def _gla_sample(h, wa, ba, ng, state, mix, s_all, l):
    bt = GLA_BT
    r0 = ROWS_P // bt
    blk = lambda c, w: pl.BlockSpec((bt, w), lambda s: (r0 + s, c // w))
    st = pl.BlockSpec((None, bt, GLA_KW, GLA_DV), lambda s: (l, s, 0, 0))
    n_in = 9
    return pl.pallas_call(
        _drop_inputs(_gla_sample_kernel, n_in, 2),
        grid=(N_SAMPLE // bt,),
        in_specs=([blk(C_GQ, GLA_KW), blk(C_GK, GLA_KW), blk(C_GV, GLA_WIDTH), blk(C_GG, GLA_WIDTH), blk(C_GA, 128)]
                  + _gla_param_specs(l) + [st, _any(), _any()]),
        out_specs=[blk(M_C, GLA_WIDTH), st],
        out_shape=[jax.ShapeDtypeStruct((ROWS, D_MODEL), BF16), jax.ShapeDtypeStruct(s_all.shape, F32)],
        input_output_aliases={n_in: 0, n_in + 1: 1},
        compiler_params=_cparams("arbitrary"),
        name="gla_sample",
    )(h, h, h, h, h, wa, ba, ng, state, mix, s_all)


def _mix_prompt_kernel(l, sink_ref, q_ref, kp_ref, kc_ref, vp_ref, vc_ref, u_ref, gq_ref, gk_ref, gv_ref, gg_ref, ga_ref,
                       bb_ref, cre_ref, cim_ref, d_ref, wg_ref, bg_ref, tab_ref, wa_ref, ba_ref, ng_ref,
                       w1_ref, w2_ref, w3_ref,
                       o_ref, hpre_ref, hpim_ref, s_ref, c1_ref, c2_ref, c3_ref, hre, him):
    s_blk = pl.program_id(0)

    @pl.when(s_blk == 0)
    def _():
        hpre_ref[...] = jnp.zeros_like(hpre_ref)
        hpim_ref[...] = jnp.zeros_like(hpim_ref)
        s_ref[...] = jnp.zeros_like(s_ref)

    c1_ref[...] = w1_ref[...].astype(BF16)
    c2_ref[...] = w2_ref[...].astype(BF16)
    c3_ref[...] = w3_ref[...].astype(BF16)
    _s5_block(u_ref, bb_ref, cre_ref, cim_ref, d_ref, wg_ref, bg_ref, tab_ref, o_ref, hpre_ref, hpim_ref, hre, him)
    _swa_block(l, s_blk, sink_ref, q_ref, kp_ref, kc_ref, vp_ref, vc_ref, o_ref)
    _gla_block(gq_ref, gk_ref, gv_ref, gg_ref, ga_ref, wa_ref, ba_ref, ng_ref, o_ref, s_ref)


def _mix_prompt(sink, h, s5_prm, gla_prm, w_casts, mix, l):
    cur = _seq_block
    prev = lambda s: _seq_block(jnp.maximum(s - 1, 0))
    blk = lambda c, w, at=cur: pl.BlockSpec((BLK, w), lambda s: (at(s), c // w))
    const = lambda shape: pl.BlockSpec(shape, lambda s: (0,) * len(shape))
    casts = [_cast_specs(w, rows, l) for w, rows in zip(w_casts, (32, 176, 32))]
    in_specs = ([pl.BlockSpec(memory_space=pltpu.SMEM), blk(C_Q, A_WIDTH),
                 blk(C_K, KV_WIDTH, prev), blk(C_K, KV_WIDTH), blk(C_V, KV_WIDTH, prev), blk(C_V, KV_WIDTH),
                 blk(C_U, S5_WIDTH),
                 blk(C_GQ, GLA_KW), blk(C_GK, GLA_KW), blk(C_GV, GLA_WIDTH), blk(C_GG, GLA_WIDTH), blk(C_GA, 128)]
                + _s5_param_specs(l) + _gla_param_specs(l) + [c[0] for c in casts])
    n_in = len(in_specs)
    return pl.pallas_call(
        _drop_inputs(functools.partial(_mix_prompt_kernel, l), n_in, 1),
        grid=(N_PBLK,),
        in_specs=in_specs + [_any()],
        out_specs=[pl.BlockSpec((BLK, D_MODEL), lambda s: (cur(s), 0)),
                   const((SUB, S5_LANES)), const((SUB, S5_LANES)), const((GLA_KW, GLA_DV))] + [c[1] for c in casts],
        out_shape=[jax.ShapeDtypeStruct((ROWS, D_MODEL), BF16),
                   jax.ShapeDtypeStruct((SUB, S5_LANES), F32), jax.ShapeDtypeStruct((SUB, S5_LANES), F32),
                   jax.ShapeDtypeStruct((GLA_KW, GLA_DV), F32)] + [c[2] for c in casts],
        scratch_shapes=[pltpu.VMEM((BLK, S5_LANES), F32), pltpu.VMEM((BLK, S5_LANES), F32)],
        input_output_aliases={n_in: 0},
        compiler_params=_cparams("arbitrary"),
        name="mix_prompt",
    )(sink, *([h] * 11), *_s5_params(s5_prm), *gla_prm, *w_casts, mix)


OUT_TM = 768


def _out_proj_kernel(mix_ref, w_ref, x_ref, g_ref, b_ref, o_ref):
    i = pl.program_id(0)
    o_ref[...] = _dot(mix_ref[...], w_ref[...])
    _deepnorm_ln(o_ref, x_ref, g_ref, b_ref, o_ref, i * OUT_TM, OUT_TM)


def _out_proj(mix, w, x, g, b, l):
    tm = OUT_TM
    return pl.pallas_call(
        _out_proj_kernel,
        grid=(ROWS // tm,),
        in_specs=[
            pl.BlockSpec((tm, D_MODEL), lambda i: (i, 0)),
            pl.BlockSpec((D_MODEL, D_MODEL), lambda i: (0, 0), pipeline_mode=pl.Buffered(1)),
            pl.BlockSpec((tm, D_MODEL), lambda i: (i, 0)),
            _layer_spec((1, D_MODEL), l), _layer_spec((1, D_MODEL), l),
        ],
        out_specs=pl.BlockSpec((tm, D_MODEL), lambda i: (i, 0)),
        out_shape=jax.ShapeDtypeStruct((ROWS, D_MODEL), F32),
        compiler_params=_cparams("arbitrary"),
        name="out_proj",
    )(mix, w, x, g, b)


FFN_TM = 768
FFN_TF = 512
FFN_HALO = 16
FFN_NI = ROWS // FFN_TM
assert ROWS % FFN_TM == 0 and FFN_TM >= N_SAMPLE + BLK + SUB and META_ROW % FFN_HALO == 0


def _ffn_kernel(final, x_ref, xh_ref, wu_ref, wg_ref, cw_ref, cb_ref, wd_ref, p0_ref, p1_ref, g_ref, b_ref,
                o_ref, *rest):
    if final:
        ys_ref, ut_ref, us_ref, xb_ref, a_ref = rest
    else:
        ut_ref, us_ref, xb_ref, a_ref = rest
    i, f = pl.program_id(0), pl.program_id(1)
    tm, hl = FFN_TM, FFN_HALO
    last_i = FFN_NI - 1

    @pl.when(f == 0)
    def _():
        xb_ref[...] = x_ref[...].astype(BF16)
        o_ref[...] = jnp.zeros_like(o_ref)

    xb = xb_ref[...]
    u = _dot(xb, wu_ref[...])
    gte = _dot(xb, wg_ref[...])
    cw0, cw1, cw2, cb = cw_ref[0:1, :], cw_ref[1:2, :], cw_ref[2:3, :], cb_ref[...]

    def act(um2, um1, u0, g0):
        return (jax.nn.gelu(cb + cw0 * um2 + cw1 * um1 + cw2 * u0) * g0).astype(BF16)

    um2, um1 = pltpu.roll(u, 2, 0), pltpu.roll(u, 1, 0)
    body = slice(hl, tm - N_SAMPLE)
    a_ref[body, :] = act(um2[body, :], um1[body, :], u[body, :], gte[body, :])
    ext = jnp.concatenate([_dot(xh_ref[...].astype(BF16), wu_ref[...]), u[0:hl, :]], axis=0)
    a_ref[0:hl, :] = act(ext[hl - 2:2 * hl - 2, :], ext[hl - 1:2 * hl - 1, :], u[0:hl, :], gte[0:hl, :])
    tail = slice(tm - N_SAMPLE, tm)
    samp = i == last_i
    a_ref[tail, :] = act(jnp.where(samp, p0_ref[...], um2[tail, :]), jnp.where(samp, p1_ref[...], um1[tail, :]),
                         u[tail, :], gte[tail, :])
    o_ref[...] += _dot(a_ref[...], wd_ref[...])
    ut_ref[...] = u[tm - N_SAMPLE - BLK - SUB:tm - N_SAMPLE - BLK, :]
    us_ref[0] = p1_ref[...]
    us_ref[1] = u[tm - N_SAMPLE:, :]

    @pl.when(f == pl.num_programs(1) - 1)
    def _():
        _deepnorm_ln(o_ref, x_ref, g_ref, b_ref, o_ref, i * tm, tm)
        if final:
            ys_ref[...] = o_ref[tm - N_SAMPLE:, :]


def _ffn(x, wup, cw, cb, wd, conv_p0, conv_p1, g, b, l, final):
    tm, tf = FFN_TM, FFN_TF
    nf = D_FF // tf
    tail = lambda i, f: jnp.where(i == FFN_NI - 1, f, 0)
    halo = lambda i, f: (jnp.where(i == 0, ROWS_P // FFN_HALO, i * (tm // FFN_HALO)) - 1, 0)
    y_spec = pl.BlockSpec((tm, D_MODEL), lambda i, f: (i, 0))
    y_specs = [y_spec, pl.BlockSpec((N_SAMPLE, D_MODEL), lambda i, f: (0, 0))] if final else [y_spec]
    y_shapes = ([jax.ShapeDtypeStruct((SEQ, D_MODEL), F32), jax.ShapeDtypeStruct((N_SAMPLE, D_MODEL), F32)]
                if final else [jax.ShapeDtypeStruct((ROWS, D_MODEL), F32)])
    return pl.pallas_call(
        functools.partial(_ffn_kernel, final),
        grid=(FFN_NI, nf),
        in_specs=[
            pl.BlockSpec((tm, D_MODEL), lambda i, f: (i, 0)),
            pl.BlockSpec((FFN_HALO, D_MODEL), halo),
            pl.BlockSpec((D_MODEL, tf), lambda i, f: (0, f)),
            pl.BlockSpec((D_MODEL, tf), lambda i, f: (0, nf + f)),
            pl.BlockSpec((None, CONV_W, tf), lambda i, f: (l, 0, f)),
            pl.BlockSpec((None, 1, tf), lambda i, f: (l, 0, f)),
            pl.BlockSpec((tf, D_MODEL), lambda i, f: (f, 0)),
            pl.BlockSpec((None, N_SAMPLE, tf), lambda i, f: (l, 0, tail(i, f))),
            pl.BlockSpec((None, N_SAMPLE, tf), lambda i, f: (l, 0, tail(i, f))),
            _layer_spec((1, D_MODEL), l), _layer_spec((1, D_MODEL), l),
        ],
        out_specs=y_specs + [
            pl.BlockSpec((SUB, tf), lambda i, f: (0, tail(i, f))),
            pl.BlockSpec((2, N_SAMPLE, tf), lambda i, f: (0, 0, tail(i, f))),
        ],
        out_shape=y_shapes + [
            jax.ShapeDtypeStruct((SUB, D_FF), F32),
            jax.ShapeDtypeStruct((2, N_SAMPLE, D_FF), F32),
        ],
        scratch_shapes=[pltpu.VMEM((tm, D_MODEL), BF16), pltpu.VMEM((tm, tf), BF16)],
        compiler_params=_cparams("arbitrary", "arbitrary"),
        name="conv_ffn",
    )(x, x, wup, wup, cw, cb, wd, conv_p0, conv_p1, g, b)


def _rope_table():
    inv = jnp.power(ROPE_THETA, -jnp.arange(ROPE_HALF, dtype=F32) / ROPE_HALF)
    pos = jnp.concatenate([jnp.arange(SEQ, dtype=jnp.int32) + N_META, jnp.zeros((LEAD,), jnp.int32),
                           jnp.arange(N_META, dtype=jnp.int32), jnp.full((N_SAMPLE,), PAST_LEN, jnp.int32)])
    ang = pos.astype(F32)[:, None] * inv[None, :]
    return jnp.concatenate([jnp.cos(ang), jnp.sin(ang)], axis=1)


def _s5_tables(lam_re, lam_im, log_dt, b_re, b_im, c_re, c_im, d, w_glu, b_glu):
    L = lam_re.shape[0]
    dt = jnp.exp(log_dt)[..., None]
    mag = jnp.exp(lam_re * dt)
    ab_re, ab_im = mag * jnp.cos(lam_im * dt), mag * jnp.sin(lam_im * dt)
    den = lam_re * lam_re + lam_im * lam_im
    nr, ni = ab_re - 1.0, ab_im
    f_re = (nr * lam_re + ni * lam_im) / den
    f_im = (ni * lam_re - nr * lam_im) / den
    bb_re = f_re[..., None] * b_re - f_im[..., None] * b_im
    bb_im = f_re[..., None] * b_im + f_im[..., None] * b_re
    gi, go = S5_GROUPS // S5_IN_PIECES, S5_GROUPS // S5_OUT_PIECES

    def bd_in(t):
        t = t.reshape(L, S5_IN_PIECES, gi, S5_STATE, S5_CH)
        return jnp.einsum('lqgph,gk->lqghkp', t, jnp.eye(gi, dtype=F32)).reshape(L, S5_IN_PIECES, S5_IN_K, S5_IN_N)

    def bd_out(t):
        t = t.reshape(L, S5_OUT_PIECES, go, S5_CH, S5_STATE)
        return jnp.einsum('lqghp,gk->lqgpkh', t, jnp.eye(go, dtype=F32)).reshape(
            L, S5_OUT_PIECES, S5_OUT_K, S5_OUT_N)

    ar, ai = ab_re.reshape(L, 1, S5_LANES), ab_im.reshape(L, 1, S5_LANES)
    pows = [(ar, ai)]
    for _ in range(SUB - 1):
        pr, pi = pows[-1]
        pows.append((pr * ar - pi * ai, pr * ai + pi * ar))
    row = jnp.arange(SUB)[None, :, None]
    parts = []
    for sft in S5_SHIFTS:
        pr, pi = pows[sft - 1]
        parts += [jnp.where(row >= sft, pr, 0.0), jnp.where(row >= sft, pi, 0.0)]
    parts += [jnp.concatenate([p[0] for p in pows], 1), jnp.concatenate([p[1] for p in pows], 1)]
    return dict(bb=jnp.concatenate([bd_in(bb_re), bd_in(bb_im)], axis=3).astype(BF16),
                c_re=bd_out(c_re).astype(BF16), c_im=bd_out(c_im).astype(BF16),
                d=d.reshape(L, 1, S5_WIDTH), w_glu=w_glu.astype(BF16), b_glu=b_glu.reshape(L, 1, S5_WIDTH),
                tab=jnp.concatenate(parts, 1))


@jax.jit
def kernel(x_prompt, x_sample, cache_swa_k, cache_swa_v, state_ssm_re, state_ssm_im, state_gla, state_conv,
           meta_tokens, ln_in_g, ln_in_b, w_in, attn_sink, s5_lam_re, s5_lam_im, s5_log_dt, s5_b_re, s5_b_im,
           s5_c_re, s5_c_im, s5_d, s5_w_glu, s5_b_glu, gla_w_a2, gla_b_a, gla_norm_g, w_out, ln1_g, ln1_b,
           ffn_w_up, ffn_conv_w, ffn_conv_b, ffn_w_down, ln2_g, ln2_b):
    L = DEPTH
    row3 = lambda t: t.reshape(L, 1, -1)
    w_in_b = w_in.astype(BF16)
    wa_b = jnp.pad(gla_w_a2, ((0, 0), (0, 128 - GLA_LOWRANK), (0, 0))).astype(BF16)
    gla_args = (wa_b, row3(gla_b_a), row3(gla_norm_g))
    ln1, ln2 = (row3(ln1_g), row3(ln1_b)), (row3(ln2_g), row3(ln2_b))
    conv_b3 = row3(ffn_conv_b)
    cs = _rope_table()
    s5p = _s5_tables(s5_lam_re, s5_lam_im, s5_log_dt, s5_b_re, s5_b_im, s5_c_re, s5_c_im, s5_d, s5_w_glu, s5_b_glu)
    ck = cache_swa_k.reshape(L, N_SAMPLE, WINDOW, KV_WIDTH)
    cv = cache_swa_v.reshape(L, N_SAMPLE, WINDOW, KV_WIDTH)
    h0re = state_ssm_re.reshape(L, N_SAMPLE, S5_LANES)
    h0im = state_ssm_im.reshape(L, N_SAMPLE, S5_LANES)
    sg = state_gla.reshape(L, N_SAMPLE, GLA_KW, GLA_DV)
    conv_p0, conv_p1 = state_conv[:, :, 0, :], state_conv[:, :, 1, :]
    mix = jnp.zeros((ROWS, D_MODEL), BF16)
    s_all = jnp.zeros((L, N_SAMPLE, GLA_KW, GLA_DV), F32)

    x = _ln_in(x_prompt.reshape(SEQ, D_MODEL), meta_tokens, x_sample.reshape(N_SAMPLE, D_MODEL),
               ln_in_g.reshape(1, D_MODEL), ln_in_b.reshape(1, D_MODEL))
    outs = [[] for _ in range(9)]
    for l in range(L):
        h = _in_proj(x, w_in_b, cs, l)
        mix, hp_re, hp_im, s_p, w_up_b, w_down_b, w_out_b = _mix_prompt(
            attn_sink, h, s5p, gla_args, (ffn_w_up, ffn_w_down, w_out), mix, l)
        mix, ck, cv = _swa_sample(attn_sink, h, ck, cv, mix, l)
        mix, hs_re, hs_im = _s5_sample(h, s5p, h0re, h0im, mix, l)
        mix, s_all = _gla_sample(h, *gla_args, sg, mix, s_all, l)
        x = _out_proj(mix, w_out_b, x, *ln1, l)
        res = _ffn(x, w_up_b, ffn_conv_w, conv_b3, w_down_b, conv_p0, conv_p1, *ln2, l, l == L - 1)
        if l == L - 1:
            y_prompt, y_sample, u_tail, u_s = res
        else:
            x, u_tail, u_s = res
        kv_p = lambda c: h[SEQ - WINDOW:SEQ, c:c + KV_WIDTH].reshape(1, WINDOW, A_KV_HEADS, HEAD_DIM)
        new = (kv_p(C_K), kv_p(C_V),
               hp_re[SUB - 1].reshape(1, S5_GROUPS, S5_STATE), hp_im[SUB - 1].reshape(1, S5_GROUPS, S5_STATE),
               s_p.reshape(1, GLA_HEADS, GLA_DK, GLA_DV), u_tail[SUB - 2:SUB][None],
               hs_re.reshape(N_SAMPLE, S5_GROUPS, S5_STATE), hs_im.reshape(N_SAMPLE, S5_GROUPS, S5_STATE),
               jnp.transpose(u_s, (1, 0, 2)))
        for lst, val in zip(outs, new):
            lst.append(val)
    st = [jnp.stack(lst) for lst in outs]
    return (y_prompt.reshape(1, SEQ, D_MODEL), y_sample.reshape(N_SAMPLE, 1, D_MODEL),
            st[0], st[1], st[2], st[3], st[4], st[5],
            ck.reshape(L, N_SAMPLE, WINDOW, A_KV_HEADS, HEAD_DIM),
            cv.reshape(L, N_SAMPLE, WINDOW, A_KV_HEADS, HEAD_DIM),
            st[6], st[7], s_all.reshape(L, N_SAMPLE, GLA_HEADS, GLA_DK, GLA_DV), st[8])
```

```python
import functools

import jax
import jax.numpy as jnp
from jax import lax
from jax.experimental import pallas as pl
from jax.experimental.pallas import tpu as pltpu

F32 = jnp.float32
BF16 = jnp.bfloat16

D_MODEL = 2048
SEQ = 8192
DEPTH = 2
N_SAMPLE = 128
N_META = 16
HEAD_DIM = 64
A_WIDTH = 1024
A_HEADS = 16
A_KV_HEADS = 4
GQA = 4
KV_WIDTH = A_KV_HEADS * HEAD_DIM
WINDOW = 128
ROPE_DIM = 16
ROPE_HALF = ROPE_DIM // 2
ROPE_THETA = 500000.0
PAST_LEN = 8192
S5_WIDTH = 512
S5_CH = 16
S5_GROUPS = 32
S5_STATE = 64
S5_LANES = S5_GROUPS * S5_STATE
GLA_WIDTH = 512
GLA_HEADS = 4
GLA_DV = 128
GLA_DK = 64
GLA_KW = GLA_HEADS * GLA_DK
GLA_LOWRANK = 16
GLA_TAU = 16.0
GLA_CHUNK = 64
D_FF = 5632
CONV_W = 3
LN_EPS = 1e-5
ALPHA = (2 * DEPTH) ** 0.25
NEG_INF = -1e30

BLK = 128
T_PROMPT = N_META + SEQ
LEAD = (-T_PROMPT) % BLK
N_PBLK = (LEAD + T_PROMPT) // BLK
ROWS_P = N_PBLK * BLK
ROWS = ROWS_P + N_SAMPLE
META_ROW = SEQ + LEAD
SUB = 8

C_Q, C_K, C_V, C_U, C_GQ, C_GK, C_GV, C_GG, C_GA = 0, 1024, 1280, 1536, 2048, 2304, 2560, 3072, 3584
IN_COLS = 3600
IN_TILE = 1280
IN_PAD = 3 * IN_TILE
ROPE_COLS = A_WIDTH + KV_WIDTH
M_A, M_B, M_C = 0, A_WIDTH, A_WIDTH + S5_WIDTH

VMEM_LIMIT = 56 * 1024 * 1024


def _cparams(*sem):
    return pltpu.CompilerParams(dimension_semantics=sem, vmem_limit_bytes=VMEM_LIMIT)


def _any():
    return pl.BlockSpec(memory_space=pl.ANY)


def _drop_inputs(kern, start, count):
    def wrapped(*refs):
        return kern(*refs[:start], *refs[start + count:])
    return wrapped


def _seq_block(s):
    return jnp.where(s == 0, N_PBLK - 1, s - 1)


def _layer_spec(shape, l):
    return pl.BlockSpec((None,) + shape, lambda *_: (l,) + (0,) * len(shape))


def _cast_specs(w, rows, l):
    n, cols = w.shape[1] // rows, w.shape[2]
    chunk = lambda s: jnp.minimum(s, n - 1)
    return (pl.BlockSpec((None, rows, cols), lambda s: (l, chunk(s), 0)),
            pl.BlockSpec((rows, cols), lambda s: (chunk(s), 0)),
            jax.ShapeDtypeStruct(w.shape[1:], BF16))


def _layer_norm_rows(x, g, b):
    mu = jnp.mean(x, -1, keepdims=True)
    xc = x - mu
    var = jnp.mean(xc * xc, -1, keepdims=True)
    return xc * lax.rsqrt(var + LN_EPS) * g + b


def _zero_pad_rows(y, first_row):
    row = first_row + lax.broadcasted_iota(jnp.int32, y.shape, 0)
    return jnp.where((row >= SEQ) & (row < META_ROW), 0.0, y)


def _deepnorm_ln(acc_ref, x_ref, g_ref, b_ref, o_ref, first_row, rows):
    g, b = g_ref[...], b_ref[...]
    for c in range(rows // BLK):
        r = slice(c * BLK, (c + 1) * BLK)
        y = _layer_norm_rows(ALPHA * x_ref[r, :] + acc_ref[r, :], g, b)
        o_ref[r, :] = _zero_pad_rows(y, first_row + c * BLK)


def _dot(a, b):
    return jnp.dot(a, b, preferred_element_type=F32)


def _dot_nt(a, b):
    return lax.dot_general(a, b, (((1,), (1,)), ((), ())), preferred_element_type=F32)


def _dot_tn(a, b):
    return lax.dot_general(a, b, (((0,), (0,)), ((), ())), preferred_element_type=F32)


def _log_sigmoid(x):
    return jnp.minimum(x, 0.0) - jnp.log(1.0 + jnp.exp(-jnp.abs(x)))


LN_IN_TM = 256
assert ROWS - SEQ == LN_IN_TM and SEQ % LN_IN_TM == 0


def _ln_in_kernel(xp_ref, meta_ref, xs_ref, g_ref, b_ref, o_ref):
    i = pl.program_id(0)
    g, b = g_ref[...], b_ref[...]

    @pl.when(i < SEQ // LN_IN_TM)
    def _():
        o_ref[...] = _layer_norm_rows(xp_ref[...], g, b)

    @pl.when(i == SEQ // LN_IN_TM)
    def _():
        o_ref[0:LEAD, :] = jnp.zeros((LEAD, D_MODEL), F32)
        o_ref[LEAD:BLK, :] = _layer_norm_rows(meta_ref[...], g, b)
        o_ref[BLK:, :] = _layer_norm_rows(xs_ref[...], g, b)


def _ln_in(xp, meta, xs, g, b):
    tm = LN_IN_TM
    const = lambda shape: pl.BlockSpec(shape, lambda i: (0,) * len(shape))
    return pl.pallas_call(
        _ln_in_kernel,
        grid=(ROWS // tm,),
        in_specs=[
            pl.BlockSpec((tm, D_MODEL), lambda i: (jnp.minimum(i, SEQ // tm - 1), 0)),
            const((N_META, D_MODEL)), const((N_SAMPLE, D_MODEL)), const((1, D_MODEL)), const((1, D_MODEL)),
        ],
        out_specs=pl.BlockSpec((tm, D_MODEL), lambda i: (i, 0)),
        out_shape=jax.ShapeDtypeStruct((ROWS, D_MODEL), F32),
        compiler_params=_cparams("arbitrary"),
        name="ln_in",
    )(xp, meta, xs, g, b)


IN_TM = 1056


def _in_proj_kernel(x_ref, w_ref, cs_ref, o_ref, xb_ref):
    j = pl.program_id(1)

    last_j = IN_PAD // IN_TILE - 1

    @pl.when(j == 0)
    def _():
        xb = x_ref[...].astype(BF16)
        xb_ref[...] = xb
        c8, s8 = cs_ref[:, 0:ROPE_HALF], cs_ref[:, ROPE_HALF:ROPE_DIM]
        rest = HEAD_DIM - ROPE_DIM
        one, zero = jnp.ones((IN_TM, rest), F32), jnp.zeros((IN_TM, rest), F32)
        z8 = jnp.zeros((IN_TM, ROPE_HALF), F32)
        cos = jnp.concatenate([c8, c8, one] * 2, axis=1)
        sa = jnp.concatenate([-s8, z8, zero] * 2, axis=1)
        sb = jnp.concatenate([z8, s8, zero] * 2, axis=1)
        for c in range(ROPE_COLS // 256):
            acc = _dot(xb, w_ref[:, c * 256:(c + 1) * 256])
            for hlf in range(2):
                blk = acc[:, hlf * 128:(hlf + 1) * 128]
                o_ref[:, c * 256 + hlf * 128:c * 256 + (hlf + 1) * 128] = (
                    blk * cos + pltpu.roll(blk, 128 - ROPE_HALF, 1) * sa + pltpu.roll(blk, ROPE_HALF, 1) * sb)

    @pl.when((j > 0) & (j < last_j))
    def _():
        o_ref[...] = _dot(xb_ref[...], w_ref[...])

    @pl.when(j == last_j)
    def _():
        o_ref[...] = _dot(xb_ref[...], w_ref[...])
        edge = (IN_COLS % IN_TILE) // 128 * 128
        col = edge + lax.broadcasted_iota(jnp.int32, (IN_TM, 128), 1)
        o_ref[:, edge:edge + 128] = jnp.where(col < IN_COLS % IN_TILE, o_ref[:, edge:edge + 128], 0.0)
        o_ref[:, edge + 128:] = jnp.zeros((IN_TM, IN_TILE - edge - 128), F32)


def _in_proj(x, w, cs, l):
    return pl.pallas_call(
        _in_proj_kernel,
        grid=(ROWS // IN_TM, IN_PAD // IN_TILE),
        in_specs=[
            pl.BlockSpec((IN_TM, D_MODEL), lambda i, j: (i, 0)),
            pl.BlockSpec((None, D_MODEL, IN_TILE), lambda i, j: (l, 0, j)),
            pl.BlockSpec((IN_TM, ROPE_DIM), lambda i, j: (i, 0)),
        ],
        out_specs=pl.BlockSpec((IN_TM, IN_TILE), lambda i, j: (i, j)),
        out_shape=jax.ShapeDtypeStruct((ROWS, IN_PAD), F32),
        scratch_shapes=[pltpu.VMEM((IN_TM, D_MODEL), BF16)],
        compiler_params=_cparams("arbitrary", "arbitrary"),
        name="in_proj",
    )(x, w, cs)


def _swa_bias():
    row = lax.broadcasted_iota(jnp.int32, (3, BLK, 2 * BLK), 1)
    col = lax.broadcasted_iota(jnp.int32, (3, BLK, 2 * BLK), 2)
    s_blk = lax.broadcasted_iota(jnp.int32, (3, BLK, 2 * BLK), 0)
    diff = row - col + BLK
    kpos = (s_blk - 1) * BLK + col - LEAD
    ok = (diff >= 0) & (diff <= WINDOW) & (kpos >= 0)
    return jnp.where(ok, 0.0, NEG_INF).astype(F32)


def _swa_block(l, sink_ref, bias_ref, q_ref, kp_ref, kc_ref, vp_ref, vc_ref, o_ref):
    k2 = jnp.concatenate([kp_ref[...], kc_ref[...]], axis=0).astype(BF16)
    v2 = jnp.concatenate([vp_ref[...], vc_ref[...]], axis=0).astype(BF16)
    nq = GQA * BLK
    bias = jnp.concatenate([bias_ref[...]] * GQA, axis=0)
    low = lax.broadcasted_iota(jnp.int32, (BLK, 2 * HEAD_DIM), 1) < HEAD_DIM
    g_of_row = lax.broadcasted_iota(jnp.int32, (nq, 1), 0) // BLK
    ones = jnp.ones((2 * BLK, 2 * HEAD_DIM), BF16)
    scores, sinks = [], []
    for kv in range(A_KV_HEADS):
        kh = k2[:, kv * HEAD_DIM:(kv + 1) * HEAD_DIM]
        kk = jnp.concatenate([kh, kh], axis=1)
        parts = []
        for pair in range(GQA // 2):
            c0 = (kv * GQA + 2 * pair) * HEAD_DIM
            qp = q_ref[:, c0:c0 + 2 * HEAD_DIM] * HEAD_DIM ** -0.5
            parts += [jnp.where(low, qp, 0.0).astype(BF16), jnp.where(low, 0.0, qp).astype(BF16)]
        q4 = jnp.concatenate(parts, axis=0)
        sk = jnp.zeros((nq, 1), F32)
        for g in range(GQA):
            sk = jnp.where(g_of_row == g, sink_ref[l, kv * GQA + g], sk)
        scores.append(_dot_nt(q4, kk) + bias)
        sinks.append(sk)
    probs, sink_terms = [], []
    for s, sk in zip(scores, sinks):
        m = jnp.maximum(jnp.max(s, -1, keepdims=True), sk)
        probs.append(jnp.exp(s - m).astype(BF16))
        sink_terms.append(jnp.exp(sk - m))
    for kv in range(A_KV_HEADS):
        vh = v2[:, kv * HEAD_DIM:(kv + 1) * HEAD_DIM]
        v3 = jnp.concatenate([vh, vh, ones], axis=1)
        oa = _dot(probs[kv], v3)
        on = oa[:, 0:2 * HEAD_DIM] / (oa[:, 2 * HEAD_DIM:] + sink_terms[kv])
        for pair in range(GQA // 2):
            c0 = (kv * GQA + 2 * pair) * HEAD_DIM
            lo, hi = on[2 * pair * BLK:(2 * pair + 1) * BLK], on[(2 * pair + 1) * BLK:(2 * pair + 2) * BLK]
            o_ref[:, M_A + c0:M_A + c0 + 2 * HEAD_DIM] = jnp.where(low, lo, hi).astype(BF16)


SWA_BT = 16


def _swa_sample_kernel(l, sink_ref, q_ref, kn_ref, vn_ref, ck_ref, cv_ref, o_ref, cko_ref, cvo_ref):
    bt = SWA_BT
    kn, vn = kn_ref[...], vn_ref[...]
    k2 = ck_ref[...].reshape(bt * WINDOW, KV_WIDTH).astype(BF16)
    v2 = cv_ref[...].reshape(bt * WINDOW, KV_WIDTH).astype(BF16)
    row = lax.broadcasted_iota(jnp.int32, (WINDOW, KV_WIDTH), 0)
    for j in range(bt):
        cko_ref[j] = jnp.where(row == WINDOW - 1, kn[j:j + 1, :], pltpu.roll(ck_ref[j], WINDOW - 1, 0))
        cvo_ref[j] = jnp.where(row == WINDOW - 1, vn[j:j + 1, :], pltpu.roll(cv_ref[j], WINDOW - 1, 0))
    nq = GQA * bt
    rowb = lax.broadcasted_iota(jnp.int32, (nq, bt * WINDOW), 0) % bt
    colb = lax.broadcasted_iota(jnp.int32, (nq, bt * WINDOW), 1) // WINDOW
    same = rowb == colb
    rg = lax.broadcasted_iota(jnp.int32, (nq, 1), 0) // bt
    for kv in range(A_KV_HEADS):
        sl = slice(kv * HEAD_DIM, (kv + 1) * HEAD_DIM)
        q4 = jnp.concatenate(
            [q_ref[:, (kv * GQA + g) * HEAD_DIM:(kv * GQA + g + 1) * HEAD_DIM] for g in range(GQA)], axis=0)
        q4 = q4 * HEAD_DIM ** -0.5
        kn4 = jnp.concatenate([kn[:, sl]] * GQA, axis=0)
        vn4 = jnp.concatenate([vn[:, sl]] * GQA, axis=0)
        sk = jnp.zeros((nq, 1), F32)
        for g in range(GQA):
            sk = jnp.where(rg == g, sink_ref[l, kv * GQA + g], sk)
        s = jnp.where(same, _dot_nt(q4.astype(BF16), k2[:, sl]), NEG_INF)
        s_self = jnp.sum(q4 * kn4, -1, keepdims=True)
        m = jnp.maximum(jnp.maximum(jnp.max(s, -1, keepdims=True), s_self), sk)
        p = jnp.exp(s - m)
        p_self = jnp.exp(s_self - m)
        den = jnp.sum(p, -1, keepdims=True) + p_self + jnp.exp(sk - m)
        o = (_dot(p.astype(BF16), v2[:, sl]) + p_self * vn4) / den
        for g in range(GQA):
            hh = kv * GQA + g
            o_ref[:, hh * HEAD_DIM:(hh + 1) * HEAD_DIM] = o[g * bt:(g + 1) * bt].astype(BF16)


def _swa_sample(sink, h, cache_k, cache_v, mix, l):
    bt = SWA_BT
    r0 = ROWS_P // bt
    cache = pl.BlockSpec((None, bt, WINDOW, KV_WIDTH), lambda s: (l, s, 0, 0))
    n_in = 6
    return pl.pallas_call(
        _drop_inputs(functools.partial(_swa_sample_kernel, l), n_in, 1),
        grid=(N_SAMPLE // bt,),
        in_specs=[
            pl.BlockSpec(memory_space=pltpu.SMEM),
            pl.BlockSpec((bt, A_WIDTH), lambda s: (r0 + s, C_Q // A_WIDTH)),
            pl.BlockSpec((bt, KV_WIDTH), lambda s: (r0 + s, C_K // KV_WIDTH)),
            pl.BlockSpec((bt, KV_WIDTH), lambda s: (r0 + s, C_V // KV_WIDTH)),
            cache, cache,
            _any(),
        ],
        out_specs=[pl.BlockSpec((bt, A_WIDTH), lambda s: (r0 + s, M_A // A_WIDTH)), cache, cache],
        out_shape=[jax.ShapeDtypeStruct((ROWS, D_MODEL), BF16),
                   jax.ShapeDtypeStruct(cache_k.shape, F32), jax.ShapeDtypeStruct(cache_v.shape, F32)],
        input_output_aliases={n_in: 0, 4: 1, 5: 2},
        compiler_params=_cparams("arbitrary"),
        name="swa_sample",
    )(sink, h, h, h, cache_k, cache_v, mix)


S5_SHIFTS = (1, 2, 4)
S5_NTAB = 2 * (len(S5_SHIFTS) + 1) * SUB
S5_CHUNKS = S5_LANES // 128
S5_IN_PIECES = 4
S5_IN_K = S5_WIDTH // S5_IN_PIECES
S5_IN_N = S5_LANES // S5_IN_PIECES
S5_OUT_PIECES = 2
S5_OUT_K = S5_LANES // S5_OUT_PIECES
S5_OUT_N = S5_WIDTH // S5_OUT_PIECES


def _s5_tab(tab_ref, k, lanes):
    return (tab_ref[2 * k * SUB:(2 * k + 1) * SUB, lanes], tab_ref[(2 * k + 1) * SUB:(2 * k + 2) * SUB, lanes])


def _s5_drive(ub, bb_ref, p):
    return _dot(ub[:, p * S5_IN_K:(p + 1) * S5_IN_K], bb_ref[p])


def _s5_readout(h_piece, u, cre_ref, cim_ref, d_ref, wg_ref, bg_ref):
    ys = []
    for j in range(S5_OUT_PIECES):
        hr, hi = h_piece(j)
        ys.append(_dot(hr.astype(BF16), cre_ref[j]) - _dot(hi.astype(BF16), cim_ref[j]))
    y = jnp.concatenate(ys, axis=1) + d_ref[...] * u
    z = jax.nn.gelu(y)
    gate = jax.nn.sigmoid(_dot(z.astype(BF16), wg_ref[...]) + bg_ref[...])
    return (z * gate).astype(BF16)


def _s5_block(u_ref, bb_ref, cre_ref, cim_ref, d_ref, wg_ref, bg_ref, ab_ref, o_ref, hpre_ref, hpim_ref,
              xre, xim, hre, him):
    u = u_ref[...]
    ub = u.astype(BF16)
    drive = [_s5_drive(ub, bb_ref, p) for p in range(S5_IN_PIECES)]
    x3r = jnp.concatenate([d[:, 0:S5_IN_N] for d in drive], axis=1).reshape(BLK, S5_CHUNKS, 128)
    x3i = jnp.concatenate([d[:, S5_IN_N:] for d in drive], axis=1).reshape(BLK, S5_CHUNKS, 128)
    ar, ai = ab_ref[0:S5_CHUNKS, :], ab_ref[S5_CHUNKS:, :]
    hr, hi = hpre_ref[...], hpim_ref[...]
    hs_r, hs_i = [], []
    for t in range(BLK):
        hr, hi = ar * hr - ai * hi + x3r[t], ar * hi + ai * hr + x3i[t]
        hs_r.append(hr)
        hs_i.append(hi)
    hpre_ref[...] = hr
    hpim_ref[...] = hi
    h2r = jnp.stack(hs_r, axis=0).reshape(BLK, S5_LANES)
    h2i = jnp.stack(hs_i, axis=0).reshape(BLK, S5_LANES)

    def piece(j):
        return h2r[:, j * S5_OUT_K:(j + 1) * S5_OUT_K], h2i[:, j * S5_OUT_K:(j + 1) * S5_OUT_K]

    o_ref[:, M_B:M_B + S5_WIDTH] = _s5_readout(piece, u, cre_ref, cim_ref, d_ref, wg_ref, bg_ref)


def _s5_sample_kernel(u_ref, bb_ref, cre_ref, cim_ref, d_ref, wg_ref, bg_ref, tab_ref, h0re_ref, h0im_ref,
                      o_ref, hsre_ref, hsim_ref):
    u = u_ref[...]
    ub = u.astype(BF16)
    for p in range(S5_IN_PIECES):
        lanes = slice(p * S5_IN_N, (p + 1) * S5_IN_N)
        bu = _s5_drive(ub, bb_ref, p)
        pr, pi = _s5_tab(tab_ref, len(S5_SHIFTS), lanes)
        ar, ai = pr[0:1, :], pi[0:1, :]
        h0r, h0i = h0re_ref[:, lanes], h0im_ref[:, lanes]
        hsre_ref[:, lanes] = ar * h0r - ai * h0i + bu[:, 0:S5_IN_N]
        hsim_ref[:, lanes] = ar * h0i + ai * h0r + bu[:, S5_IN_N:]
    piece = lambda j: (hsre_ref[:, j * S5_OUT_K:(j + 1) * S5_OUT_K], hsim_ref[:, j * S5_OUT_K:(j + 1) * S5_OUT_K])
    o_ref[...] = _s5_readout(piece, u, cre_ref, cim_ref, d_ref, wg_ref, bg_ref)


def _s5_param_specs(l, prompt):
    return [
        _layer_spec((S5_IN_PIECES, S5_IN_K, 2 * S5_IN_N), l),
        _layer_spec((S5_OUT_PIECES, S5_OUT_K, S5_OUT_N), l), _layer_spec((S5_OUT_PIECES, S5_OUT_K, S5_OUT_N), l),
        _layer_spec((1, S5_WIDTH), l), _layer_spec((S5_WIDTH, S5_WIDTH), l), _layer_spec((1, S5_WIDTH), l),
        _layer_spec((2 * S5_CHUNKS, 128), l) if prompt else _layer_spec((S5_NTAB, S5_LANES), l),
    ]


def _s5_params(prm, prompt):
    return (prm["bb"], prm["c_re"], prm["c_im"], prm["d"], prm["w_glu"], prm["b_glu"],
            prm["ab"] if prompt else prm["tab"])


def _s5_sample(h, prm, h0re, h0im, mix, l):
    const = lambda shape: pl.BlockSpec(shape, lambda i: (0,) * len(shape))
    n_in = 10
    return pl.pallas_call(
        _drop_inputs(_s5_sample_kernel, n_in, 1),
        grid=(1,),
        in_specs=([pl.BlockSpec((N_SAMPLE, S5_WIDTH), lambda i: (N_PBLK, C_U // S5_WIDTH))]
                  + _s5_param_specs(l, False) + [_layer_spec((N_SAMPLE, S5_LANES), l), _layer_spec((N_SAMPLE, S5_LANES), l), _any()]),
        out_specs=[
            pl.BlockSpec((N_SAMPLE, S5_WIDTH), lambda i: (N_PBLK, M_B // S5_WIDTH)),
            const((N_SAMPLE, S5_LANES)), const((N_SAMPLE, S5_LANES)),
        ],
        out_shape=[
            jax.ShapeDtypeStruct((ROWS, D_MODEL), BF16),
            jax.ShapeDtypeStruct((N_SAMPLE, S5_LANES), F32), jax.ShapeDtypeStruct((N_SAMPLE, S5_LANES), F32),
        ],
        input_output_aliases={n_in: 0},
        compiler_params=_cparams("arbitrary"),
        name="s5_sample",
    )(h, *_s5_params(prm, False), h0re, h0im, mix)


def _gla_log_gate(ac, wa_ref, ba_ref):
    return _log_sigmoid(_dot(ac.astype(BF16), wa_ref[...]) + ba_ref[...]) * (1.0 / GLA_TAU)


def _gla_finish(o_heads, gate, ng):
    outs = [o * lax.rsqrt(jnp.mean(o * o, -1, keepdims=True) + LN_EPS) for o in o_heads]
    return jnp.concatenate(outs, axis=1) * ng * (gate * jax.nn.sigmoid(gate))


def _gla_block(q_ref, k_ref, v_ref, gate_ref, ac_ref, wa_ref, ba_ref, ng_ref, o_ref, s_ref):
    lg = _gla_log_gate(ac_ref[...], wa_ref, ba_ref)
    L = GLA_CHUNK
    row = lax.broadcasted_iota(jnp.int32, (L, GLA_KW), 0)
    causal = lax.broadcasted_iota(jnp.int32, (L, L), 0) >= lax.broadcasted_iota(jnp.int32, (L, L), 1)
    chunks = range(BLK // L)
    heads = range(GLA_HEADS)
    ks = lambda hh: slice(hh * GLA_DK, (hh + 1) * GLA_DK)
    vs = lambda hh: slice(hh * GLA_DV, (hh + 1) * GLA_DV)
    q_in, k_out, k_dec, dec, vb = [], [], [], [], []
    for c in chunks:
        rows = slice(c * L, (c + 1) * L)
        b = lg[rows, :]
        d = 1
        while d < L:
            b = b + jnp.where(row >= d, pltpu.roll(b, d, 0), 0.0)
            d *= 2
        b_end = b[L - 1:L, :]
        kc = k_ref[rows, :]
        q_in.append((q_ref[rows, :] * GLA_DK ** -0.5 * jnp.exp(b)).astype(BF16))
        k_out.append((kc * jnp.exp(-b)).astype(BF16))
        k_dec.append((kc * jnp.exp(b_end - b)).astype(BF16))
        dec.append(jnp.transpose(jnp.broadcast_to(jnp.exp(b_end), (GLA_DV, GLA_KW))))
        vb.append(v_ref[rows, :].astype(BF16))
    att = [[jnp.where(causal, _dot_nt(q_in[c][:, ks(hh)], k_out[c][:, ks(hh)]), 0.0).astype(BF16) for hh in heads]
           for c in chunks]
    kv = [[_dot_tn(k_dec[c][:, ks(hh)], vb[c][:, vs(hh)]) for hh in heads] for c in chunks]
    state = [s_ref[...]]
    for c in chunks:
        state.append(dec[c] * state[c] + jnp.concatenate(kv[c], axis=0))
    s_ref[...] = state[-1]
    for c in chunks:
        sb = state[c].astype(BF16)
        o_heads = [_dot(q_in[c][:, ks(hh)], sb[ks(hh), :]) + _dot(att[c][hh], vb[c][:, vs(hh)]) for hh in heads]
        o_ref[c * L:(c + 1) * L, M_C:M_C + GLA_WIDTH] = _gla_finish(
            o_heads, gate_ref[c * L:(c + 1) * L, :], ng_ref[...]).astype(BF16)


def _gla_param_specs(l):
    return [_layer_spec((128, GLA_KW), l), _layer_spec((1, GLA_KW), l), _layer_spec((1, GLA_WIDTH), l)]


GLA_BT = 16


def _gla_sample_kernel(q_ref, k_ref, v_ref, gate_ref, ac_ref, wa_ref, ba_ref, ng_ref, s_ref, o_ref, so_ref):
    bt = GLA_BT
    lg = _gla_log_gate(ac_ref[...], wa_ref, ba_ref)
    v = v_ref[...]

    def cols(x):
        return jnp.transpose(jnp.concatenate([x, jnp.zeros((128 - bt, GLA_KW), F32)], axis=0))

    eg_t, k_t, q_t = cols(jnp.exp(lg)), cols(k_ref[...]), cols(q_ref[...] * GLA_DK ** -0.5)
    o_rows = []
    for j in range(bt):
        bc = lambda t: jnp.broadcast_to(t[:, j:j + 1], (GLA_KW, GLA_DV))
        vrow = jnp.concatenate(
            [jnp.broadcast_to(v[j:j + 1, hh * GLA_DV:(hh + 1) * GLA_DV], (GLA_DK, GLA_DV)) for hh in range(GLA_HEADS)],
            axis=0)
        s_new = bc(eg_t) * s_ref[j] + bc(k_t) * vrow
        so_ref[j] = s_new
        qs = bc(q_t) * s_new
        o_rows.append(jnp.concatenate(
            [jnp.sum(qs[hh * GLA_DK:(hh + 1) * GLA_DK, :], axis=0, keepdims=True) for hh in range(GLA_HEADS)], axis=1))
    o = jnp.concatenate(o_rows, axis=0)
    o_heads = [o[:, hh * GLA_DV:(hh + 1) * GLA_DV] for hh in range(GLA_HEADS)]
    o_ref[...] = _gla_finish(o_heads, gate_ref[...], ng_ref[...]).astype(BF16)


def _gla_sample(h, wa, ba, ng, state, mix, s_all, l):
    bt = GLA_BT
    r0 = ROWS_P // bt
    blk = lambda c, w: pl.BlockSpec((bt, w), lambda s: (r0 + s, c // w))
    st = pl.BlockSpec((None, bt, GLA_KW, GLA_DV), lambda s: (l, s, 0, 0))
    n_in = 9
    return pl.pallas_call(
        _drop_inputs(_gla_sample_kernel, n_in, 2),
        grid=(N_SAMPLE // bt,),
        in_specs=([blk(C_GQ, GLA_KW), blk(C_GK, GLA_KW), blk(C_GV, GLA_WIDTH), blk(C_GG, GLA_WIDTH), blk(C_GA, 128)]
                  + _gla_param_specs(l) + [st, _any(), _any()]),
        out_specs=[blk(M_C, GLA_WIDTH), st],
        out_shape=[jax.ShapeDtypeStruct((ROWS, D_MODEL), BF16), jax.ShapeDtypeStruct(s_all.shape, F32)],
        input_output_aliases={n_in: 0, n_in + 1: 1},
        compiler_params=_cparams("arbitrary"),
        name="gla_sample",
    )(h, h, h, h, h, wa, ba, ng, state, mix, s_all)


def _mix_prompt_kernel(l, sink_ref, bias_ref, q_ref, kp_ref, kc_ref, vp_ref, vc_ref, u_ref, gq_ref, gk_ref, gv_ref, gg_ref, ga_ref,
                       bb_ref, cre_ref, cim_ref, d_ref, wg_ref, bg_ref, ab_ref, wa_ref, ba_ref, ng_ref,
                       w1_ref, w2_ref, w3_ref,
                       o_ref, hpre_ref, hpim_ref, s_ref, c1_ref, c2_ref, c3_ref, xre, xim, hre, him):
    s_blk = pl.program_id(0)

    @pl.when(s_blk == 0)
    def _():
        hpre_ref[...] = jnp.zeros_like(hpre_ref)
        hpim_ref[...] = jnp.zeros_like(hpim_ref)
        s_ref[...] = jnp.zeros_like(s_ref)

    c1_ref[...] = w1_ref[...].astype(BF16)
    c2_ref[...] = w2_ref[...].astype(BF16)
    c3_ref[...] = w3_ref[...].astype(BF16)
    _s5_block(u_ref, bb_ref, cre_ref, cim_ref, d_ref, wg_ref, bg_ref, ab_ref, o_ref, hpre_ref, hpim_ref,
              xre, xim, hre, him)
    _swa_block(l, sink_ref, bias_ref, q_ref, kp_ref, kc_ref, vp_ref, vc_ref, o_ref)
    _gla_block(gq_ref, gk_ref, gv_ref, gg_ref, ga_ref, wa_ref, ba_ref, ng_ref, o_ref, s_ref)


def _mix_prompt(sink, h, s5_prm, gla_prm, w_casts, mix, l):
    cur = _seq_block
    prev = lambda s: _seq_block(jnp.maximum(s - 1, 0))
    blk = lambda c, w, at=cur: pl.BlockSpec((BLK, w), lambda s: (at(s), c // w))
    const = lambda shape: pl.BlockSpec(shape, lambda s: (0,) * len(shape))
    casts = [_cast_specs(w, rows, l) for w, rows in zip(w_casts, (32, 176, 32))]
    in_specs = ([pl.BlockSpec(memory_space=pltpu.SMEM),
                 pl.BlockSpec((None, BLK, 2 * BLK), lambda s: (jnp.minimum(s, 2), 0, 0)), blk(C_Q, A_WIDTH),
                 blk(C_K, KV_WIDTH, prev), blk(C_K, KV_WIDTH), blk(C_V, KV_WIDTH, prev), blk(C_V, KV_WIDTH),
                 blk(C_U, S5_WIDTH),
                 blk(C_GQ, GLA_KW), blk(C_GK, GLA_KW), blk(C_GV, GLA_WIDTH), blk(C_GG, GLA_WIDTH), blk(C_GA, 128)]
                + _s5_param_specs(l, True) + _gla_param_specs(l) + [c[0] for c in casts])
    n_in = len(in_specs)
    return pl.pallas_call(
        _drop_inputs(functools.partial(_mix_prompt_kernel, l), n_in, 1),
        grid=(N_PBLK,),
        in_specs=in_specs + [_any()],
        out_specs=[pl.BlockSpec((BLK, D_MODEL), lambda s: (cur(s), 0)),
                   const((S5_CHUNKS, 128)), const((S5_CHUNKS, 128)), const((GLA_KW, GLA_DV))] + [c[1] for c in casts],
        out_shape=[jax.ShapeDtypeStruct((ROWS, D_MODEL), BF16),
                   jax.ShapeDtypeStruct((S5_CHUNKS, 128), F32), jax.ShapeDtypeStruct((S5_CHUNKS, 128), F32),
                   jax.ShapeDtypeStruct((GLA_KW, GLA_DV), F32)] + [c[2] for c in casts],
        scratch_shapes=[pltpu.VMEM((S5_CHUNKS * BLK, 128), F32)] * 4,
        input_output_aliases={n_in: 0},
        compiler_params=_cparams("arbitrary"),
        name="mix_prompt",
    )(sink, _swa_bias(), *([h] * 11), *_s5_params(s5_prm, True), *gla_prm, *w_casts, mix)


OUT_TM = 768
OUT_PARTS = 3
LN_ROWS = 16


def _out_proj_kernel(mix_ref, w_ref, x_ref, g_ref, b_ref, o_ref):
    i = pl.program_id(0)
    g, b = g_ref[...], b_ref[...]
    part = OUT_TM // OUT_PARTS
    rows = [slice(k * part, (k + 1) * part) for k in range(OUT_PARTS)]
    def norm(k, acc):
        for c in range(part // LN_ROWS):
            r0 = k * part + c * LN_ROWS
            y = _layer_norm_rows(ALPHA * x_ref[r0:r0 + LN_ROWS, :] + acc[c * LN_ROWS:(c + 1) * LN_ROWS, :], g, b)
            o_ref[r0:r0 + LN_ROWS, :] = _zero_pad_rows(y, i * OUT_TM + r0)

    prev = None
    for k, r in enumerate(rows):
        acc = _dot(mix_ref[r, :], w_ref[...])
        if prev is not None:
            norm(k - 1, prev)
        prev = acc
    norm(OUT_PARTS - 1, prev)


def _out_proj(mix, w, x, g, b, l):
    tm = OUT_TM
    return pl.pallas_call(
        _out_proj_kernel,
        grid=(ROWS // tm,),
        in_specs=[
            pl.BlockSpec((tm, D_MODEL), lambda i: (i, 0)),
            pl.BlockSpec((D_MODEL, D_MODEL), lambda i: (0, 0), pipeline_mode=pl.Buffered(1)),
            pl.BlockSpec((tm, D_MODEL), lambda i: (i, 0)),
            _layer_spec((1, D_MODEL), l), _layer_spec((1, D_MODEL), l),
        ],
        out_specs=pl.BlockSpec((tm, D_MODEL), lambda i: (i, 0)),
        out_shape=jax.ShapeDtypeStruct((ROWS, D_MODEL), F32),
        compiler_params=_cparams("arbitrary"),
        name="out_proj",
    )(mix, w, x, g, b)


FFN_TM = 768
FFN_TF = 512
FFN_HALO = 16
FFN_NI = ROWS // FFN_TM
assert ROWS % FFN_TM == 0 and FFN_TM >= N_SAMPLE + BLK + SUB and META_ROW % FFN_HALO == 0


def _ffn_kernel(final, x_ref, xh_ref, wu_ref, wg_ref, cw_ref, cb_ref, wd_ref, p0_ref, p1_ref, g_ref, b_ref,
                o_ref, *rest):
    if final:
        ys_ref, ut_ref, us_ref, xb_ref, a_ref = rest
    else:
        ut_ref, us_ref, xb_ref, a_ref = rest
    i, f = pl.program_id(0), pl.program_id(1)
    tm, hl = FFN_TM, FFN_HALO
    last_i = FFN_NI - 1

    @pl.when(f == 0)
    def _():
        xb_ref[...] = x_ref[...].astype(BF16)
        o_ref[...] = jnp.zeros_like(o_ref)

    xb = xb_ref[...]
    u = _dot(xb, wu_ref[...])
    gte = _dot(xb, wg_ref[...])
    cw0, cw1, cw2, cb = cw_ref[0:1, :], cw_ref[1:2, :], cw_ref[2:3, :], cb_ref[...]

    def act(um2, um1, u0, g0):
        return (jax.nn.gelu(cb + cw0 * um2 + cw1 * um1 + cw2 * u0) * g0).astype(BF16)

    um2, um1 = pltpu.roll(u, 2, 0), pltpu.roll(u, 1, 0)
    body = slice(hl, tm - N_SAMPLE)
    a_ref[body, :] = act(um2[body, :], um1[body, :], u[body, :], gte[body, :])
    ext = jnp.concatenate([_dot(xh_ref[...].astype(BF16), wu_ref[...]), u[0:hl, :]], axis=0)
    a_ref[0:hl, :] = act(ext[hl - 2:2 * hl - 2, :], ext[hl - 1:2 * hl - 1, :], u[0:hl, :], gte[0:hl, :])
    tail = slice(tm - N_SAMPLE, tm)
    samp = i == last_i
    a_ref[tail, :] = act(jnp.where(samp, p0_ref[...], um2[tail, :]), jnp.where(samp, p1_ref[...], um1[tail, :]),
                         u[tail, :], gte[tail, :])
    o_ref[...] += _dot(a_ref[...], wd_ref[...])
    ut_ref[...] = u[tm - N_SAMPLE - BLK - SUB:tm - N_SAMPLE - BLK, :]
    us_ref[0] = p1_ref[...]
    us_ref[1] = u[tm - N_SAMPLE:, :]

    @pl.when(f == pl.num_programs(1) - 1)
    def _():
        _deepnorm_ln(o_ref, x_ref, g_ref, b_ref, o_ref, i * tm, tm)
        if final:
            ys_ref[...] = o_ref[tm - N_SAMPLE:, :]


def _ffn(x, wup, cw, cb, wd, conv_p0, conv_p1, g, b, l, final):
    tm, tf = FFN_TM, FFN_TF
    nf = D_FF // tf
    tail = lambda i, f: jnp.where(i == FFN_NI - 1, f, 0)
    halo = lambda i, f: (jnp.where(i == 0, ROWS_P // FFN_HALO, i * (tm // FFN_HALO)) - 1, 0)
    y_spec = pl.BlockSpec((tm, D_MODEL), lambda i, f: (i, 0))
    y_specs = [y_spec, pl.BlockSpec((N_SAMPLE, D_MODEL), lambda i, f: (0, 0))] if final else [y_spec]
    y_shapes = ([jax.ShapeDtypeStruct((SEQ, D_MODEL), F32), jax.ShapeDtypeStruct((N_SAMPLE, D_MODEL), F32)]
                if final else [jax.ShapeDtypeStruct((ROWS, D_MODEL), F32)])
    return pl.pallas_call(
        functools.partial(_ffn_kernel, final),
        grid=(FFN_NI, nf),
        in_specs=[
            pl.BlockSpec((tm, D_MODEL), lambda i, f: (i, 0)),
            pl.BlockSpec((FFN_HALO, D_MODEL), halo),
            pl.BlockSpec((D_MODEL, tf), lambda i, f: (0, f)),
            pl.BlockSpec((D_MODEL, tf), lambda i, f: (0, nf + f)),
            pl.BlockSpec((None, CONV_W, tf), lambda i, f: (l, 0, f)),
            pl.BlockSpec((None, 1, tf), lambda i, f: (l, 0, f)),
            pl.BlockSpec((tf, D_MODEL), lambda i, f: (f, 0)),
            pl.BlockSpec((None, N_SAMPLE, tf), lambda i, f: (l, 0, tail(i, f))),
            pl.BlockSpec((None, N_SAMPLE, tf), lambda i, f: (l, 0, tail(i, f))),
            _layer_spec((1, D_MODEL), l), _layer_spec((1, D_MODEL), l),
        ],
        out_specs=y_specs + [
            pl.BlockSpec((SUB, tf), lambda i, f: (0, tail(i, f))),
            pl.BlockSpec((2, N_SAMPLE, tf), lambda i, f: (0, 0, tail(i, f))),
        ],
        out_shape=y_shapes + [
            jax.ShapeDtypeStruct((SUB, D_FF), F32),
            jax.ShapeDtypeStruct((2, N_SAMPLE, D_FF), F32),
        ],
        scratch_shapes=[pltpu.VMEM((tm, D_MODEL), BF16), pltpu.VMEM((tm, tf), BF16)],
        compiler_params=_cparams("arbitrary", "arbitrary"),
        name="conv_ffn",
    )(x, x, wup, wup, cw, cb, wd, conv_p0, conv_p1, g, b)


def _rope_table():
    inv = jnp.power(ROPE_THETA, -jnp.arange(ROPE_HALF, dtype=F32) / ROPE_HALF)
    pos = jnp.concatenate([jnp.arange(SEQ, dtype=jnp.int32) + N_META, jnp.zeros((LEAD,), jnp.int32),
                           jnp.arange(N_META, dtype=jnp.int32), jnp.full((N_SAMPLE,), PAST_LEN, jnp.int32)])
    ang = pos.astype(F32)[:, None] * inv[None, :]
    return jnp.concatenate([jnp.cos(ang), jnp.sin(ang)], axis=1)


def _s5_tables(lam_re, lam_im, log_dt, b_re, b_im, c_re, c_im, d, w_glu, b_glu):
    L = lam_re.shape[0]
    dt = jnp.exp(log_dt)[..., None]
    mag = jnp.exp(lam_re * dt)
    ab_re, ab_im = mag * jnp.cos(lam_im * dt), mag * jnp.sin(lam_im * dt)
    den = lam_re * lam_re + lam_im * lam_im
    nr, ni = ab_re - 1.0, ab_im
    f_re = (nr * lam_re + ni * lam_im) / den
    f_im = (ni * lam_re - nr * lam_im) / den
    bb_re = f_re[..., None] * b_re - f_im[..., None] * b_im
    bb_im = f_re[..., None] * b_im + f_im[..., None] * b_re
    gi, go = S5_GROUPS // S5_IN_PIECES, S5_GROUPS // S5_OUT_PIECES

    def bd_in(t):
        t = t.reshape(L, S5_IN_PIECES, gi, S5_STATE, S5_CH)
        return jnp.einsum('lqgph,gk->lqghkp', t, jnp.eye(gi, dtype=F32)).reshape(L, S5_IN_PIECES, S5_IN_K, S5_IN_N)

    def bd_out(t):
        t = t.reshape(L, S5_OUT_PIECES, go, S5_CH, S5_STATE)
        return jnp.einsum('lqghp,gk->lqgpkh', t, jnp.eye(go, dtype=F32)).reshape(
            L, S5_OUT_PIECES, S5_OUT_K, S5_OUT_N)

    ar, ai = ab_re.reshape(L, 1, S5_LANES), ab_im.reshape(L, 1, S5_LANES)
    pows = [(ar, ai)]
    for _ in range(SUB - 1):
        pr, pi = pows[-1]
        pows.append((pr * ar - pi * ai, pr * ai + pi * ar))
    row = jnp.arange(SUB)[None, :, None]
    parts = []
    for sft in S5_SHIFTS:
        pr, pi = pows[sft - 1]
        parts += [jnp.where(row >= sft, pr, 0.0), jnp.where(row >= sft, pi, 0.0)]
    parts += [jnp.concatenate([p[0] for p in pows], 1), jnp.concatenate([p[1] for p in pows], 1)]
    return dict(bb=jnp.concatenate([bd_in(bb_re), bd_in(bb_im)], axis=3).astype(BF16),
                c_re=bd_out(c_re).astype(BF16), c_im=bd_out(c_im).astype(BF16),
                d=d.reshape(L, 1, S5_WIDTH), w_glu=w_glu.astype(BF16), b_glu=b_glu.reshape(L, 1, S5_WIDTH),
                tab=jnp.concatenate(parts, 1),
                ab=jnp.concatenate([ab_re.reshape(L, S5_CHUNKS, 128), ab_im.reshape(L, S5_CHUNKS, 128)], axis=1))


@jax.jit
def kernel(x_prompt, x_sample, cache_swa_k, cache_swa_v, state_ssm_re, state_ssm_im, state_gla, state_conv,
           meta_tokens, ln_in_g, ln_in_b, w_in, attn_sink, s5_lam_re, s5_lam_im, s5_log_dt, s5_b_re, s5_b_im,
           s5_c_re, s5_c_im, s5_d, s5_w_glu, s5_b_glu, gla_w_a2, gla_b_a, gla_norm_g, w_out, ln1_g, ln1_b,
           ffn_w_up, ffn_conv_w, ffn_conv_b, ffn_w_down, ln2_g, ln2_b):
    L = DEPTH
    row3 = lambda t: t.reshape(L, 1, -1)
    w_in_b = w_in.astype(BF16)
    wa_b = jnp.pad(gla_w_a2, ((0, 0), (0, 128 - GLA_LOWRANK), (0, 0))).astype(BF16)
    gla_args = (wa_b, row3(gla_b_a), row3(gla_norm_g))
    ln1, ln2 = (row3(ln1_g), row3(ln1_b)), (row3(ln2_g), row3(ln2_b))
    conv_b3 = row3(ffn_conv_b)
    cs = _rope_table()
    s5p = _s5_tables(s5_lam_re, s5_lam_im, s5_log_dt, s5_b_re, s5_b_im, s5_c_re, s5_c_im, s5_d, s5_w_glu, s5_b_glu)
    ck = cache_swa_k.reshape(L, N_SAMPLE, WINDOW, KV_WIDTH)
    cv = cache_swa_v.reshape(L, N_SAMPLE, WINDOW, KV_WIDTH)
    h0re = state_ssm_re.reshape(L, N_SAMPLE, S5_LANES)
    h0im = state_ssm_im.reshape(L, N_SAMPLE, S5_LANES)
    sg = state_gla.reshape(L, N_SAMPLE, GLA_KW, GLA_DV)
    conv_p0, conv_p1 = state_conv[:, :, 0, :], state_conv[:, :, 1, :]
    mix = jnp.zeros((ROWS, D_MODEL), BF16)
    s_all = jnp.zeros((L, N_SAMPLE, GLA_KW, GLA_DV), F32)

    x = _ln_in(x_prompt.reshape(SEQ, D_MODEL), meta_tokens, x_sample.reshape(N_SAMPLE, D_MODEL),
               ln_in_g.reshape(1, D_MODEL), ln_in_b.reshape(1, D_MODEL))
    outs = [[] for _ in range(9)]
    for l in range(L):
        h = _in_proj(x, w_in_b, cs, l)
        mix, hp_re, hp_im, s_p, w_up_b, w_down_b, w_out_b = _mix_prompt(
            attn_sink, h, s5p, gla_args, (ffn_w_up, ffn_w_down, w_out), mix, l)
        mix, ck, cv = _swa_sample(attn_sink, h, ck, cv, mix, l)
        mix, hs_re, hs_im = _s5_sample(h, s5p, h0re, h0im, mix, l)
        mix, s_all = _gla_sample(h, *gla_args, sg, mix, s_all, l)
        x = _out_proj(mix, w_out_b, x, *ln1, l)
        res = _ffn(x, w_up_b, ffn_conv_w, conv_b3, w_down_b, conv_p0, conv_p1, *ln2, l, l == L - 1)
        if l == L - 1:
            y_prompt, y_sample, u_tail, u_s = res
        else:
            x, u_tail, u_s = res
        kv_p = lambda c: h[SEQ - WINDOW:SEQ, c:c + KV_WIDTH].reshape(1, WINDOW, A_KV_HEADS, HEAD_DIM)
        new = (kv_p(C_K), kv_p(C_V),
               hp_re.reshape(1, S5_GROUPS, S5_STATE), hp_im.reshape(1, S5_GROUPS, S5_STATE),
               s_p.reshape(1, GLA_HEADS, GLA_DK, GLA_DV), u_tail[SUB - 2:SUB][None],
               hs_re.reshape(N_SAMPLE, S5_GROUPS, S5_STATE), hs_im.reshape(N_SAMPLE, S5_GROUPS, S5_STATE),
               jnp.transpose(u_s, (1, 0, 2)))
        for lst, val in zip(outs, new):
            lst.append(val)
    st = [jnp.stack(lst) for lst in outs]
    return (y_prompt.reshape(1, SEQ, D_MODEL), y_sample.reshape(N_SAMPLE, 1, D_MODEL),
            st[0], st[1], st[2], st[3], st[4], st[5],
            ck.reshape(L, N_SAMPLE, WINDOW, A_KV_HEADS, HEAD_DIM),
            cv.reshape(L, N_SAMPLE, WINDOW, A_KV_HEADS, HEAD_DIM),
            st[6], st[7], s_all.reshape(L, N_SAMPLE, GLA_HEADS, GLA_DK, GLA_DV), st[8])
```

```python
import functools

import jax
import jax.numpy as jnp
from jax import lax
from jax.experimental import pallas as pl
from jax.experimental.pallas import tpu as pltpu

F32 = jnp.float32
BF16 = jnp.bfloat16

D_MODEL = 2048
SEQ = 8192
DEPTH = 2
N_SAMPLE = 128
N_META = 16
HEAD_DIM = 64
A_WIDTH = 1024
A_HEADS = 16
A_KV_HEADS = 4
GQA = 4
KV_WIDTH = A_KV_HEADS * HEAD_DIM
WINDOW = 128
ROPE_DIM = 16
ROPE_HALF = ROPE_DIM // 2
ROPE_THETA = 500000.0
PAST_LEN = 8192
S5_WIDTH = 512
S5_CH = 16
S5_GROUPS = 32
S5_STATE = 64
S5_LANES = S5_GROUPS * S5_STATE
GLA_WIDTH = 512
GLA_HEADS = 4
GLA_DV = 128
GLA_DK = 64
GLA_KW = GLA_HEADS * GLA_DK
GLA_LOWRANK = 16
GLA_TAU = 16.0
GLA_CHUNK = 64
D_FF = 5632
CONV_W = 3
LN_EPS = 1e-5
ALPHA = (2 * DEPTH) ** 0.25
NEG_INF = -1e30

BLK = 128
T_PROMPT = N_META + SEQ
LEAD = (-T_PROMPT) % BLK
N_PBLK = (LEAD + T_PROMPT) // BLK
ROWS_P = N_PBLK * BLK
ROWS = ROWS_P + N_SAMPLE
META_ROW = SEQ + LEAD
SUB = 8

C_Q, C_K, C_V, C_U, C_GQ, C_GK, C_GV, C_GG, C_GA = 0, 1024, 1280, 1536, 2048, 2304, 2560, 3072, 3584
IN_COLS = 3600
IN_TILE = 1280
IN_PAD = 3 * IN_TILE
ROPE_COLS = A_WIDTH + KV_WIDTH
M_A, M_B, M_C = 0, A_WIDTH, A_WIDTH + S5_WIDTH

VMEM_LIMIT = 56 * 1024 * 1024


def _cparams(*sem):
    return pltpu.CompilerParams(dimension_semantics=sem, vmem_limit_bytes=VMEM_LIMIT)


def _any():
    return pl.BlockSpec(memory_space=pl.ANY)


def _drop_inputs(kern, start, count):
    def wrapped(*refs):
        return kern(*refs[:start], *refs[start + count:])
    return wrapped


def _seq_block(s):
    return jnp.where(s == 0, N_PBLK - 1, s - 1)


def _layer_spec(shape, l):
    return pl.BlockSpec((None,) + shape, lambda *_: (l,) + (0,) * len(shape))


def _cast_specs(w, rows, l):
    n, cols = w.shape[1] // rows, w.shape[2]
    chunk = lambda s: jnp.minimum(s, n - 1)
    return (pl.BlockSpec((None, rows, cols), lambda s: (l, chunk(s), 0)),
            pl.BlockSpec((rows, cols), lambda s: (chunk(s), 0)),
            jax.ShapeDtypeStruct(w.shape[1:], BF16))


def _layer_norm_rows(x, g, b):
    mu = jnp.mean(x, -1, keepdims=True)
    xc = x - mu
    var = jnp.mean(xc * xc, -1, keepdims=True)
    return xc * lax.rsqrt(var + LN_EPS) * g + b


def _zero_pad_rows(y, first_row):
    row = first_row + lax.broadcasted_iota(jnp.int32, y.shape, 0)
    return jnp.where((row >= SEQ) & (row < META_ROW), 0.0, y)


def _deepnorm_ln(acc_ref, x_ref, g_ref, b_ref, o_ref, first_row, rows):
    g, b = g_ref[...], b_ref[...]
    for c in range(rows // BLK):
        r = slice(c * BLK, (c + 1) * BLK)
        y = _layer_norm_rows(ALPHA * x_ref[r, :] + acc_ref[r, :], g, b)
        o_ref[r, :] = _zero_pad_rows(y, first_row + c * BLK)


def _dot(a, b):
    return jnp.dot(a, b, preferred_element_type=F32)


def _dot_nt(a, b):
    return lax.dot_general(a, b, (((1,), (1,)), ((), ())), preferred_element_type=F32)


def _dot_tn(a, b):
    return lax.dot_general(a, b, (((0,), (0,)), ((), ())), preferred_element_type=F32)


def _log_sigmoid(x):
    return jnp.minimum(x, 0.0) - jnp.log(1.0 + jnp.exp(-jnp.abs(x)))


LN_IN_TM = 256
assert ROWS - SEQ == LN_IN_TM and SEQ % LN_IN_TM == 0


def _ln_in_kernel(xp_ref, meta_ref, xs_ref, g_ref, b_ref, o_ref):
    i = pl.program_id(0)
    g, b = g_ref[...], b_ref[...]

    @pl.when(i < SEQ // LN_IN_TM)
    def _():
        o_ref[...] = _layer_norm_rows(xp_ref[...], g, b)

    @pl.when(i == SEQ // LN_IN_TM)
    def _():
        o_ref[0:LEAD, :] = jnp.zeros((LEAD, D_MODEL), F32)
        o_ref[LEAD:BLK, :] = _layer_norm_rows(meta_ref[...], g, b)
        o_ref[BLK:, :] = _layer_norm_rows(xs_ref[...], g, b)


def _ln_in(xp, meta, xs, g, b):
    tm = LN_IN_TM
    const = lambda shape: pl.BlockSpec(shape, lambda i: (0,) * len(shape))
    return pl.pallas_call(
        _ln_in_kernel,
        grid=(ROWS // tm,),
        in_specs=[
            pl.BlockSpec((tm, D_MODEL), lambda i: (jnp.minimum(i, SEQ // tm - 1), 0)),
            const((N_META, D_MODEL)), const((N_SAMPLE, D_MODEL)), const((1, D_MODEL)), const((1, D_MODEL)),
        ],
        out_specs=pl.BlockSpec((tm, D_MODEL), lambda i: (i, 0)),
        out_shape=jax.ShapeDtypeStruct((ROWS, D_MODEL), F32),
        compiler_params=_cparams("arbitrary"),
        name="ln_in",
    )(xp, meta, xs, g, b)


IN_TM = 1056


def _in_proj_kernel(x_ref, w_ref, cs_ref, o_ref, xb_ref):
    j = pl.program_id(1)

    last_j = IN_PAD // IN_TILE - 1

    @pl.when(j == 0)
    def _():
        xb = x_ref[...].astype(BF16)
        xb_ref[...] = xb
        c8, s8 = cs_ref[:, 0:ROPE_HALF], cs_ref[:, ROPE_HALF:ROPE_DIM]
        rest = HEAD_DIM - ROPE_DIM
        one, zero = jnp.ones((IN_TM, rest), F32), jnp.zeros((IN_TM, rest), F32)
        z8 = jnp.zeros((IN_TM, ROPE_HALF), F32)
        cos = jnp.concatenate([c8, c8, one] * 2, axis=1)
        sa = jnp.concatenate([-s8, z8, zero] * 2, axis=1)
        sb = jnp.concatenate([z8, s8, zero] * 2, axis=1)
        for c in range(ROPE_COLS // 256):
            acc = _dot(xb, w_ref[:, c * 256:(c + 1) * 256])
            for hlf in range(2):
                blk = acc[:, hlf * 128:(hlf + 1) * 128]
                o_ref[:, c * 256 + hlf * 128:c * 256 + (hlf + 1) * 128] = (
                    blk * cos + pltpu.roll(blk, 128 - ROPE_HALF, 1) * sa + pltpu.roll(blk, ROPE_HALF, 1) * sb)

    @pl.when((j > 0) & (j < last_j))
    def _():
        o_ref[...] = _dot(xb_ref[...], w_ref[...])

    @pl.when(j == last_j)
    def _():
        o_ref[...] = _dot(xb_ref[...], w_ref[...])
        edge = (IN_COLS % IN_TILE) // 128 * 128
        col = edge + lax.broadcasted_iota(jnp.int32, (IN_TM, 128), 1)
        o_ref[:, edge:edge + 128] = jnp.where(col < IN_COLS % IN_TILE, o_ref[:, edge:edge + 128], 0.0)
        o_ref[:, edge + 128:] = jnp.zeros((IN_TM, IN_TILE - edge - 128), F32)


def _in_proj(x, w, cs, l):
    return pl.pallas_call(
        _in_proj_kernel,
        grid=(ROWS // IN_TM, IN_PAD // IN_TILE),
        in_specs=[
            pl.BlockSpec((IN_TM, D_MODEL), lambda i, j: (i, 0)),
            pl.BlockSpec((D_MODEL, IN_TILE), lambda i, j: (0, j)),
            pl.BlockSpec((IN_TM, ROPE_DIM), lambda i, j: (i, 0)),
        ],
        out_specs=pl.BlockSpec((IN_TM, IN_TILE), lambda i, j: (i, j)),
        out_shape=jax.ShapeDtypeStruct((ROWS, IN_PAD), F32),
        scratch_shapes=[pltpu.VMEM((IN_TM, D_MODEL), BF16)],
        compiler_params=_cparams("arbitrary", "arbitrary"),
        name="in_proj",
    )(x, w, cs)


def _swa_bias():
    row = lax.broadcasted_iota(jnp.int32, (3, BLK, 2 * BLK), 1)
    col = lax.broadcasted_iota(jnp.int32, (3, BLK, 2 * BLK), 2)
    s_blk = lax.broadcasted_iota(jnp.int32, (3, BLK, 2 * BLK), 0)
    diff = row - col + BLK
    kpos = (s_blk - 1) * BLK + col - LEAD
    ok = (diff >= 0) & (diff <= WINDOW) & (kpos >= 0)
    return jnp.where(ok, 0.0, NEG_INF).astype(F32)


def _swa_block(l, sink_ref, bias_ref, q_ref, kp_ref, kc_ref, vp_ref, vc_ref, o_ref):
    k2 = jnp.concatenate([kp_ref[...], kc_ref[...]], axis=0).astype(BF16)
    v2 = jnp.concatenate([vp_ref[...], vc_ref[...]], axis=0).astype(BF16)
    nq = GQA * BLK
    bias = jnp.concatenate([bias_ref[...]] * GQA, axis=0)
    low = lax.broadcasted_iota(jnp.int32, (BLK, 2 * HEAD_DIM), 1) < HEAD_DIM
    g_of_row = lax.broadcasted_iota(jnp.int32, (nq, 1), 0) // BLK
    ones = jnp.ones((2 * BLK, 2 * HEAD_DIM), BF16)
    scores, sinks = [], []
    for kv in range(A_KV_HEADS):
        kh = k2[:, kv * HEAD_DIM:(kv + 1) * HEAD_DIM]
        kk = jnp.concatenate([kh, kh], axis=1)
        parts = []
        for pair in range(GQA // 2):
            c0 = (kv * GQA + 2 * pair) * HEAD_DIM
            qp = q_ref[:, c0:c0 + 2 * HEAD_DIM] * HEAD_DIM ** -0.5
            parts += [jnp.where(low, qp, 0.0).astype(BF16), jnp.where(low, 0.0, qp).astype(BF16)]
        q4 = jnp.concatenate(parts, axis=0)
        sk = jnp.zeros((nq, 1), F32)
        for g in range(GQA):
            sk = jnp.where(g_of_row == g, sink_ref[l, kv * GQA + g], sk)
        scores.append(_dot_nt(q4, kk) + bias)
        sinks.append(sk)
    probs, sink_terms = [], []
    for s, sk in zip(scores, sinks):
        m = jnp.maximum(jnp.max(s, -1, keepdims=True), sk)
        probs.append(jnp.exp(s - m).astype(BF16))
        sink_terms.append(jnp.exp(sk - m))
    for kv in range(A_KV_HEADS):
        vh = v2[:, kv * HEAD_DIM:(kv + 1) * HEAD_DIM]
        v3 = jnp.concatenate([vh, vh, ones], axis=1)
        oa = _dot(probs[kv], v3)
        on = oa[:, 0:2 * HEAD_DIM] / (oa[:, 2 * HEAD_DIM:] + sink_terms[kv])
        for pair in range(GQA // 2):
            c0 = (kv * GQA + 2 * pair) * HEAD_DIM
            lo, hi = on[2 * pair * BLK:(2 * pair + 1) * BLK], on[(2 * pair + 1) * BLK:(2 * pair + 2) * BLK]
            o_ref[:, M_A + c0:M_A + c0 + 2 * HEAD_DIM] = jnp.where(low, lo, hi).astype(BF16)


SWA_BT = 16


def _swa_sample_kernel(l, sink_ref, q_ref, kn_ref, vn_ref, ck_ref, cv_ref, o_ref, cko_ref, cvo_ref):
    bt = SWA_BT
    kn, vn = kn_ref[...], vn_ref[...]
    k2 = ck_ref[...].reshape(bt * WINDOW, KV_WIDTH).astype(BF16)
    v2 = cv_ref[...].reshape(bt * WINDOW, KV_WIDTH).astype(BF16)
    row = lax.broadcasted_iota(jnp.int32, (WINDOW, KV_WIDTH), 0)
    for j in range(bt):
        cko_ref[j] = jnp.where(row == WINDOW - 1, kn[j:j + 1, :], pltpu.roll(ck_ref[j], WINDOW - 1, 0))
        cvo_ref[j] = jnp.where(row == WINDOW - 1, vn[j:j + 1, :], pltpu.roll(cv_ref[j], WINDOW - 1, 0))
    nq = GQA * bt
    rowb = lax.broadcasted_iota(jnp.int32, (nq, bt * WINDOW), 0) % bt
    colb = lax.broadcasted_iota(jnp.int32, (nq, bt * WINDOW), 1) // WINDOW
    same = rowb == colb
    rg = lax.broadcasted_iota(jnp.int32, (nq, 1), 0) // bt
    for kv in range(A_KV_HEADS):
        sl = slice(kv * HEAD_DIM, (kv + 1) * HEAD_DIM)
        q4 = jnp.concatenate(
            [q_ref[:, (kv * GQA + g) * HEAD_DIM:(kv * GQA + g + 1) * HEAD_DIM] for g in range(GQA)], axis=0)
        q4 = q4 * HEAD_DIM ** -0.5
        kn4 = jnp.concatenate([kn[:, sl]] * GQA, axis=0)
        vn4 = jnp.concatenate([vn[:, sl]] * GQA, axis=0)
        sk = jnp.zeros((nq, 1), F32)
        for g in range(GQA):
            sk = jnp.where(rg == g, sink_ref[l, kv * GQA + g], sk)
        s = jnp.where(same, _dot_nt(q4.astype(BF16), k2[:, sl]), NEG_INF)
        s_self = jnp.sum(q4 * kn4, -1, keepdims=True)
        m = jnp.maximum(jnp.maximum(jnp.max(s, -1, keepdims=True), s_self), sk)
        p = jnp.exp(s - m)
        p_self = jnp.exp(s_self - m)
        den = jnp.sum(p, -1, keepdims=True) + p_self + jnp.exp(sk - m)
        o = (_dot(p.astype(BF16), v2[:, sl]) + p_self * vn4) / den
        for g in range(GQA):
            hh = kv * GQA + g
            o_ref[:, hh * HEAD_DIM:(hh + 1) * HEAD_DIM] = o[g * bt:(g + 1) * bt].astype(BF16)


S5_SHIFTS = (1, 2, 4)
S5_NTAB = 2 * (len(S5_SHIFTS) + 1) * SUB
S5_CHUNKS = S5_LANES // 128
S5_IN_PIECES = 4
S5_IN_K = S5_WIDTH // S5_IN_PIECES
S5_IN_N = S5_LANES // S5_IN_PIECES
S5_OUT_PIECES = 2
S5_OUT_K = S5_LANES // S5_OUT_PIECES
S5_OUT_N = S5_WIDTH // S5_OUT_PIECES


def _s5_tab(tab_ref, k, lanes):
    return (tab_ref[2 * k * SUB:(2 * k + 1) * SUB, lanes], tab_ref[(2 * k + 1) * SUB:(2 * k + 2) * SUB, lanes])


def _s5_drive(ub, bb_ref, p):
    return _dot(ub[:, p * S5_IN_K:(p + 1) * S5_IN_K], bb_ref[p])


def _s5_readout(h_piece, u, cre_ref, cim_ref, d_ref, wg_ref, bg_ref):
    ys = []
    for j in range(S5_OUT_PIECES):
        hr, hi = h_piece(j)
        ys.append(_dot(hr.astype(BF16), cre_ref[j]) - _dot(hi.astype(BF16), cim_ref[j]))
    y = jnp.concatenate(ys, axis=1) + d_ref[...] * u
    z = jax.nn.gelu(y)
    gate = jax.nn.sigmoid(_dot(z.astype(BF16), wg_ref[...]) + bg_ref[...])
    return (z * gate).astype(BF16)


def _s5_block(u_ref, bb_ref, cre_ref, cim_ref, d_ref, wg_ref, bg_ref, ab_ref, o_ref, hpre_ref, hpim_ref,
              xre, xim, hre, him):
    u = u_ref[...]
    ub = u.astype(BF16)
    drive = [_s5_drive(ub, bb_ref, p) for p in range(S5_IN_PIECES)]
    x3r = jnp.concatenate([d[:, 0:S5_IN_N] for d in drive], axis=1).reshape(BLK, S5_CHUNKS, 128)
    x3i = jnp.concatenate([d[:, S5_IN_N:] for d in drive], axis=1).reshape(BLK, S5_CHUNKS, 128)
    ar, ai = ab_ref[0:S5_CHUNKS, :], ab_ref[S5_CHUNKS:, :]
    hr, hi = hpre_ref[...], hpim_ref[...]
    hs_r, hs_i = [], []
    for t in range(BLK):
        hr, hi = ar * hr - ai * hi + x3r[t], ar * hi + ai * hr + x3i[t]
        hs_r.append(hr)
        hs_i.append(hi)
    hpre_ref[...] = hr
    hpim_ref[...] = hi
    h2r = jnp.stack(hs_r, axis=0).reshape(BLK, S5_LANES)
    h2i = jnp.stack(hs_i, axis=0).reshape(BLK, S5_LANES)

    def piece(j):
        return h2r[:, j * S5_OUT_K:(j + 1) * S5_OUT_K], h2i[:, j * S5_OUT_K:(j + 1) * S5_OUT_K]

    o_ref[:, M_B:M_B + S5_WIDTH] = _s5_readout(piece, u, cre_ref, cim_ref, d_ref, wg_ref, bg_ref)


def _s5_sample_kernel(u_ref, bb_ref, cre_ref, cim_ref, d_ref, wg_ref, bg_ref, tab_ref, h0re_ref, h0im_ref,
                      o_ref, hsre_ref, hsim_ref):
    u = u_ref[...]
    ub = u.astype(BF16)
    for p in range(S5_IN_PIECES):
        lanes = slice(p * S5_IN_N, (p + 1) * S5_IN_N)
        bu = _s5_drive(ub, bb_ref, p)
        pr, pi = _s5_tab(tab_ref, len(S5_SHIFTS), lanes)
        ar, ai = pr[0:1, :], pi[0:1, :]
        h0r, h0i = h0re_ref[:, lanes], h0im_ref[:, lanes]
        hsre_ref[:, lanes] = ar * h0r - ai * h0i + bu[:, 0:S5_IN_N]
        hsim_ref[:, lanes] = ar * h0i + ai * h0r + bu[:, S5_IN_N:]
    piece = lambda j: (hsre_ref[:, j * S5_OUT_K:(j + 1) * S5_OUT_K], hsim_ref[:, j * S5_OUT_K:(j + 1) * S5_OUT_K])
    o_ref[:, M_B:M_B + S5_WIDTH] = _s5_readout(piece, u, cre_ref, cim_ref, d_ref, wg_ref, bg_ref)


def _s5_param_specs(l, prompt):
    return [
        _layer_spec((S5_IN_PIECES, S5_IN_K, 2 * S5_IN_N), l),
        _layer_spec((S5_OUT_PIECES, S5_OUT_K, S5_OUT_N), l), _layer_spec((S5_OUT_PIECES, S5_OUT_K, S5_OUT_N), l),
        _layer_spec((1, S5_WIDTH), l), _layer_spec((S5_WIDTH, S5_WIDTH), l), _layer_spec((1, S5_WIDTH), l),
        _layer_spec((2 * S5_CHUNKS, 128), l) if prompt else _layer_spec((S5_NTAB, S5_LANES), l),
    ]


def _s5_params(prm, prompt):
    return (prm["bb"], prm["c_re"], prm["c_im"], prm["d"], prm["w_glu"], prm["b_glu"],
            prm["ab"] if prompt else prm["tab"])


def _gla_log_gate(ac, wa_ref, ba_ref):
    return _log_sigmoid(_dot(ac.astype(BF16), wa_ref[...]) + ba_ref[...]) * (1.0 / GLA_TAU)


def _gla_finish(o_heads, gate, ng):
    outs = [o * lax.rsqrt(jnp.mean(o * o, -1, keepdims=True) + LN_EPS) for o in o_heads]
    return jnp.concatenate(outs, axis=1) * ng * (gate * jax.nn.sigmoid(gate))


def _gla_block(q_ref, k_ref, v_ref, gate_ref, ac_ref, wa_ref, ba_ref, ng_ref, o_ref, s_ref):
    lg = _gla_log_gate(ac_ref[...], wa_ref, ba_ref)
    L = GLA_CHUNK
    row = lax.broadcasted_iota(jnp.int32, (L, GLA_KW), 0)
    causal = lax.broadcasted_iota(jnp.int32, (L, L), 0) >= lax.broadcasted_iota(jnp.int32, (L, L), 1)
    chunks = range(BLK // L)
    heads = range(GLA_HEADS)
    ks = lambda hh: slice(hh * GLA_DK, (hh + 1) * GLA_DK)
    vs = lambda hh: slice(hh * GLA_DV, (hh + 1) * GLA_DV)
    q_in, k_out, k_dec, dec, vb = [], [], [], [], []
    for c in chunks:
        rows = slice(c * L, (c + 1) * L)
        b = lg[rows, :]
        d = 1
        while d < L:
            b = b + jnp.where(row >= d, pltpu.roll(b, d, 0), 0.0)
            d *= 2
        b_end = b[L - 1:L, :]
        kc = k_ref[rows, :]
        q_in.append((q_ref[rows, :] * GLA_DK ** -0.5 * jnp.exp(b)).astype(BF16))
        k_out.append((kc * jnp.exp(-b)).astype(BF16))
        k_dec.append((kc * jnp.exp(b_end - b)).astype(BF16))
        dec.append(jnp.transpose(jnp.broadcast_to(jnp.exp(b_end), (GLA_DV, GLA_KW))))
        vb.append(v_ref[rows, :].astype(BF16))
    att = [[jnp.where(causal, _dot_nt(q_in[c][:, ks(hh)], k_out[c][:, ks(hh)]), 0.0).astype(BF16) for hh in heads]
           for c in chunks]
    kv = [[_dot_tn(k_dec[c][:, ks(hh)], vb[c][:, vs(hh)]) for hh in heads] for c in chunks]
    state = [s_ref[...]]
    for c in chunks:
        state.append(dec[c] * state[c] + jnp.concatenate(kv[c], axis=0))
    s_ref[...] = state[-1]
    for c in chunks:
        sb = state[c].astype(BF16)
        o_heads = [_dot(q_in[c][:, ks(hh)], sb[ks(hh), :]) + _dot(att[c][hh], vb[c][:, vs(hh)]) for hh in heads]
        o_ref[c * L:(c + 1) * L, M_C:M_C + GLA_WIDTH] = _gla_finish(
            o_heads, gate_ref[c * L:(c + 1) * L, :], ng_ref[...]).astype(BF16)


def _gla_param_specs(l):
    return [_layer_spec((128, GLA_KW), l), _layer_spec((1, GLA_KW), l), _layer_spec((1, GLA_WIDTH), l)]


GLA_BT = 16


def _gla_sample_kernel(q_ref, k_ref, v_ref, gate_ref, ac_ref, wa_ref, ba_ref, ng_ref, s_ref, o_ref, so_ref):
    bt = GLA_BT
    lg = _gla_log_gate(ac_ref[...], wa_ref, ba_ref)
    v = v_ref[...]

    def cols(x):
        return jnp.transpose(jnp.concatenate([x, jnp.zeros((128 - bt, GLA_KW), F32)], axis=0))

    eg_t, k_t, q_t = cols(jnp.exp(lg)), cols(k_ref[...]), cols(q_ref[...] * GLA_DK ** -0.5)
    o_rows = []
    for j in range(bt):
        bc = lambda t: jnp.broadcast_to(t[:, j:j + 1], (GLA_KW, GLA_DV))
        vrow = jnp.concatenate(
            [jnp.broadcast_to(v[j:j + 1, hh * GLA_DV:(hh + 1) * GLA_DV], (GLA_DK, GLA_DV)) for hh in range(GLA_HEADS)],
            axis=0)
        s_new = bc(eg_t) * s_ref[j] + bc(k_t) * vrow
        so_ref[j] = s_new
        qs = bc(q_t) * s_new
        o_rows.append(jnp.concatenate(
            [jnp.sum(qs[hh * GLA_DK:(hh + 1) * GLA_DK, :], axis=0, keepdims=True) for hh in range(GLA_HEADS)], axis=1))
    o = jnp.concatenate(o_rows, axis=0)
    o_heads = [o[:, hh * GLA_DV:(hh + 1) * GLA_DV] for hh in range(GLA_HEADS)]
    o_ref[:, M_C:M_C + GLA_WIDTH] = _gla_finish(o_heads, gate_ref[...], ng_ref[...]).astype(BF16)


assert SWA_BT == GLA_BT
SAMPLE_BT = SWA_BT


def _mix_sample_kernel(l, sink_ref, q_ref, kn_ref, vn_ref, ck_ref, cv_ref,
                       u_ref, bb_ref, cre_ref, cim_ref, d_ref, wg_ref, bg_ref, tab_ref, h0re_ref, h0im_ref,
                       gq_ref, gk_ref, gv_ref, gg_ref, ga_ref, wa_ref, ba_ref, ng_ref, s_ref,
                       o_ref, cko_ref, cvo_ref, hsre_ref, hsim_ref, so_ref):
    _swa_sample_kernel(l, sink_ref, q_ref, kn_ref, vn_ref, ck_ref, cv_ref, o_ref, cko_ref, cvo_ref)
    _s5_sample_kernel(u_ref, bb_ref, cre_ref, cim_ref, d_ref, wg_ref, bg_ref, tab_ref, h0re_ref, h0im_ref,
                      o_ref, hsre_ref, hsim_ref)
    _gla_sample_kernel(gq_ref, gk_ref, gv_ref, gg_ref, ga_ref, wa_ref, ba_ref, ng_ref, s_ref, o_ref, so_ref)


def _mix_sample(sink, h, cache_k, cache_v, s5_prm, h0re, h0im, gla_prm, state, mix, s_all, l):
    bt = SAMPLE_BT
    r0 = ROWS_P // bt
    blk = lambda c, w: pl.BlockSpec((bt, w), lambda s: (r0 + s, c // w))
    cache = pl.BlockSpec((None, bt, WINDOW, KV_WIDTH), lambda s: (l, s, 0, 0))
    h0 = pl.BlockSpec((None, bt, S5_LANES), lambda s: (l, s, 0))
    hs = pl.BlockSpec((bt, S5_LANES), lambda s: (s, 0))
    st = pl.BlockSpec((None, bt, GLA_KW, GLA_DV), lambda s: (l, s, 0, 0))
    in_specs = ([pl.BlockSpec(memory_space=pltpu.SMEM), blk(C_Q, A_WIDTH), blk(C_K, KV_WIDTH), blk(C_V, KV_WIDTH),
                 cache, cache, blk(C_U, S5_WIDTH)]
                + _s5_param_specs(l, False) + [h0, h0]
                + [blk(C_GQ, GLA_KW), blk(C_GK, GLA_KW), blk(C_GV, GLA_WIDTH), blk(C_GG, GLA_WIDTH), blk(C_GA, 128)]
                + _gla_param_specs(l) + [st])
    n_in = len(in_specs)
    hs_shape = jax.ShapeDtypeStruct((N_SAMPLE, S5_LANES), F32)
    return pl.pallas_call(
        _drop_inputs(functools.partial(_mix_sample_kernel, l), n_in, 2),
        grid=(N_SAMPLE // bt,),
        in_specs=in_specs + [_any(), _any()],
        out_specs=[blk(0, D_MODEL), cache, cache, hs, hs, st],
        out_shape=[jax.ShapeDtypeStruct((ROWS, D_MODEL), BF16),
                   jax.ShapeDtypeStruct(cache_k.shape, F32), jax.ShapeDtypeStruct(cache_v.shape, F32),
                   hs_shape, hs_shape, jax.ShapeDtypeStruct(s_all.shape, F32)],
        input_output_aliases={n_in: 0, 4: 1, 5: 2, n_in + 1: 5},
        compiler_params=_cparams("arbitrary"),
        name="mix_sample",
    )(sink, h, h, h, cache_k, cache_v, h, *_s5_params(s5_prm, False), h0re, h0im,
      h, h, h, h, h, *gla_prm, state, mix, s_all)


def _mix_prompt_kernel(l, n_cast, sink_ref, bias_ref, q_ref, kp_ref, kc_ref, vp_ref, vc_ref, u_ref,
                       gq_ref, gk_ref, gv_ref, gg_ref, ga_ref,
                       bb_ref, cre_ref, cim_ref, d_ref, wg_ref, bg_ref, ab_ref, wa_ref, ba_ref, ng_ref, *rest):
    w_refs, (o_ref, hpre_ref, hpim_ref, s_ref) = rest[:n_cast], rest[n_cast:n_cast + 4]
    c_refs, (xre, xim, hre, him) = rest[n_cast + 4:2 * n_cast + 4], rest[2 * n_cast + 4:]
    s_blk = pl.program_id(0)

    @pl.when(s_blk == 0)
    def _():
        hpre_ref[...] = jnp.zeros_like(hpre_ref)
        hpim_ref[...] = jnp.zeros_like(hpim_ref)
        s_ref[...] = jnp.zeros_like(s_ref)

    for w_ref, c_ref in zip(w_refs, c_refs):
        c_ref[...] = w_ref[...].astype(BF16)
    _s5_block(u_ref, bb_ref, cre_ref, cim_ref, d_ref, wg_ref, bg_ref, ab_ref, o_ref, hpre_ref, hpim_ref,
              xre, xim, hre, him)
    _swa_block(l, sink_ref, bias_ref, q_ref, kp_ref, kc_ref, vp_ref, vc_ref, o_ref)
    _gla_block(gq_ref, gk_ref, gv_ref, gg_ref, ga_ref, wa_ref, ba_ref, ng_ref, o_ref, s_ref)


def _mix_prompt(sink, h, s5_prm, gla_prm, w_casts, mix, l):
    cur = _seq_block
    prev = lambda s: _seq_block(jnp.maximum(s - 1, 0))
    blk = lambda c, w, at=cur: pl.BlockSpec((BLK, w), lambda s: (at(s), c // w))
    const = lambda shape: pl.BlockSpec(shape, lambda s: (0,) * len(shape))
    casts = [_cast_specs(w, rows, layer) for w, rows, layer in w_casts]
    in_specs = ([pl.BlockSpec(memory_space=pltpu.SMEM),
                 pl.BlockSpec((None, BLK, 2 * BLK), lambda s: (jnp.minimum(s, 2), 0, 0)), blk(C_Q, A_WIDTH),
                 blk(C_K, KV_WIDTH, prev), blk(C_K, KV_WIDTH), blk(C_V, KV_WIDTH, prev), blk(C_V, KV_WIDTH),
                 blk(C_U, S5_WIDTH),
                 blk(C_GQ, GLA_KW), blk(C_GK, GLA_KW), blk(C_GV, GLA_WIDTH), blk(C_GG, GLA_WIDTH), blk(C_GA, 128)]
                + _s5_param_specs(l, True) + _gla_param_specs(l) + [c[0] for c in casts])
    n_in = len(in_specs)
    return pl.pallas_call(
        _drop_inputs(functools.partial(_mix_prompt_kernel, l, len(casts)), n_in, 1),
        grid=(N_PBLK,),
        in_specs=in_specs + [_any()],
        out_specs=[pl.BlockSpec((BLK, D_MODEL), lambda s: (cur(s), 0)),
                   const((S5_CHUNKS, 128)), const((S5_CHUNKS, 128)), const((GLA_KW, GLA_DV))] + [c[1] for c in casts],
        out_shape=[jax.ShapeDtypeStruct((ROWS, D_MODEL), BF16),
                   jax.ShapeDtypeStruct((S5_CHUNKS, 128), F32), jax.ShapeDtypeStruct((S5_CHUNKS, 128), F32),
                   jax.ShapeDtypeStruct((GLA_KW, GLA_DV), F32)] + [c[2] for c in casts],
        scratch_shapes=[pltpu.VMEM((S5_CHUNKS * BLK, 128), F32)] * 4,
        input_output_aliases={n_in: 0},
        compiler_params=_cparams("arbitrary"),
        name="mix_prompt",
    )(sink, _swa_bias(), *([h] * 11), *_s5_params(s5_prm, True), *gla_prm, *[w for w, _, _ in w_casts], mix)


OUT_TM = 768
OUT_PARTS = 3
LN_ROWS = 16


def _out_proj_kernel(mix_ref, w_ref, x_ref, g_ref, b_ref, o_ref):
    i = pl.program_id(0)
    g, b = g_ref[...], b_ref[...]
    part = OUT_TM // OUT_PARTS
    rows = [slice(k * part, (k + 1) * part) for k in range(OUT_PARTS)]
    def norm(k, acc):
        for c in range(part // LN_ROWS):
            r0 = k * part + c * LN_ROWS
            y = _layer_norm_rows(ALPHA * x_ref[r0:r0 + LN_ROWS, :] + acc[c * LN_ROWS:(c + 1) * LN_ROWS, :], g, b)
            o_ref[r0:r0 + LN_ROWS, :] = _zero_pad_rows(y, i * OUT_TM + r0)

    prev = None
    for k, r in enumerate(rows):
        acc = _dot(mix_ref[r, :], w_ref[...])
        if prev is not None:
            norm(k - 1, prev)
        prev = acc
    norm(OUT_PARTS - 1, prev)


def _out_proj(mix, w, x, g, b, l):
    tm = OUT_TM
    return pl.pallas_call(
        _out_proj_kernel,
        grid=(ROWS // tm,),
        in_specs=[
            pl.BlockSpec((tm, D_MODEL), lambda i: (i, 0)),
            pl.BlockSpec((D_MODEL, D_MODEL), lambda i: (0, 0), pipeline_mode=pl.Buffered(1)),
            pl.BlockSpec((tm, D_MODEL), lambda i: (i, 0)),
            _layer_spec((1, D_MODEL), l), _layer_spec((1, D_MODEL), l),
        ],
        out_specs=pl.BlockSpec((tm, D_MODEL), lambda i: (i, 0)),
        out_shape=jax.ShapeDtypeStruct((ROWS, D_MODEL), F32),
        compiler_params=_cparams("arbitrary"),
        name="out_proj",
    )(mix, w, x, g, b)


FFN_TM = 768
FFN_TF = 512
FFN_HALO = 16
FFN_NI = ROWS // FFN_TM
assert ROWS % FFN_TM == 0 and FFN_TM >= N_SAMPLE + BLK + SUB and META_ROW % FFN_HALO == 0


def _ffn_kernel(final, x_ref, xh_ref, wu_ref, wg_ref, cw_ref, cb_ref, wd_ref, p0_ref, p1_ref, g_ref, b_ref,
                o_ref, *rest):
    if final:
        ys_ref, ut_ref, us_ref, xb_ref, a_ref = rest
    else:
        ut_ref, us_ref, xb_ref, a_ref = rest
    i, f = pl.program_id(0), pl.program_id(1)
    tm, hl = FFN_TM, FFN_HALO
    last_i = FFN_NI - 1

    @pl.when(f == 0)
    def _():
        xb_ref[...] = x_ref[...].astype(BF16)
        o_ref[...] = jnp.zeros_like(o_ref)

    cw0, cw1, cw2, cb = cw_ref[0:1, :], cw_ref[1:2, :], cw_ref[2:3, :], cb_ref[...]

    def act(um2, um1, u0, g0):
        return (jax.nn.gelu(cb + cw0 * um2 + cw1 * um1 + cw2 * u0) * g0).astype(BF16)

    xb = xb_ref[...]
    u = _dot(xb, wu_ref[...])
    gte = _dot(xb, wg_ref[...])
    um2, um1 = pltpu.roll(u, 2, 0), pltpu.roll(u, 1, 0)
    body = slice(hl, tm - N_SAMPLE)
    a_ref[body, :] = act(um2[body, :], um1[body, :], u[body, :], gte[body, :])
    ext = jnp.concatenate([_dot(xh_ref[...].astype(BF16), wu_ref[...]), u[0:hl, :]], axis=0)
    a_ref[0:hl, :] = act(ext[hl - 2:2 * hl - 2, :], ext[hl - 1:2 * hl - 1, :], u[0:hl, :], gte[0:hl, :])
    tail = slice(tm - N_SAMPLE, tm)
    samp = i == last_i
    a_ref[tail, :] = act(jnp.where(samp, p0_ref[...], um2[tail, :]), jnp.where(samp, p1_ref[...], um1[tail, :]),
                         u[tail, :], gte[tail, :])
    o_ref[...] += _dot(a_ref[...], wd_ref[...])
    ut_ref[...] = u[tm - N_SAMPLE - BLK - SUB:tm - N_SAMPLE - BLK, :]
    us_ref[0] = p1_ref[...]
    us_ref[1] = u[tm - N_SAMPLE:, :]

    @pl.when(f == pl.num_programs(1) - 1)
    def _():
        _deepnorm_ln(o_ref, x_ref, g_ref, b_ref, o_ref, i * tm, tm)
        if final:
            ys_ref[...] = o_ref[tm - N_SAMPLE:, :]


def _ffn(x, wup, cw, cb, wd, conv_p0, conv_p1, g, b, l, final):
    tm, tf = FFN_TM, FFN_TF
    nf = D_FF // tf
    tail = lambda i, f: jnp.where(i == FFN_NI - 1, f, 0)
    halo = lambda i, f: (jnp.where(i == 0, ROWS_P // FFN_HALO, i * (tm // FFN_HALO)) - 1, 0)
    y_spec = pl.BlockSpec((tm, D_MODEL), lambda i, f: (i, 0))
    y_specs = [y_spec, pl.BlockSpec((N_SAMPLE, D_MODEL), lambda i, f: (0, 0))] if final else [y_spec]
    y_shapes = ([jax.ShapeDtypeStruct((SEQ, D_MODEL), F32), jax.ShapeDtypeStruct((N_SAMPLE, D_MODEL), F32)]
                if final else [jax.ShapeDtypeStruct((ROWS, D_MODEL), F32)])
    return pl.pallas_call(
        functools.partial(_ffn_kernel, final),
        grid=(FFN_NI, nf),
        in_specs=[
            pl.BlockSpec((tm, D_MODEL), lambda i, f: (i, 0)),
            pl.BlockSpec((FFN_HALO, D_MODEL), halo),
            pl.BlockSpec((D_MODEL, tf), lambda i, f: (0, f)),
            pl.BlockSpec((D_MODEL, tf), lambda i, f: (0, nf + f)),
            pl.BlockSpec((None, CONV_W, tf), lambda i, f: (l, 0, f)),
            pl.BlockSpec((None, 1, tf), lambda i, f: (l, 0, f)),
            pl.BlockSpec((tf, D_MODEL), lambda i, f: (f, 0)),
            pl.BlockSpec((None, N_SAMPLE, tf), lambda i, f: (l, 0, tail(i, f))),
            pl.BlockSpec((None, N_SAMPLE, tf), lambda i, f: (l, 0, tail(i, f))),
            _layer_spec((1, D_MODEL), l), _layer_spec((1, D_MODEL), l),
        ],
        out_specs=y_specs + [
            pl.BlockSpec((SUB, tf), lambda i, f: (0, tail(i, f))),
            pl.BlockSpec((2, N_SAMPLE, tf), lambda i, f: (0, 0, tail(i, f))),
        ],
        out_shape=y_shapes + [
            jax.ShapeDtypeStruct((SUB, D_FF), F32),
            jax.ShapeDtypeStruct((2, N_SAMPLE, D_FF), F32),
        ],
        scratch_shapes=[pltpu.VMEM((tm, D_MODEL), BF16), pltpu.VMEM((tm, tf), BF16)],
        compiler_params=_cparams("arbitrary", "arbitrary"),
        name="conv_ffn",
    )(x, x, wup, wup, cw, cb, wd, conv_p0, conv_p1, g, b)


def _rope_table():
    inv = jnp.power(ROPE_THETA, -jnp.arange(ROPE_HALF, dtype=F32) / ROPE_HALF)
    pos = jnp.concatenate([jnp.arange(SEQ, dtype=jnp.int32) + N_META, jnp.zeros((LEAD,), jnp.int32),
                           jnp.arange(N_META, dtype=jnp.int32), jnp.full((N_SAMPLE,), PAST_LEN, jnp.int32)])
    ang = pos.astype(F32)[:, None] * inv[None, :]
    return jnp.concatenate([jnp.cos(ang), jnp.sin(ang)], axis=1)


def _s5_tables(lam_re, lam_im, log_dt, b_re, b_im, c_re, c_im, d, w_glu, b_glu):
    L = lam_re.shape[0]
    dt = jnp.exp(log_dt)[..., None]
    mag = jnp.exp(lam_re * dt)
    ab_re, ab_im = mag * jnp.cos(lam_im * dt), mag * jnp.sin(lam_im * dt)
    den = lam_re * lam_re + lam_im * lam_im
    nr, ni = ab_re - 1.0, ab_im
    f_re = (nr * lam_re + ni * lam_im) / den
    f_im = (ni * lam_re - nr * lam_im) / den
    bb_re = f_re[..., None] * b_re - f_im[..., None] * b_im
    bb_im = f_re[..., None] * b_im + f_im[..., None] * b_re
    gi, go = S5_GROUPS // S5_IN_PIECES, S5_GROUPS // S5_OUT_PIECES

    def bd_in(t):
        t = t.reshape(L, S5_IN_PIECES, gi, S5_STATE, S5_CH)
        return jnp.einsum('lqgph,gk->lqghkp', t, jnp.eye(gi, dtype=F32)).reshape(L, S5_IN_PIECES, S5_IN_K, S5_IN_N)

    def bd_out(t):
        t = t.reshape(L, S5_OUT_PIECES, go, S5_CH, S5_STATE)
        return jnp.einsum('lqghp,gk->lqgpkh', t, jnp.eye(go, dtype=F32)).reshape(
            L, S5_OUT_PIECES, S5_OUT_K, S5_OUT_N)

    ar, ai = ab_re.reshape(L, 1, S5_LANES), ab_im.reshape(L, 1, S5_LANES)
    pows = [(ar, ai)]
    for _ in range(SUB - 1):
        pr, pi = pows[-1]
        pows.append((pr * ar - pi * ai, pr * ai + pi * ar))
    row = jnp.arange(SUB)[None, :, None]
    parts = []
    for sft in S5_SHIFTS:
        pr, pi = pows[sft - 1]
        parts += [jnp.where(row >= sft, pr, 0.0), jnp.where(row >= sft, pi, 0.0)]
    parts += [jnp.concatenate([p[0] for p in pows], 1), jnp.concatenate([p[1] for p in pows], 1)]
    return dict(bb=jnp.concatenate([bd_in(bb_re), bd_in(bb_im)], axis=3).astype(BF16),
                c_re=bd_out(c_re).astype(BF16), c_im=bd_out(c_im).astype(BF16),
                d=d.reshape(L, 1, S5_WIDTH), w_glu=w_glu.astype(BF16), b_glu=b_glu.reshape(L, 1, S5_WIDTH),
                tab=jnp.concatenate(parts, 1),
                ab=jnp.concatenate([ab_re.reshape(L, S5_CHUNKS, 128), ab_im.reshape(L, S5_CHUNKS, 128)], axis=1))


@jax.jit
def kernel(x_prompt, x_sample, cache_swa_k, cache_swa_v, state_ssm_re, state_ssm_im, state_gla, state_conv,
           meta_tokens, ln_in_g, ln_in_b, w_in, attn_sink, s5_lam_re, s5_lam_im, s5_log_dt, s5_b_re, s5_b_im,
           s5_c_re, s5_c_im, s5_d, s5_w_glu, s5_b_glu, gla_w_a2, gla_b_a, gla_norm_g, w_out, ln1_g, ln1_b,
           ffn_w_up, ffn_conv_w, ffn_conv_b, ffn_w_down, ln2_g, ln2_b):
    L = DEPTH
    row3 = lambda t: t.reshape(L, 1, -1)
    w_in_b = w_in[0].astype(BF16)
    wa_b = jnp.pad(gla_w_a2, ((0, 0), (0, 128 - GLA_LOWRANK), (0, 0))).astype(BF16)
    gla_args = (wa_b, row3(gla_b_a), row3(gla_norm_g))
    ln1, ln2 = (row3(ln1_g), row3(ln1_b)), (row3(ln2_g), row3(ln2_b))
    conv_b3 = row3(ffn_conv_b)
    cs = _rope_table()
    s5p = _s5_tables(s5_lam_re, s5_lam_im, s5_log_dt, s5_b_re, s5_b_im, s5_c_re, s5_c_im, s5_d, s5_w_glu, s5_b_glu)
    ck = cache_swa_k.reshape(L, N_SAMPLE, WINDOW, KV_WIDTH)
    cv = cache_swa_v.reshape(L, N_SAMPLE, WINDOW, KV_WIDTH)
    h0re = state_ssm_re.reshape(L, N_SAMPLE, S5_LANES)
    h0im = state_ssm_im.reshape(L, N_SAMPLE, S5_LANES)
    sg = state_gla.reshape(L, N_SAMPLE, GLA_KW, GLA_DV)
    conv_p0, conv_p1 = state_conv[:, :, 0, :], state_conv[:, :, 1, :]
    mix = jnp.zeros((ROWS, D_MODEL), BF16)
    s_all = jnp.zeros((L, N_SAMPLE, GLA_KW, GLA_DV), F32)

    x = _ln_in(x_prompt.reshape(SEQ, D_MODEL), meta_tokens, x_sample.reshape(N_SAMPLE, D_MODEL),
               ln_in_g.reshape(1, D_MODEL), ln_in_b.reshape(1, D_MODEL))
    outs = [[] for _ in range(9)]
    for l in range(L):
        h = _in_proj(x, w_in_b, cs, l)
        casts = [(ffn_w_up, 32, l), (ffn_w_down, 176, l), (w_out, 32, l)] + ([(w_in, 32, l + 1)] if l + 1 < L else [])
        mix, hp_re, hp_im, s_p, w_up_b, w_down_b, w_out_b, *w_next = _mix_prompt(
            attn_sink, h, s5p, gla_args, casts, mix, l)
        w_in_b = w_next[0] if w_next else None
        mix, ck, cv, hs_re, hs_im, s_all = _mix_sample(
            attn_sink, h, ck, cv, s5p, h0re, h0im, gla_args, sg, mix, s_all, l)
        x = _out_proj(mix, w_out_b, x, *ln1, l)
        res = _ffn(x, w_up_b, ffn_conv_w, conv_b3, w_down_b, conv_p0, conv_p1, *ln2, l, l == L - 1)
        if l == L - 1:
            y_prompt, y_sample, u_tail, u_s = res
        else:
            x, u_tail, u_s = res
        kv_p = lambda c: h[SEQ - WINDOW:SEQ, c:c + KV_WIDTH].reshape(1, WINDOW, A_KV_HEADS, HEAD_DIM)
        new = (kv_p(C_K), kv_p(C_V),
               hp_re.reshape(1, S5_GROUPS, S5_STATE), hp_im.reshape(1, S5_GROUPS, S5_STATE),
               s_p.reshape(1, GLA_HEADS, GLA_DK, GLA_DV), u_tail[SUB - 2:SUB][None],
               hs_re.reshape(N_SAMPLE, S5_GROUPS, S5_STATE), hs_im.reshape(N_SAMPLE, S5_GROUPS, S5_STATE),
               jnp.transpose(u_s, (1, 0, 2)))
        for lst, val in zip(outs, new):
            lst.append(val)
    st = [jnp.stack(lst) for lst in outs]
    return (y_prompt.reshape(1, SEQ, D_MODEL), y_sample.reshape(N_SAMPLE, 1, D_MODEL),
            st[0], st[1], st[2], st[3], st[4], st[5],
            ck.reshape(L, N_SAMPLE, WINDOW, A_KV_HEADS, HEAD_DIM),
            cv.reshape(L, N_SAMPLE, WINDOW, A_KV_HEADS, HEAD_DIM),
            st[6], st[7], s_all.reshape(L, N_SAMPLE, GLA_HEADS, GLA_DK, GLA_DV), st[8])
```

```python
import functools

import jax
import jax.numpy as jnp
from jax import lax
from jax.experimental import pallas as pl
from jax.experimental.pallas import tpu as pltpu

F32 = jnp.float32
BF16 = jnp.bfloat16

D_MODEL = 2048
SEQ = 8192
DEPTH = 2
N_SAMPLE = 128
N_META = 16
HEAD_DIM = 64
A_WIDTH = 1024
A_HEADS = 16
A_KV_HEADS = 4
GQA = 4
KV_WIDTH = A_KV_HEADS * HEAD_DIM
WINDOW = 128
ROPE_DIM = 16
ROPE_HALF = ROPE_DIM // 2
ROPE_THETA = 500000.0
PAST_LEN = 8192
S5_WIDTH = 512
S5_CH = 16
S5_GROUPS = 32
S5_STATE = 64
S5_LANES = S5_GROUPS * S5_STATE
GLA_WIDTH = 512
GLA_HEADS = 4
GLA_DV = 128
GLA_DK = 64
GLA_KW = GLA_HEADS * GLA_DK
GLA_LOWRANK = 16
GLA_TAU = 16.0
GLA_CHUNK = 64
D_FF = 5632
CONV_W = 3
LN_EPS = 1e-5
ALPHA = (2 * DEPTH) ** 0.25
NEG_INF = -1e30

BLK = 128
T_PROMPT = N_META + SEQ
LEAD = (-T_PROMPT) % BLK
N_PBLK = (LEAD + T_PROMPT) // BLK
ROWS_P = N_PBLK * BLK
ROWS = ROWS_P + N_SAMPLE
META_ROW = SEQ + LEAD
SUB = 8

C_Q, C_K, C_V, C_U, C_GQ, C_GK, C_GV, C_GG, C_GA = 0, 1024, 1280, 1536, 2048, 2304, 2560, 3072, 3584
IN_COLS = 3600
IN_TILE = 1280
IN_PAD = 3 * IN_TILE
ROPE_COLS = A_WIDTH + KV_WIDTH
M_A, M_B, M_C = 0, A_WIDTH, A_WIDTH + S5_WIDTH

VMEM_LIMIT = 56 * 1024 * 1024


def _cparams(*sem):
    return pltpu.CompilerParams(dimension_semantics=sem, vmem_limit_bytes=VMEM_LIMIT)


def _any():
    return pl.BlockSpec(memory_space=pl.ANY)


def _drop_inputs(kern, start, count):
    def wrapped(*refs):
        return kern(*refs[:start], *refs[start + count:])
    return wrapped


def _seq_block(s):
    return jnp.where(s == 0, N_PBLK - 1, s - 1)


def _layer_spec(shape, l):
    return pl.BlockSpec((None,) + shape, lambda *_: (l,) + (0,) * len(shape))


def _cast_specs(w, rows, l):
    n, cols = w.shape[1] // rows, w.shape[2]
    chunk = lambda s: jnp.minimum(s, n - 1)
    return (pl.BlockSpec((None, rows, cols), lambda s: (l, chunk(s), 0)),
            pl.BlockSpec((rows, cols), lambda s: (chunk(s), 0)),
            jax.ShapeDtypeStruct(w.shape[1:], BF16))


def _layer_norm_rows(x, g, b):
    mu = jnp.mean(x, -1, keepdims=True)
    xc = x - mu
    var = jnp.mean(xc * xc, -1, keepdims=True)
    return xc * lax.rsqrt(var + LN_EPS) * g + b


def _zero_pad_rows(y, first_row):
    row = first_row + lax.broadcasted_iota(jnp.int32, y.shape, 0)
    return jnp.where((row >= SEQ) & (row < META_ROW), 0.0, y)


def _deepnorm_ln(acc_ref, x_ref, g_ref, b_ref, o_ref, first_row, rows):
    g, b = g_ref[...], b_ref[...]
    for c in range(rows // BLK):
        r = slice(c * BLK, (c + 1) * BLK)
        y = _layer_norm_rows(ALPHA * x_ref[r, :] + acc_ref[r, :], g, b)
        o_ref[r, :] = _zero_pad_rows(y, first_row + c * BLK)


def _dot(a, b):
    return jnp.dot(a, b, preferred_element_type=F32)


def _dot_nt(a, b):
    return lax.dot_general(a, b, (((1,), (1,)), ((), ())), preferred_element_type=F32)


def _dot_tn(a, b):
    return lax.dot_general(a, b, (((0,), (0,)), ((), ())), preferred_element_type=F32)


def _log_sigmoid(x):
    return jnp.minimum(x, 0.0) - jnp.log(1.0 + jnp.exp(-jnp.abs(x)))


LN_IN_TM = 512
LN_IN_STEPS = SEQ // LN_IN_TM + 1
assert SEQ % LN_IN_TM == 0 and ROWS - SEQ <= LN_IN_TM


def _ln_in_kernel(xp_ref, meta_ref, xs_ref, g_ref, b_ref, wsrc_ref, o_ref, wdst_ref):
    i = pl.program_id(0)
    g, b = g_ref[...], b_ref[...]
    wdst_ref[...] = wsrc_ref[...].astype(BF16)

    @pl.when(i < LN_IN_STEPS - 1)
    def _():
        for c in range(LN_IN_TM // BLK):
            r = slice(c * BLK, (c + 1) * BLK)
            o_ref[r, :] = _layer_norm_rows(xp_ref[r, :], g, b)

    @pl.when(i == LN_IN_STEPS - 1)
    def _():
        o_ref[0:LEAD, :] = jnp.zeros((LEAD, D_MODEL), F32)
        o_ref[LEAD:BLK, :] = _layer_norm_rows(meta_ref[...], g, b)
        o_ref[BLK:BLK + N_SAMPLE, :] = _layer_norm_rows(xs_ref[...], g, b)


def _ln_in(xp, meta, xs, g, b, w_cast):
    tm = LN_IN_TM
    const = lambda shape: pl.BlockSpec(shape, lambda i: (0,) * len(shape))
    w_in_spec, w_out_spec, w_shape = _cast_specs(w_cast, w_cast.shape[1] // (LN_IN_STEPS - 1), 0)
    return pl.pallas_call(
        _ln_in_kernel,
        grid=(LN_IN_STEPS,),
        in_specs=[
            pl.BlockSpec((tm, D_MODEL), lambda i: (jnp.minimum(i, SEQ // tm - 1), 0)),
            const((N_META, D_MODEL)), const((N_SAMPLE, D_MODEL)), const((1, D_MODEL)), const((1, D_MODEL)),
            w_in_spec,
        ],
        out_specs=[pl.BlockSpec((tm, D_MODEL), lambda i: (i, 0)), w_out_spec],
        out_shape=[jax.ShapeDtypeStruct((ROWS, D_MODEL), F32), w_shape],
        compiler_params=_cparams("arbitrary"),
        name="ln_in",
    )(xp, meta, xs, g, b, w_cast)


IN_TM = 1056


def _in_proj_kernel(x_ref, w_ref, cs_ref, o_ref, xb_ref):
    j = pl.program_id(1)

    last_j = IN_PAD // IN_TILE - 1

    @pl.when(j == 0)
    def _():
        xb = x_ref[...].astype(BF16)
        xb_ref[...] = xb
        c8, s8 = cs_ref[:, 0:ROPE_HALF], cs_ref[:, ROPE_HALF:ROPE_DIM]
        rest = HEAD_DIM - ROPE_DIM
        one, zero = jnp.ones((IN_TM, rest), F32), jnp.zeros((IN_TM, rest), F32)
        z8 = jnp.zeros((IN_TM, ROPE_HALF), F32)
        cos = jnp.concatenate([c8, c8, one] * 2, axis=1)
        sa = jnp.concatenate([-s8, z8, zero] * 2, axis=1)
        sb = jnp.concatenate([z8, s8, zero] * 2, axis=1)
        for c in range(ROPE_COLS // 256):
            acc = _dot(xb, w_ref[:, c * 256:(c + 1) * 256])
            for hlf in range(2):
                blk = acc[:, hlf * 128:(hlf + 1) * 128]
                o_ref[:, c * 256 + hlf * 128:c * 256 + (hlf + 1) * 128] = (
                    blk * cos + pltpu.roll(blk, 128 - ROPE_HALF, 1) * sa + pltpu.roll(blk, ROPE_HALF, 1) * sb)

    @pl.when((j > 0) & (j < last_j))
    def _():
        o_ref[...] = _dot(xb_ref[...], w_ref[...])

    @pl.when(j == last_j)
    def _():
        o_ref[...] = _dot(xb_ref[...], w_ref[...])
        edge = (IN_COLS % IN_TILE) // 128 * 128
        col = edge + lax.broadcasted_iota(jnp.int32, (IN_TM, 128), 1)
        o_ref[:, edge:edge + 128] = jnp.where(col < IN_COLS % IN_TILE, o_ref[:, edge:edge + 128], 0.0)
        o_ref[:, edge + 128:] = jnp.zeros((IN_TM, IN_TILE - edge - 128), F32)


def _in_proj(x, w, cs, l):
    return pl.pallas_call(
        _in_proj_kernel,
        grid=(ROWS // IN_TM, IN_PAD // IN_TILE),
        in_specs=[
            pl.BlockSpec((IN_TM, D_MODEL), lambda i, j: (i, 0)),
            pl.BlockSpec((D_MODEL, IN_TILE), lambda i, j: (0, j)),
            pl.BlockSpec((IN_TM, ROPE_DIM), lambda i, j: (i, 0)),
        ],
        out_specs=pl.BlockSpec((IN_TM, IN_TILE), lambda i, j: (i, j)),
        out_shape=jax.ShapeDtypeStruct((ROWS, IN_PAD), F32),
        scratch_shapes=[pltpu.VMEM((IN_TM, D_MODEL), BF16)],
        compiler_params=_cparams("arbitrary", "arbitrary"),
        name="in_proj",
    )(x, w, cs)


def _swa_bias():
    row = lax.broadcasted_iota(jnp.int32, (3, BLK, 2 * BLK), 1)
    col = lax.broadcasted_iota(jnp.int32, (3, BLK, 2 * BLK), 2)
    s_blk = lax.broadcasted_iota(jnp.int32, (3, BLK, 2 * BLK), 0)
    diff = row - col + BLK
    kpos = (s_blk - 1) * BLK + col - LEAD
    ok = (diff >= 0) & (diff <= WINDOW) & (kpos >= 0)
    return jnp.where(ok, 0.0, NEG_INF).astype(F32)


def _swa_block(l, sink_ref, bias_ref, q_ref, kp_ref, kc_ref, vp_ref, vc_ref, o_ref):
    k2 = jnp.concatenate([kp_ref[...], kc_ref[...]], axis=0).astype(BF16)
    v2 = jnp.concatenate([vp_ref[...], vc_ref[...]], axis=0).astype(BF16)
    nq = GQA * BLK
    bias = jnp.concatenate([bias_ref[...]] * GQA, axis=0)
    low = lax.broadcasted_iota(jnp.int32, (BLK, 2 * HEAD_DIM), 1) < HEAD_DIM
    g_of_row = lax.broadcasted_iota(jnp.int32, (nq, 1), 0) // BLK
    ones = jnp.ones((2 * BLK, 2 * HEAD_DIM), BF16)
    scores, sinks = [], []
    for kv in range(A_KV_HEADS):
        kh = k2[:, kv * HEAD_DIM:(kv + 1) * HEAD_DIM]
        kk = jnp.concatenate([kh, kh], axis=1)
        parts = []
        for pair in range(GQA // 2):
            c0 = (kv * GQA + 2 * pair) * HEAD_DIM
            qp = q_ref[:, c0:c0 + 2 * HEAD_DIM] * HEAD_DIM ** -0.5
            parts += [jnp.where(low, qp, 0.0).astype(BF16), jnp.where(low, 0.0, qp).astype(BF16)]
        q4 = jnp.concatenate(parts, axis=0)
        sk = jnp.zeros((nq, 1), F32)
        for g in range(GQA):
            sk = jnp.where(g_of_row == g, sink_ref[l, kv * GQA + g], sk)
        scores.append(_dot_nt(q4, kk) + bias)
        sinks.append(sk)
    probs, sink_terms = [], []
    for s, sk in zip(scores, sinks):
        m = jnp.maximum(jnp.max(s, -1, keepdims=True), sk)
        probs.append(jnp.exp(s - m).astype(BF16))
        sink_terms.append(jnp.exp(sk - m))
    for kv in range(A_KV_HEADS):
        vh = v2[:, kv * HEAD_DIM:(kv + 1) * HEAD_DIM]
        v3 = jnp.concatenate([vh, vh, ones], axis=1)
        oa = _dot(probs[kv], v3)
        on = oa[:, 0:2 * HEAD_DIM] / (oa[:, 2 * HEAD_DIM:] + sink_terms[kv])
        for pair in range(GQA // 2):
            c0 = (kv * GQA + 2 * pair) * HEAD_DIM
            lo, hi = on[2 * pair * BLK:(2 * pair + 1) * BLK], on[(2 * pair + 1) * BLK:(2 * pair + 2) * BLK]
            o_ref[:, M_A + c0:M_A + c0 + 2 * HEAD_DIM] = jnp.where(low, lo, hi).astype(BF16)


SWA_BT = 16


def _swa_sample_kernel(l, sink_ref, q_ref, kn_ref, vn_ref, ck_ref, cv_ref, o_ref, cko_ref, cvo_ref):
    bt = SWA_BT
    kn, vn = kn_ref[...], vn_ref[...]
    k2 = ck_ref[...].reshape(bt * WINDOW, KV_WIDTH).astype(BF16)
    v2 = cv_ref[...].reshape(bt * WINDOW, KV_WIDTH).astype(BF16)
    row = lax.broadcasted_iota(jnp.int32, (WINDOW, KV_WIDTH), 0)
    for j in range(bt):
        cko_ref[j] = jnp.where(row == WINDOW - 1, kn[j:j + 1, :], pltpu.roll(ck_ref[j], WINDOW - 1, 0))
        cvo_ref[j] = jnp.where(row == WINDOW - 1, vn[j:j + 1, :], pltpu.roll(cv_ref[j], WINDOW - 1, 0))
    nq = GQA * bt
    rowb = lax.broadcasted_iota(jnp.int32, (nq, bt * WINDOW), 0) % bt
    colb = lax.broadcasted_iota(jnp.int32, (nq, bt * WINDOW), 1) // WINDOW
    same = rowb == colb
    rg = lax.broadcasted_iota(jnp.int32, (nq, 1), 0) // bt
    for kv in range(A_KV_HEADS):
        sl = slice(kv * HEAD_DIM, (kv + 1) * HEAD_DIM)
        q4 = jnp.concatenate(
            [q_ref[:, (kv * GQA + g) * HEAD_DIM:(kv * GQA + g + 1) * HEAD_DIM] for g in range(GQA)], axis=0)
        q4 = q4 * HEAD_DIM ** -0.5
        kn4 = jnp.concatenate([kn[:, sl]] * GQA, axis=0)
        vn4 = jnp.concatenate([vn[:, sl]] * GQA, axis=0)
        sk = jnp.zeros((nq, 1), F32)
        for g in range(GQA):
            sk = jnp.where(rg == g, sink_ref[l, kv * GQA + g], sk)
        s = jnp.where(same, _dot_nt(q4.astype(BF16), k2[:, sl]), NEG_INF)
        s_self = jnp.sum(q4 * kn4, -1, keepdims=True)
        m = jnp.maximum(jnp.maximum(jnp.max(s, -1, keepdims=True), s_self), sk)
        p = jnp.exp(s - m)
        p_self = jnp.exp(s_self - m)
        den = jnp.sum(p, -1, keepdims=True) + p_self + jnp.exp(sk - m)
        o = (_dot(p.astype(BF16), v2[:, sl]) + p_self * vn4) / den
        for g in range(GQA):
            hh = kv * GQA + g
            o_ref[:, hh * HEAD_DIM:(hh + 1) * HEAD_DIM] = o[g * bt:(g + 1) * bt].astype(BF16)


S5_SHIFTS = (1, 2, 4)
S5_NTAB = 2 * (len(S5_SHIFTS) + 1) * SUB
S5_CHUNKS = S5_LANES // 128
S5_IN_PIECES = 4
S5_IN_K = S5_WIDTH // S5_IN_PIECES
S5_IN_N = S5_LANES // S5_IN_PIECES
S5_OUT_PIECES = 2
S5_OUT_K = S5_LANES // S5_OUT_PIECES
S5_OUT_N = S5_WIDTH // S5_OUT_PIECES


def _s5_tab(tab_ref, k, lanes):
    return (tab_ref[2 * k * SUB:(2 * k + 1) * SUB, lanes], tab_ref[(2 * k + 1) * SUB:(2 * k + 2) * SUB, lanes])


def _s5_drive(ub, bb_ref, p):
    return _dot(ub[:, p * S5_IN_K:(p + 1) * S5_IN_K], bb_ref[p])


def _s5_readout(h_piece, u, cre_ref, cim_ref, d_ref, wg_ref, bg_ref):
    ys = []
    for j in range(S5_OUT_PIECES):
        hr, hi = h_piece(j)
        ys.append(_dot(hr.astype(BF16), cre_ref[j]) - _dot(hi.astype(BF16), cim_ref[j]))
    y = jnp.concatenate(ys, axis=1) + d_ref[...] * u
    z = jax.nn.gelu(y)
    gate = jax.nn.sigmoid(_dot(z.astype(BF16), wg_ref[...]) + bg_ref[...])
    return (z * gate).astype(BF16)


def _s5_block(u_ref, bb_ref, cre_ref, cim_ref, d_ref, wg_ref, bg_ref, ab_ref, o_ref, hpre_ref, hpim_ref,
              xre, xim, hre, him):
    u = u_ref[...]
    ub = u.astype(BF16)
    drive = [_s5_drive(ub, bb_ref, p) for p in range(S5_IN_PIECES)]
    x3r = jnp.concatenate([d[:, 0:S5_IN_N] for d in drive], axis=1).reshape(BLK, S5_CHUNKS, 128)
    x3i = jnp.concatenate([d[:, S5_IN_N:] for d in drive], axis=1).reshape(BLK, S5_CHUNKS, 128)
    ar, ai = ab_ref[0:S5_CHUNKS, :], ab_ref[S5_CHUNKS:, :]
    hr, hi = hpre_ref[...], hpim_ref[...]
    hs_r, hs_i = [], []
    for t in range(BLK):
        hr, hi = ar * hr - ai * hi + x3r[t], ar * hi + ai * hr + x3i[t]
        hs_r.append(hr)
        hs_i.append(hi)
    hpre_ref[...] = hr
    hpim_ref[...] = hi
    h2r = jnp.stack(hs_r, axis=0).reshape(BLK, S5_LANES)
    h2i = jnp.stack(hs_i, axis=0).reshape(BLK, S5_LANES)

    def piece(j):
        return h2r[:, j * S5_OUT_K:(j + 1) * S5_OUT_K], h2i[:, j * S5_OUT_K:(j + 1) * S5_OUT_K]

    o_ref[:, M_B:M_B + S5_WIDTH] = _s5_readout(piece, u, cre_ref, cim_ref, d_ref, wg_ref, bg_ref)


def _s5_sample_kernel(u_ref, bb_ref, cre_ref, cim_ref, d_ref, wg_ref, bg_ref, tab_ref, h0re_ref, h0im_ref,
                      o_ref, hsre_ref, hsim_ref):
    u = u_ref[...]
    ub = u.astype(BF16)
    for p in range(S5_IN_PIECES):
        lanes = slice(p * S5_IN_N, (p + 1) * S5_IN_N)
        bu = _s5_drive(ub, bb_ref, p)
        pr, pi = _s5_tab(tab_ref, len(S5_SHIFTS), lanes)
        ar, ai = pr[0:1, :], pi[0:1, :]
        h0r, h0i = h0re_ref[:, lanes], h0im_ref[:, lanes]
        hsre_ref[:, lanes] = ar * h0r - ai * h0i + bu[:, 0:S5_IN_N]
        hsim_ref[:, lanes] = ar * h0i + ai * h0r + bu[:, S5_IN_N:]
    piece = lambda j: (hsre_ref[:, j * S5_OUT_K:(j + 1) * S5_OUT_K], hsim_ref[:, j * S5_OUT_K:(j + 1) * S5_OUT_K])
    o_ref[:, M_B:M_B + S5_WIDTH] = _s5_readout(piece, u, cre_ref, cim_ref, d_ref, wg_ref, bg_ref)


def _s5_param_specs(l, prompt):
    return [
        _layer_spec((S5_IN_PIECES, S5_IN_K, 2 * S5_IN_N), l),
        _layer_spec((S5_OUT_PIECES, S5_OUT_K, S5_OUT_N), l), _layer_spec((S5_OUT_PIECES, S5_OUT_K, S5_OUT_N), l),
        _layer_spec((1, S5_WIDTH), l), _layer_spec((S5_WIDTH, S5_WIDTH), l), _layer_spec((1, S5_WIDTH), l),
        _layer_spec((2 * S5_CHUNKS, 128), l) if prompt else _layer_spec((S5_NTAB, S5_LANES), l),
    ]


def _s5_params(prm, prompt):
    return (prm["bb"], prm["c_re"], prm["c_im"], prm["d"], prm["w_glu"], prm["b_glu"],
            prm["ab"] if prompt else prm["tab"])


def _gla_log_gate(ac, wa_ref, ba_ref):
    return _log_sigmoid(_dot(ac.astype(BF16), wa_ref[...]) + ba_ref[...]) * (1.0 / GLA_TAU)


def _gla_finish(o_heads, gate, ng):
    outs = [o * lax.rsqrt(jnp.mean(o * o, -1, keepdims=True) + LN_EPS) for o in o_heads]
    return jnp.concatenate(outs, axis=1) * ng * (gate * jax.nn.sigmoid(gate))


def _gla_block(q_ref, k_ref, v_ref, gate_ref, ac_ref, wa_ref, ba_ref, ng_ref, o_ref, s_ref):
    lg = _gla_log_gate(ac_ref[...], wa_ref, ba_ref)
    L = GLA_CHUNK
    row = lax.broadcasted_iota(jnp.int32, (L, GLA_KW), 0)
    causal = lax.broadcasted_iota(jnp.int32, (L, L), 0) >= lax.broadcasted_iota(jnp.int32, (L, L), 1)
    chunks = range(BLK // L)
    heads = range(GLA_HEADS)
    ks = lambda hh: slice(hh * GLA_DK, (hh + 1) * GLA_DK)
    vs = lambda hh: slice(hh * GLA_DV, (hh + 1) * GLA_DV)
    q_in, k_out, k_dec, dec, vb = [], [], [], [], []
    for c in chunks:
        rows = slice(c * L, (c + 1) * L)
        b = lg[rows, :]
        d = 1
        while d < L:
            b = b + jnp.where(row >= d, pltpu.roll(b, d, 0), 0.0)
            d *= 2
        b_end = b[L - 1:L, :]
        kc = k_ref[rows, :]
        q_in.append((q_ref[rows, :] * GLA_DK ** -0.5 * jnp.exp(b)).astype(BF16))
        k_out.append((kc * jnp.exp(-b)).astype(BF16))
        k_dec.append((kc * jnp.exp(b_end - b)).astype(BF16))
        dec.append(jnp.transpose(jnp.broadcast_to(jnp.exp(b_end), (GLA_DV, GLA_KW))))
        vb.append(v_ref[rows, :].astype(BF16))
    att = [[jnp.where(causal, _dot_nt(q_in[c][:, ks(hh)], k_out[c][:, ks(hh)]), 0.0).astype(BF16) for hh in heads]
           for c in chunks]
    kv = [[_dot_tn(k_dec[c][:, ks(hh)], vb[c][:, vs(hh)]) for hh in heads] for c in chunks]
    state = [s_ref[...]]
    for c in chunks:
        state.append(dec[c] * state[c] + jnp.concatenate(kv[c], axis=0))
    s_ref[...] = state[-1]
    for c in chunks:
        sb = state[c].astype(BF16)
        o_heads = [_dot(q_in[c][:, ks(hh)], sb[ks(hh), :]) + _dot(att[c][hh], vb[c][:, vs(hh)]) for hh in heads]
        o_ref[c * L:(c + 1) * L, M_C:M_C + GLA_WIDTH] = _gla_finish(
            o_heads, gate_ref[c * L:(c + 1) * L, :], ng_ref[...]).astype(BF16)


def _gla_param_specs(l):
    return [_layer_spec((128, GLA_KW), l), _layer_spec((1, GLA_KW), l), _layer_spec((1, GLA_WIDTH), l)]


GLA_BT = 16


def _gla_sample_kernel(q_ref, k_ref, v_ref, gate_ref, ac_ref, wa_ref, ba_ref, ng_ref, s_ref, o_ref, so_ref):
    bt = GLA_BT
    lg = _gla_log_gate(ac_ref[...], wa_ref, ba_ref)
    v = v_ref[...]

    def cols(x):
        return jnp.transpose(jnp.concatenate([x, jnp.zeros((128 - bt, GLA_KW), F32)], axis=0))

    eg_t, k_t, q_t = cols(jnp.exp(lg)), cols(k_ref[...]), cols(q_ref[...] * GLA_DK ** -0.5)
    o_rows = []
    for j in range(bt):
        bc = lambda t: jnp.broadcast_to(t[:, j:j + 1], (GLA_KW, GLA_DV))
        vrow = jnp.concatenate(
            [jnp.broadcast_to(v[j:j + 1, hh * GLA_DV:(hh + 1) * GLA_DV], (GLA_DK, GLA_DV)) for hh in range(GLA_HEADS)],
            axis=0)
        s_new = bc(eg_t) * s_ref[j] + bc(k_t) * vrow
        so_ref[j] = s_new
        qs = bc(q_t) * s_new
        o_rows.append(jnp.concatenate(
            [jnp.sum(qs[hh * GLA_DK:(hh + 1) * GLA_DK, :], axis=0, keepdims=True) for hh in range(GLA_HEADS)], axis=1))
    o = jnp.concatenate(o_rows, axis=0)
    o_heads = [o[:, hh * GLA_DV:(hh + 1) * GLA_DV] for hh in range(GLA_HEADS)]
    o_ref[:, M_C:M_C + GLA_WIDTH] = _gla_finish(o_heads, gate_ref[...], ng_ref[...]).astype(BF16)


assert SWA_BT == GLA_BT
SAMPLE_BT = SWA_BT


def _mix_sample_kernel(l, sink_ref, q_ref, kn_ref, vn_ref, ck_ref, cv_ref,
                       u_ref, bb_ref, cre_ref, cim_ref, d_ref, wg_ref, bg_ref, tab_ref, h0re_ref, h0im_ref,
                       gq_ref, gk_ref, gv_ref, gg_ref, ga_ref, wa_ref, ba_ref, ng_ref, s_ref,
                       o_ref, cko_ref, cvo_ref, hsre_ref, hsim_ref, so_ref):
    _swa_sample_kernel(l, sink_ref, q_ref, kn_ref, vn_ref, ck_ref, cv_ref, o_ref, cko_ref, cvo_ref)
    _s5_sample_kernel(u_ref, bb_ref, cre_ref, cim_ref, d_ref, wg_ref, bg_ref, tab_ref, h0re_ref, h0im_ref,
                      o_ref, hsre_ref, hsim_ref)
    _gla_sample_kernel(gq_ref, gk_ref, gv_ref, gg_ref, ga_ref, wa_ref, ba_ref, ng_ref, s_ref, o_ref, so_ref)


def _mix_sample(sink, h, cache_k, cache_v, s5_prm, h0re, h0im, gla_prm, state, mix, s_all, l):
    bt = SAMPLE_BT
    r0 = ROWS_P // bt
    blk = lambda c, w: pl.BlockSpec((bt, w), lambda s: (r0 + s, c // w))
    cache = pl.BlockSpec((None, bt, WINDOW, KV_WIDTH), lambda s: (l, s, 0, 0))
    h0 = pl.BlockSpec((None, bt, S5_LANES), lambda s: (l, s, 0))
    hs = pl.BlockSpec((bt, S5_LANES), lambda s: (s, 0))
    st = pl.BlockSpec((None, bt, GLA_KW, GLA_DV), lambda s: (l, s, 0, 0))
    in_specs = ([pl.BlockSpec(memory_space=pltpu.SMEM), blk(C_Q, A_WIDTH), blk(C_K, KV_WIDTH), blk(C_V, KV_WIDTH),
                 cache, cache, blk(C_U, S5_WIDTH)]
                + _s5_param_specs(l, False) + [h0, h0]
                + [blk(C_GQ, GLA_KW), blk(C_GK, GLA_KW), blk(C_GV, GLA_WIDTH), blk(C_GG, GLA_WIDTH), blk(C_GA, 128)]
                + _gla_param_specs(l) + [st])
    n_in = len(in_specs)
    hs_shape = jax.ShapeDtypeStruct((N_SAMPLE, S5_LANES), F32)
    return pl.pallas_call(
        _drop_inputs(functools.partial(_mix_sample_kernel, l), n_in, 2),
        grid=(N_SAMPLE // bt,),
        in_specs=in_specs + [_any(), _any()],
        out_specs=[blk(0, D_MODEL), cache, cache, hs, hs, st],
        out_shape=[jax.ShapeDtypeStruct((ROWS, D_MODEL), BF16),
                   jax.ShapeDtypeStruct(cache_k.shape, F32), jax.ShapeDtypeStruct(cache_v.shape, F32),
                   hs_shape, hs_shape, jax.ShapeDtypeStruct(s_all.shape, F32)],
        input_output_aliases={n_in: 0, 4: 1, 5: 2, n_in + 1: 5},
        compiler_params=_cparams("arbitrary"),
        name="mix_sample",
    )(sink, h, h, h, cache_k, cache_v, h, *_s5_params(s5_prm, False), h0re, h0im,
      h, h, h, h, h, *gla_prm, state, mix, s_all)


def _mix_prompt_kernel(l, n_cast, sink_ref, bias_ref, q_ref, kp_ref, kc_ref, vp_ref, vc_ref, u_ref,
                       gq_ref, gk_ref, gv_ref, gg_ref, ga_ref,
                       bb_ref, cre_ref, cim_ref, d_ref, wg_ref, bg_ref, ab_ref, wa_ref, ba_ref, ng_ref, *rest):
    w_refs, (o_ref, hpre_ref, hpim_ref, s_ref) = rest[:n_cast], rest[n_cast:n_cast + 4]
    c_refs, (xre, xim, hre, him) = rest[n_cast + 4:2 * n_cast + 4], rest[2 * n_cast + 4:]
    s_blk = pl.program_id(0)

    @pl.when(s_blk == 0)
    def _():
        hpre_ref[...] = jnp.zeros_like(hpre_ref)
        hpim_ref[...] = jnp.zeros_like(hpim_ref)
        s_ref[...] = jnp.zeros_like(s_ref)

    @pl.when(s_blk < N_PBLK)
    def _():
        for w_ref, c_ref in zip(w_refs, c_refs):
            c_ref[...] = w_ref[...].astype(BF16)
        _s5_block(u_ref, bb_ref, cre_ref, cim_ref, d_ref, wg_ref, bg_ref, ab_ref, o_ref, hpre_ref, hpim_ref,
                  xre, xim, hre, him)
        _swa_block(l, sink_ref, bias_ref, q_ref, kp_ref, kc_ref, vp_ref, vc_ref, o_ref)
        _gla_block(gq_ref, gk_ref, gv_ref, gg_ref, ga_ref, wa_ref, ba_ref, ng_ref, o_ref, s_ref)

    @pl.when(s_blk == N_PBLK)
    def _():
        o_ref[...] = jnp.zeros_like(o_ref)


def _mix_prompt(sink, h, s5_prm, gla_prm, w_casts, l):
    cur = _seq_block
    prev = lambda s: _seq_block(jnp.maximum(s - 1, 0))
    blk = lambda c, w, at=cur: pl.BlockSpec((BLK, w), lambda s: (at(s), c // w))
    const = lambda shape: pl.BlockSpec(shape, lambda s: (0,) * len(shape))
    casts = [_cast_specs(w, rows, layer) for w, rows, layer in w_casts]
    in_specs = ([pl.BlockSpec(memory_space=pltpu.SMEM),
                 pl.BlockSpec((None, BLK, 2 * BLK), lambda s: (jnp.minimum(s, 2), 0, 0)), blk(C_Q, A_WIDTH),
                 blk(C_K, KV_WIDTH, prev), blk(C_K, KV_WIDTH), blk(C_V, KV_WIDTH, prev), blk(C_V, KV_WIDTH),
                 blk(C_U, S5_WIDTH),
                 blk(C_GQ, GLA_KW), blk(C_GK, GLA_KW), blk(C_GV, GLA_WIDTH), blk(C_GG, GLA_WIDTH), blk(C_GA, 128)]
                + _s5_param_specs(l, True) + _gla_param_specs(l) + [c[0] for c in casts])
    out_blk = lambda s: jnp.where(s == N_PBLK, N_PBLK, cur(s))
    return pl.pallas_call(
        functools.partial(_mix_prompt_kernel, l, len(casts)),
        grid=(N_PBLK + 1,),
        in_specs=in_specs,
        out_specs=[pl.BlockSpec((BLK, D_MODEL), lambda s: (out_blk(s), 0)),
                   const((S5_CHUNKS, 128)), const((S5_CHUNKS, 128)), const((GLA_KW, GLA_DV))] + [c[1] for c in casts],
        out_shape=[jax.ShapeDtypeStruct((ROWS, D_MODEL), BF16),
                   jax.ShapeDtypeStruct((S5_CHUNKS, 128), F32), jax.ShapeDtypeStruct((S5_CHUNKS, 128), F32),
                   jax.ShapeDtypeStruct((GLA_KW, GLA_DV), F32)] + [c[2] for c in casts],
        scratch_shapes=[pltpu.VMEM((S5_CHUNKS * BLK, 128), F32)] * 4,
        compiler_params=_cparams("arbitrary"),
        name="mix_prompt",
    )(sink, _swa_bias(), *([h] * 11), *_s5_params(s5_prm, True), *gla_prm, *[w for w, _, _ in w_casts])


OUT_TM = 768
OUT_PARTS = 3
LN_ROWS = 16


def _out_proj_kernel(mix_ref, w_ref, x_ref, g_ref, b_ref, o_ref):
    i = pl.program_id(0)
    g, b = g_ref[...], b_ref[...]
    part = OUT_TM // OUT_PARTS
    rows = [slice(k * part, (k + 1) * part) for k in range(OUT_PARTS)]
    def norm(k, acc):
        for c in range(part // LN_ROWS):
            r0 = k * part + c * LN_ROWS
            y = _layer_norm_rows(ALPHA * x_ref[r0:r0 + LN_ROWS, :] + acc[c * LN_ROWS:(c + 1) * LN_ROWS, :], g, b)
            o_ref[r0:r0 + LN_ROWS, :] = _zero_pad_rows(y, i * OUT_TM + r0)

    prev = None
    for k, r in enumerate(rows):
        acc = _dot(mix_ref[r, :], w_ref[...])
        if prev is not None:
            norm(k - 1, prev)
        prev = acc
    norm(OUT_PARTS - 1, prev)


def _out_proj(mix, w, x, g, b, l):
    tm = OUT_TM
    return pl.pallas_call(
        _out_proj_kernel,
        grid=(ROWS // tm,),
        in_specs=[
            pl.BlockSpec((tm, D_MODEL), lambda i: (i, 0)),
            pl.BlockSpec((D_MODEL, D_MODEL), lambda i: (0, 0), pipeline_mode=pl.Buffered(1)),
            pl.BlockSpec((tm, D_MODEL), lambda i: (i, 0)),
            _layer_spec((1, D_MODEL), l), _layer_spec((1, D_MODEL), l),
        ],
        out_specs=pl.BlockSpec((tm, D_MODEL), lambda i: (i, 0)),
        out_shape=jax.ShapeDtypeStruct((ROWS, D_MODEL), F32),
        compiler_params=_cparams("arbitrary"),
        name="out_proj",
    )(mix, w, x, g, b)


FFN_TM = 768
FFN_TF = 512
FFN_HALO = 16
FFN_NI = ROWS // FFN_TM
assert ROWS % FFN_TM == 0 and FFN_TM >= N_SAMPLE + BLK + SUB and META_ROW % FFN_HALO == 0


def _ffn_kernel(final, x_ref, xh_ref, wu_ref, wg_ref, cw_ref, cb_ref, wd_ref, p0_ref, p1_ref, g_ref, b_ref,
                o_ref, *rest):
    if final:
        ys_ref, ut_ref, us_ref, xb_ref, a_ref = rest
    else:
        ut_ref, us_ref, xb_ref, a_ref = rest
    i, f = pl.program_id(0), pl.program_id(1)
    tm, hl = FFN_TM, FFN_HALO
    last_i = FFN_NI - 1

    @pl.when(f == 0)
    def _():
        xb_ref[...] = x_ref[...].astype(BF16)
        o_ref[...] = jnp.zeros_like(o_ref)

    cw0, cw1, cw2, cb = cw_ref[0:1, :], cw_ref[1:2, :], cw_ref[2:3, :], cb_ref[...]

    def act(um2, um1, u0, g0):
        return (jax.nn.gelu(cb + cw0 * um2 + cw1 * um1 + cw2 * u0) * g0).astype(BF16)

    xb = xb_ref[...]
    u = _dot(xb, wu_ref[...])
    gte = _dot(xb, wg_ref[...])
    um2, um1 = pltpu.roll(u, 2, 0), pltpu.roll(u, 1, 0)
    body = slice(hl, tm - N_SAMPLE)
    a_ref[body, :] = act(um2[body, :], um1[body, :], u[body, :], gte[body, :])
    ext = jnp.concatenate([_dot(xh_ref[...].astype(BF16), wu_ref[...]), u[0:hl, :]], axis=0)
    a_ref[0:hl, :] = act(ext[hl - 2:2 * hl - 2, :], ext[hl - 1:2 * hl - 1, :], u[0:hl, :], gte[0:hl, :])
    tail = slice(tm - N_SAMPLE, tm)
    samp = i == last_i
    a_ref[tail, :] = act(jnp.where(samp, p0_ref[...], um2[tail, :]), jnp.where(samp, p1_ref[...], um1[tail, :]),
                         u[tail, :], gte[tail, :])
    o_ref[...] += _dot(a_ref[...], wd_ref[...])
    ut_ref[...] = u[tm - N_SAMPLE - BLK - SUB:tm - N_SAMPLE - BLK, :]
    us_ref[0] = p1_ref[...]
    us_ref[1] = u[tm - N_SAMPLE:, :]

    @pl.when(f == pl.num_programs(1) - 1)
    def _():
        _deepnorm_ln(o_ref, x_ref, g_ref, b_ref, o_ref, i * tm, tm)
        if final:
            ys_ref[...] = o_ref[tm - N_SAMPLE:, :]


def _ffn(x, wup, cw, cb, wd, conv_p0, conv_p1, g, b, l, final):
    tm, tf = FFN_TM, FFN_TF
    nf = D_FF // tf
    tail = lambda i, f: jnp.where(i == FFN_NI - 1, f, 0)
    halo = lambda i, f: (jnp.where(i == 0, ROWS_P // FFN_HALO, i * (tm // FFN_HALO)) - 1, 0)
    y_spec = pl.BlockSpec((tm, D_MODEL), lambda i, f: (i, 0))
    y_specs = [y_spec, pl.BlockSpec((N_SAMPLE, D_MODEL), lambda i, f: (0, 0))] if final else [y_spec]
    y_shapes = ([jax.ShapeDtypeStruct((SEQ, D_MODEL), F32), jax.ShapeDtypeStruct((N_SAMPLE, D_MODEL), F32)]
                if final else [jax.ShapeDtypeStruct((ROWS, D_MODEL), F32)])
    return pl.pallas_call(
        functools.partial(_ffn_kernel, final),
        grid=(FFN_NI, nf),
        in_specs=[
            pl.BlockSpec((tm, D_MODEL), lambda i, f: (i, 0)),
            pl.BlockSpec((FFN_HALO, D_MODEL), halo),
            pl.BlockSpec((D_MODEL, tf), lambda i, f: (0, f)),
            pl.BlockSpec((D_MODEL, tf), lambda i, f: (0, nf + f)),
            pl.BlockSpec((None, CONV_W, tf), lambda i, f: (l, 0, f)),
            pl.BlockSpec((None, 1, tf), lambda i, f: (l, 0, f)),
            pl.BlockSpec((tf, D_MODEL), lambda i, f: (f, 0)),
            pl.BlockSpec((None, N_SAMPLE, tf), lambda i, f: (l, 0, tail(i, f))),
            pl.BlockSpec((None, N_SAMPLE, tf), lambda i, f: (l, 0, tail(i, f))),
            _layer_spec((1, D_MODEL), l), _layer_spec((1, D_MODEL), l),
        ],
        out_specs=y_specs + [
            pl.BlockSpec((SUB, tf), lambda i, f: (0, tail(i, f))),
            pl.BlockSpec((2, N_SAMPLE, tf), lambda i, f: (0, 0, tail(i, f))),
        ],
        out_shape=y_shapes + [
            jax.ShapeDtypeStruct((SUB, D_FF), F32),
            jax.ShapeDtypeStruct((2, N_SAMPLE, D_FF), F32),
        ],
        scratch_shapes=[pltpu.VMEM((tm, D_MODEL), BF16), pltpu.VMEM((tm, tf), BF16)],
        compiler_params=_cparams("arbitrary", "arbitrary"),
        name="conv_ffn",
    )(x, x, wup, wup, cw, cb, wd, conv_p0, conv_p1, g, b)


def _rope_table():
    inv = jnp.power(ROPE_THETA, -jnp.arange(ROPE_HALF, dtype=F32) / ROPE_HALF)
    pos = jnp.concatenate([jnp.arange(SEQ, dtype=jnp.int32) + N_META, jnp.zeros((LEAD,), jnp.int32),
                           jnp.arange(N_META, dtype=jnp.int32), jnp.full((N_SAMPLE,), PAST_LEN, jnp.int32)])
    ang = pos.astype(F32)[:, None] * inv[None, :]
    return jnp.concatenate([jnp.cos(ang), jnp.sin(ang)], axis=1)


def _s5_tables(lam_re, lam_im, log_dt, b_re, b_im, c_re, c_im, d, w_glu, b_glu):
    L = lam_re.shape[0]
    dt = jnp.exp(log_dt)[..., None]
    mag = jnp.exp(lam_re * dt)
    ab_re, ab_im = mag * jnp.cos(lam_im * dt), mag * jnp.sin(lam_im * dt)
    den = lam_re * lam_re + lam_im * lam_im
    nr, ni = ab_re - 1.0, ab_im
    f_re = (nr * lam_re + ni * lam_im) / den
    f_im = (ni * lam_re - nr * lam_im) / den
    bb_re = f_re[..., None] * b_re - f_im[..., None] * b_im
    bb_im = f_re[..., None] * b_im + f_im[..., None] * b_re
    gi, go = S5_GROUPS // S5_IN_PIECES, S5_GROUPS // S5_OUT_PIECES

    def bd_in(t):
        t = t.reshape(L, S5_IN_PIECES, gi, S5_STATE, S5_CH)
        return jnp.einsum('lqgph,gk->lqghkp', t, jnp.eye(gi, dtype=F32)).reshape(L, S5_IN_PIECES, S5_IN_K, S5_IN_N)

    def bd_out(t):
        t = t.reshape(L, S5_OUT_PIECES, go, S5_CH, S5_STATE)
        return jnp.einsum('lqghp,gk->lqgpkh', t, jnp.eye(go, dtype=F32)).reshape(
            L, S5_OUT_PIECES, S5_OUT_K, S5_OUT_N)

    ar, ai = ab_re.reshape(L, 1, S5_LANES), ab_im.reshape(L, 1, S5_LANES)
    pows = [(ar, ai)]
    for _ in range(SUB - 1):
        pr, pi = pows[-1]
        pows.append((pr * ar - pi * ai, pr * ai + pi * ar))
    row = jnp.arange(SUB)[None, :, None]
    parts = []
    for sft in S5_SHIFTS:
        pr, pi = pows[sft - 1]
        parts += [jnp.where(row >= sft, pr, 0.0), jnp.where(row >= sft, pi, 0.0)]
    parts += [jnp.concatenate([p[0] for p in pows], 1), jnp.concatenate([p[1] for p in pows], 1)]
    return dict(bb=jnp.concatenate([bd_in(bb_re), bd_in(bb_im)], axis=3).astype(BF16),
                c_re=bd_out(c_re).astype(BF16), c_im=bd_out(c_im).astype(BF16),
                d=d.reshape(L, 1, S5_WIDTH), w_glu=w_glu.astype(BF16), b_glu=b_glu.reshape(L, 1, S5_WIDTH),
                tab=jnp.concatenate(parts, 1),
                ab=jnp.concatenate([ab_re.reshape(L, S5_CHUNKS, 128), ab_im.reshape(L, S5_CHUNKS, 128)], axis=1))


@jax.jit
def kernel(x_prompt, x_sample, cache_swa_k, cache_swa_v, state_ssm_re, state_ssm_im, state_gla, state_conv,
           meta_tokens, ln_in_g, ln_in_b, w_in, attn_sink, s5_lam_re, s5_lam_im, s5_log_dt, s5_b_re, s5_b_im,
           s5_c_re, s5_c_im, s5_d, s5_w_glu, s5_b_glu, gla_w_a2, gla_b_a, gla_norm_g, w_out, ln1_g, ln1_b,
           ffn_w_up, ffn_conv_w, ffn_conv_b, ffn_w_down, ln2_g, ln2_b):
    L = DEPTH
    row3 = lambda t: t.reshape(L, 1, -1)
    wa_b = jnp.pad(gla_w_a2, ((0, 0), (0, 128 - GLA_LOWRANK), (0, 0))).astype(BF16)
    gla_args = (wa_b, row3(gla_b_a), row3(gla_norm_g))
    ln1, ln2 = (row3(ln1_g), row3(ln1_b)), (row3(ln2_g), row3(ln2_b))
    conv_b3 = row3(ffn_conv_b)
    cs = _rope_table()
    s5p = _s5_tables(s5_lam_re, s5_lam_im, s5_log_dt, s5_b_re, s5_b_im, s5_c_re, s5_c_im, s5_d, s5_w_glu, s5_b_glu)
    ck = cache_swa_k.reshape(L, N_SAMPLE, WINDOW, KV_WIDTH)
    cv = cache_swa_v.reshape(L, N_SAMPLE, WINDOW, KV_WIDTH)
    h0re = state_ssm_re.reshape(L, N_SAMPLE, S5_LANES)
    h0im = state_ssm_im.reshape(L, N_SAMPLE, S5_LANES)
    sg = state_gla.reshape(L, N_SAMPLE, GLA_KW, GLA_DV)
    conv_p0, conv_p1 = state_conv[:, :, 0, :], state_conv[:, :, 1, :]
    s_all = jnp.zeros((L, N_SAMPLE, GLA_KW, GLA_DV), F32)

    x, w_in_b = _ln_in(x_prompt.reshape(SEQ, D_MODEL), meta_tokens, x_sample.reshape(N_SAMPLE, D_MODEL),
                       ln_in_g.reshape(1, D_MODEL), ln_in_b.reshape(1, D_MODEL), w_in)
    outs = [[] for _ in range(9)]
    for l in range(L):
        h = _in_proj(x, w_in_b, cs, l)
        casts = [(ffn_w_up, 32, l), (ffn_w_down, 176, l), (w_out, 32, l)] + ([(w_in, 32, l + 1)] if l + 1 < L else [])
        mix, hp_re, hp_im, s_p, w_up_b, w_down_b, w_out_b, *w_next = _mix_prompt(
            attn_sink, h, s5p, gla_args, casts, l)
        w_in_b = w_next[0] if w_next else None
        mix, ck, cv, hs_re, hs_im, s_all = _mix_sample(
            attn_sink, h, ck, cv, s5p, h0re, h0im, gla_args, sg, mix, s_all, l)
        x = _out_proj(mix, w_out_b, x, *ln1, l)
        res = _ffn(x, w_up_b, ffn_conv_w, conv_b3, w_down_b, conv_p0, conv_p1, *ln2, l, l == L - 1)
        if l == L - 1:
            y_prompt, y_sample, u_tail, u_s = res
        else:
            x, u_tail, u_s = res
        kv_p = lambda c: h[SEQ - WINDOW:SEQ, c:c + KV_WIDTH].reshape(1, WINDOW, A_KV_HEADS, HEAD_DIM)
        new = (kv_p(C_K), kv_p(C_V),
               hp_re.reshape(1, S5_GROUPS, S5_STATE), hp_im.reshape(1, S5_GROUPS, S5_STATE),
               s_p.reshape(1, GLA_HEADS, GLA_DK, GLA_DV), u_tail[SUB - 2:SUB][None],
               hs_re.reshape(N_SAMPLE, S5_GROUPS, S5_STATE), hs_im.reshape(N_SAMPLE, S5_GROUPS, S5_STATE),
               jnp.transpose(u_s, (1, 0, 2)))
        for lst, val in zip(outs, new):
            lst.append(val)
    st = [jnp.stack(lst) for lst in outs]
    return (y_prompt.reshape(1, SEQ, D_MODEL), y_sample.reshape(N_SAMPLE, 1, D_MODEL),
            st[0], st[1], st[2], st[3], st[4], st[5],
            ck.reshape(L, N_SAMPLE, WINDOW, A_KV_HEADS, HEAD_DIM),
            cv.reshape(L, N_SAMPLE, WINDOW, A_KV_HEADS, HEAD_DIM),
            st[6], st[7], s_all.reshape(L, N_SAMPLE, GLA_HEADS, GLA_DK, GLA_DV), st[8])
```

```python
import functools

import jax
import jax.numpy as jnp
from jax import lax
from jax.experimental import pallas as pl
from jax.experimental.pallas import tpu as pltpu

F32 = jnp.float32
BF16 = jnp.bfloat16

D_MODEL = 2048
SEQ = 8192
DEPTH = 2
N_SAMPLE = 128
N_META = 16
HEAD_DIM = 64
A_WIDTH = 1024
A_HEADS = 16
A_KV_HEADS = 4
GQA = 4
KV_WIDTH = A_KV_HEADS * HEAD_DIM
WINDOW = 128
ROPE_DIM = 16
ROPE_HALF = ROPE_DIM // 2
ROPE_THETA = 500000.0
PAST_LEN = 8192
S5_WIDTH = 512
S5_CH = 16
S5_GROUPS = 32
S5_STATE = 64
S5_LANES = S5_GROUPS * S5_STATE
GLA_WIDTH = 512
GLA_HEADS = 4
GLA_DV = 128
GLA_DK = 64
GLA_KW = GLA_HEADS * GLA_DK
GLA_LOWRANK = 16
GLA_TAU = 16.0
GLA_CHUNK = 64
D_FF = 5632
CONV_W = 3
LN_EPS = 1e-5
ALPHA = (2 * DEPTH) ** 0.25
NEG_INF = -1e30

BLK = 128
T_PROMPT = N_META + SEQ
LEAD = (-T_PROMPT) % BLK
N_PBLK = (LEAD + T_PROMPT) // BLK
ROWS_P = N_PBLK * BLK
ROWS = ROWS_P + N_SAMPLE
META_ROW = SEQ + LEAD
SUB = 8

C_Q, C_K, C_V, C_U, C_GQ, C_GK, C_GV, C_GG, C_GA = 0, 1024, 1280, 1536, 2048, 2304, 2560, 3072, 3584
IN_COLS = 3600
IN_TILE = 1280
IN_PAD = 3 * IN_TILE
ROPE_COLS = A_WIDTH + KV_WIDTH
M_A, M_B, M_C = 0, A_WIDTH, A_WIDTH + S5_WIDTH

VMEM_LIMIT = 56 * 1024 * 1024


def _cparams(*sem):
    return pltpu.CompilerParams(dimension_semantics=sem, vmem_limit_bytes=VMEM_LIMIT)


def _any():
    return pl.BlockSpec(memory_space=pl.ANY)


def _drop_inputs(kern, start, count):
    def wrapped(*refs):
        return kern(*refs[:start], *refs[start + count:])
    return wrapped


def _seq_block(s):
    return jnp.where(s == 0, N_PBLK - 1, s - 1)


def _layer_spec(shape, l):
    return pl.BlockSpec((None,) + shape, lambda *_: (l,) + (0,) * len(shape))


def _cast_specs(w, rows, l):
    n, cols = w.shape[1] // rows, w.shape[2]
    chunk = lambda s: jnp.minimum(s, n - 1)
    return (pl.BlockSpec((None, rows, cols), lambda s: (l, chunk(s), 0)),
            pl.BlockSpec((rows, cols), lambda s: (chunk(s), 0)),
            jax.ShapeDtypeStruct(w.shape[1:], BF16))


def _layer_norm_rows(x, g, b):
    mu = jnp.mean(x, -1, keepdims=True)
    xc = x - mu
    var = jnp.mean(xc * xc, -1, keepdims=True)
    return xc * lax.rsqrt(var + LN_EPS) * g + b


def _zero_pad_rows(y, first_row):
    row = first_row + lax.broadcasted_iota(jnp.int32, y.shape, 0)
    return jnp.where((row >= SEQ) & (row < META_ROW), 0.0, y)


def _deepnorm_ln(acc_ref, x_ref, g_ref, b_ref, o_ref, first_row, rows):
    g, b = g_ref[...], b_ref[...]
    for c in range(rows // BLK):
        r = slice(c * BLK, (c + 1) * BLK)
        y = _layer_norm_rows(ALPHA * x_ref[r, :] + acc_ref[r, :], g, b)
        o_ref[r, :] = _zero_pad_rows(y, first_row + c * BLK)


def _dot(a, b):
    return jnp.dot(a, b, preferred_element_type=F32)


def _dot_nt(a, b):
    return lax.dot_general(a, b, (((1,), (1,)), ((), ())), preferred_element_type=F32)


def _dot_tn(a, b):
    return lax.dot_general(a, b, (((0,), (0,)), ((), ())), preferred_element_type=F32)


def _log_sigmoid(x):
    return jnp.minimum(x, 0.0) - jnp.log(1.0 + jnp.exp(-jnp.abs(x)))


LN_IN_TM = 512
LN_IN_STEPS = SEQ // LN_IN_TM + 1
assert SEQ % LN_IN_TM == 0 and ROWS - SEQ <= LN_IN_TM


def _ln_in_kernel(xp_ref, meta_ref, xs_ref, g_ref, b_ref, o_ref):
    i = pl.program_id(0)
    g, b = g_ref[...], b_ref[...]

    @pl.when(i < LN_IN_STEPS - 1)
    def _():
        for c in range(LN_IN_TM // BLK):
            r = slice(c * BLK, (c + 1) * BLK)
            o_ref[r, :] = _layer_norm_rows(xp_ref[r, :], g, b)

    @pl.when(i == LN_IN_STEPS - 1)
    def _():
        o_ref[0:LEAD, :] = jnp.zeros((LEAD, D_MODEL), F32)
        o_ref[LEAD:BLK, :] = _layer_norm_rows(meta_ref[...], g, b)
        o_ref[BLK:BLK + N_SAMPLE, :] = _layer_norm_rows(xs_ref[...], g, b)


def _ln_in(xp, meta, xs, g, b):
    tm = LN_IN_TM
    const = lambda shape: pl.BlockSpec(shape, lambda i: (0,) * len(shape))
    return pl.pallas_call(
        _ln_in_kernel,
        grid=(LN_IN_STEPS,),
        in_specs=[
            pl.BlockSpec((tm, D_MODEL), lambda i: (jnp.minimum(i, SEQ // tm - 1), 0)),
            const((N_META, D_MODEL)), const((N_SAMPLE, D_MODEL)), const((1, D_MODEL)), const((1, D_MODEL)),
        ],
        out_specs=pl.BlockSpec((tm, D_MODEL), lambda i: (i, 0)),
        out_shape=jax.ShapeDtypeStruct((ROWS, D_MODEL), F32),
        compiler_params=_cparams("arbitrary"),
        name="ln_in",
    )(xp, meta, xs, g, b)


IN_TM = 1056


def _in_proj_kernel(x_ref, w_ref, cs_ref, o_ref, xb_ref):
    j = pl.program_id(1)

    last_j = IN_PAD // IN_TILE - 1

    @pl.when(j == 0)
    def _():
        xb = x_ref[...].astype(BF16)
        xb_ref[...] = xb
        c8, s8 = cs_ref[:, 0:ROPE_HALF], cs_ref[:, ROPE_HALF:ROPE_DIM]
        rest = HEAD_DIM - ROPE_DIM
        one, zero = jnp.ones((IN_TM, rest), F32), jnp.zeros((IN_TM, rest), F32)
        z8 = jnp.zeros((IN_TM, ROPE_HALF), F32)
        cos = jnp.concatenate([c8, c8, one] * 2, axis=1)
        sa = jnp.concatenate([-s8, z8, zero] * 2, axis=1)
        sb = jnp.concatenate([z8, s8, zero] * 2, axis=1)
        for c in range(ROPE_COLS // 256):
            acc = _dot(xb, w_ref[:, c * 256:(c + 1) * 256])
            for hlf in range(2):
                blk = acc[:, hlf * 128:(hlf + 1) * 128]
                o_ref[:, c * 256 + hlf * 128:c * 256 + (hlf + 1) * 128] = (
                    blk * cos + pltpu.roll(blk, 128 - ROPE_HALF, 1) * sa + pltpu.roll(blk, ROPE_HALF, 1) * sb)

    @pl.when((j > 0) & (j < last_j))
    def _():
        o_ref[...] = _dot(xb_ref[...], w_ref[...])

    @pl.when(j == last_j)
    def _():
        o_ref[...] = _dot(xb_ref[...], w_ref[...])
        edge = (IN_COLS % IN_TILE) // 128 * 128
        col = edge + lax.broadcasted_iota(jnp.int32, (IN_TM, 128), 1)
        o_ref[:, edge:edge + 128] = jnp.where(col < IN_COLS % IN_TILE, o_ref[:, edge:edge + 128], 0.0)
        o_ref[:, edge + 128:] = jnp.zeros((IN_TM, IN_TILE - edge - 128), F32)


def _in_proj(x, w, cs, l):
    return pl.pallas_call(
        _in_proj_kernel,
        grid=(ROWS // IN_TM, IN_PAD // IN_TILE),
        in_specs=[
            pl.BlockSpec((IN_TM, D_MODEL), lambda i, j: (i, 0)),
            pl.BlockSpec((None, D_MODEL, IN_TILE), lambda i, j: (l, 0, j)),
            pl.BlockSpec((IN_TM, ROPE_DIM), lambda i, j: (i, 0)),
        ],
        out_specs=pl.BlockSpec((IN_TM, IN_TILE), lambda i, j: (i, j)),
        out_shape=jax.ShapeDtypeStruct((ROWS, IN_PAD), F32),
        scratch_shapes=[pltpu.VMEM((IN_TM, D_MODEL), BF16)],
        compiler_params=_cparams("arbitrary", "arbitrary"),
        name="in_proj",
    )(x, w, cs)


def _swa_bias():
    row = lax.broadcasted_iota(jnp.int32, (3, BLK, 2 * BLK), 1)
    col = lax.broadcasted_iota(jnp.int32, (3, BLK, 2 * BLK), 2)
    s_blk = lax.broadcasted_iota(jnp.int32, (3, BLK, 2 * BLK), 0)
    diff = row - col + BLK
    kpos = (s_blk - 1) * BLK + col - LEAD
    ok = (diff >= 0) & (diff <= WINDOW) & (kpos >= 0)
    return jnp.where(ok, 0.0, NEG_INF).astype(F32)


def _swa_block(l, sink_ref, bias_ref, q_ref, kp_ref, kc_ref, vp_ref, vc_ref, o_ref):
    k2 = jnp.concatenate([kp_ref[...], kc_ref[...]], axis=0).astype(BF16)
    v2 = jnp.concatenate([vp_ref[...], vc_ref[...]], axis=0).astype(BF16)
    nq = GQA * BLK
    bias = jnp.concatenate([bias_ref[...]] * GQA, axis=0)
    low = lax.broadcasted_iota(jnp.int32, (BLK, 2 * HEAD_DIM), 1) < HEAD_DIM
    g_of_row = lax.broadcasted_iota(jnp.int32, (nq, 1), 0) // BLK
    ones = jnp.ones((2 * BLK, 2 * HEAD_DIM), BF16)
    scores, sinks = [], []
    for kv in range(A_KV_HEADS):
        kh = k2[:, kv * HEAD_DIM:(kv + 1) * HEAD_DIM]
        kk = jnp.concatenate([kh, kh], axis=1)
        parts = []
        for pair in range(GQA // 2):
            c0 = (kv * GQA + 2 * pair) * HEAD_DIM
            qp = q_ref[:, c0:c0 + 2 * HEAD_DIM] * HEAD_DIM ** -0.5
            parts += [jnp.where(low, qp, 0.0).astype(BF16), jnp.where(low, 0.0, qp).astype(BF16)]
        q4 = jnp.concatenate(parts, axis=0)
        sk = jnp.zeros((nq, 1), F32)
        for g in range(GQA):
            sk = jnp.where(g_of_row == g, sink_ref[l, kv * GQA + g], sk)
        scores.append(_dot_nt(q4, kk) + bias)
        sinks.append(sk)
    probs, sink_terms = [], []
    for s, sk in zip(scores, sinks):
        m = jnp.maximum(jnp.max(s, -1, keepdims=True), sk)
        probs.append(jnp.exp(s - m).astype(BF16))
        sink_terms.append(jnp.exp(sk - m))
    for kv in range(A_KV_HEADS):
        vh = v2[:, kv * HEAD_DIM:(kv + 1) * HEAD_DIM]
        v3 = jnp.concatenate([vh, vh, ones], axis=1)
        oa = _dot(probs[kv], v3)
        on = oa[:, 0:2 * HEAD_DIM] / (oa[:, 2 * HEAD_DIM:] + sink_terms[kv])
        for pair in range(GQA // 2):
            c0 = (kv * GQA + 2 * pair) * HEAD_DIM
            lo, hi = on[2 * pair * BLK:(2 * pair + 1) * BLK], on[(2 * pair + 1) * BLK:(2 * pair + 2) * BLK]
            o_ref[:, M_A + c0:M_A + c0 + 2 * HEAD_DIM] = jnp.where(low, lo, hi).astype(BF16)


SWA_BT = 16


def _swa_sample_kernel(l, sink_ref, q_ref, kn_ref, vn_ref, ck_ref, cv_ref, o_ref, cko_ref, cvo_ref):
    bt = SWA_BT
    kn, vn = kn_ref[...], vn_ref[...]
    k2 = ck_ref[...].reshape(bt * WINDOW, KV_WIDTH).astype(BF16)
    v2 = cv_ref[...].reshape(bt * WINDOW, KV_WIDTH).astype(BF16)
    row = lax.broadcasted_iota(jnp.int32, (WINDOW, KV_WIDTH), 0)
    for j in range(bt):
        cko_ref[j] = jnp.where(row == WINDOW - 1, kn[j:j + 1, :], pltpu.roll(ck_ref[j], WINDOW - 1, 0))
        cvo_ref[j] = jnp.where(row == WINDOW - 1, vn[j:j + 1, :], pltpu.roll(cv_ref[j], WINDOW - 1, 0))
    nq = GQA * bt
    rowb = lax.broadcasted_iota(jnp.int32, (nq, bt * WINDOW), 0) % bt
    colb = lax.broadcasted_iota(jnp.int32, (nq, bt * WINDOW), 1) // WINDOW
    same = rowb == colb
    rg = lax.broadcasted_iota(jnp.int32, (nq, 1), 0) // bt
    for kv in range(A_KV_HEADS):
        sl = slice(kv * HEAD_DIM, (kv + 1) * HEAD_DIM)
        q4 = jnp.concatenate(
            [q_ref[:, (kv * GQA + g) * HEAD_DIM:(kv * GQA + g + 1) * HEAD_DIM] for g in range(GQA)], axis=0)
        q4 = q4 * HEAD_DIM ** -0.5
        kn4 = jnp.concatenate([kn[:, sl]] * GQA, axis=0)
        vn4 = jnp.concatenate([vn[:, sl]] * GQA, axis=0)
        sk = jnp.zeros((nq, 1), F32)
        for g in range(GQA):
            sk = jnp.where(rg == g, sink_ref[l, kv * GQA + g], sk)
        s = jnp.where(same, _dot_nt(q4.astype(BF16), k2[:, sl]), NEG_INF)
        s_self = jnp.sum(q4 * kn4, -1, keepdims=True)
        m = jnp.maximum(jnp.maximum(jnp.max(s, -1, keepdims=True), s_self), sk)
        p = jnp.exp(s - m)
        p_self = jnp.exp(s_self - m)
        den = jnp.sum(p, -1, keepdims=True) + p_self + jnp.exp(sk - m)
        o = (_dot(p.astype(BF16), v2[:, sl]) + p_self * vn4) / den
        for g in range(GQA):
            hh = kv * GQA + g
            o_ref[:, hh * HEAD_DIM:(hh + 1) * HEAD_DIM] = o[g * bt:(g + 1) * bt].astype(BF16)


S5_SHIFTS = (1, 2, 4)
S5_NTAB = 2 * (len(S5_SHIFTS) + 1) * SUB
S5_CHUNKS = S5_LANES // 128
S5_IN_PIECES = 4
S5_IN_K = S5_WIDTH // S5_IN_PIECES
S5_IN_N = S5_LANES // S5_IN_PIECES
S5_OUT_PIECES = 2
S5_OUT_K = S5_LANES // S5_OUT_PIECES
S5_OUT_N = S5_WIDTH // S5_OUT_PIECES


def _s5_tab(tab_ref, k, lanes):
    return (tab_ref[2 * k * SUB:(2 * k + 1) * SUB, lanes], tab_ref[(2 * k + 1) * SUB:(2 * k + 2) * SUB, lanes])


def _s5_drive(ub, bb_ref, p):
    return _dot(ub[:, p * S5_IN_K:(p + 1) * S5_IN_K], bb_ref[p])


def _s5_readout(h_piece, u, cre_ref, cim_ref, d_ref, wg_ref, bg_ref):
    ys = []
    for j in range(S5_OUT_PIECES):
        hr, hi = h_piece(j)
        ys.append(_dot(hr.astype(BF16), cre_ref[j]) - _dot(hi.astype(BF16), cim_ref[j]))
    y = jnp.concatenate(ys, axis=1) + d_ref[...] * u
    z = jax.nn.gelu(y)
    gate = jax.nn.sigmoid(_dot(z.astype(BF16), wg_ref[...]) + bg_ref[...])
    return (z * gate).astype(BF16)


def _s5_block(u_ref, bb_ref, cre_ref, cim_ref, d_ref, wg_ref, bg_ref, ab_ref, o_ref, hpre_ref, hpim_ref,
              xre, xim, hre, him):
    u = u_ref[...]
    ub = u.astype(BF16)
    drive = [_s5_drive(ub, bb_ref, p) for p in range(S5_IN_PIECES)]
    x3r = jnp.concatenate([d[:, 0:S5_IN_N] for d in drive], axis=1).reshape(BLK, S5_CHUNKS, 128)
    x3i = jnp.concatenate([d[:, S5_IN_N:] for d in drive], axis=1).reshape(BLK, S5_CHUNKS, 128)
    ar, ai = ab_ref[0:S5_CHUNKS, :], ab_ref[S5_CHUNKS:, :]
    hr, hi = hpre_ref[...], hpim_ref[...]
    hs_r, hs_i = [], []
    for t in range(BLK):
        hr, hi = ar * hr - ai * hi + x3r[t], ar * hi + ai * hr + x3i[t]
        hs_r.append(hr)
        hs_i.append(hi)
    hpre_ref[...] = hr
    hpim_ref[...] = hi
    h2r = jnp.stack(hs_r, axis=0).reshape(BLK, S5_LANES)
    h2i = jnp.stack(hs_i, axis=0).reshape(BLK, S5_LANES)

    def piece(j):
        return h2r[:, j * S5_OUT_K:(j + 1) * S5_OUT_K], h2i[:, j * S5_OUT_K:(j + 1) * S5_OUT_K]

    o_ref[:, M_B:M_B + S5_WIDTH] = _s5_readout(piece, u, cre_ref, cim_ref, d_ref, wg_ref, bg_ref)


def _s5_sample_kernel(u_ref, bb_ref, cre_ref, cim_ref, d_ref, wg_ref, bg_ref, tab_ref, h0re_ref, h0im_ref,
                      o_ref, hsre_ref, hsim_ref):
    u = u_ref[...]
    ub = u.astype(BF16)
    for p in range(S5_IN_PIECES):
        lanes = slice(p * S5_IN_N, (p + 1) * S5_IN_N)
        bu = _s5_drive(ub, bb_ref, p)
        pr, pi = _s5_tab(tab_ref, len(S5_SHIFTS), lanes)
        ar, ai = pr[0:1, :], pi[0:1, :]
        h0r, h0i = h0re_ref[:, lanes], h0im_ref[:, lanes]
        hsre_ref[:, lanes] = ar * h0r - ai * h0i + bu[:, 0:S5_IN_N]
        hsim_ref[:, lanes] = ar * h0i + ai * h0r + bu[:, S5_IN_N:]
    piece = lambda j: (hsre_ref[:, j * S5_OUT_K:(j + 1) * S5_OUT_K], hsim_ref[:, j * S5_OUT_K:(j + 1) * S5_OUT_K])
    o_ref[:, M_B:M_B + S5_WIDTH] = _s5_readout(piece, u, cre_ref, cim_ref, d_ref, wg_ref, bg_ref)


def _s5_param_specs(l, prompt):
    return [
        _layer_spec((S5_IN_PIECES, S5_IN_K, 2 * S5_IN_N), l),
        _layer_spec((S5_OUT_PIECES, S5_OUT_K, S5_OUT_N), l), _layer_spec((S5_OUT_PIECES, S5_OUT_K, S5_OUT_N), l),
        _layer_spec((1, S5_WIDTH), l), _layer_spec((S5_WIDTH, S5_WIDTH), l), _layer_spec((1, S5_WIDTH), l),
        _layer_spec((2 * S5_CHUNKS, 128), l) if prompt else _layer_spec((S5_NTAB, S5_LANES), l),
    ]


def _s5_params(prm, prompt):
    return (prm["bb"], prm["c_re"], prm["c_im"], prm["d"], prm["w_glu"], prm["b_glu"],
            prm["ab"] if prompt else prm["tab"])


def _gla_log_gate(ac, wa_ref, ba_ref):
    return _log_sigmoid(_dot(ac.astype(BF16), wa_ref[...]) + ba_ref[...]) * (1.0 / GLA_TAU)


def _gla_finish(o_heads, gate, ng):
    outs = [o * lax.rsqrt(jnp.mean(o * o, -1, keepdims=True) + LN_EPS) for o in o_heads]
    return jnp.concatenate(outs, axis=1) * ng * (gate * jax.nn.sigmoid(gate))


def _gla_block(q_ref, k_ref, v_ref, gate_ref, ac_ref, wa_ref, ba_ref, ng_ref, o_ref, s_ref):
    lg = _gla_log_gate(ac_ref[...], wa_ref, ba_ref)
    L = GLA_CHUNK
    row = lax.broadcasted_iota(jnp.int32, (L, GLA_KW), 0)
    causal = lax.broadcasted_iota(jnp.int32, (L, L), 0) >= lax.broadcasted_iota(jnp.int32, (L, L), 1)
    chunks = range(BLK // L)
    heads = range(GLA_HEADS)
    ks = lambda hh: slice(hh * GLA_DK, (hh + 1) * GLA_DK)
    vs = lambda hh: slice(hh * GLA_DV, (hh + 1) * GLA_DV)
    q_in, k_out, k_dec, dec, vb = [], [], [], [], []
    for c in chunks:
        rows = slice(c * L, (c + 1) * L)
        b = lg[rows, :]
        d = 1
        while d < L:
            b = b + jnp.where(row >= d, pltpu.roll(b, d, 0), 0.0)
            d *= 2
        b_end = b[L - 1:L, :]
        kc = k_ref[rows, :]
        q_in.append((q_ref[rows, :] * GLA_DK ** -0.5 * jnp.exp(b)).astype(BF16))
        k_out.append((kc * jnp.exp(-b)).astype(BF16))
        k_dec.append((kc * jnp.exp(b_end - b)).astype(BF16))
        dec.append(jnp.transpose(jnp.broadcast_to(jnp.exp(b_end), (GLA_DV, GLA_KW))))
        vb.append(v_ref[rows, :].astype(BF16))
    att = [[jnp.where(causal, _dot_nt(q_in[c][:, ks(hh)], k_out[c][:, ks(hh)]), 0.0).astype(BF16) for hh in heads]
           for c in chunks]
    kv = [[_dot_tn(k_dec[c][:, ks(hh)], vb[c][:, vs(hh)]) for hh in heads] for c in chunks]
    state = [s_ref[...]]
    for c in chunks:
        state.append(dec[c] * state[c] + jnp.concatenate(kv[c], axis=0))
    s_ref[...] = state[-1]
    for c in chunks:
        sb = state[c].astype(BF16)
        o_heads = [_dot(q_in[c][:, ks(hh)], sb[ks(hh), :]) + _dot(att[c][hh], vb[c][:, vs(hh)]) for hh in heads]
        o_ref[c * L:(c + 1) * L, M_C:M_C + GLA_WIDTH] = _gla_finish(
            o_heads, gate_ref[c * L:(c + 1) * L, :], ng_ref[...]).astype(BF16)


def _gla_param_specs(l):
    return [_layer_spec((128, GLA_KW), l), _layer_spec((1, GLA_KW), l), _layer_spec((1, GLA_WIDTH), l)]


GLA_BT = 16


def _gla_sample_kernel(q_ref, k_ref, v_ref, gate_ref, ac_ref, wa_ref, ba_ref, ng_ref, s_ref, o_ref, so_ref):
    bt = GLA_BT
    lg = _gla_log_gate(ac_ref[...], wa_ref, ba_ref)
    v = v_ref[...]

    def cols(x):
        return jnp.transpose(jnp.concatenate([x, jnp.zeros((128 - bt, GLA_KW), F32)], axis=0))

    eg_t, k_t, q_t = cols(jnp.exp(lg)), cols(k_ref[...]), cols(q_ref[...] * GLA_DK ** -0.5)
    o_rows = []
    for j in range(bt):
        bc = lambda t: jnp.broadcast_to(t[:, j:j + 1], (GLA_KW, GLA_DV))
        vrow = jnp.concatenate(
            [jnp.broadcast_to(v[j:j + 1, hh * GLA_DV:(hh + 1) * GLA_DV], (GLA_DK, GLA_DV)) for hh in range(GLA_HEADS)],
            axis=0)
        s_new = bc(eg_t) * s_ref[j] + bc(k_t) * vrow
        so_ref[j] = s_new
        qs = bc(q_t) * s_new
        o_rows.append(jnp.concatenate(
            [jnp.sum(qs[hh * GLA_DK:(hh + 1) * GLA_DK, :], axis=0, keepdims=True) for hh in range(GLA_HEADS)], axis=1))
    o = jnp.concatenate(o_rows, axis=0)
    o_heads = [o[:, hh * GLA_DV:(hh + 1) * GLA_DV] for hh in range(GLA_HEADS)]
    o_ref[:, M_C:M_C + GLA_WIDTH] = _gla_finish(o_heads, gate_ref[...], ng_ref[...]).astype(BF16)


assert SWA_BT == GLA_BT
SAMPLE_BT = SWA_BT


def _mix_sample_kernel(l, sink_ref, q_ref, kn_ref, vn_ref, ck_ref, cv_ref,
                       u_ref, bb_ref, cre_ref, cim_ref, d_ref, wg_ref, bg_ref, tab_ref, h0re_ref, h0im_ref,
                       gq_ref, gk_ref, gv_ref, gg_ref, ga_ref, wa_ref, ba_ref, ng_ref, s_ref,
                       o_ref, cko_ref, cvo_ref, hsre_ref, hsim_ref, so_ref):
    _swa_sample_kernel(l, sink_ref, q_ref, kn_ref, vn_ref, ck_ref, cv_ref, o_ref, cko_ref, cvo_ref)
    _s5_sample_kernel(u_ref, bb_ref, cre_ref, cim_ref, d_ref, wg_ref, bg_ref, tab_ref, h0re_ref, h0im_ref,
                      o_ref, hsre_ref, hsim_ref)
    _gla_sample_kernel(gq_ref, gk_ref, gv_ref, gg_ref, ga_ref, wa_ref, ba_ref, ng_ref, s_ref, o_ref, so_ref)


def _mix_sample(sink, h, cache_k, cache_v, s5_prm, h0re, h0im, gla_prm, state, mix, s_all, l):
    bt = SAMPLE_BT
    r0 = ROWS_P // bt
    blk = lambda c, w: pl.BlockSpec((bt, w), lambda s: (r0 + s, c // w))
    cache = pl.BlockSpec((None, bt, WINDOW, KV_WIDTH), lambda s: (l, s, 0, 0))
    h0 = pl.BlockSpec((None, bt, S5_LANES), lambda s: (l, s, 0))
    hs = pl.BlockSpec((bt, S5_LANES), lambda s: (s, 0))
    st = pl.BlockSpec((None, bt, GLA_KW, GLA_DV), lambda s: (l, s, 0, 0))
    in_specs = ([pl.BlockSpec(memory_space=pltpu.SMEM), blk(C_Q, A_WIDTH), blk(C_K, KV_WIDTH), blk(C_V, KV_WIDTH),
                 cache, cache, blk(C_U, S5_WIDTH)]
                + _s5_param_specs(l, False) + [h0, h0]
                + [blk(C_GQ, GLA_KW), blk(C_GK, GLA_KW), blk(C_GV, GLA_WIDTH), blk(C_GG, GLA_WIDTH), blk(C_GA, 128)]
                + _gla_param_specs(l) + [st])
    n_in = len(in_specs)
    hs_shape = jax.ShapeDtypeStruct((N_SAMPLE, S5_LANES), F32)
    return pl.pallas_call(
        _drop_inputs(functools.partial(_mix_sample_kernel, l), n_in, 2),
        grid=(N_SAMPLE // bt,),
        in_specs=in_specs + [_any(), _any()],
        out_specs=[blk(0, D_MODEL), cache, cache, hs, hs, st],
        out_shape=[jax.ShapeDtypeStruct((ROWS, D_MODEL), BF16),
                   jax.ShapeDtypeStruct(cache_k.shape, F32), jax.ShapeDtypeStruct(cache_v.shape, F32),
                   hs_shape, hs_shape, jax.ShapeDtypeStruct(s_all.shape, F32)],
        input_output_aliases={n_in: 0, 4: 1, 5: 2, n_in + 1: 5},
        compiler_params=_cparams("arbitrary"),
        name="mix_sample",
    )(sink, h, h, h, cache_k, cache_v, h, *_s5_params(s5_prm, False), h0re, h0im,
      h, h, h, h, h, *gla_prm, state, mix, s_all)


def _mix_prompt_kernel(l, n_cast, sink_ref, bias_ref, q_ref, kp_ref, kc_ref, vp_ref, vc_ref, u_ref,
                       gq_ref, gk_ref, gv_ref, gg_ref, ga_ref,
                       bb_ref, cre_ref, cim_ref, d_ref, wg_ref, bg_ref, ab_ref, wa_ref, ba_ref, ng_ref, *rest):
    w_refs, (o_ref, hpre_ref, hpim_ref, s_ref) = rest[:n_cast], rest[n_cast:n_cast + 4]
    c_refs, (xre, xim, hre, him) = rest[n_cast + 4:2 * n_cast + 4], rest[2 * n_cast + 4:]
    s_blk = pl.program_id(0)

    @pl.when(s_blk == 0)
    def _():
        hpre_ref[...] = jnp.zeros_like(hpre_ref)
        hpim_ref[...] = jnp.zeros_like(hpim_ref)
        s_ref[...] = jnp.zeros_like(s_ref)

    for w_ref, c_ref in zip(w_refs, c_refs):
        c_ref[...] = w_ref[...].astype(BF16)
    _s5_block(u_ref, bb_ref, cre_ref, cim_ref, d_ref, wg_ref, bg_ref, ab_ref, o_ref, hpre_ref, hpim_ref,
              xre, xim, hre, him)
    _swa_block(l, sink_ref, bias_ref, q_ref, kp_ref, kc_ref, vp_ref, vc_ref, o_ref)
    _gla_block(gq_ref, gk_ref, gv_ref, gg_ref, ga_ref, wa_ref, ba_ref, ng_ref, o_ref, s_ref)


def _mix_prompt(sink, h, s5_prm, gla_prm, w_casts, mix, l):
    cur = _seq_block
    prev = lambda s: _seq_block(jnp.maximum(s - 1, 0))
    blk = lambda c, w, at=cur: pl.BlockSpec((BLK, w), lambda s: (at(s), c // w))
    const = lambda shape: pl.BlockSpec(shape, lambda s: (0,) * len(shape))
    casts = [_cast_specs(w, rows, layer) for w, rows, layer in w_casts]
    in_specs = ([pl.BlockSpec(memory_space=pltpu.SMEM),
                 pl.BlockSpec((None, BLK, 2 * BLK), lambda s: (jnp.minimum(s, 2), 0, 0)), blk(C_Q, A_WIDTH),
                 blk(C_K, KV_WIDTH, prev), blk(C_K, KV_WIDTH), blk(C_V, KV_WIDTH, prev), blk(C_V, KV_WIDTH),
                 blk(C_U, S5_WIDTH),
                 blk(C_GQ, GLA_KW), blk(C_GK, GLA_KW), blk(C_GV, GLA_WIDTH), blk(C_GG, GLA_WIDTH), blk(C_GA, 128)]
                + _s5_param_specs(l, True) + _gla_param_specs(l) + [c[0] for c in casts])
    n_in = len(in_specs)
    return pl.pallas_call(
        _drop_inputs(functools.partial(_mix_prompt_kernel, l, len(casts)), n_in, 1),
        grid=(N_PBLK,),
        in_specs=in_specs + [_any()],
        out_specs=[pl.BlockSpec((BLK, D_MODEL), lambda s: (cur(s), 0)),
                   const((S5_CHUNKS, 128)), const((S5_CHUNKS, 128)), const((GLA_KW, GLA_DV))] + [c[1] for c in casts],
        out_shape=[jax.ShapeDtypeStruct((ROWS, D_MODEL), BF16),
                   jax.ShapeDtypeStruct((S5_CHUNKS, 128), F32), jax.ShapeDtypeStruct((S5_CHUNKS, 128), F32),
                   jax.ShapeDtypeStruct((GLA_KW, GLA_DV), F32)] + [c[2] for c in casts],
        scratch_shapes=[pltpu.VMEM((S5_CHUNKS * BLK, 128), F32)] * 4,
        input_output_aliases={n_in: 0},
        compiler_params=_cparams("arbitrary"),
        name="mix_prompt",
    )(sink, _swa_bias(), *([h] * 11), *_s5_params(s5_prm, True), *gla_prm, *[w for w, _, _ in w_casts], mix)


OUT_TM = 768
OUT_PARTS = 3
LN_ROWS = 16


def _out_proj_kernel(mix_ref, w_ref, x_ref, g_ref, b_ref, o_ref):
    i = pl.program_id(0)
    g, b = g_ref[...], b_ref[...]
    part = OUT_TM // OUT_PARTS
    rows = [slice(k * part, (k + 1) * part) for k in range(OUT_PARTS)]
    def norm(k, acc):
        for c in range(part // LN_ROWS):
            r0 = k * part + c * LN_ROWS
            y = _layer_norm_rows(ALPHA * x_ref[r0:r0 + LN_ROWS, :] + acc[c * LN_ROWS:(c + 1) * LN_ROWS, :], g, b)
            o_ref[r0:r0 + LN_ROWS, :] = _zero_pad_rows(y, i * OUT_TM + r0)

    prev = None
    for k, r in enumerate(rows):
        acc = _dot(mix_ref[r, :], w_ref[...])
        if prev is not None:
            norm(k - 1, prev)
        prev = acc
    norm(OUT_PARTS - 1, prev)


def _out_proj(mix, w, x, g, b, l):
    tm = OUT_TM
    return pl.pallas_call(
        _out_proj_kernel,
        grid=(ROWS // tm,),
        in_specs=[
            pl.BlockSpec((tm, D_MODEL), lambda i: (i, 0)),
            pl.BlockSpec((D_MODEL, D_MODEL), lambda i: (0, 0), pipeline_mode=pl.Buffered(1)),
            pl.BlockSpec((tm, D_MODEL), lambda i: (i, 0)),
            _layer_spec((1, D_MODEL), l), _layer_spec((1, D_MODEL), l),
        ],
        out_specs=pl.BlockSpec((tm, D_MODEL), lambda i: (i, 0)),
        out_shape=jax.ShapeDtypeStruct((ROWS, D_MODEL), F32),
        compiler_params=_cparams("arbitrary"),
        name="out_proj",
    )(mix, w, x, g, b)


FFN_TM = 768
FFN_TF = 512
FFN_HALO = 16
FFN_NI = ROWS // FFN_TM
assert ROWS % FFN_TM == 0 and FFN_TM >= N_SAMPLE + BLK + SUB and META_ROW % FFN_HALO == 0


def _ffn_kernel(final, x_ref, xh_ref, wu_ref, wg_ref, cw_ref, cb_ref, wd_ref, p0_ref, p1_ref, g_ref, b_ref,
                o_ref, *rest):
    if final:
        ys_ref, ut_ref, us_ref, xb_ref, a_ref = rest
    else:
        ut_ref, us_ref, xb_ref, a_ref = rest
    i, f = pl.program_id(0), pl.program_id(1)
    tm, hl = FFN_TM, FFN_HALO
    last_i = FFN_NI - 1

    @pl.when(f == 0)
    def _():
        xb_ref[...] = x_ref[...].astype(BF16)
        o_ref[...] = jnp.zeros_like(o_ref)

    cw0, cw1, cw2, cb = cw_ref[0:1, :], cw_ref[1:2, :], cw_ref[2:3, :], cb_ref[...]

    def act(um2, um1, u0, g0):
        return (jax.nn.gelu(cb + cw0 * um2 + cw1 * um1 + cw2 * u0) * g0).astype(BF16)

    xb = xb_ref[...]
    u = _dot(xb, wu_ref[...])
    gte = _dot(xb, wg_ref[...])
    um2, um1 = pltpu.roll(u, 2, 0), pltpu.roll(u, 1, 0)
    body = slice(hl, tm - N_SAMPLE)
    a_ref[body, :] = act(um2[body, :], um1[body, :], u[body, :], gte[body, :])
    ext = jnp.concatenate([_dot(xh_ref[...].astype(BF16), wu_ref[...]), u[0:hl, :]], axis=0)
    a_ref[0:hl, :] = act(ext[hl - 2:2 * hl - 2, :], ext[hl - 1:2 * hl - 1, :], u[0:hl, :], gte[0:hl, :])
    tail = slice(tm - N_SAMPLE, tm)
    samp = i == last_i
    a_ref[tail, :] = act(jnp.where(samp, p0_ref[...], um2[tail, :]), jnp.where(samp, p1_ref[...], um1[tail, :]),
                         u[tail, :], gte[tail, :])
    o_ref[...] += _dot(a_ref[...], wd_ref[...])
    ut_ref[...] = u[tm - N_SAMPLE - BLK - SUB:tm - N_SAMPLE - BLK, :]
    us_ref[0] = p1_ref[...]
    us_ref[1] = u[tm - N_SAMPLE:, :]

    @pl.when(f == pl.num_programs(1) - 1)
    def _():
        _deepnorm_ln(o_ref, x_ref, g_ref, b_ref, o_ref, i * tm, tm)
        if final:
            ys_ref[...] = o_ref[tm - N_SAMPLE:, :]


def _ffn(x, wup, cw, cb, wd, conv_p0, conv_p1, g, b, l, final):
    tm, tf = FFN_TM, FFN_TF
    nf = D_FF // tf
    tail = lambda i, f: jnp.where(i == FFN_NI - 1, f, 0)
    halo = lambda i, f: (jnp.where(i == 0, ROWS_P // FFN_HALO, i * (tm // FFN_HALO)) - 1, 0)
    y_spec = pl.BlockSpec((tm, D_MODEL), lambda i, f: (i, 0))
    y_specs = [y_spec, pl.BlockSpec((N_SAMPLE, D_MODEL), lambda i, f: (0, 0))] if final else [y_spec]
    y_shapes = ([jax.ShapeDtypeStruct((SEQ, D_MODEL), F32), jax.ShapeDtypeStruct((N_SAMPLE, D_MODEL), F32)]
                if final else [jax.ShapeDtypeStruct((ROWS, D_MODEL), F32)])
    return pl.pallas_call(
        functools.partial(_ffn_kernel, final),
        grid=(FFN_NI, nf),
        in_specs=[
            pl.BlockSpec((tm, D_MODEL), lambda i, f: (i, 0)),
            pl.BlockSpec((FFN_HALO, D_MODEL), halo),
            pl.BlockSpec((D_MODEL, tf), lambda i, f: (0, f)),
            pl.BlockSpec((D_MODEL, tf), lambda i, f: (0, nf + f)),
            pl.BlockSpec((None, CONV_W, tf), lambda i, f: (l, 0, f)),
            pl.BlockSpec((None, 1, tf), lambda i, f: (l, 0, f)),
            pl.BlockSpec((tf, D_MODEL), lambda i, f: (f, 0)),
            pl.BlockSpec((None, N_SAMPLE, tf), lambda i, f: (l, 0, tail(i, f))),
            pl.BlockSpec((None, N_SAMPLE, tf), lambda i, f: (l, 0, tail(i, f))),
            _layer_spec((1, D_MODEL), l), _layer_spec((1, D_MODEL), l),
        ],
        out_specs=y_specs + [
            pl.BlockSpec((SUB, tf), lambda i, f: (0, tail(i, f))),
            pl.BlockSpec((2, N_SAMPLE, tf), lambda i, f: (0, 0, tail(i, f))),
        ],
        out_shape=y_shapes + [
            jax.ShapeDtypeStruct((SUB, D_FF), F32),
            jax.ShapeDtypeStruct((2, N_SAMPLE, D_FF), F32),
        ],
        scratch_shapes=[pltpu.VMEM((tm, D_MODEL), BF16), pltpu.VMEM((tm, tf), BF16)],
        compiler_params=_cparams("arbitrary", "arbitrary"),
        name="conv_ffn",
    )(x, x, wup, wup, cw, cb, wd, conv_p0, conv_p1, g, b)


def _rope_table():
    inv = jnp.power(ROPE_THETA, -jnp.arange(ROPE_HALF, dtype=F32) / ROPE_HALF)
    pos = jnp.concatenate([jnp.arange(SEQ, dtype=jnp.int32) + N_META, jnp.zeros((LEAD,), jnp.int32),
                           jnp.arange(N_META, dtype=jnp.int32), jnp.full((N_SAMPLE,), PAST_LEN, jnp.int32)])
    ang = pos.astype(F32)[:, None] * inv[None, :]
    return jnp.concatenate([jnp.cos(ang), jnp.sin(ang)], axis=1)


def _s5_tables(lam_re, lam_im, log_dt, b_re, b_im, c_re, c_im, d, w_glu, b_glu):
    L = lam_re.shape[0]
    dt = jnp.exp(log_dt)[..., None]
    mag = jnp.exp(lam_re * dt)
    ab_re, ab_im = mag * jnp.cos(lam_im * dt), mag * jnp.sin(lam_im * dt)
    den = lam_re * lam_re + lam_im * lam_im
    nr, ni = ab_re - 1.0, ab_im
    f_re = (nr * lam_re + ni * lam_im) / den
    f_im = (ni * lam_re - nr * lam_im) / den
    bb_re = f_re[..., None] * b_re - f_im[..., None] * b_im
    bb_im = f_re[..., None] * b_im + f_im[..., None] * b_re
    gi, go = S5_GROUPS // S5_IN_PIECES, S5_GROUPS // S5_OUT_PIECES

    def bd_in(t):
        t = t.reshape(L, S5_IN_PIECES, gi, S5_STATE, S5_CH)
        return jnp.einsum('lqgph,gk->lqghkp', t, jnp.eye(gi, dtype=F32)).reshape(L, S5_IN_PIECES, S5_IN_K, S5_IN_N)

    def bd_out(t):
        t = t.reshape(L, S5_OUT_PIECES, go, S5_CH, S5_STATE)
        return jnp.einsum('lqghp,gk->lqgpkh', t, jnp.eye(go, dtype=F32)).reshape(
            L, S5_OUT_PIECES, S5_OUT_K, S5_OUT_N)

    ar, ai = ab_re.reshape(L, 1, S5_LANES), ab_im.reshape(L, 1, S5_LANES)
    pows = [(ar, ai)]
    for _ in range(SUB - 1):
        pr, pi = pows[-1]
        pows.append((pr * ar - pi * ai, pr * ai + pi * ar))
    row = jnp.arange(SUB)[None, :, None]
    parts = []
    for sft in S5_SHIFTS:
        pr, pi = pows[sft - 1]
        parts += [jnp.where(row >= sft, pr, 0.0), jnp.where(row >= sft, pi, 0.0)]
    parts += [jnp.concatenate([p[0] for p in pows], 1), jnp.concatenate([p[1] for p in pows], 1)]
    return dict(bb=jnp.concatenate([bd_in(bb_re), bd_in(bb_im)], axis=3).astype(BF16),
                c_re=bd_out(c_re).astype(BF16), c_im=bd_out(c_im).astype(BF16),
                d=d.reshape(L, 1, S5_WIDTH), w_glu=w_glu.astype(BF16), b_glu=b_glu.reshape(L, 1, S5_WIDTH),
                tab=jnp.concatenate(parts, 1),
                ab=jnp.concatenate([ab_re.reshape(L, S5_CHUNKS, 128), ab_im.reshape(L, S5_CHUNKS, 128)], axis=1))


@jax.jit
def kernel(x_prompt, x_sample, cache_swa_k, cache_swa_v, state_ssm_re, state_ssm_im, state_gla, state_conv,
           meta_tokens, ln_in_g, ln_in_b, w_in, attn_sink, s5_lam_re, s5_lam_im, s5_log_dt, s5_b_re, s5_b_im,
           s5_c_re, s5_c_im, s5_d, s5_w_glu, s5_b_glu, gla_w_a2, gla_b_a, gla_norm_g, w_out, ln1_g, ln1_b,
           ffn_w_up, ffn_conv_w, ffn_conv_b, ffn_w_down, ln2_g, ln2_b):
    L = DEPTH
    row3 = lambda t: t.reshape(L, 1, -1)
    wa_b = jnp.pad(gla_w_a2, ((0, 0), (0, 128 - GLA_LOWRANK), (0, 0))).astype(BF16)
    gla_args = (wa_b, row3(gla_b_a), row3(gla_norm_g))
    ln1, ln2 = (row3(ln1_g), row3(ln1_b)), (row3(ln2_g), row3(ln2_b))
    conv_b3 = row3(ffn_conv_b)
    cs = _rope_table()
    s5p = _s5_tables(s5_lam_re, s5_lam_im, s5_log_dt, s5_b_re, s5_b_im, s5_c_re, s5_c_im, s5_d, s5_w_glu, s5_b_glu)
    ck = cache_swa_k.reshape(L, N_SAMPLE, WINDOW, KV_WIDTH)
    cv = cache_swa_v.reshape(L, N_SAMPLE, WINDOW, KV_WIDTH)
    h0re = state_ssm_re.reshape(L, N_SAMPLE, S5_LANES)
    h0im = state_ssm_im.reshape(L, N_SAMPLE, S5_LANES)
    sg = state_gla.reshape(L, N_SAMPLE, GLA_KW, GLA_DV)
    conv_p0, conv_p1 = state_conv[:, :, 0, :], state_conv[:, :, 1, :]
    mix = jnp.zeros((ROWS, D_MODEL), BF16)
    s_all = jnp.zeros((L, N_SAMPLE, GLA_KW, GLA_DV), F32)
    w_in_b = w_in.astype(BF16)

    x = _ln_in(x_prompt.reshape(SEQ, D_MODEL), meta_tokens, x_sample.reshape(N_SAMPLE, D_MODEL),
               ln_in_g.reshape(1, D_MODEL), ln_in_b.reshape(1, D_MODEL))
    outs = [[] for _ in range(9)]
    for l in range(L):
        h = _in_proj(x, w_in_b, cs, l)
        casts = [(ffn_w_up, 32, l), (ffn_w_down, 176, l), (w_out, 32, l)]
        mix, hp_re, hp_im, s_p, w_up_b, w_down_b, w_out_b = _mix_prompt(
            attn_sink, h, s5p, gla_args, casts, mix, l)
        mix, ck, cv, hs_re, hs_im, s_all = _mix_sample(
            attn_sink, h, ck, cv, s5p, h0re, h0im, gla_args, sg, mix, s_all, l)
        x = _out_proj(mix, w_out_b, x, *ln1, l)
        res = _ffn(x, w_up_b, ffn_conv_w, conv_b3, w_down_b, conv_p0, conv_p1, *ln2, l, l == L - 1)
        if l == L - 1:
            y_prompt, y_sample, u_tail, u_s = res
        else:
            x, u_tail, u_s = res
        kv_p = lambda c: h[SEQ - WINDOW:SEQ, c:c + KV_WIDTH].reshape(1, WINDOW, A_KV_HEADS, HEAD_DIM)
        new = (kv_p(C_K), kv_p(C_V),
               hp_re.reshape(1, S5_GROUPS, S5_STATE), hp_im.reshape(1, S5_GROUPS, S5_STATE),
               s_p.reshape(1, GLA_HEADS, GLA_DK, GLA_DV), u_tail[SUB - 2:SUB][None],
               hs_re.reshape(N_SAMPLE, S5_GROUPS, S5_STATE), hs_im.reshape(N_SAMPLE, S5_GROUPS, S5_STATE),
               jnp.transpose(u_s, (1, 0, 2)))
        for lst, val in zip(outs, new):
            lst.append(val)
    st = [jnp.stack(lst) for lst in outs]
    return (y_prompt.reshape(1, SEQ, D_MODEL), y_sample.reshape(N_SAMPLE, 1, D_MODEL),
            st[0], st[1], st[2], st[3], st[4], st[5],
            ck.reshape(L, N_SAMPLE, WINDOW, A_KV_HEADS, HEAD_DIM),
            cv.reshape(L, N_SAMPLE, WINDOW, A_KV_HEADS, HEAD_DIM),
            st[6], st[7], s_all.reshape(L, N_SAMPLE, GLA_HEADS, GLA_DK, GLA_DV), st[8])
```

```python
import functools

import jax
import jax.numpy as jnp
from jax import lax
from jax.experimental import pallas as pl
from jax.experimental.pallas import tpu as pltpu

F32 = jnp.float32
BF16 = jnp.bfloat16

D_MODEL = 2048
SEQ = 8192
DEPTH = 2
N_SAMPLE = 128
N_META = 16
HEAD_DIM = 64
A_WIDTH = 1024
A_HEADS = 16
A_KV_HEADS = 4
GQA = 4
KV_WIDTH = A_KV_HEADS * HEAD_DIM
WINDOW = 128
ROPE_DIM = 16
ROPE_HALF = ROPE_DIM // 2
ROPE_THETA = 500000.0
PAST_LEN = 8192
S5_WIDTH = 512
S5_CH = 16
S5_GROUPS = 32
S5_STATE = 64
S5_LANES = S5_GROUPS * S5_STATE
GLA_WIDTH = 512
GLA_HEADS = 4
GLA_DV = 128
GLA_DK = 64
GLA_KW = GLA_HEADS * GLA_DK
GLA_LOWRANK = 16
GLA_TAU = 16.0
GLA_CHUNK = 64
D_FF = 5632
CONV_W = 3
LN_EPS = 1e-5
ALPHA = (2 * DEPTH) ** 0.25
NEG_INF = -1e30

BLK = 128
T_PROMPT = N_META + SEQ
LEAD = (-T_PROMPT) % BLK
N_PBLK = (LEAD + T_PROMPT) // BLK
ROWS_P = N_PBLK * BLK
ROWS = ROWS_P + N_SAMPLE
META_ROW = SEQ + LEAD
SUB = 8

C_Q, C_K, C_V, C_U, C_GQ, C_GK, C_GV, C_GG, C_GA = 0, 1024, 1280, 1536, 2048, 2304, 2560, 3072, 3584
IN_COLS = 3600
IN_TILE = 1280
IN_PAD = 3 * IN_TILE
ROPE_COLS = A_WIDTH + KV_WIDTH
M_A, M_B, M_C = 0, A_WIDTH, A_WIDTH + S5_WIDTH

VMEM_LIMIT = 56 * 1024 * 1024


def _cparams(*sem):
    return pltpu.CompilerParams(dimension_semantics=sem, vmem_limit_bytes=VMEM_LIMIT)


def _any():
    return pl.BlockSpec(memory_space=pl.ANY)


def _drop_inputs(kern, start, count):
    def wrapped(*refs):
        return kern(*refs[:start], *refs[start + count:])
    return wrapped


def _seq_block(s):
    return jnp.where(s == 0, N_PBLK - 1, s - 1)


def _layer_spec(shape, l):
    return pl.BlockSpec((None,) + shape, lambda *_: (l,) + (0,) * len(shape))


def _cast_specs(w, rows, l):
    n, cols = w.shape[1] // rows, w.shape[2]
    chunk = lambda s: jnp.minimum(s, n - 1)
    return (pl.BlockSpec((None, rows, cols), lambda s: (l, chunk(s), 0)),
            pl.BlockSpec((rows, cols), lambda s: (chunk(s), 0)),
            jax.ShapeDtypeStruct(w.shape[1:], BF16))


def _layer_norm_rows(x, g, b):
    mu = jnp.mean(x, -1, keepdims=True)
    xc = x - mu
    var = jnp.mean(xc * xc, -1, keepdims=True)
    return xc * lax.rsqrt(var + LN_EPS) * g + b


def _zero_pad_rows(y, first_row):
    row = first_row + lax.broadcasted_iota(jnp.int32, y.shape, 0)
    return jnp.where((row >= SEQ) & (row < META_ROW), 0.0, y)


def _deepnorm_ln(acc_ref, x_ref, g_ref, b_ref, o_ref, first_row, rows):
    g, b = g_ref[...], b_ref[...]
    for c in range(rows // BLK):
        r = slice(c * BLK, (c + 1) * BLK)
        y = _layer_norm_rows(ALPHA * x_ref[r, :] + acc_ref[r, :], g, b)
        o_ref[r, :] = _zero_pad_rows(y, first_row + c * BLK)


def _dot(a, b):
    return jnp.dot(a, b, preferred_element_type=F32)


def _dot_nt(a, b):
    return lax.dot_general(a, b, (((1,), (1,)), ((), ())), preferred_element_type=F32)


def _dot_tn(a, b):
    return lax.dot_general(a, b, (((0,), (0,)), ((), ())), preferred_element_type=F32)


def _log_sigmoid(x):
    return jnp.minimum(x, 0.0) - jnp.log(1.0 + jnp.exp(-jnp.abs(x)))


LN_IN_TM = 512
LN_IN_STEPS = SEQ // LN_IN_TM + 1
assert SEQ % LN_IN_TM == 0 and ROWS - SEQ <= LN_IN_TM


def _ln_in_kernel(xp_ref, meta_ref, xs_ref, g_ref, b_ref, o_ref):
    i = pl.program_id(0)
    g, b = g_ref[...], b_ref[...]

    @pl.when(i < LN_IN_STEPS - 1)
    def _():
        for c in range(LN_IN_TM // BLK):
            r = slice(c * BLK, (c + 1) * BLK)
            o_ref[r, :] = _layer_norm_rows(xp_ref[r, :], g, b)

    @pl.when(i == LN_IN_STEPS - 1)
    def _():
        o_ref[0:LEAD, :] = jnp.zeros((LEAD, D_MODEL), F32)
        o_ref[LEAD:BLK, :] = _layer_norm_rows(meta_ref[...], g, b)
        o_ref[BLK:BLK + N_SAMPLE, :] = _layer_norm_rows(xs_ref[...], g, b)


def _ln_in(xp, meta, xs, g, b):
    tm = LN_IN_TM
    const = lambda shape: pl.BlockSpec(shape, lambda i: (0,) * len(shape))
    return pl.pallas_call(
        _ln_in_kernel,
        grid=(LN_IN_STEPS,),
        in_specs=[
            pl.BlockSpec((tm, D_MODEL), lambda i: (jnp.minimum(i, SEQ // tm - 1), 0)),
            const((N_META, D_MODEL)), const((N_SAMPLE, D_MODEL)), const((1, D_MODEL)), const((1, D_MODEL)),
        ],
        out_specs=pl.BlockSpec((tm, D_MODEL), lambda i: (i, 0)),
        out_shape=jax.ShapeDtypeStruct((ROWS, D_MODEL), F32),
        compiler_params=_cparams("arbitrary"),
        name="ln_in",
    )(xp, meta, xs, g, b)


IN_TM = 1056


def _in_proj_kernel(x_ref, w_ref, cs_ref, o_ref, xb_ref):
    j = pl.program_id(1)

    last_j = IN_PAD // IN_TILE - 1

    @pl.when(j == 0)
    def _():
        xb = x_ref[...].astype(BF16)
        xb_ref[...] = xb
        c8, s8 = cs_ref[:, 0:ROPE_HALF], cs_ref[:, ROPE_HALF:ROPE_DIM]
        rest = HEAD_DIM - ROPE_DIM
        one, zero = jnp.ones((IN_TM, rest), F32), jnp.zeros((IN_TM, rest), F32)
        z8 = jnp.zeros((IN_TM, ROPE_HALF), F32)
        cos = jnp.concatenate([c8, c8, one] * 2, axis=1)
        sa = jnp.concatenate([-s8, z8, zero] * 2, axis=1)
        sb = jnp.concatenate([z8, s8, zero] * 2, axis=1)
        for c in range(ROPE_COLS // 256):
            acc = _dot_nt(xb, w_ref[c * 256:(c + 1) * 256, :])
            for hlf in range(2):
                blk = acc[:, hlf * 128:(hlf + 1) * 128]
                o_ref[:, c * 256 + hlf * 128:c * 256 + (hlf + 1) * 128] = (
                    blk * cos + pltpu.roll(blk, 128 - ROPE_HALF, 1) * sa + pltpu.roll(blk, ROPE_HALF, 1) * sb)

    @pl.when((j > 0) & (j < last_j))
    def _():
        o_ref[...] = _dot_nt(xb_ref[...], w_ref[...])

    @pl.when(j == last_j)
    def _():
        o_ref[...] = _dot_nt(xb_ref[...], w_ref[...])
        edge = (IN_COLS % IN_TILE) // 128 * 128
        col = edge + lax.broadcasted_iota(jnp.int32, (IN_TM, 128), 1)
        o_ref[:, edge:edge + 128] = jnp.where(col < IN_COLS % IN_TILE, o_ref[:, edge:edge + 128], 0.0)
        o_ref[:, edge + 128:] = jnp.zeros((IN_TM, IN_TILE - edge - 128), F32)


def _in_proj(x, w, cs, l):
    return pl.pallas_call(
        _in_proj_kernel,
        grid=(ROWS // IN_TM, IN_PAD // IN_TILE),
        in_specs=[
            pl.BlockSpec((IN_TM, D_MODEL), lambda i, j: (i, 0)),
            pl.BlockSpec((None, IN_TILE, D_MODEL), lambda i, j: (l, j, 0)),
            pl.BlockSpec((IN_TM, ROPE_DIM), lambda i, j: (i, 0)),
        ],
        out_specs=pl.BlockSpec((IN_TM, IN_TILE), lambda i, j: (i, j)),
        out_shape=jax.ShapeDtypeStruct((ROWS, IN_PAD), F32),
        scratch_shapes=[pltpu.VMEM((IN_TM, D_MODEL), BF16)],
        compiler_params=_cparams("arbitrary", "arbitrary"),
        name="in_proj",
    )(x, w, cs)


def _swa_bias():
    row = lax.broadcasted_iota(jnp.int32, (3, BLK, 2 * BLK), 1)
    col = lax.broadcasted_iota(jnp.int32, (3, BLK, 2 * BLK), 2)
    s_blk = lax.broadcasted_iota(jnp.int32, (3, BLK, 2 * BLK), 0)
    diff = row - col + BLK
    kpos = (s_blk - 1) * BLK + col - LEAD
    ok = (diff >= 0) & (diff <= WINDOW) & (kpos >= 0)
    return jnp.where(ok, 0.0, NEG_INF).astype(F32)


def _swa_block(l, sink_ref, bias_ref, q_ref, kp_ref, kc_ref, vp_ref, vc_ref, o_ref):
    k2 = jnp.concatenate([kp_ref[...], kc_ref[...]], axis=0).astype(BF16)
    v2 = jnp.concatenate([vp_ref[...], vc_ref[...]], axis=0).astype(BF16)
    nq = GQA * BLK
    bias = jnp.concatenate([bias_ref[...]] * GQA, axis=0)
    low = lax.broadcasted_iota(jnp.int32, (BLK, 2 * HEAD_DIM), 1) < HEAD_DIM
    g_of_row = lax.broadcasted_iota(jnp.int32, (nq, 1), 0) // BLK
    ones = jnp.ones((2 * BLK, 2 * HEAD_DIM), BF16)
    scores, sinks = [], []
    for kv in range(A_KV_HEADS):
        kh = k2[:, kv * HEAD_DIM:(kv + 1) * HEAD_DIM]
        kk = jnp.concatenate([kh, kh], axis=1)
        parts = []
        for pair in range(GQA // 2):
            c0 = (kv * GQA + 2 * pair) * HEAD_DIM
            qp = q_ref[:, c0:c0 + 2 * HEAD_DIM] * HEAD_DIM ** -0.5
            parts += [jnp.where(low, qp, 0.0).astype(BF16), jnp.where(low, 0.0, qp).astype(BF16)]
        q4 = jnp.concatenate(parts, axis=0)
        sk = jnp.zeros((nq, 1), F32)
        for g in range(GQA):
            sk = jnp.where(g_of_row == g, sink_ref[l, kv * GQA + g], sk)
        scores.append(_dot_nt(q4, kk) + bias)
        sinks.append(sk)
    probs, sink_terms = [], []
    for s, sk in zip(scores, sinks):
        m = jnp.maximum(jnp.max(s, -1, keepdims=True), sk)
        probs.append(jnp.exp(s - m).astype(BF16))
        sink_terms.append(jnp.exp(sk - m))
    for kv in range(A_KV_HEADS):
        vh = v2[:, kv * HEAD_DIM:(kv + 1) * HEAD_DIM]
        v3 = jnp.concatenate([vh, vh, ones], axis=1)
        oa = _dot(probs[kv], v3)
        on = oa[:, 0:2 * HEAD_DIM] / (oa[:, 2 * HEAD_DIM:] + sink_terms[kv])
        for pair in range(GQA // 2):
            c0 = (kv * GQA + 2 * pair) * HEAD_DIM
            lo, hi = on[2 * pair * BLK:(2 * pair + 1) * BLK], on[(2 * pair + 1) * BLK:(2 * pair + 2) * BLK]
            o_ref[:, M_A + c0:M_A + c0 + 2 * HEAD_DIM] = jnp.where(low, lo, hi).astype(BF16)


SWA_BT = 16


def _swa_sample_kernel(l, sink_ref, q_ref, kn_ref, vn_ref, ck_ref, cv_ref, o_ref, cko_ref, cvo_ref):
    bt = SWA_BT
    kn, vn = kn_ref[...], vn_ref[...]

    def cols(x):
        return jnp.transpose(jnp.concatenate([x, jnp.zeros((128 - bt, KV_WIDTH), F32)], axis=0))

    kn_t, vn_t = cols(kn), cols(vn)
    last = lax.broadcasted_iota(jnp.int32, (HEAD_DIM, WINDOW), 1) == WINDOW - 1
    for j in range(bt):
        for kv in range(A_KV_HEADS):
            hd = slice(kv * HEAD_DIM, (kv + 1) * HEAD_DIM)
            cko_ref[j, kv] = jnp.where(last, kn_t[hd, j:j + 1], pltpu.roll(ck_ref[j, kv], WINDOW - 1, 1))
            cvo_ref[j, kv] = jnp.where(last, vn_t[hd, j:j + 1], pltpu.roll(cv_ref[j, kv], WINDOW - 1, 1))
    nq = GQA * bt
    rowb = lax.broadcasted_iota(jnp.int32, (nq, bt * WINDOW), 0) % bt
    colb = lax.broadcasted_iota(jnp.int32, (nq, bt * WINDOW), 1) // WINDOW
    same = rowb == colb
    rg = lax.broadcasted_iota(jnp.int32, (nq, 1), 0) // bt
    for kv in range(A_KV_HEADS):
        sl = slice(kv * HEAD_DIM, (kv + 1) * HEAD_DIM)
        q4 = jnp.concatenate(
            [q_ref[:, (kv * GQA + g) * HEAD_DIM:(kv * GQA + g + 1) * HEAD_DIM] for g in range(GQA)], axis=0)
        q4 = q4 * HEAD_DIM ** -0.5
        kn4 = jnp.concatenate([kn[:, sl]] * GQA, axis=0)
        vn4 = jnp.concatenate([vn[:, sl]] * GQA, axis=0)
        sk = jnp.zeros((nq, 1), F32)
        for g in range(GQA):
            sk = jnp.where(rg == g, sink_ref[l, kv * GQA + g], sk)
        kt = jnp.concatenate([ck_ref[j, kv] for j in range(bt)], axis=1).astype(BF16)
        vt = jnp.concatenate([cv_ref[j, kv] for j in range(bt)], axis=1).astype(BF16)
        s = jnp.where(same, _dot(q4.astype(BF16), kt), NEG_INF)
        s_self = jnp.sum(q4 * kn4, -1, keepdims=True)
        m = jnp.maximum(jnp.maximum(jnp.max(s, -1, keepdims=True), s_self), sk)
        p = jnp.exp(s - m)
        p_self = jnp.exp(s_self - m)
        den = jnp.sum(p, -1, keepdims=True) + p_self + jnp.exp(sk - m)
        o = (_dot_nt(p.astype(BF16), vt) + p_self * vn4) / den
        for g in range(GQA):
            hh = kv * GQA + g
            o_ref[:, hh * HEAD_DIM:(hh + 1) * HEAD_DIM] = o[g * bt:(g + 1) * bt].astype(BF16)


S5_SHIFTS = (1, 2, 4)
S5_NTAB = 2 * (len(S5_SHIFTS) + 1) * SUB
S5_CHUNKS = S5_LANES // 128
S5_IN_PIECES = 4
S5_IN_K = S5_WIDTH // S5_IN_PIECES
S5_IN_N = S5_LANES // S5_IN_PIECES
S5_OUT_PIECES = 2
S5_OUT_K = S5_LANES // S5_OUT_PIECES
S5_OUT_N = S5_WIDTH // S5_OUT_PIECES


def _s5_tab(tab_ref, k, lanes):
    return (tab_ref[2 * k * SUB:(2 * k + 1) * SUB, lanes], tab_ref[(2 * k + 1) * SUB:(2 * k + 2) * SUB, lanes])


def _s5_drive(ub, bb_ref, p):
    return _dot(ub[:, p * S5_IN_K:(p + 1) * S5_IN_K], bb_ref[p])


def _s5_readout(h_piece, u, cre_ref, cim_ref, d_ref, wg_ref, bg_ref):
    ys = []
    for j in range(S5_OUT_PIECES):
        hr, hi = h_piece(j)
        ys.append(_dot(hr.astype(BF16), cre_ref[j]) - _dot(hi.astype(BF16), cim_ref[j]))
    y = jnp.concatenate(ys, axis=1) + d_ref[...] * u
    z = jax.nn.gelu(y)
    gate = jax.nn.sigmoid(_dot(z.astype(BF16), wg_ref[...]) + bg_ref[...])
    return (z * gate).astype(BF16)


def _s5_block(u_ref, bb_ref, cre_ref, cim_ref, d_ref, wg_ref, bg_ref, ab_ref, o_ref, hpre_ref, hpim_ref,
              xre, xim, hre, him):
    u = u_ref[...]
    ub = u.astype(BF16)
    drive = [_s5_drive(ub, bb_ref, p) for p in range(S5_IN_PIECES)]
    x3r = jnp.concatenate([d[:, 0:S5_IN_N] for d in drive], axis=1).reshape(BLK, S5_CHUNKS, 128)
    x3i = jnp.concatenate([d[:, S5_IN_N:] for d in drive], axis=1).reshape(BLK, S5_CHUNKS, 128)
    ar, ai = ab_ref[0:S5_CHUNKS, :], ab_ref[S5_CHUNKS:, :]
    hr, hi = hpre_ref[...], hpim_ref[...]
    hs_r, hs_i = [], []
    for t in range(BLK):
        hr, hi = ar * hr - ai * hi + x3r[t], ar * hi + ai * hr + x3i[t]
        hs_r.append(hr)
        hs_i.append(hi)
    hpre_ref[...] = hr
    hpim_ref[...] = hi
    h2r = jnp.stack(hs_r, axis=0).reshape(BLK, S5_LANES)
    h2i = jnp.stack(hs_i, axis=0).reshape(BLK, S5_LANES)

    def piece(j):
        return h2r[:, j * S5_OUT_K:(j + 1) * S5_OUT_K], h2i[:, j * S5_OUT_K:(j + 1) * S5_OUT_K]

    o_ref[:, M_B:M_B + S5_WIDTH] = _s5_readout(piece, u, cre_ref, cim_ref, d_ref, wg_ref, bg_ref)


def _s5_sample_kernel(u_ref, bb_ref, cre_ref, cim_ref, d_ref, wg_ref, bg_ref, tab_ref, h0re_ref, h0im_ref,
                      o_ref, hsre_ref, hsim_ref):
    u = u_ref[...]
    ub = u.astype(BF16)
    for p in range(S5_IN_PIECES):
        lanes = slice(p * S5_IN_N, (p + 1) * S5_IN_N)
        bu = _s5_drive(ub, bb_ref, p)
        pr, pi = _s5_tab(tab_ref, len(S5_SHIFTS), lanes)
        ar, ai = pr[0:1, :], pi[0:1, :]
        h0r, h0i = h0re_ref[:, lanes], h0im_ref[:, lanes]
        hsre_ref[:, lanes] = ar * h0r - ai * h0i + bu[:, 0:S5_IN_N]
        hsim_ref[:, lanes] = ar * h0i + ai * h0r + bu[:, S5_IN_N:]
    piece = lambda j: (hsre_ref[:, j * S5_OUT_K:(j + 1) * S5_OUT_K], hsim_ref[:, j * S5_OUT_K:(j + 1) * S5_OUT_K])
    o_ref[:, M_B:M_B + S5_WIDTH] = _s5_readout(piece, u, cre_ref, cim_ref, d_ref, wg_ref, bg_ref)


def _s5_param_specs(l, prompt):
    return [
        _layer_spec((S5_IN_PIECES, S5_IN_K, 2 * S5_IN_N), l),
        _layer_spec((S5_OUT_PIECES, S5_OUT_K, S5_OUT_N), l), _layer_spec((S5_OUT_PIECES, S5_OUT_K, S5_OUT_N), l),
        _layer_spec((1, S5_WIDTH), l), _layer_spec((S5_WIDTH, S5_WIDTH), l), _layer_spec((1, S5_WIDTH), l),
        _layer_spec((2 * S5_CHUNKS, 128), l) if prompt else _layer_spec((S5_NTAB, S5_LANES), l),
    ]


def _s5_params(prm, prompt):
    return (prm["bb"], prm["c_re"], prm["c_im"], prm["d"], prm["w_glu"], prm["b_glu"],
            prm["ab"] if prompt else prm["tab"])


def _gla_log_gate(ac, wa_ref, ba_ref):
    return _log_sigmoid(_dot(ac.astype(BF16), wa_ref[...]) + ba_ref[...]) * (1.0 / GLA_TAU)


def _gla_finish(o_heads, gate, ng):
    outs = [o * lax.rsqrt(jnp.mean(o * o, -1, keepdims=True) + LN_EPS) for o in o_heads]
    return jnp.concatenate(outs, axis=1) * ng * (gate * jax.nn.sigmoid(gate))


def _gla_block(q_ref, k_ref, v_ref, gate_ref, ac_ref, wa_ref, ba_ref, ng_ref, o_ref, s_ref):
    lg = _gla_log_gate(ac_ref[...], wa_ref, ba_ref)
    L = GLA_CHUNK
    row = lax.broadcasted_iota(jnp.int32, (L, GLA_KW), 0)
    causal = lax.broadcasted_iota(jnp.int32, (L, L), 0) >= lax.broadcasted_iota(jnp.int32, (L, L), 1)
    chunks = range(BLK // L)
    heads = range(GLA_HEADS)
    ks = lambda hh: slice(hh * GLA_DK, (hh + 1) * GLA_DK)
    vs = lambda hh: slice(hh * GLA_DV, (hh + 1) * GLA_DV)
    q_in, k_out, k_dec, dec, vb = [], [], [], [], []
    for c in chunks:
        rows = slice(c * L, (c + 1) * L)
        b = lg[rows, :]
        d = 1
        while d < L:
            b = b + jnp.where(row >= d, pltpu.roll(b, d, 0), 0.0)
            d *= 2
        b_end = b[L - 1:L, :]
        kc = k_ref[rows, :]
        q_in.append((q_ref[rows, :] * GLA_DK ** -0.5 * jnp.exp(b)).astype(BF16))
        k_out.append((kc * jnp.exp(-b)).astype(BF16))
        k_dec.append((kc * jnp.exp(b_end - b)).astype(BF16))
        dec.append(jnp.transpose(jnp.broadcast_to(jnp.exp(b_end), (GLA_DV, GLA_KW))))
        vb.append(v_ref[rows, :].astype(BF16))
    att = [[jnp.where(causal, _dot_nt(q_in[c][:, ks(hh)], k_out[c][:, ks(hh)]), 0.0).astype(BF16) for hh in heads]
           for c in chunks]
    kv = [[_dot_tn(k_dec[c][:, ks(hh)], vb[c][:, vs(hh)]) for hh in heads] for c in chunks]
    state = [s_ref[...]]
    for c in chunks:
        state.append(dec[c] * state[c] + jnp.concatenate(kv[c], axis=0))
    s_ref[...] = state[-1]
    for c in chunks:
        sb = state[c].astype(BF16)
        o_heads = [_dot(q_in[c][:, ks(hh)], sb[ks(hh), :]) + _dot(att[c][hh], vb[c][:, vs(hh)]) for hh in heads]
        o_ref[c * L:(c + 1) * L, M_C:M_C + GLA_WIDTH] = _gla_finish(
            o_heads, gate_ref[c * L:(c + 1) * L, :], ng_ref[...]).astype(BF16)


def _gla_param_specs(l):
    return [_layer_spec((128, GLA_KW), l), _layer_spec((1, GLA_KW), l), _layer_spec((1, GLA_WIDTH), l)]


GLA_BT = 16


def _gla_sample_kernel(q_ref, k_ref, v_ref, gate_ref, ac_ref, wa_ref, ba_ref, ng_ref, s_ref, o_ref, so_ref):
    bt = GLA_BT
    lg = _gla_log_gate(ac_ref[...], wa_ref, ba_ref)
    v = v_ref[...]

    def cols(x):
        return jnp.transpose(jnp.concatenate([x, jnp.zeros((128 - bt, GLA_KW), F32)], axis=0))

    eg_t, k_t, q_t = cols(jnp.exp(lg)), cols(k_ref[...]), cols(q_ref[...] * GLA_DK ** -0.5)
    o_rows = []
    for j in range(bt):
        bc = lambda t: jnp.broadcast_to(t[:, j:j + 1], (GLA_KW, GLA_DV))
        vrow = jnp.concatenate(
            [jnp.broadcast_to(v[j:j + 1, hh * GLA_DV:(hh + 1) * GLA_DV], (GLA_DK, GLA_DV)) for hh in range(GLA_HEADS)],
            axis=0)
        s_new = bc(eg_t) * s_ref[j] + bc(k_t) * vrow
        so_ref[j] = s_new
        qs = bc(q_t) * s_new
        o_rows.append(jnp.concatenate(
            [jnp.sum(qs[hh * GLA_DK:(hh + 1) * GLA_DK, :], axis=0, keepdims=True) for hh in range(GLA_HEADS)], axis=1))
    o = jnp.concatenate(o_rows, axis=0)
    o_heads = [o[:, hh * GLA_DV:(hh + 1) * GLA_DV] for hh in range(GLA_HEADS)]
    o_ref[:, M_C:M_C + GLA_WIDTH] = _gla_finish(o_heads, gate_ref[...], ng_ref[...]).astype(BF16)


assert SWA_BT == GLA_BT
SAMPLE_BT = SWA_BT


def _mix_sample_kernel(l, sink_ref, q_ref, kn_ref, vn_ref, ck_ref, cv_ref,
                       u_ref, bb_ref, cre_ref, cim_ref, d_ref, wg_ref, bg_ref, tab_ref, h0re_ref, h0im_ref,
                       gq_ref, gk_ref, gv_ref, gg_ref, ga_ref, wa_ref, ba_ref, ng_ref, s_ref,
                       o_ref, cko_ref, cvo_ref, hsre_ref, hsim_ref, so_ref):
    _swa_sample_kernel(l, sink_ref, q_ref, kn_ref, vn_ref, ck_ref, cv_ref, o_ref, cko_ref, cvo_ref)
    _s5_sample_kernel(u_ref, bb_ref, cre_ref, cim_ref, d_ref, wg_ref, bg_ref, tab_ref, h0re_ref, h0im_ref,
                      o_ref, hsre_ref, hsim_ref)
    _gla_sample_kernel(gq_ref, gk_ref, gv_ref, gg_ref, ga_ref, wa_ref, ba_ref, ng_ref, s_ref, o_ref, so_ref)


def _mix_sample(sink, h, cache_k, cache_v, s5_prm, h0re, h0im, gla_prm, state, mix, ck_all, cv_all, s_all, l):
    bt = SAMPLE_BT
    r0 = ROWS_P // bt
    blk = lambda c, w: pl.BlockSpec((bt, w), lambda s: (r0 + s, c // w))
    cache = pl.BlockSpec((None, bt, A_KV_HEADS, HEAD_DIM, WINDOW), lambda s: (l, s, 0, 0, 0))
    h0 = pl.BlockSpec((None, bt, S5_LANES), lambda s: (l, s, 0))
    hs = pl.BlockSpec((bt, S5_LANES), lambda s: (s, 0))
    st = pl.BlockSpec((None, bt, GLA_KW, GLA_DV), lambda s: (l, s, 0, 0))
    in_specs = ([pl.BlockSpec(memory_space=pltpu.SMEM), blk(C_Q, A_WIDTH), blk(C_K, KV_WIDTH), blk(C_V, KV_WIDTH),
                 cache, cache, blk(C_U, S5_WIDTH)]
                + _s5_param_specs(l, False) + [h0, h0]
                + [blk(C_GQ, GLA_KW), blk(C_GK, GLA_KW), blk(C_GV, GLA_WIDTH), blk(C_GG, GLA_WIDTH), blk(C_GA, 128)]
                + _gla_param_specs(l) + [st])
    n_in = len(in_specs)
    hs_shape = jax.ShapeDtypeStruct((N_SAMPLE, S5_LANES), F32)
    return pl.pallas_call(
        _drop_inputs(functools.partial(_mix_sample_kernel, l), n_in, 4),
        grid=(N_SAMPLE // bt,),
        in_specs=in_specs + [_any()] * 4,
        out_specs=[blk(0, D_MODEL), cache, cache, hs, hs, st],
        out_shape=[jax.ShapeDtypeStruct((ROWS, D_MODEL), BF16),
                   jax.ShapeDtypeStruct(ck_all.shape, F32), jax.ShapeDtypeStruct(cv_all.shape, F32),
                   hs_shape, hs_shape, jax.ShapeDtypeStruct(s_all.shape, F32)],
        input_output_aliases={n_in: 0, n_in + 1: 1, n_in + 2: 2, n_in + 3: 5},
        compiler_params=_cparams("arbitrary"),
        name="mix_sample",
    )(sink, h, h, h, cache_k, cache_v, h, *_s5_params(s5_prm, False), h0re, h0im,
      h, h, h, h, h, *gla_prm, state, mix, ck_all, cv_all, s_all)


def _mix_prompt_kernel(l, n_cast, sink_ref, bias_ref, q_ref, kp_ref, kc_ref, vp_ref, vc_ref, u_ref,
                       gq_ref, gk_ref, gv_ref, gg_ref, ga_ref,
                       bb_ref, cre_ref, cim_ref, d_ref, wg_ref, bg_ref, ab_ref, wa_ref, ba_ref, ng_ref, *rest):
    w_refs, (o_ref, hpre_ref, hpim_ref, s_ref) = rest[:n_cast], rest[n_cast:n_cast + 4]
    c_refs, (xre, xim, hre, him) = rest[n_cast + 4:2 * n_cast + 4], rest[2 * n_cast + 4:]
    s_blk = pl.program_id(0)

    @pl.when(s_blk == 0)
    def _():
        hpre_ref[...] = jnp.zeros_like(hpre_ref)
        hpim_ref[...] = jnp.zeros_like(hpim_ref)
        s_ref[...] = jnp.zeros_like(s_ref)

    for w_ref, c_ref in zip(w_refs, c_refs):
        c_ref[...] = w_ref[...].astype(BF16)
    _s5_block(u_ref, bb_ref, cre_ref, cim_ref, d_ref, wg_ref, bg_ref, ab_ref, o_ref, hpre_ref, hpim_ref,
              xre, xim, hre, him)
    _swa_block(l, sink_ref, bias_ref, q_ref, kp_ref, kc_ref, vp_ref, vc_ref, o_ref)
    _gla_block(gq_ref, gk_ref, gv_ref, gg_ref, ga_ref, wa_ref, ba_ref, ng_ref, o_ref, s_ref)


def _mix_prompt(sink, h, s5_prm, gla_prm, w_casts, mix, l):
    cur = _seq_block
    prev = lambda s: _seq_block(jnp.maximum(s - 1, 0))
    blk = lambda c, w, at=cur: pl.BlockSpec((BLK, w), lambda s: (at(s), c // w))
    const = lambda shape: pl.BlockSpec(shape, lambda s: (0,) * len(shape))
    casts = [_cast_specs(w, rows, layer) for w, rows, layer in w_casts]
    in_specs = ([pl.BlockSpec(memory_space=pltpu.SMEM),
                 pl.BlockSpec((None, BLK, 2 * BLK), lambda s: (jnp.minimum(s, 2), 0, 0)), blk(C_Q, A_WIDTH),
                 blk(C_K, KV_WIDTH, prev), blk(C_K, KV_WIDTH), blk(C_V, KV_WIDTH, prev), blk(C_V, KV_WIDTH),
                 blk(C_U, S5_WIDTH),
                 blk(C_GQ, GLA_KW), blk(C_GK, GLA_KW), blk(C_GV, GLA_WIDTH), blk(C_GG, GLA_WIDTH), blk(C_GA, 128)]
                + _s5_param_specs(l, True) + _gla_param_specs(l) + [c[0] for c in casts])
    n_in = len(in_specs)
    return pl.pallas_call(
        _drop_inputs(functools.partial(_mix_prompt_kernel, l, len(casts)), n_in, 1),
        grid=(N_PBLK,),
        in_specs=in_specs + [_any()],
        out_specs=[pl.BlockSpec((BLK, D_MODEL), lambda s: (cur(s), 0)),
                   const((S5_CHUNKS, 128)), const((S5_CHUNKS, 128)), const((GLA_KW, GLA_DV))] + [c[1] for c in casts],
        out_shape=[jax.ShapeDtypeStruct((ROWS, D_MODEL), BF16),
                   jax.ShapeDtypeStruct((S5_CHUNKS, 128), F32), jax.ShapeDtypeStruct((S5_CHUNKS, 128), F32),
                   jax.ShapeDtypeStruct((GLA_KW, GLA_DV), F32)] + [c[2] for c in casts],
        scratch_shapes=[pltpu.VMEM((S5_CHUNKS * BLK, 128), F32)] * 4,
        input_output_aliases={n_in: 0},
        compiler_params=_cparams("arbitrary"),
        name="mix_prompt",
    )(sink, _swa_bias(), *([h] * 11), *_s5_params(s5_prm, True), *gla_prm, *[w for w, _, _ in w_casts], mix)


OUT_TM = 768
OUT_PARTS = 3
LN_ROWS = 16


def _out_proj_kernel(mix_ref, w_ref, x_ref, g_ref, b_ref, o_ref):
    i = pl.program_id(0)
    g, b = g_ref[...], b_ref[...]
    part = OUT_TM // OUT_PARTS
    rows = [slice(k * part, (k + 1) * part) for k in range(OUT_PARTS)]
    def norm(k, acc):
        for c in range(part // LN_ROWS):
            r0 = k * part + c * LN_ROWS
            y = _layer_norm_rows(ALPHA * x_ref[r0:r0 + LN_ROWS, :] + acc[c * LN_ROWS:(c + 1) * LN_ROWS, :], g, b)
            o_ref[r0:r0 + LN_ROWS, :] = _zero_pad_rows(y, i * OUT_TM + r0)

    prev = None
    for k, r in enumerate(rows):
        acc = _dot(mix_ref[r, :], w_ref[...])
        if prev is not None:
            norm(k - 1, prev)
        prev = acc
    norm(OUT_PARTS - 1, prev)


def _out_proj(mix, w, x, g, b, l):
    tm = OUT_TM
    return pl.pallas_call(
        _out_proj_kernel,
        grid=(ROWS // tm,),
        in_specs=[
            pl.BlockSpec((tm, D_MODEL), lambda i: (i, 0)),
            pl.BlockSpec((D_MODEL, D_MODEL), lambda i: (0, 0), pipeline_mode=pl.Buffered(1)),
            pl.BlockSpec((tm, D_MODEL), lambda i: (i, 0)),
            _layer_spec((1, D_MODEL), l), _layer_spec((1, D_MODEL), l),
        ],
        out_specs=pl.BlockSpec((tm, D_MODEL), lambda i: (i, 0)),
        out_shape=jax.ShapeDtypeStruct((ROWS, D_MODEL), F32),
        compiler_params=_cparams("arbitrary"),
        name="out_proj",
    )(mix, w, x, g, b)


FFN_TM = 768
FFN_TF = 512
FFN_HALO = 16
FFN_NI = ROWS // FFN_TM
assert ROWS % FFN_TM == 0 and FFN_TM >= N_SAMPLE + BLK + SUB and META_ROW % FFN_HALO == 0


def _ffn_kernel(final, x_ref, xh_ref, wu_ref, wg_ref, cw_ref, cb_ref, wd_ref, p0_ref, p1_ref, g_ref, b_ref,
                o_ref, *rest):
    if final:
        ys_ref, ut_ref, us_ref, xb_ref, a_ref = rest
    else:
        ut_ref, us_ref, xb_ref, a_ref = rest
    i, f = pl.program_id(0), pl.program_id(1)
    tm, hl = FFN_TM, FFN_HALO
    last_i = FFN_NI - 1

    @pl.when(f == 0)
    def _():
        xb_ref[...] = x_ref[...].astype(BF16)
        o_ref[...] = jnp.zeros_like(o_ref)

    cw0, cw1, cw2, cb = cw_ref[0:1, :], cw_ref[1:2, :], cw_ref[2:3, :], cb_ref[...]

    def act(um2, um1, u0, g0):
        return (jax.nn.gelu(cb + cw0 * um2 + cw1 * um1 + cw2 * u0) * g0).astype(BF16)

    xb = xb_ref[...]
    u = _dot(xb, wu_ref[...])
    gte = _dot(xb, wg_ref[...])
    um2, um1 = pltpu.roll(u, 2, 0), pltpu.roll(u, 1, 0)
    body = slice(hl, tm - N_SAMPLE)
    a_ref[body, :] = act(um2[body, :], um1[body, :], u[body, :], gte[body, :])
    ext = jnp.concatenate([_dot(xh_ref[...].astype(BF16), wu_ref[...]), u[0:hl, :]], axis=0)
    a_ref[0:hl, :] = act(ext[hl - 2:2 * hl - 2, :], ext[hl - 1:2 * hl - 1, :], u[0:hl, :], gte[0:hl, :])
    tail = slice(tm - N_SAMPLE, tm)
    samp = i == last_i
    a_ref[tail, :] = act(jnp.where(samp, p0_ref[...], um2[tail, :]), jnp.where(samp, p1_ref[...], um1[tail, :]),
                         u[tail, :], gte[tail, :])
    o_ref[...] += _dot(a_ref[...], wd_ref[...])
    ut_ref[...] = u[tm - N_SAMPLE - BLK - SUB:tm - N_SAMPLE - BLK, :]
    us_ref[0] = p1_ref[...]
    us_ref[1] = u[tm - N_SAMPLE:, :]

    @pl.when(f == pl.num_programs(1) - 1)
    def _():
        _deepnorm_ln(o_ref, x_ref, g_ref, b_ref, o_ref, i * tm, tm)
        if final:
            ys_ref[...] = o_ref[tm - N_SAMPLE:, :]


def _ffn(x, wup, cw, cb, wd, conv_p0, conv_p1, g, b, l, final):
    tm, tf = FFN_TM, FFN_TF
    nf = D_FF // tf
    tail = lambda i, f: jnp.where(i == FFN_NI - 1, f, 0)
    halo = lambda i, f: (jnp.where(i == 0, ROWS_P // FFN_HALO, i * (tm // FFN_HALO)) - 1, 0)
    y_spec = pl.BlockSpec((tm, D_MODEL), lambda i, f: (i, 0))
    y_specs = [y_spec, pl.BlockSpec((N_SAMPLE, D_MODEL), lambda i, f: (0, 0))] if final else [y_spec]
    y_shapes = ([jax.ShapeDtypeStruct((SEQ, D_MODEL), F32), jax.ShapeDtypeStruct((N_SAMPLE, D_MODEL), F32)]
                if final else [jax.ShapeDtypeStruct((ROWS, D_MODEL), F32)])
    return pl.pallas_call(
        functools.partial(_ffn_kernel, final),
        grid=(FFN_NI, nf),
        in_specs=[
            pl.BlockSpec((tm, D_MODEL), lambda i, f: (i, 0)),
            pl.BlockSpec((FFN_HALO, D_MODEL), halo),
            pl.BlockSpec((D_MODEL, tf), lambda i, f: (0, f)),
            pl.BlockSpec((D_MODEL, tf), lambda i, f: (0, nf + f)),
            pl.BlockSpec((None, CONV_W, tf), lambda i, f: (l, 0, f)),
            pl.BlockSpec((None, 1, tf), lambda i, f: (l, 0, f)),
            pl.BlockSpec((tf, D_MODEL), lambda i, f: (f, 0)),
            pl.BlockSpec((None, N_SAMPLE, tf), lambda i, f: (l, 0, tail(i, f))),
            pl.BlockSpec((None, N_SAMPLE, tf), lambda i, f: (l, 0, tail(i, f))),
            _layer_spec((1, D_MODEL), l), _layer_spec((1, D_MODEL), l),
        ],
        out_specs=y_specs + [
            pl.BlockSpec((SUB, tf), lambda i, f: (0, tail(i, f))),
            pl.BlockSpec((2, N_SAMPLE, tf), lambda i, f: (0, 0, tail(i, f))),
        ],
        out_shape=y_shapes + [
            jax.ShapeDtypeStruct((SUB, D_FF), F32),
            jax.ShapeDtypeStruct((2, N_SAMPLE, D_FF), F32),
        ],
        scratch_shapes=[pltpu.VMEM((tm, D_MODEL), BF16), pltpu.VMEM((tm, tf), BF16)],
        compiler_params=_cparams("arbitrary", "arbitrary"),
        name="conv_ffn",
    )(x, x, wup, wup, cw, cb, wd, conv_p0, conv_p1, g, b)


def _rope_table():
    inv = jnp.power(ROPE_THETA, -jnp.arange(ROPE_HALF, dtype=F32) / ROPE_HALF)
    pos = jnp.concatenate([jnp.arange(SEQ, dtype=jnp.int32) + N_META, jnp.zeros((LEAD,), jnp.int32),
                           jnp.arange(N_META, dtype=jnp.int32), jnp.full((N_SAMPLE,), PAST_LEN, jnp.int32)])
    ang = pos.astype(F32)[:, None] * inv[None, :]
    return jnp.concatenate([jnp.cos(ang), jnp.sin(ang)], axis=1)


def _s5_tables(lam_re, lam_im, log_dt, b_re, b_im, c_re, c_im, d, w_glu, b_glu):
    L = lam_re.shape[0]
    dt = jnp.exp(log_dt)[..., None]
    mag = jnp.exp(lam_re * dt)
    ab_re, ab_im = mag * jnp.cos(lam_im * dt), mag * jnp.sin(lam_im * dt)
    den = lam_re * lam_re + lam_im * lam_im
    nr, ni = ab_re - 1.0, ab_im
    f_re = (nr * lam_re + ni * lam_im) / den
    f_im = (ni * lam_re - nr * lam_im) / den
    bb_re = f_re[..., None] * b_re - f_im[..., None] * b_im
    bb_im = f_re[..., None] * b_im + f_im[..., None] * b_re
    gi, go = S5_GROUPS // S5_IN_PIECES, S5_GROUPS // S5_OUT_PIECES

    def bd_in(t):
        t = t.reshape(L, S5_IN_PIECES, gi, S5_STATE, S5_CH)
        return jnp.einsum('lqgph,gk->lqghkp', t, jnp.eye(gi, dtype=F32)).reshape(L, S5_IN_PIECES, S5_IN_K, S5_IN_N)

    def bd_out(t):
        t = t.reshape(L, S5_OUT_PIECES, go, S5_CH, S5_STATE)
        return jnp.einsum('lqghp,gk->lqgpkh', t, jnp.eye(go, dtype=F32)).reshape(
            L, S5_OUT_PIECES, S5_OUT_K, S5_OUT_N)

    ar, ai = ab_re.reshape(L, 1, S5_LANES), ab_im.reshape(L, 1, S5_LANES)
    pows = [(ar, ai)]
    for _ in range(SUB - 1):
        pr, pi = pows[-1]
        pows.append((pr * ar - pi * ai, pr * ai + pi * ar))
    row = jnp.arange(SUB)[None, :, None]
    parts = []
    for sft in S5_SHIFTS:
        pr, pi = pows[sft - 1]
        parts += [jnp.where(row >= sft, pr, 0.0), jnp.where(row >= sft, pi, 0.0)]
    parts += [jnp.concatenate([p[0] for p in pows], 1), jnp.concatenate([p[1] for p in pows], 1)]
    return dict(bb=jnp.concatenate([bd_in(bb_re), bd_in(bb_im)], axis=3).astype(BF16),
                c_re=bd_out(c_re).astype(BF16), c_im=bd_out(c_im).astype(BF16),
                d=d.reshape(L, 1, S5_WIDTH), w_glu=w_glu.astype(BF16), b_glu=b_glu.reshape(L, 1, S5_WIDTH),
                tab=jnp.concatenate(parts, 1),
                ab=jnp.concatenate([ab_re.reshape(L, S5_CHUNKS, 128), ab_im.reshape(L, S5_CHUNKS, 128)], axis=1))


@jax.jit
def kernel(x_prompt, x_sample, cache_swa_k, cache_swa_v, state_ssm_re, state_ssm_im, state_gla, state_conv,
           meta_tokens, ln_in_g, ln_in_b, w_in, attn_sink, s5_lam_re, s5_lam_im, s5_log_dt, s5_b_re, s5_b_im,
           s5_c_re, s5_c_im, s5_d, s5_w_glu, s5_b_glu, gla_w_a2, gla_b_a, gla_norm_g, w_out, ln1_g, ln1_b,
           ffn_w_up, ffn_conv_w, ffn_conv_b, ffn_w_down, ln2_g, ln2_b):
    L = DEPTH
    row3 = lambda t: t.reshape(L, 1, -1)
    wa_b = jnp.pad(gla_w_a2, ((0, 0), (0, 128 - GLA_LOWRANK), (0, 0))).astype(BF16)
    gla_args = (wa_b, row3(gla_b_a), row3(gla_norm_g))
    ln1, ln2 = (row3(ln1_g), row3(ln1_b)), (row3(ln2_g), row3(ln2_b))
    conv_b3 = row3(ffn_conv_b)
    cs = _rope_table()
    s5p = _s5_tables(s5_lam_re, s5_lam_im, s5_log_dt, s5_b_re, s5_b_im, s5_c_re, s5_c_im, s5_d, s5_w_glu, s5_b_glu)
    cache_t = lambda c: jnp.transpose(c, (0, 1, 3, 4, 2))
    ck, cv = cache_t(cache_swa_k), cache_t(cache_swa_v)
    ck_all, cv_all = jnp.zeros(ck.shape, F32), jnp.zeros(cv.shape, F32)
    h0re = state_ssm_re.reshape(L, N_SAMPLE, S5_LANES)
    h0im = state_ssm_im.reshape(L, N_SAMPLE, S5_LANES)
    sg = state_gla.reshape(L, N_SAMPLE, GLA_KW, GLA_DV)
    conv_p0, conv_p1 = state_conv[:, :, 0, :], state_conv[:, :, 1, :]
    mix = jnp.zeros((ROWS, D_MODEL), BF16)
    s_all = jnp.zeros((L, N_SAMPLE, GLA_KW, GLA_DV), F32)
    w_in_b = jnp.swapaxes(w_in, 1, 2).astype(BF16)

    x = _ln_in(x_prompt.reshape(SEQ, D_MODEL), meta_tokens, x_sample.reshape(N_SAMPLE, D_MODEL),
               ln_in_g.reshape(1, D_MODEL), ln_in_b.reshape(1, D_MODEL))
    outs = [[] for _ in range(9)]
    for l in range(L):
        h = _in_proj(x, w_in_b, cs, l)
        casts = [(ffn_w_up, 32, l), (ffn_w_down, 176, l), (w_out, 32, l)]
        mix, hp_re, hp_im, s_p, w_up_b, w_down_b, w_out_b = _mix_prompt(
            attn_sink, h, s5p, gla_args, casts, mix, l)
        mix, ck_all, cv_all, hs_re, hs_im, s_all = _mix_sample(
            attn_sink, h, ck, cv, s5p, h0re, h0im, gla_args, sg, mix, ck_all, cv_all, s_all, l)
        x = _out_proj(mix, w_out_b, x, *ln1, l)
        res = _ffn(x, w_up_b, ffn_conv_w, conv_b3, w_down_b, conv_p0, conv_p1, *ln2, l, l == L - 1)
        if l == L - 1:
            y_prompt, y_sample, u_tail, u_s = res
        else:
            x, u_tail, u_s = res
        kv_p = lambda c: h[SEQ - WINDOW:SEQ, c:c + KV_WIDTH].reshape(1, WINDOW, A_KV_HEADS, HEAD_DIM)
        new = (kv_p(C_K), kv_p(C_V),
               hp_re.reshape(1, S5_GROUPS, S5_STATE), hp_im.reshape(1, S5_GROUPS, S5_STATE),
               s_p.reshape(1, GLA_HEADS, GLA_DK, GLA_DV), u_tail[SUB - 2:SUB][None],
               hs_re.reshape(N_SAMPLE, S5_GROUPS, S5_STATE), hs_im.reshape(N_SAMPLE, S5_GROUPS, S5_STATE),
               jnp.transpose(u_s, (1, 0, 2)))
        for lst, val in zip(outs, new):
            lst.append(val)
    st = [jnp.stack(lst) for lst in outs]
    return (y_prompt.reshape(1, SEQ, D_MODEL), y_sample.reshape(N_SAMPLE, 1, D_MODEL),
            st[0], st[1], st[2], st[3], st[4], st[5],
            jnp.transpose(ck_all, (0, 1, 4, 2, 3)), jnp.transpose(cv_all, (0, 1, 4, 2, 3)),
            st[6], st[7], s_all.reshape(L, N_SAMPLE, GLA_HEADS, GLA_DK, GLA_DV), st[8])
```

```python
import functools

import jax
import jax.numpy as jnp
from jax import lax
from jax.experimental import pallas as pl
from jax.experimental.pallas import tpu as pltpu

F32 = jnp.float32
BF16 = jnp.bfloat16

D_MODEL = 2048
SEQ = 8192
DEPTH = 2
N_SAMPLE = 128
N_META = 16
HEAD_DIM = 64
A_WIDTH = 1024
A_HEADS = 16
A_KV_HEADS = 4
GQA = 4
KV_WIDTH = A_KV_HEADS * HEAD_DIM
WINDOW = 128
ROPE_DIM = 16
ROPE_HALF = ROPE_DIM // 2
ROPE_THETA = 500000.0
PAST_LEN = 8192
S5_WIDTH = 512
S5_CH = 16
S5_GROUPS = 32
S5_STATE = 64
S5_LANES = S5_GROUPS * S5_STATE
GLA_WIDTH = 512
GLA_HEADS = 4
GLA_DV = 128
GLA_DK = 64
GLA_KW = GLA_HEADS * GLA_DK
GLA_LOWRANK = 16
GLA_TAU = 16.0
GLA_CHUNK = 64
D_FF = 5632
CONV_W = 3
LN_EPS = 1e-5
ALPHA = (2 * DEPTH) ** 0.25
NEG_INF = -1e30

BLK = 128
T_PROMPT = N_META + SEQ
LEAD = (-T_PROMPT) % BLK
N_PBLK = (LEAD + T_PROMPT) // BLK
ROWS_P = N_PBLK * BLK
ROWS = ROWS_P + N_SAMPLE
META_ROW = SEQ + LEAD
SUB = 8

C_Q, C_K, C_V, C_U, C_GQ, C_GK, C_GV, C_GG, C_GA = 0, 1024, 1280, 1536, 2048, 2304, 2560, 3072, 3584
IN_COLS = 3600
IN_TILE = 1280
IN_PAD = 3 * IN_TILE
ROPE_COLS = A_WIDTH + KV_WIDTH
M_A, M_B, M_C = 0, A_WIDTH, A_WIDTH + S5_WIDTH

VMEM_LIMIT = 56 * 1024 * 1024


def _cparams(*sem):
    return pltpu.CompilerParams(dimension_semantics=sem, vmem_limit_bytes=VMEM_LIMIT)


def _any():
    return pl.BlockSpec(memory_space=pl.ANY)


def _drop_inputs(kern, start, count):
    def wrapped(*refs):
        return kern(*refs[:start], *refs[start + count:])
    return wrapped


def _seq_block(s):
    return jnp.where(s == 0, N_PBLK - 1, s - 1)


def _layer_spec(shape, l):
    return pl.BlockSpec((None,) + shape, lambda *_: (l,) + (0,) * len(shape))


def _cast_specs(w, rows, l):
    n, cols = w.shape[1] // rows, w.shape[2]
    chunk = lambda s: jnp.minimum(s, n - 1)
    return (pl.BlockSpec((None, rows, cols), lambda s: (l, chunk(s), 0)),
            pl.BlockSpec((rows, cols), lambda s: (chunk(s), 0)),
            jax.ShapeDtypeStruct(w.shape[1:], BF16))


def _layer_norm_rows(x, g, b):
    mu = jnp.mean(x, -1, keepdims=True)
    xc = x - mu
    var = jnp.mean(xc * xc, -1, keepdims=True)
    return xc * lax.rsqrt(var + LN_EPS) * g + b


def _zero_pad_rows(y, first_row):
    row = first_row + lax.broadcasted_iota(jnp.int32, y.shape, 0)
    return jnp.where((row >= SEQ) & (row < META_ROW), 0.0, y)


def _deepnorm_ln(acc_ref, x_ref, g_ref, b_ref, o_ref, first_row, rows):
    g, b = g_ref[...], b_ref[...]
    for c in range(rows // BLK):
        r = slice(c * BLK, (c + 1) * BLK)
        y = _layer_norm_rows(ALPHA * x_ref[r, :] + acc_ref[r, :], g, b)
        o_ref[r, :] = _zero_pad_rows(y, first_row + c * BLK)


def _dot(a, b):
    return jnp.dot(a, b, preferred_element_type=F32)


def _dot_nt(a, b):
    return lax.dot_general(a, b, (((1,), (1,)), ((), ())), preferred_element_type=F32)


def _dot_tn(a, b):
    return lax.dot_general(a, b, (((0,), (0,)), ((), ())), preferred_element_type=F32)


def _log_sigmoid(x):
    return jnp.minimum(x, 0.0) - jnp.log(1.0 + jnp.exp(-jnp.abs(x)))


LN_IN_TM = 512
LN_IN_STEPS = SEQ // LN_IN_TM + 1
assert SEQ % LN_IN_TM == 0 and ROWS - SEQ <= LN_IN_TM


def _ln_in_kernel(xp_ref, meta_ref, xs_ref, g_ref, b_ref, o_ref):
    i = pl.program_id(0)
    g, b = g_ref[...], b_ref[...]

    @pl.when(i < LN_IN_STEPS - 1)
    def _():
        for c in range(LN_IN_TM // BLK):
            r = slice(c * BLK, (c + 1) * BLK)
            o_ref[r, :] = _layer_norm_rows(xp_ref[r, :], g, b)

    @pl.when(i == LN_IN_STEPS - 1)
    def _():
        o_ref[0:LEAD, :] = jnp.zeros((LEAD, D_MODEL), F32)
        o_ref[LEAD:BLK, :] = _layer_norm_rows(meta_ref[...], g, b)
        o_ref[BLK:BLK + N_SAMPLE, :] = _layer_norm_rows(xs_ref[...], g, b)


def _ln_in(xp, meta, xs, g, b):
    tm = LN_IN_TM
    const = lambda shape: pl.BlockSpec(shape, lambda i: (0,) * len(shape))
    return pl.pallas_call(
        _ln_in_kernel,
        grid=(LN_IN_STEPS,),
        in_specs=[
            pl.BlockSpec((tm, D_MODEL), lambda i: (jnp.minimum(i, SEQ // tm - 1), 0)),
            const((N_META, D_MODEL)), const((N_SAMPLE, D_MODEL)), const((1, D_MODEL)), const((1, D_MODEL)),
        ],
        out_specs=pl.BlockSpec((tm, D_MODEL), lambda i: (i, 0)),
        out_shape=jax.ShapeDtypeStruct((ROWS, D_MODEL), F32),
        compiler_params=_cparams("arbitrary"),
        name="ln_in",
    )(xp, meta, xs, g, b)


IN_TM = 1056


def _in_proj_kernel(x_ref, w_ref, cs_ref, o_ref, xb_ref):
    j = pl.program_id(1)

    last_j = IN_PAD // IN_TILE - 1

    @pl.when(j == 0)
    def _():
        xb = x_ref[...].astype(BF16)
        xb_ref[...] = xb
        c8, s8 = cs_ref[:, 0:ROPE_HALF], cs_ref[:, ROPE_HALF:ROPE_DIM]
        rest = HEAD_DIM - ROPE_DIM
        one, zero = jnp.ones((IN_TM, rest), F32), jnp.zeros((IN_TM, rest), F32)
        z8 = jnp.zeros((IN_TM, ROPE_HALF), F32)
        cos = jnp.concatenate([c8, c8, one] * 2, axis=1)
        sa = jnp.concatenate([-s8, z8, zero] * 2, axis=1)
        sb = jnp.concatenate([z8, s8, zero] * 2, axis=1)
        for c in range(ROPE_COLS // 256):
            acc = _dot_nt(xb, w_ref[c * 256:(c + 1) * 256, :])
            for hlf in range(2):
                blk = acc[:, hlf * 128:(hlf + 1) * 128]
                o_ref[:, c * 256 + hlf * 128:c * 256 + (hlf + 1) * 128] = (
                    blk * cos + pltpu.roll(blk, 128 - ROPE_HALF, 1) * sa + pltpu.roll(blk, ROPE_HALF, 1) * sb)

    @pl.when((j > 0) & (j < last_j))
    def _():
        o_ref[...] = _dot_nt(xb_ref[...], w_ref[...])

    @pl.when(j == last_j)
    def _():
        o_ref[...] = _dot_nt(xb_ref[...], w_ref[...])
        edge = (IN_COLS % IN_TILE) // 128 * 128
        col = edge + lax.broadcasted_iota(jnp.int32, (IN_TM, 128), 1)
        o_ref[:, edge:edge + 128] = jnp.where(col < IN_COLS % IN_TILE, o_ref[:, edge:edge + 128], 0.0)
        o_ref[:, edge + 128:] = jnp.zeros((IN_TM, IN_TILE - edge - 128), F32)


def _in_proj(x, w, cs, l):
    return pl.pallas_call(
        _in_proj_kernel,
        grid=(ROWS // IN_TM, IN_PAD // IN_TILE),
        in_specs=[
            pl.BlockSpec((IN_TM, D_MODEL), lambda i, j: (i, 0)),
            pl.BlockSpec((None, IN_TILE, D_MODEL), lambda i, j: (l, j, 0)),
            pl.BlockSpec((IN_TM, ROPE_DIM), lambda i, j: (i, 0)),
        ],
        out_specs=pl.BlockSpec((IN_TM, IN_TILE), lambda i, j: (i, j)),
        out_shape=jax.ShapeDtypeStruct((ROWS, IN_PAD), F32),
        scratch_shapes=[pltpu.VMEM((IN_TM, D_MODEL), BF16)],
        compiler_params=_cparams("arbitrary", "arbitrary"),
        name="in_proj",
    )(x, w, cs)


def _swa_bias():
    row = lax.broadcasted_iota(jnp.int32, (3, BLK, 2 * BLK), 1)
    col = lax.broadcasted_iota(jnp.int32, (3, BLK, 2 * BLK), 2)
    s_blk = lax.broadcasted_iota(jnp.int32, (3, BLK, 2 * BLK), 0)
    diff = row - col + BLK
    kpos = (s_blk - 1) * BLK + col - LEAD
    ok = (diff >= 0) & (diff <= WINDOW) & (kpos >= 0)
    return jnp.where(ok, 0.0, NEG_INF).astype(F32)


def _swa_block(l, sink_ref, bias_ref, q_ref, kp_ref, kc_ref, vp_ref, vc_ref, o_ref):
    k2 = jnp.concatenate([kp_ref[...], kc_ref[...]], axis=0).astype(BF16)
    v2 = jnp.concatenate([vp_ref[...], vc_ref[...]], axis=0).astype(BF16)
    nq = GQA * BLK
    bias = jnp.concatenate([bias_ref[...]] * GQA, axis=0)
    low = lax.broadcasted_iota(jnp.int32, (BLK, 2 * HEAD_DIM), 1) < HEAD_DIM
    g_of_row = lax.broadcasted_iota(jnp.int32, (nq, 1), 0) // BLK
    ones = jnp.ones((2 * BLK, 2 * HEAD_DIM), BF16)
    scores, sinks = [], []
    for kv in range(A_KV_HEADS):
        kh = k2[:, kv * HEAD_DIM:(kv + 1) * HEAD_DIM]
        kk = jnp.concatenate([kh, kh], axis=1)
        parts = []
        for pair in range(GQA // 2):
            c0 = (kv * GQA + 2 * pair) * HEAD_DIM
            qp = q_ref[:, c0:c0 + 2 * HEAD_DIM] * HEAD_DIM ** -0.5
            parts += [jnp.where(low, qp, 0.0).astype(BF16), jnp.where(low, 0.0, qp).astype(BF16)]
        q4 = jnp.concatenate(parts, axis=0)
        sk = jnp.zeros((nq, 1), F32)
        for g in range(GQA):
            sk = jnp.where(g_of_row == g, sink_ref[l, kv * GQA + g], sk)
        scores.append(_dot_nt(q4, kk) + bias)
        sinks.append(sk)
    probs, sink_terms = [], []
    for s, sk in zip(scores, sinks):
        m = jnp.maximum(jnp.max(s, -1, keepdims=True), sk)
        probs.append(jnp.exp(s - m).astype(BF16))
        sink_terms.append(jnp.exp(sk - m))
    for kv in range(A_KV_HEADS):
        vh = v2[:, kv * HEAD_DIM:(kv + 1) * HEAD_DIM]
        v3 = jnp.concatenate([vh, vh, ones], axis=1)
        oa = _dot(probs[kv], v3)
        on = oa[:, 0:2 * HEAD_DIM] / (oa[:, 2 * HEAD_DIM:] + sink_terms[kv])
        for pair in range(GQA // 2):
            c0 = (kv * GQA + 2 * pair) * HEAD_DIM
            lo, hi = on[2 * pair * BLK:(2 * pair + 1) * BLK], on[(2 * pair + 1) * BLK:(2 * pair + 2) * BLK]
            o_ref[:, M_A + c0:M_A + c0 + 2 * HEAD_DIM] = jnp.where(low, lo, hi).astype(BF16)


SWA_BT = 16


def _swa_sample_kernel(l, sink_ref, q_ref, kn_ref, vn_ref, ck_ref, cv_ref, o_ref, cko_ref, cvo_ref):
    bt = SWA_BT
    kn, vn = kn_ref[...], vn_ref[...]

    def cols(x):
        return jnp.transpose(jnp.concatenate([x, jnp.zeros((128 - bt, KV_WIDTH), F32)], axis=0))

    kn_t, vn_t = cols(kn), cols(vn)
    last = lax.broadcasted_iota(jnp.int32, (HEAD_DIM, WINDOW), 1) == WINDOW - 1
    for j in range(bt):
        for kv in range(A_KV_HEADS):
            hd = slice(kv * HEAD_DIM, (kv + 1) * HEAD_DIM)
            cko_ref[j, kv] = jnp.where(last, kn_t[hd, j:j + 1], pltpu.roll(ck_ref[j, kv], WINDOW - 1, 1))
            cvo_ref[j, kv] = jnp.where(last, vn_t[hd, j:j + 1], pltpu.roll(cv_ref[j, kv], WINDOW - 1, 1))
    nq = GQA * bt
    rowb = lax.broadcasted_iota(jnp.int32, (nq, bt * WINDOW), 0) % bt
    colb = lax.broadcasted_iota(jnp.int32, (nq, bt * WINDOW), 1) // WINDOW
    same = rowb == colb
    rg = lax.broadcasted_iota(jnp.int32, (nq, 1), 0) // bt
    for kv in range(A_KV_HEADS):
        sl = slice(kv * HEAD_DIM, (kv + 1) * HEAD_DIM)
        q4 = jnp.concatenate(
            [q_ref[:, (kv * GQA + g) * HEAD_DIM:(kv * GQA + g + 1) * HEAD_DIM] for g in range(GQA)], axis=0)
        q4 = q4 * HEAD_DIM ** -0.5
        kn4 = jnp.concatenate([kn[:, sl]] * GQA, axis=0)
        vn4 = jnp.concatenate([vn[:, sl]] * GQA, axis=0)
        sk = jnp.zeros((nq, 1), F32)
        for g in range(GQA):
            sk = jnp.where(rg == g, sink_ref[l, kv * GQA + g], sk)
        kt = jnp.concatenate([ck_ref[j, kv] for j in range(bt)], axis=1).astype(BF16)
        vt = jnp.concatenate([cv_ref[j, kv] for j in range(bt)], axis=1).astype(BF16)
        s = jnp.where(same, _dot(q4.astype(BF16), kt), NEG_INF)
        s_self = jnp.sum(q4 * kn4, -1, keepdims=True)
        m = jnp.maximum(jnp.maximum(jnp.max(s, -1, keepdims=True), s_self), sk)
        p = jnp.exp(s - m)
        p_self = jnp.exp(s_self - m)
        den = jnp.sum(p, -1, keepdims=True) + p_self + jnp.exp(sk - m)
        o = (_dot_nt(p.astype(BF16), vt) + p_self * vn4) / den
        for g in range(GQA):
            hh = kv * GQA + g
            o_ref[:, hh * HEAD_DIM:(hh + 1) * HEAD_DIM] = o[g * bt:(g + 1) * bt].astype(BF16)


S5_SHIFTS = (1, 2, 4)
S5_NTAB = 2 * (len(S5_SHIFTS) + 1) * SUB
S5_CHUNKS = S5_LANES // 128
S5_IN_PIECES = 4
S5_IN_K = S5_WIDTH // S5_IN_PIECES
S5_IN_N = S5_LANES // S5_IN_PIECES
S5_OUT_PIECES = 2
S5_OUT_K = S5_LANES // S5_OUT_PIECES
S5_OUT_N = S5_WIDTH // S5_OUT_PIECES


def _s5_tab(tab_ref, k, lanes):
    return (tab_ref[2 * k * SUB:(2 * k + 1) * SUB, lanes], tab_ref[(2 * k + 1) * SUB:(2 * k + 2) * SUB, lanes])


def _s5_drive(ub, bb_ref, p):
    return _dot(ub[:, p * S5_IN_K:(p + 1) * S5_IN_K], bb_ref[p])


def _s5_readout(h_piece, u, cre_ref, cim_ref, d_ref, wg_ref, bg_ref):
    ys = []
    for j in range(S5_OUT_PIECES):
        hr, hi = h_piece(j)
        ys.append(_dot(hr.astype(BF16), cre_ref[j]) - _dot(hi.astype(BF16), cim_ref[j]))
    y = jnp.concatenate(ys, axis=1) + d_ref[...] * u
    z = jax.nn.gelu(y)
    gate = jax.nn.sigmoid(_dot(z.astype(BF16), wg_ref[...]) + bg_ref[...])
    return (z * gate).astype(BF16)


def _s5_block(u_ref, bb_ref, cre_ref, cim_ref, d_ref, wg_ref, bg_ref, ab_ref, o_ref, hpre_ref, hpim_ref):
    u = u_ref[...]
    ub = u.astype(BF16)
    drive = [_s5_drive(ub, bb_ref, p) for p in range(S5_IN_PIECES)]
    x3r = jnp.concatenate([d[:, 0:S5_IN_N] for d in drive], axis=1).reshape(BLK, S5_CHUNKS, 128)
    x3i = jnp.concatenate([d[:, S5_IN_N:] for d in drive], axis=1).reshape(BLK, S5_CHUNKS, 128)
    ar, ai = ab_ref[0:S5_CHUNKS, :], ab_ref[S5_CHUNKS:, :]
    hr, hi = hpre_ref[...], hpim_ref[...]
    hs_r, hs_i = [], []
    for t in range(BLK):
        hr, hi = ar * hr - ai * hi + x3r[t], ar * hi + ai * hr + x3i[t]
        hs_r.append(hr)
        hs_i.append(hi)
    hpre_ref[...] = hr
    hpim_ref[...] = hi
    h2r = jnp.stack(hs_r, axis=0).reshape(BLK, S5_LANES)
    h2i = jnp.stack(hs_i, axis=0).reshape(BLK, S5_LANES)

    def piece(j):
        return h2r[:, j * S5_OUT_K:(j + 1) * S5_OUT_K], h2i[:, j * S5_OUT_K:(j + 1) * S5_OUT_K]

    o_ref[:, M_B:M_B + S5_WIDTH] = _s5_readout(piece, u, cre_ref, cim_ref, d_ref, wg_ref, bg_ref)


def _s5_sample_kernel(u_ref, bb_ref, cre_ref, cim_ref, d_ref, wg_ref, bg_ref, tab_ref, h0re_ref, h0im_ref,
                      o_ref, hsre_ref, hsim_ref):
    u = u_ref[...]
    ub = u.astype(BF16)
    for p in range(S5_IN_PIECES):
        lanes = slice(p * S5_IN_N, (p + 1) * S5_IN_N)
        bu = _s5_drive(ub, bb_ref, p)
        pr, pi = _s5_tab(tab_ref, len(S5_SHIFTS), lanes)
        ar, ai = pr[0:1, :], pi[0:1, :]
        h0r, h0i = h0re_ref[:, lanes], h0im_ref[:, lanes]
        hsre_ref[:, lanes] = ar * h0r - ai * h0i + bu[:, 0:S5_IN_N]
        hsim_ref[:, lanes] = ar * h0i + ai * h0r + bu[:, S5_IN_N:]
    piece = lambda j: (hsre_ref[:, j * S5_OUT_K:(j + 1) * S5_OUT_K], hsim_ref[:, j * S5_OUT_K:(j + 1) * S5_OUT_K])
    o_ref[:, M_B:M_B + S5_WIDTH] = _s5_readout(piece, u, cre_ref, cim_ref, d_ref, wg_ref, bg_ref)


def _s5_param_specs(l, prompt):
    return [
        _layer_spec((S5_IN_PIECES, S5_IN_K, 2 * S5_IN_N), l),
        _layer_spec((S5_OUT_PIECES, S5_OUT_K, S5_OUT_N), l), _layer_spec((S5_OUT_PIECES, S5_OUT_K, S5_OUT_N), l),
        _layer_spec((1, S5_WIDTH), l), _layer_spec((S5_WIDTH, S5_WIDTH), l), _layer_spec((1, S5_WIDTH), l),
        _layer_spec((2 * S5_CHUNKS, 128), l) if prompt else _layer_spec((S5_NTAB, S5_LANES), l),
    ]


def _s5_params(prm, prompt):
    return (prm["bb"], prm["c_re"], prm["c_im"], prm["d"], prm["w_glu"], prm["b_glu"],
            prm["ab"] if prompt else prm["tab"])


def _gla_log_gate(ac, wa_ref, ba_ref):
    return _log_sigmoid(_dot(ac.astype(BF16), wa_ref[...]) + ba_ref[...]) * (1.0 / GLA_TAU)


def _gla_finish(o_heads, gate, ng):
    outs = [o * lax.rsqrt(jnp.mean(o * o, -1, keepdims=True) + LN_EPS) for o in o_heads]
    return jnp.concatenate(outs, axis=1) * ng * (gate * jax.nn.sigmoid(gate))


def _gla_block(q_ref, k_ref, v_ref, gate_ref, ac_ref, wa_ref, ba_ref, ng_ref, o_ref, s_ref):
    lg = _gla_log_gate(ac_ref[...], wa_ref, ba_ref)
    L = GLA_CHUNK
    row = lax.broadcasted_iota(jnp.int32, (L, GLA_KW), 0)
    causal = lax.broadcasted_iota(jnp.int32, (L, L), 0) >= lax.broadcasted_iota(jnp.int32, (L, L), 1)
    chunks = range(BLK // L)
    heads = range(GLA_HEADS)
    ks = lambda hh: slice(hh * GLA_DK, (hh + 1) * GLA_DK)
    vs = lambda hh: slice(hh * GLA_DV, (hh + 1) * GLA_DV)
    q_in, k_out, k_dec, dec, vb = [], [], [], [], []
    for c in chunks:
        rows = slice(c * L, (c + 1) * L)
        b = lg[rows, :]
        d = 1
        while d < L:
            b = b + jnp.where(row >= d, pltpu.roll(b, d, 0), 0.0)
            d *= 2
        b_end = b[L - 1:L, :]
        kc = k_ref[rows, :]
        q_in.append((q_ref[rows, :] * GLA_DK ** -0.5 * jnp.exp(b)).astype(BF16))
        k_out.append((kc * jnp.exp(-b)).astype(BF16))
        k_dec.append((kc * jnp.exp(b_end - b)).astype(BF16))
        dec.append(jnp.transpose(jnp.broadcast_to(jnp.exp(b_end), (GLA_DV, GLA_KW))))
        vb.append(v_ref[rows, :].astype(BF16))
    att = [[jnp.where(causal, _dot_nt(q_in[c][:, ks(hh)], k_out[c][:, ks(hh)]), 0.0).astype(BF16) for hh in heads]
           for c in chunks]
    kv = [[_dot_tn(k_dec[c][:, ks(hh)], vb[c][:, vs(hh)]) for hh in heads] for c in chunks]
    state = [s_ref[...]]
    for c in chunks:
        state.append(dec[c] * state[c] + jnp.concatenate(kv[c], axis=0))
    s_ref[...] = state[-1]
    for c in chunks:
        sb = state[c].astype(BF16)
        o_heads = [_dot(q_in[c][:, ks(hh)], sb[ks(hh), :]) + _dot(att[c][hh], vb[c][:, vs(hh)]) for hh in heads]
        o_ref[c * L:(c + 1) * L, M_C:M_C + GLA_WIDTH] = _gla_finish(
            o_heads, gate_ref[c * L:(c + 1) * L, :], ng_ref[...]).astype(BF16)


def _gla_param_specs(l):
    return [_layer_spec((128, GLA_KW), l), _layer_spec((1, GLA_KW), l), _layer_spec((1, GLA_WIDTH), l)]


GLA_BT = 16


def _gla_sample_kernel(q_ref, k_ref, v_ref, gate_ref, ac_ref, wa_ref, ba_ref, ng_ref, s_ref, o_ref, so_ref):
    bt = GLA_BT
    lg = _gla_log_gate(ac_ref[...], wa_ref, ba_ref)
    v = v_ref[...]

    def cols(x):
        return jnp.transpose(jnp.concatenate([x, jnp.zeros((128 - bt, GLA_KW), F32)], axis=0))

    eg_t, k_t, q_t = cols(jnp.exp(lg)), cols(k_ref[...]), cols(q_ref[...] * GLA_DK ** -0.5)
    o_rows = []
    for j in range(bt):
        bc = lambda t: jnp.broadcast_to(t[:, j:j + 1], (GLA_KW, GLA_DV))
        vrow = jnp.concatenate(
            [jnp.broadcast_to(v[j:j + 1, hh * GLA_DV:(hh + 1) * GLA_DV], (GLA_DK, GLA_DV)) for hh in range(GLA_HEADS)],
            axis=0)
        s_new = bc(eg_t) * s_ref[j] + bc(k_t) * vrow
        so_ref[j] = s_new
        qs = bc(q_t) * s_new
        o_rows.append(jnp.concatenate(
            [jnp.sum(qs[hh * GLA_DK:(hh + 1) * GLA_DK, :], axis=0, keepdims=True) for hh in range(GLA_HEADS)], axis=1))
    o = jnp.concatenate(o_rows, axis=0)
    o_heads = [o[:, hh * GLA_DV:(hh + 1) * GLA_DV] for hh in range(GLA_HEADS)]
    o_ref[:, M_C:M_C + GLA_WIDTH] = _gla_finish(o_heads, gate_ref[...], ng_ref[...]).astype(BF16)


assert SWA_BT == GLA_BT
SAMPLE_BT = SWA_BT


def _mix_sample_kernel(l, sink_ref, q_ref, kn_ref, vn_ref, ck_ref, cv_ref,
                       u_ref, bb_ref, cre_ref, cim_ref, d_ref, wg_ref, bg_ref, tab_ref, h0re_ref, h0im_ref,
                       gq_ref, gk_ref, gv_ref, gg_ref, ga_ref, wa_ref, ba_ref, ng_ref, s_ref,
                       o_ref, cko_ref, cvo_ref, hsre_ref, hsim_ref, so_ref):
    if l == 0:
        for r in (cko_ref, cvo_ref, so_ref):
            r[1:] = jnp.zeros((r.shape[0] - 1,) + r.shape[1:], F32)
        cko_ref, cvo_ref, so_ref = cko_ref.at[0], cvo_ref.at[0], so_ref.at[0]
    _swa_sample_kernel(l, sink_ref, q_ref, kn_ref, vn_ref, ck_ref, cv_ref, o_ref, cko_ref, cvo_ref)
    _s5_sample_kernel(u_ref, bb_ref, cre_ref, cim_ref, d_ref, wg_ref, bg_ref, tab_ref, h0re_ref, h0im_ref,
                      o_ref, hsre_ref, hsim_ref)
    _gla_sample_kernel(gq_ref, gk_ref, gv_ref, gg_ref, ga_ref, wa_ref, ba_ref, ng_ref, s_ref, o_ref, so_ref)


def _mix_sample(sink, h, cache_k, cache_v, s5_prm, h0re, h0im, gla_prm, state, mix, ck_all, cv_all, s_all, l):
    bt = SAMPLE_BT
    r0 = ROWS_P // bt
    blk = lambda c, w: pl.BlockSpec((bt, w), lambda s: (r0 + s, c // w))
    cache = pl.BlockSpec((None, bt, A_KV_HEADS, HEAD_DIM, WINDOW), lambda s: (l, s, 0, 0, 0))
    h0 = pl.BlockSpec((None, bt, S5_LANES), lambda s: (l, s, 0))
    hs = pl.BlockSpec((bt, S5_LANES), lambda s: (s, 0))
    st = pl.BlockSpec((None, bt, GLA_KW, GLA_DV), lambda s: (l, s, 0, 0))
    in_specs = ([pl.BlockSpec(memory_space=pltpu.SMEM), blk(C_Q, A_WIDTH), blk(C_K, KV_WIDTH), blk(C_V, KV_WIDTH),
                 cache, cache, blk(C_U, S5_WIDTH)]
                + _s5_param_specs(l, False) + [h0, h0]
                + [blk(C_GQ, GLA_KW), blk(C_GK, GLA_KW), blk(C_GV, GLA_WIDTH), blk(C_GG, GLA_WIDTH), blk(C_GA, 128)]
                + _gla_param_specs(l) + [st])
    n_in = len(in_specs)
    hs_shape = jax.ShapeDtypeStruct((N_SAMPLE, S5_LANES), F32)
    if l == 0:
        all_layers = lambda shape: pl.BlockSpec((DEPTH, bt) + shape, lambda s: (0, s) + (0,) * len(shape))
        cache_o, st_o = all_layers((A_KV_HEADS, HEAD_DIM, WINDOW)), all_layers((GLA_KW, GLA_DV))
        aliased, aliases = [mix], {n_in: 0}
    else:
        cache_o, st_o = cache, st
        aliased, aliases = [mix, ck_all, cv_all, s_all], {n_in: 0, n_in + 1: 1, n_in + 2: 2, n_in + 3: 5}
    return pl.pallas_call(
        _drop_inputs(functools.partial(_mix_sample_kernel, l), n_in, len(aliased)),
        grid=(N_SAMPLE // bt,),
        in_specs=in_specs + [_any()] * len(aliased),
        out_specs=[blk(0, D_MODEL), cache_o, cache_o, hs, hs, st_o],
        out_shape=[jax.ShapeDtypeStruct((ROWS, D_MODEL), BF16),
                   jax.ShapeDtypeStruct(cache_k.shape, F32), jax.ShapeDtypeStruct(cache_v.shape, F32),
                   hs_shape, hs_shape, jax.ShapeDtypeStruct(state.shape, F32)],
        input_output_aliases=aliases,
        compiler_params=_cparams("arbitrary"),
        name="mix_sample",
    )(sink, h, h, h, cache_k, cache_v, h, *_s5_params(s5_prm, False), h0re, h0im,
      h, h, h, h, h, *gla_prm, state, *aliased)


def _mix_prompt_kernel(l, n_cast, sink_ref, bias_ref, q_ref, kp_ref, kc_ref, vp_ref, vc_ref, u_ref,
                       gq_ref, gk_ref, gv_ref, gg_ref, ga_ref,
                       bb_ref, cre_ref, cim_ref, d_ref, wg_ref, bg_ref, ab_ref, wa_ref, ba_ref, ng_ref, *rest):
    w_refs, (o_ref, hpre_ref, hpim_ref, s_ref) = rest[:n_cast], rest[n_cast:n_cast + 4]
    c_refs = rest[n_cast + 4:]
    s_blk = pl.program_id(0)

    @pl.when(s_blk == 0)
    def _():
        hpre_ref[...] = jnp.zeros_like(hpre_ref)
        hpim_ref[...] = jnp.zeros_like(hpim_ref)
        s_ref[...] = jnp.zeros_like(s_ref)

    for w_ref, c_ref in zip(w_refs, c_refs):
        c_ref[...] = w_ref[...].astype(BF16)
    _s5_block(u_ref, bb_ref, cre_ref, cim_ref, d_ref, wg_ref, bg_ref, ab_ref, o_ref, hpre_ref, hpim_ref)
    _swa_block(l, sink_ref, bias_ref, q_ref, kp_ref, kc_ref, vp_ref, vc_ref, o_ref)
    _gla_block(gq_ref, gk_ref, gv_ref, gg_ref, ga_ref, wa_ref, ba_ref, ng_ref, o_ref, s_ref)


def _mix_prompt(sink, h, s5_prm, gla_prm, w_casts, mix, l):
    cur = _seq_block
    prev = lambda s: _seq_block(jnp.maximum(s - 1, 0))
    blk = lambda c, w, at=cur: pl.BlockSpec((BLK, w), lambda s: (at(s), c // w))
    const = lambda shape: pl.BlockSpec(shape, lambda s: (0,) * len(shape))
    casts = [_cast_specs(w, rows, layer) for w, rows, layer in w_casts]
    in_specs = ([pl.BlockSpec(memory_space=pltpu.SMEM),
                 pl.BlockSpec((None, BLK, 2 * BLK), lambda s: (jnp.minimum(s, 2), 0, 0)), blk(C_Q, A_WIDTH),
                 blk(C_K, KV_WIDTH, prev), blk(C_K, KV_WIDTH), blk(C_V, KV_WIDTH, prev), blk(C_V, KV_WIDTH),
                 blk(C_U, S5_WIDTH),
                 blk(C_GQ, GLA_KW), blk(C_GK, GLA_KW), blk(C_GV, GLA_WIDTH), blk(C_GG, GLA_WIDTH), blk(C_GA, 128)]
                + _s5_param_specs(l, True) + _gla_param_specs(l) + [c[0] for c in casts])
    n_in = len(in_specs)
    return pl.pallas_call(
        _drop_inputs(functools.partial(_mix_prompt_kernel, l, len(casts)), n_in, 1),
        grid=(N_PBLK,),
        in_specs=in_specs + [_any()],
        out_specs=[pl.BlockSpec((BLK, D_MODEL), lambda s: (cur(s), 0)),
                   const((S5_CHUNKS, 128)), const((S5_CHUNKS, 128)), const((GLA_KW, GLA_DV))] + [c[1] for c in casts],
        out_shape=[jax.ShapeDtypeStruct((ROWS, D_MODEL), BF16),
                   jax.ShapeDtypeStruct((S5_CHUNKS, 128), F32), jax.ShapeDtypeStruct((S5_CHUNKS, 128), F32),
                   jax.ShapeDtypeStruct((GLA_KW, GLA_DV), F32)] + [c[2] for c in casts],
        input_output_aliases={n_in: 0},
        compiler_params=_cparams("arbitrary"),
        name="mix_prompt",
    )(sink, _swa_bias(), *([h] * 11), *_s5_params(s5_prm, True), *gla_prm, *[w for w, _, _ in w_casts], mix)


OUT_TM = 768
OUT_PARTS = 3
LN_ROWS = 16


def _out_proj_kernel(mix_ref, w_ref, x_ref, g_ref, b_ref, o_ref):
    i = pl.program_id(0)
    g, b = g_ref[...], b_ref[...]
    part = OUT_TM // OUT_PARTS
    rows = [slice(k * part, (k + 1) * part) for k in range(OUT_PARTS)]
    def norm(k, acc):
        for c in range(part // LN_ROWS):
            r0 = k * part + c * LN_ROWS
            y = _layer_norm_rows(ALPHA * x_ref[r0:r0 + LN_ROWS, :] + acc[c * LN_ROWS:(c + 1) * LN_ROWS, :], g, b)
            o_ref[r0:r0 + LN_ROWS, :] = _zero_pad_rows(y, i * OUT_TM + r0)

    prev = None
    for k, r in enumerate(rows):
        acc = _dot(mix_ref[r, :], w_ref[...])
        if prev is not None:
            norm(k - 1, prev)
        prev = acc
    norm(OUT_PARTS - 1, prev)


def _out_proj(mix, w, x, g, b, l):
    tm = OUT_TM
    return pl.pallas_call(
        _out_proj_kernel,
        grid=(ROWS // tm,),
        in_specs=[
            pl.BlockSpec((tm, D_MODEL), lambda i: (i, 0)),
            pl.BlockSpec((D_MODEL, D_MODEL), lambda i: (0, 0), pipeline_mode=pl.Buffered(1)),
            pl.BlockSpec((tm, D_MODEL), lambda i: (i, 0)),
            _layer_spec((1, D_MODEL), l), _layer_spec((1, D_MODEL), l),
        ],
        out_specs=pl.BlockSpec((tm, D_MODEL), lambda i: (i, 0)),
        out_shape=jax.ShapeDtypeStruct((ROWS, D_MODEL), F32),
        compiler_params=_cparams("arbitrary"),
        name="out_proj",
    )(mix, w, x, g, b)


FFN_TM = 768
FFN_TF = 512
FFN_HALO = 16
FFN_NI = ROWS // FFN_TM
assert ROWS % FFN_TM == 0 and FFN_TM >= N_SAMPLE + BLK + SUB and META_ROW % FFN_HALO == 0


def _ffn_kernel(final, x_ref, xh_ref, wu_ref, wg_ref, cw_ref, cb_ref, wd_ref, p0_ref, p1_ref, g_ref, b_ref,
                o_ref, *rest):
    if final:
        ys_ref, ut_ref, us_ref, xb_ref, a_ref = rest
    else:
        ut_ref, us_ref, xb_ref, a_ref = rest
    i, f = pl.program_id(0), pl.program_id(1)
    tm, hl = FFN_TM, FFN_HALO
    last_i = FFN_NI - 1

    @pl.when(f == 0)
    def _():
        xb_ref[...] = x_ref[...].astype(BF16)
        o_ref[...] = jnp.zeros_like(o_ref)

    cw0, cw1, cw2, cb = cw_ref[0:1, :], cw_ref[1:2, :], cw_ref[2:3, :], cb_ref[...]

    def act(um2, um1, u0, g0):
        return (jax.nn.gelu(cb + cw0 * um2 + cw1 * um1 + cw2 * u0) * g0).astype(BF16)

    xb = xb_ref[...]
    u = _dot(xb, wu_ref[...])
    gte = _dot(xb, wg_ref[...])
    um2, um1 = pltpu.roll(u, 2, 0), pltpu.roll(u, 1, 0)
    body = slice(hl, tm - N_SAMPLE)
    a_ref[body, :] = act(um2[body, :], um1[body, :], u[body, :], gte[body, :])
    ext = jnp.concatenate([_dot(xh_ref[...].astype(BF16), wu_ref[...]), u[0:hl, :]], axis=0)
    a_ref[0:hl, :] = act(ext[hl - 2:2 * hl - 2, :], ext[hl - 1:2 * hl - 1, :], u[0:hl, :], gte[0:hl, :])
    tail = slice(tm - N_SAMPLE, tm)
    samp = i == last_i
    a_ref[tail, :] = act(jnp.where(samp, p0_ref[...], um2[tail, :]), jnp.where(samp, p1_ref[...], um1[tail, :]),
                         u[tail, :], gte[tail, :])
    o_ref[...] += _dot(a_ref[...], wd_ref[...])
    ut_ref[...] = u[tm - N_SAMPLE - BLK - SUB:tm - N_SAMPLE - BLK, :]
    us_ref[0] = p1_ref[...]
    us_ref[1] = u[tm - N_SAMPLE:, :]

    @pl.when(f == pl.num_programs(1) - 1)
    def _():
        _deepnorm_ln(o_ref, x_ref, g_ref, b_ref, o_ref, i * tm, tm)
        if final:
            ys_ref[...] = o_ref[tm - N_SAMPLE:, :]


def _ffn(x, wup, cw, cb, wd, conv_p0, conv_p1, g, b, l, final):
    tm, tf = FFN_TM, FFN_TF
    nf = D_FF // tf
    tail = lambda i, f: jnp.where(i == FFN_NI - 1, f, 0)
    halo = lambda i, f: (jnp.where(i == 0, ROWS_P // FFN_HALO, i * (tm // FFN_HALO)) - 1, 0)
    y_spec = pl.BlockSpec((tm, D_MODEL), lambda i, f: (i, 0))
    y_specs = [y_spec, pl.BlockSpec((N_SAMPLE, D_MODEL), lambda i, f: (0, 0))] if final else [y_spec]
    y_shapes = ([jax.ShapeDtypeStruct((SEQ, D_MODEL), F32), jax.ShapeDtypeStruct((N_SAMPLE, D_MODEL), F32)]
                if final else [jax.ShapeDtypeStruct((ROWS, D_MODEL), F32)])
    return pl.pallas_call(
        functools.partial(_ffn_kernel, final),
        grid=(FFN_NI, nf),
        in_specs=[
            pl.BlockSpec((tm, D_MODEL), lambda i, f: (i, 0)),
            pl.BlockSpec((FFN_HALO, D_MODEL), halo),
            pl.BlockSpec((D_MODEL, tf), lambda i, f: (0, f)),
            pl.BlockSpec((D_MODEL, tf), lambda i, f: (0, nf + f)),
            pl.BlockSpec((None, CONV_W, tf), lambda i, f: (l, 0, f)),
            pl.BlockSpec((None, 1, tf), lambda i, f: (l, 0, f)),
            pl.BlockSpec((tf, D_MODEL), lambda i, f: (f, 0)),
            pl.BlockSpec((None, N_SAMPLE, tf), lambda i, f: (l, 0, tail(i, f))),
            pl.BlockSpec((None, N_SAMPLE, tf), lambda i, f: (l, 0, tail(i, f))),
            _layer_spec((1, D_MODEL), l), _layer_spec((1, D_MODEL), l),
        ],
        out_specs=y_specs + [
            pl.BlockSpec((SUB, tf), lambda i, f: (0, tail(i, f))),
            pl.BlockSpec((2, N_SAMPLE, tf), lambda i, f: (0, 0, tail(i, f))),
        ],
        out_shape=y_shapes + [
            jax.ShapeDtypeStruct((SUB, D_FF), F32),
            jax.ShapeDtypeStruct((2, N_SAMPLE, D_FF), F32),
        ],
        scratch_shapes=[pltpu.VMEM((tm, D_MODEL), BF16), pltpu.VMEM((tm, tf), BF16)],
        compiler_params=_cparams("arbitrary", "arbitrary"),
        name="conv_ffn",
    )(x, x, wup, wup, cw, cb, wd, conv_p0, conv_p1, g, b)


def _rope_table():
    inv = jnp.power(ROPE_THETA, -jnp.arange(ROPE_HALF, dtype=F32) / ROPE_HALF)
    pos = jnp.concatenate([jnp.arange(SEQ, dtype=jnp.int32) + N_META, jnp.zeros((LEAD,), jnp.int32),
                           jnp.arange(N_META, dtype=jnp.int32), jnp.full((N_SAMPLE,), PAST_LEN, jnp.int32)])
    ang = pos.astype(F32)[:, None] * inv[None, :]
    return jnp.concatenate([jnp.cos(ang), jnp.sin(ang)], axis=1)


def _s5_tables(lam_re, lam_im, log_dt, b_re, b_im, c_re, c_im, d, w_glu, b_glu):
    L = lam_re.shape[0]
    dt = jnp.exp(log_dt)[..., None]
    mag = jnp.exp(lam_re * dt)
    ab_re, ab_im = mag * jnp.cos(lam_im * dt), mag * jnp.sin(lam_im * dt)
    den = lam_re * lam_re + lam_im * lam_im
    nr, ni = ab_re - 1.0, ab_im
    f_re = (nr * lam_re + ni * lam_im) / den
    f_im = (ni * lam_re - nr * lam_im) / den
    bb_re = f_re[..., None] * b_re - f_im[..., None] * b_im
    bb_im = f_re[..., None] * b_im + f_im[..., None] * b_re
    gi, go = S5_GROUPS // S5_IN_PIECES, S5_GROUPS // S5_OUT_PIECES

    def bd_in(t):
        t = t.reshape(L, S5_IN_PIECES, gi, S5_STATE, S5_CH)
        return jnp.einsum('lqgph,gk->lqghkp', t, jnp.eye(gi, dtype=F32)).reshape(L, S5_IN_PIECES, S5_IN_K, S5_IN_N)

    def bd_out(t):
        t = t.reshape(L, S5_OUT_PIECES, go, S5_CH, S5_STATE)
        return jnp.einsum('lqghp,gk->lqgpkh', t, jnp.eye(go, dtype=F32)).reshape(
            L, S5_OUT_PIECES, S5_OUT_K, S5_OUT_N)

    ar, ai = ab_re.reshape(L, 1, S5_LANES), ab_im.reshape(L, 1, S5_LANES)
    pows = [(ar, ai)]
    for _ in range(SUB - 1):
        pr, pi = pows[-1]
        pows.append((pr * ar - pi * ai, pr * ai + pi * ar))
    row = jnp.arange(SUB)[None, :, None]
    parts = []
    for sft in S5_SHIFTS:
        pr, pi = pows[sft - 1]
        parts += [jnp.where(row >= sft, pr, 0.0), jnp.where(row >= sft, pi, 0.0)]
    parts += [jnp.concatenate([p[0] for p in pows], 1), jnp.concatenate([p[1] for p in pows], 1)]
    return dict(bb=jnp.concatenate([bd_in(bb_re), bd_in(bb_im)], axis=3).astype(BF16),
                c_re=bd_out(c_re).astype(BF16), c_im=bd_out(c_im).astype(BF16),
                d=d.reshape(L, 1, S5_WIDTH), w_glu=w_glu.astype(BF16), b_glu=b_glu.reshape(L, 1, S5_WIDTH),
                tab=jnp.concatenate(parts, 1),
                ab=jnp.concatenate([ab_re.reshape(L, S5_CHUNKS, 128), ab_im.reshape(L, S5_CHUNKS, 128)], axis=1))


@jax.jit
def kernel(x_prompt, x_sample, cache_swa_k, cache_swa_v, state_ssm_re, state_ssm_im, state_gla, state_conv,
           meta_tokens, ln_in_g, ln_in_b, w_in, attn_sink, s5_lam_re, s5_lam_im, s5_log_dt, s5_b_re, s5_b_im,
           s5_c_re, s5_c_im, s5_d, s5_w_glu, s5_b_glu, gla_w_a2, gla_b_a, gla_norm_g, w_out, ln1_g, ln1_b,
           ffn_w_up, ffn_conv_w, ffn_conv_b, ffn_w_down, ln2_g, ln2_b):
    L = DEPTH
    row3 = lambda t: t.reshape(L, 1, -1)
    wa_b = jnp.pad(gla_w_a2, ((0, 0), (0, 128 - GLA_LOWRANK), (0, 0))).astype(BF16)
    gla_args = (wa_b, row3(gla_b_a), row3(gla_norm_g))
    ln1, ln2 = (row3(ln1_g), row3(ln1_b)), (row3(ln2_g), row3(ln2_b))
    conv_b3 = row3(ffn_conv_b)
    cs = _rope_table()
    s5p = _s5_tables(s5_lam_re, s5_lam_im, s5_log_dt, s5_b_re, s5_b_im, s5_c_re, s5_c_im, s5_d, s5_w_glu, s5_b_glu)
    cache_t = lambda c: jnp.transpose(c, (0, 1, 3, 4, 2))
    ck, cv = cache_t(cache_swa_k), cache_t(cache_swa_v)
    ck_all = cv_all = s_all = None
    h0re = state_ssm_re.reshape(L, N_SAMPLE, S5_LANES)
    h0im = state_ssm_im.reshape(L, N_SAMPLE, S5_LANES)
    sg = state_gla.reshape(L, N_SAMPLE, GLA_KW, GLA_DV)
    conv_p0, conv_p1 = state_conv[:, :, 0, :], state_conv[:, :, 1, :]
    mix = jnp.zeros((ROWS, D_MODEL), BF16)
    w_in_b = jnp.swapaxes(w_in, 1, 2).astype(BF16)

    x = _ln_in(x_prompt.reshape(SEQ, D_MODEL), meta_tokens, x_sample.reshape(N_SAMPLE, D_MODEL),
               ln_in_g.reshape(1, D_MODEL), ln_in_b.reshape(1, D_MODEL))
    outs = [[] for _ in range(9)]
    for l in range(L):
        h = _in_proj(x, w_in_b, cs, l)
        casts = [(ffn_w_up, 32, l), (ffn_w_down, 176, l), (w_out, 32, l)]
        mix, hp_re, hp_im, s_p, w_up_b, w_down_b, w_out_b = _mix_prompt(
            attn_sink, h, s5p, gla_args, casts, mix, l)
        mix, ck_all, cv_all, hs_re, hs_im, s_all = _mix_sample(
            attn_sink, h, ck, cv, s5p, h0re, h0im, gla_args, sg, mix, ck_all, cv_all, s_all, l)
        x = _out_proj(mix, w_out_b, x, *ln1, l)
        res = _ffn(x, w_up_b, ffn_conv_w, conv_b3, w_down_b, conv_p0, conv_p1, *ln2, l, l == L - 1)
        if l == L - 1:
            y_prompt, y_sample, u_tail, u_s = res
        else:
            x, u_tail, u_s = res
        kv_p = lambda c: h[SEQ - WINDOW:SEQ, c:c + KV_WIDTH].reshape(1, WINDOW, A_KV_HEADS, HEAD_DIM)
        new = (kv_p(C_K), kv_p(C_V),
               hp_re.reshape(1, S5_GROUPS, S5_STATE), hp_im.reshape(1, S5_GROUPS, S5_STATE),
               s_p.reshape(1, GLA_HEADS, GLA_DK, GLA_DV), u_tail[SUB - 2:SUB][None],
               hs_re.reshape(N_SAMPLE, S5_GROUPS, S5_STATE), hs_im.reshape(N_SAMPLE, S5_GROUPS, S5_STATE),
               jnp.transpose(u_s, (1, 0, 2)))
        for lst, val in zip(outs, new):
            lst.append(val)
    st = [jnp.stack(lst) for lst in outs]
    return (y_prompt.reshape(1, SEQ, D_MODEL), y_sample.reshape(N_SAMPLE, 1, D_MODEL),
            st[0], st[1], st[2], st[3], st[4], st[5],
            jnp.transpose(ck_all, (0, 1, 4, 2, 3)), jnp.transpose(cv_all, (0, 1, 4, 2, 3)),
            st[6], st[7], s_all.reshape(L, N_SAMPLE, GLA_HEADS, GLA_DK, GLA_DV), st[8])
```

```python
import functools

import jax
import jax.numpy as jnp
from jax import lax
from jax.experimental import pallas as pl
from jax.experimental.pallas import tpu as pltpu

F32 = jnp.float32
BF16 = jnp.bfloat16

D_MODEL = 2048
SEQ = 8192
DEPTH = 2
N_SAMPLE = 128
N_META = 16
HEAD_DIM = 64
A_WIDTH = 1024
A_HEADS = 16
A_KV_HEADS = 4
GQA = 4
KV_WIDTH = A_KV_HEADS * HEAD_DIM
WINDOW = 128
ROPE_DIM = 16
ROPE_HALF = ROPE_DIM // 2
ROPE_THETA = 500000.0
PAST_LEN = 8192
S5_WIDTH = 512
S5_CH = 16
S5_GROUPS = 32
S5_STATE = 64
S5_LANES = S5_GROUPS * S5_STATE
GLA_WIDTH = 512
GLA_HEADS = 4
GLA_DV = 128
GLA_DK = 64
GLA_KW = GLA_HEADS * GLA_DK
GLA_LOWRANK = 16
GLA_TAU = 16.0
GLA_CHUNK = 64
D_FF = 5632
CONV_W = 3
LN_EPS = 1e-5
ALPHA = (2 * DEPTH) ** 0.25
NEG_INF = -1e30

BLK = 128
T_PROMPT = N_META + SEQ
LEAD = (-T_PROMPT) % BLK
N_PBLK = (LEAD + T_PROMPT) // BLK
ROWS_P = N_PBLK * BLK
ROWS = ROWS_P + N_SAMPLE
META_ROW = SEQ + LEAD
SUB = 8

C_Q, C_K, C_V, C_U, C_GQ, C_GK, C_GV, C_GG, C_GA = 0, 1024, 1280, 1536, 2048, 2304, 2560, 3072, 3584
IN_COLS = 3600
IN_TILE = 1280
IN_PAD = 3 * IN_TILE
ROPE_COLS = A_WIDTH + KV_WIDTH
M_A, M_B, M_C = 0, A_WIDTH, A_WIDTH + S5_WIDTH

VMEM_LIMIT = 56 * 1024 * 1024


def _cparams(*sem):
    return pltpu.CompilerParams(dimension_semantics=sem, vmem_limit_bytes=VMEM_LIMIT)


def _any():
    return pl.BlockSpec(memory_space=pl.ANY)


def _drop_inputs(kern, start, count):
    def wrapped(*refs):
        return kern(*refs[:start], *refs[start + count:])
    return wrapped


def _seq_block(s):
    return jnp.where(s == 0, N_PBLK - 1, s - 1)


def _layer_spec(shape, l):
    return pl.BlockSpec((None,) + shape, lambda *_: (l,) + (0,) * len(shape))


def _cast_specs(w, rows, l):
    n, cols = w.shape[1] // rows, w.shape[2]
    chunk = lambda s: jnp.minimum(s, n - 1)
    return (pl.BlockSpec((None, rows, cols), lambda s: (l, chunk(s), 0)),
            pl.BlockSpec((rows, cols), lambda s: (chunk(s), 0)),
            jax.ShapeDtypeStruct(w.shape[1:], BF16))


def _layer_norm_rows(x, g, b):
    mu = jnp.mean(x, -1, keepdims=True)
    xc = x - mu
    var = jnp.mean(xc * xc, -1, keepdims=True)
    return xc * lax.rsqrt(var + LN_EPS) * g + b


def _zero_pad_rows(y, first_row):
    row = first_row + lax.broadcasted_iota(jnp.int32, y.shape, 0)
    return jnp.where((row >= SEQ) & (row < META_ROW), 0.0, y)


def _deepnorm_ln(acc_ref, x_ref, g_ref, b_ref, o_ref, first_row, rows):
    g, b = g_ref[...], b_ref[...]
    for c in range(rows // BLK):
        r = slice(c * BLK, (c + 1) * BLK)
        y = _layer_norm_rows(ALPHA * x_ref[r, :] + acc_ref[r, :], g, b)
        o_ref[r, :] = _zero_pad_rows(y, first_row + c * BLK)


def _dot(a, b):
    return jnp.dot(a, b, preferred_element_type=F32)


def _dot_nt(a, b):
    return lax.dot_general(a, b, (((1,), (1,)), ((), ())), preferred_element_type=F32)


def _dot_tn(a, b):
    return lax.dot_general(a, b, (((0,), (0,)), ((), ())), preferred_element_type=F32)


def _log_sigmoid(x):
    return jnp.minimum(x, 0.0) - jnp.log(1.0 + jnp.exp(-jnp.abs(x)))


LN_IN_TM = 512
LN_IN_STEPS = SEQ // LN_IN_TM + 1
assert SEQ % LN_IN_TM == 0 and ROWS - SEQ <= LN_IN_TM


def _ln_in_kernel(xp_ref, meta_ref, xs_ref, g_ref, b_ref, wsrc_ref, o_ref, wdst_ref):
    i = pl.program_id(0)
    g, b = g_ref[...], b_ref[...]
    wdst_ref[...] = wsrc_ref[...].astype(BF16)

    @pl.when(i < LN_IN_STEPS - 1)
    def _():
        for c in range(LN_IN_TM // BLK):
            r = slice(c * BLK, (c + 1) * BLK)
            o_ref[r, :] = _layer_norm_rows(xp_ref[r, :], g, b)

    @pl.when(i == LN_IN_STEPS - 1)
    def _():
        o_ref[0:LEAD, :] = jnp.zeros((LEAD, D_MODEL), F32)
        o_ref[LEAD:BLK, :] = _layer_norm_rows(meta_ref[...], g, b)
        o_ref[BLK:BLK + N_SAMPLE, :] = _layer_norm_rows(xs_ref[...], g, b)


def _ln_in(xp, meta, xs, g, b, w_cast, w_rows):
    tm = LN_IN_TM
    const = lambda shape: pl.BlockSpec(shape, lambda i: (0,) * len(shape))
    w_in_spec, w_out_spec, w_shape = _cast_specs(w_cast, w_rows, 0)
    assert w_cast.shape[1] // w_rows <= LN_IN_STEPS
    return pl.pallas_call(
        _ln_in_kernel,
        grid=(LN_IN_STEPS,),
        in_specs=[
            pl.BlockSpec((tm, D_MODEL), lambda i: (jnp.minimum(i, SEQ // tm - 1), 0)),
            const((N_META, D_MODEL)), const((N_SAMPLE, D_MODEL)), const((1, D_MODEL)), const((1, D_MODEL)),
            w_in_spec,
        ],
        out_specs=[pl.BlockSpec((tm, D_MODEL), lambda i: (i, 0)), w_out_spec],
        out_shape=[jax.ShapeDtypeStruct((ROWS, D_MODEL), F32), w_shape],
        compiler_params=_cparams("arbitrary"),
        name="ln_in",
    )(xp, meta, xs, g, b, w_cast)


IN_TM = 1056


def _in_proj_kernel(x_ref, w_ref, cs_ref, o_ref, xb_ref):
    j = pl.program_id(1)

    last_j = IN_PAD // IN_TILE - 1

    @pl.when(j == 0)
    def _():
        xb = x_ref[...].astype(BF16)
        xb_ref[...] = xb
        c8, s8 = cs_ref[:, 0:ROPE_HALF], cs_ref[:, ROPE_HALF:ROPE_DIM]
        rest = HEAD_DIM - ROPE_DIM
        one, zero = jnp.ones((IN_TM, rest), F32), jnp.zeros((IN_TM, rest), F32)
        z8 = jnp.zeros((IN_TM, ROPE_HALF), F32)
        cos = jnp.concatenate([c8, c8, one] * 2, axis=1)
        sa = jnp.concatenate([-s8, z8, zero] * 2, axis=1)
        sb = jnp.concatenate([z8, s8, zero] * 2, axis=1)
        for c in range(ROPE_COLS // 256):
            acc = _dot_nt(xb, w_ref[c * 256:(c + 1) * 256, :])
            for hlf in range(2):
                blk = acc[:, hlf * 128:(hlf + 1) * 128]
                o_ref[:, c * 256 + hlf * 128:c * 256 + (hlf + 1) * 128] = (
                    blk * cos + pltpu.roll(blk, 128 - ROPE_HALF, 1) * sa + pltpu.roll(blk, ROPE_HALF, 1) * sb)

    @pl.when((j > 0) & (j < last_j))
    def _():
        o_ref[...] = _dot_nt(xb_ref[...], w_ref[...])

    @pl.when(j == last_j)
    def _():
        o_ref[...] = _dot_nt(xb_ref[...], w_ref[...])
        edge = (IN_COLS % IN_TILE) // 128 * 128
        col = edge + lax.broadcasted_iota(jnp.int32, (IN_TM, 128), 1)
        o_ref[:, edge:edge + 128] = jnp.where(col < IN_COLS % IN_TILE, o_ref[:, edge:edge + 128], 0.0)
        o_ref[:, edge + 128:] = jnp.zeros((IN_TM, IN_TILE - edge - 128), F32)


def _in_proj(x, w, cs, l):
    return pl.pallas_call(
        _in_proj_kernel,
        grid=(ROWS // IN_TM, IN_PAD // IN_TILE),
        in_specs=[
            pl.BlockSpec((IN_TM, D_MODEL), lambda i, j: (i, 0)),
            pl.BlockSpec((IN_TILE, D_MODEL), lambda i, j: (j, 0)),
            pl.BlockSpec((IN_TM, ROPE_DIM), lambda i, j: (i, 0)),
        ],
        out_specs=pl.BlockSpec((IN_TM, IN_TILE), lambda i, j: (i, j)),
        out_shape=jax.ShapeDtypeStruct((ROWS, IN_PAD), F32),
        scratch_shapes=[pltpu.VMEM((IN_TM, D_MODEL), BF16)],
        compiler_params=_cparams("arbitrary", "arbitrary"),
        name="in_proj",
    )(x, w, cs)


def _swa_bias():
    row = lax.broadcasted_iota(jnp.int32, (3, BLK, 2 * BLK), 1)
    col = lax.broadcasted_iota(jnp.int32, (3, BLK, 2 * BLK), 2)
    s_blk = lax.broadcasted_iota(jnp.int32, (3, BLK, 2 * BLK), 0)
    diff = row - col + BLK
    kpos = (s_blk - 1) * BLK + col - LEAD
    ok = (diff >= 0) & (diff <= WINDOW) & (kpos >= 0)
    return jnp.where(ok, 0.0, NEG_INF).astype(F32)


def _swa_block(l, sink_ref, bias_ref, q_ref, kp_ref, kc_ref, vp_ref, vc_ref, o_ref):
    k2 = jnp.concatenate([kp_ref[...], kc_ref[...]], axis=0).astype(BF16)
    v2 = jnp.concatenate([vp_ref[...], vc_ref[...]], axis=0).astype(BF16)
    nq = GQA * BLK
    bias = jnp.concatenate([bias_ref[...]] * GQA, axis=0)
    low = lax.broadcasted_iota(jnp.int32, (BLK, 2 * HEAD_DIM), 1) < HEAD_DIM
    g_of_row = lax.broadcasted_iota(jnp.int32, (nq, 1), 0) // BLK
    ones = jnp.ones((2 * BLK, 2 * HEAD_DIM), BF16)
    scores, sinks = [], []
    for kv in range(A_KV_HEADS):
        kh = k2[:, kv * HEAD_DIM:(kv + 1) * HEAD_DIM]
        kk = jnp.concatenate([kh, kh], axis=1)
        parts = []
        for pair in range(GQA // 2):
            c0 = (kv * GQA + 2 * pair) * HEAD_DIM
            qp = q_ref[:, c0:c0 + 2 * HEAD_DIM] * HEAD_DIM ** -0.5
            parts += [jnp.where(low, qp, 0.0).astype(BF16), jnp.where(low, 0.0, qp).astype(BF16)]
        q4 = jnp.concatenate(parts, axis=0)
        sk = jnp.zeros((nq, 1), F32)
        for g in range(GQA):
            sk = jnp.where(g_of_row == g, sink_ref[l, kv * GQA + g], sk)
        scores.append(_dot_nt(q4, kk) + bias)
        sinks.append(sk)
    probs, sink_terms = [], []
    for s, sk in zip(scores, sinks):
        m = jnp.maximum(jnp.max(s, -1, keepdims=True), sk)
        probs.append(jnp.exp(s - m).astype(BF16))
        sink_terms.append(jnp.exp(sk - m))
    for kv in range(A_KV_HEADS):
        vh = v2[:, kv * HEAD_DIM:(kv + 1) * HEAD_DIM]
        v3 = jnp.concatenate([vh, vh, ones], axis=1)
        oa = _dot(probs[kv], v3)
        on = oa[:, 0:2 * HEAD_DIM] / (oa[:, 2 * HEAD_DIM:] + sink_terms[kv])
        for pair in range(GQA // 2):
            c0 = (kv * GQA + 2 * pair) * HEAD_DIM
            lo, hi = on[2 * pair * BLK:(2 * pair + 1) * BLK], on[(2 * pair + 1) * BLK:(2 * pair + 2) * BLK]
            o_ref[:, M_A + c0:M_A + c0 + 2 * HEAD_DIM] = jnp.where(low, lo, hi).astype(BF16)


SWA_BT = 16


def _swa_sample_kernel(l, sink_ref, q_ref, kn_ref, vn_ref, ck_ref, cv_ref, o_ref, cko_ref, cvo_ref):
    bt = SWA_BT
    kn, vn = kn_ref[...], vn_ref[...]

    def cols(x):
        return jnp.transpose(jnp.concatenate([x, jnp.zeros((128 - bt, KV_WIDTH), F32)], axis=0))

    kn_t, vn_t = cols(kn), cols(vn)
    last = lax.broadcasted_iota(jnp.int32, (HEAD_DIM, WINDOW), 1) == WINDOW - 1
    for j in range(bt):
        for kv in range(A_KV_HEADS):
            hd = slice(kv * HEAD_DIM, (kv + 1) * HEAD_DIM)
            cko_ref[j, kv] = jnp.where(last, kn_t[hd, j:j + 1], pltpu.roll(ck_ref[j, kv], WINDOW - 1, 1))
            cvo_ref[j, kv] = jnp.where(last, vn_t[hd, j:j + 1], pltpu.roll(cv_ref[j, kv], WINDOW - 1, 1))
    nq = GQA * bt
    rowb = lax.broadcasted_iota(jnp.int32, (nq, bt * WINDOW), 0) % bt
    colb = lax.broadcasted_iota(jnp.int32, (nq, bt * WINDOW), 1) // WINDOW
    same = rowb == colb
    rg = lax.broadcasted_iota(jnp.int32, (nq, 1), 0) // bt
    for kv in range(A_KV_HEADS):
        sl = slice(kv * HEAD_DIM, (kv + 1) * HEAD_DIM)
        q4 = jnp.concatenate(
            [q_ref[:, (kv * GQA + g) * HEAD_DIM:(kv * GQA + g + 1) * HEAD_DIM] for g in range(GQA)], axis=0)
        q4 = q4 * HEAD_DIM ** -0.5
        kn4 = jnp.concatenate([kn[:, sl]] * GQA, axis=0)
        vn4 = jnp.concatenate([vn[:, sl]] * GQA, axis=0)
        sk = jnp.zeros((nq, 1), F32)
        for g in range(GQA):
            sk = jnp.where(rg == g, sink_ref[l, kv * GQA + g], sk)
        kt = jnp.concatenate([ck_ref[j, kv] for j in range(bt)], axis=1).astype(BF16)
        vt = jnp.concatenate([cv_ref[j, kv] for j in range(bt)], axis=1).astype(BF16)
        s = jnp.where(same, _dot(q4.astype(BF16), kt), NEG_INF)
        s_self = jnp.sum(q4 * kn4, -1, keepdims=True)
        m = jnp.maximum(jnp.maximum(jnp.max(s, -1, keepdims=True), s_self), sk)
        p = jnp.exp(s - m)
        p_self = jnp.exp(s_self - m)
        den = jnp.sum(p, -1, keepdims=True) + p_self + jnp.exp(sk - m)
        o = (_dot_nt(p.astype(BF16), vt) + p_self * vn4) / den
        for g in range(GQA):
            hh = kv * GQA + g
            o_ref[:, hh * HEAD_DIM:(hh + 1) * HEAD_DIM] = o[g * bt:(g + 1) * bt].astype(BF16)


S5_SHIFTS = (1, 2, 4)
S5_NTAB = 2 * (len(S5_SHIFTS) + 1) * SUB
S5_CHUNKS = S5_LANES // 128
S5_IN_PIECES = 4
S5_IN_K = S5_WIDTH // S5_IN_PIECES
S5_IN_N = S5_LANES // S5_IN_PIECES
S5_OUT_PIECES = 2
S5_OUT_K = S5_LANES // S5_OUT_PIECES
S5_OUT_N = S5_WIDTH // S5_OUT_PIECES


def _s5_tab(tab_ref, k, lanes):
    return (tab_ref[2 * k * SUB:(2 * k + 1) * SUB, lanes], tab_ref[(2 * k + 1) * SUB:(2 * k + 2) * SUB, lanes])


def _s5_drive(ub, bb_ref, p):
    return _dot(ub[:, p * S5_IN_K:(p + 1) * S5_IN_K], bb_ref[p])


def _s5_readout(h_piece, u, cre_ref, cim_ref, d_ref, wg_ref, bg_ref):
    ys = []
    for j in range(S5_OUT_PIECES):
        hr, hi = h_piece(j)
        ys.append(_dot(hr.astype(BF16), cre_ref[j]) - _dot(hi.astype(BF16), cim_ref[j]))
    y = jnp.concatenate(ys, axis=1) + d_ref[...] * u
    z = jax.nn.gelu(y)
    gate = jax.nn.sigmoid(_dot(z.astype(BF16), wg_ref[...]) + bg_ref[...])
    return (z * gate).astype(BF16)


def _s5_block(u_ref, bb_ref, cre_ref, cim_ref, d_ref, wg_ref, bg_ref, ab_ref, o_ref, hpre_ref, hpim_ref):
    u = u_ref[...]
    ub = u.astype(BF16)
    drive = [_s5_drive(ub, bb_ref, p) for p in range(S5_IN_PIECES)]
    x3r = jnp.concatenate([d[:, 0:S5_IN_N] for d in drive], axis=1).reshape(BLK, S5_CHUNKS, 128)
    x3i = jnp.concatenate([d[:, S5_IN_N:] for d in drive], axis=1).reshape(BLK, S5_CHUNKS, 128)
    ar, ai = ab_ref[0:S5_CHUNKS, :], ab_ref[S5_CHUNKS:, :]
    hr, hi = hpre_ref[...], hpim_ref[...]
    hs_r, hs_i = [], []
    for t in range(BLK):
        hr, hi = ar * hr - ai * hi + x3r[t], ar * hi + ai * hr + x3i[t]
        hs_r.append(hr)
        hs_i.append(hi)
    hpre_ref[...] = hr
    hpim_ref[...] = hi
    h2r = jnp.stack(hs_r, axis=0).reshape(BLK, S5_LANES)
    h2i = jnp.stack(hs_i, axis=0).reshape(BLK, S5_LANES)

    def piece(j):
        return h2r[:, j * S5_OUT_K:(j + 1) * S5_OUT_K], h2i[:, j * S5_OUT_K:(j + 1) * S5_OUT_K]

    o_ref[:, M_B:M_B + S5_WIDTH] = _s5_readout(piece, u, cre_ref, cim_ref, d_ref, wg_ref, bg_ref)


def _s5_sample_kernel(u_ref, bb_ref, cre_ref, cim_ref, d_ref, wg_ref, bg_ref, tab_ref, h0re_ref, h0im_ref,
                      o_ref, hsre_ref, hsim_ref):
    u = u_ref[...]
    ub = u.astype(BF16)
    for p in range(S5_IN_PIECES):
        lanes = slice(p * S5_IN_N, (p + 1) * S5_IN_N)
        bu = _s5_drive(ub, bb_ref, p)
        pr, pi = _s5_tab(tab_ref, len(S5_SHIFTS), lanes)
        ar, ai = pr[0:1, :], pi[0:1, :]
        h0r, h0i = h0re_ref[:, lanes], h0im_ref[:, lanes]
        hsre_ref[:, lanes] = ar * h0r - ai * h0i + bu[:, 0:S5_IN_N]
        hsim_ref[:, lanes] = ar * h0i + ai * h0r + bu[:, S5_IN_N:]
    piece = lambda j: (hsre_ref[:, j * S5_OUT_K:(j + 1) * S5_OUT_K], hsim_ref[:, j * S5_OUT_K:(j + 1) * S5_OUT_K])
    o_ref[:, M_B:M_B + S5_WIDTH] = _s5_readout(piece, u, cre_ref, cim_ref, d_ref, wg_ref, bg_ref)


def _s5_param_specs(l, prompt):
    return [
        _layer_spec((S5_IN_PIECES, S5_IN_K, 2 * S5_IN_N), l),
        _layer_spec((S5_OUT_PIECES, S5_OUT_K, S5_OUT_N), l), _layer_spec((S5_OUT_PIECES, S5_OUT_K, S5_OUT_N), l),
        _layer_spec((1, S5_WIDTH), l), _layer_spec((S5_WIDTH, S5_WIDTH), l), _layer_spec((1, S5_WIDTH), l),
        _layer_spec((2 * S5_CHUNKS, 128), l) if prompt else _layer_spec((S5_NTAB, S5_LANES), l),
    ]


def _s5_params(prm, prompt):
    return (prm["bb"], prm["c_re"], prm["c_im"], prm["d"], prm["w_glu"], prm["b_glu"],
            prm["ab"] if prompt else prm["tab"])


def _gla_log_gate(ac, wa_ref, ba_ref):
    return _log_sigmoid(_dot(ac.astype(BF16), wa_ref[...]) + ba_ref[...]) * (1.0 / GLA_TAU)


def _gla_finish(o_heads, gate, ng):
    outs = [o * lax.rsqrt(jnp.mean(o * o, -1, keepdims=True) + LN_EPS) for o in o_heads]
    return jnp.concatenate(outs, axis=1) * ng * (gate * jax.nn.sigmoid(gate))


def _gla_block(q_ref, k_ref, v_ref, gate_ref, ac_ref, wa_ref, ba_ref, ng_ref, o_ref, s_ref):
    lg = _gla_log_gate(ac_ref[...], wa_ref, ba_ref)
    L = GLA_CHUNK
    row = lax.broadcasted_iota(jnp.int32, (L, GLA_KW), 0)
    causal = lax.broadcasted_iota(jnp.int32, (L, L), 0) >= lax.broadcasted_iota(jnp.int32, (L, L), 1)
    chunks = range(BLK // L)
    heads = range(GLA_HEADS)
    ks = lambda hh: slice(hh * GLA_DK, (hh + 1) * GLA_DK)
    vs = lambda hh: slice(hh * GLA_DV, (hh + 1) * GLA_DV)
    q_in, k_out, k_dec, dec, vb = [], [], [], [], []
    for c in chunks:
        rows = slice(c * L, (c + 1) * L)
        b = lg[rows, :]
        d = 1
        while d < L:
            b = b + jnp.where(row >= d, pltpu.roll(b, d, 0), 0.0)
            d *= 2
        b_end = b[L - 1:L, :]
        kc = k_ref[rows, :]
        q_in.append((q_ref[rows, :] * GLA_DK ** -0.5 * jnp.exp(b)).astype(BF16))
        k_out.append((kc * jnp.exp(-b)).astype(BF16))
        k_dec.append((kc * jnp.exp(b_end - b)).astype(BF16))
        dec.append(jnp.transpose(jnp.broadcast_to(jnp.exp(b_end), (GLA_DV, GLA_KW))))
        vb.append(v_ref[rows, :].astype(BF16))
    att = [[jnp.where(causal, _dot_nt(q_in[c][:, ks(hh)], k_out[c][:, ks(hh)]), 0.0).astype(BF16) for hh in heads]
           for c in chunks]
    kv = [[_dot_tn(k_dec[c][:, ks(hh)], vb[c][:, vs(hh)]) for hh in heads] for c in chunks]
    state = [s_ref[...]]
    for c in chunks:
        state.append(dec[c] * state[c] + jnp.concatenate(kv[c], axis=0))
    s_ref[...] = state[-1]
    for c in chunks:
        sb = state[c].astype(BF16)
        o_heads = [_dot(q_in[c][:, ks(hh)], sb[ks(hh), :]) + _dot(att[c][hh], vb[c][:, vs(hh)]) for hh in heads]
        o_ref[c * L:(c + 1) * L, M_C:M_C + GLA_WIDTH] = _gla_finish(
            o_heads, gate_ref[c * L:(c + 1) * L, :], ng_ref[...]).astype(BF16)


def _gla_param_specs(l):
    return [_layer_spec((128, GLA_KW), l), _layer_spec((1, GLA_KW), l), _layer_spec((1, GLA_WIDTH), l)]


GLA_BT = 16


def _gla_sample_kernel(q_ref, k_ref, v_ref, gate_ref, ac_ref, wa_ref, ba_ref, ng_ref, s_ref, o_ref, so_ref):
    bt = GLA_BT
    lg = _gla_log_gate(ac_ref[...], wa_ref, ba_ref)
    v = v_ref[...]

    def cols(x):
        return jnp.transpose(jnp.concatenate([x, jnp.zeros((128 - bt, GLA_KW), F32)], axis=0))

    eg_t, k_t, q_t = cols(jnp.exp(lg)), cols(k_ref[...]), cols(q_ref[...] * GLA_DK ** -0.5)
    o_rows = []
    for j in range(bt):
        bc = lambda t: jnp.broadcast_to(t[:, j:j + 1], (GLA_KW, GLA_DV))
        vrow = jnp.concatenate(
            [jnp.broadcast_to(v[j:j + 1, hh * GLA_DV:(hh + 1) * GLA_DV], (GLA_DK, GLA_DV)) for hh in range(GLA_HEADS)],
            axis=0)
        s_new = bc(eg_t) * s_ref[j] + bc(k_t) * vrow
        so_ref[j] = s_new
        qs = bc(q_t) * s_new
        o_rows.append(jnp.concatenate(
            [jnp.sum(qs[hh * GLA_DK:(hh + 1) * GLA_DK, :], axis=0, keepdims=True) for hh in range(GLA_HEADS)], axis=1))
    o = jnp.concatenate(o_rows, axis=0)
    o_heads = [o[:, hh * GLA_DV:(hh + 1) * GLA_DV] for hh in range(GLA_HEADS)]
    o_ref[:, M_C:M_C + GLA_WIDTH] = _gla_finish(o_heads, gate_ref[...], ng_ref[...]).astype(BF16)


assert SWA_BT == GLA_BT
SAMPLE_BT = SWA_BT


def _mix_sample_kernel(l, sink_ref, q_ref, kn_ref, vn_ref, ck_ref, cv_ref,
                       u_ref, bb_ref, cre_ref, cim_ref, d_ref, wg_ref, bg_ref, tab_ref, h0re_ref, h0im_ref,
                       gq_ref, gk_ref, gv_ref, gg_ref, ga_ref, wa_ref, ba_ref, ng_ref, s_ref,
                       o_ref, cko_ref, cvo_ref, hsre_ref, hsim_ref, so_ref):
    if l == 0:
        for r in (cko_ref, cvo_ref, so_ref):
            r[1:] = jnp.zeros((r.shape[0] - 1,) + r.shape[1:], F32)
        cko_ref, cvo_ref, so_ref = cko_ref.at[0], cvo_ref.at[0], so_ref.at[0]
    _swa_sample_kernel(l, sink_ref, q_ref, kn_ref, vn_ref, ck_ref, cv_ref, o_ref, cko_ref, cvo_ref)
    _s5_sample_kernel(u_ref, bb_ref, cre_ref, cim_ref, d_ref, wg_ref, bg_ref, tab_ref, h0re_ref, h0im_ref,
                      o_ref, hsre_ref, hsim_ref)
    _gla_sample_kernel(gq_ref, gk_ref, gv_ref, gg_ref, ga_ref, wa_ref, ba_ref, ng_ref, s_ref, o_ref, so_ref)


def _mix_sample(sink, h, cache_k, cache_v, s5_prm, h0re, h0im, gla_prm, state, mix, ck_all, cv_all, s_all, l):
    bt = SAMPLE_BT
    r0 = ROWS_P // bt
    blk = lambda c, w: pl.BlockSpec((bt, w), lambda s: (r0 + s, c // w))
    cache = pl.BlockSpec((None, bt, A_KV_HEADS, HEAD_DIM, WINDOW), lambda s: (l, s, 0, 0, 0))
    h0 = pl.BlockSpec((None, bt, S5_LANES), lambda s: (l, s, 0))
    hs = pl.BlockSpec((bt, S5_LANES), lambda s: (s, 0))
    st = pl.BlockSpec((None, bt, GLA_KW, GLA_DV), lambda s: (l, s, 0, 0))
    in_specs = ([pl.BlockSpec(memory_space=pltpu.SMEM), blk(C_Q, A_WIDTH), blk(C_K, KV_WIDTH), blk(C_V, KV_WIDTH),
                 cache, cache, blk(C_U, S5_WIDTH)]
                + _s5_param_specs(l, False) + [h0, h0]
                + [blk(C_GQ, GLA_KW), blk(C_GK, GLA_KW), blk(C_GV, GLA_WIDTH), blk(C_GG, GLA_WIDTH), blk(C_GA, 128)]
                + _gla_param_specs(l) + [st])
    n_in = len(in_specs)
    hs_shape = jax.ShapeDtypeStruct((N_SAMPLE, S5_LANES), F32)
    if l == 0:
        all_layers = lambda shape: pl.BlockSpec((DEPTH, bt) + shape, lambda s: (0, s) + (0,) * len(shape))
        cache_o, st_o = all_layers((A_KV_HEADS, HEAD_DIM, WINDOW)), all_layers((GLA_KW, GLA_DV))
        aliased, aliases = [mix], {n_in: 0}
    else:
        cache_o, st_o = cache, st
        aliased, aliases = [mix, ck_all, cv_all, s_all], {n_in: 0, n_in + 1: 1, n_in + 2: 2, n_in + 3: 5}
    return pl.pallas_call(
        _drop_inputs(functools.partial(_mix_sample_kernel, l), n_in, len(aliased)),
        grid=(N_SAMPLE // bt,),
        in_specs=in_specs + [_any()] * len(aliased),
        out_specs=[blk(0, D_MODEL), cache_o, cache_o, hs, hs, st_o],
        out_shape=[jax.ShapeDtypeStruct((ROWS, D_MODEL), BF16),
                   jax.ShapeDtypeStruct(cache_k.shape, F32), jax.ShapeDtypeStruct(cache_v.shape, F32),
                   hs_shape, hs_shape, jax.ShapeDtypeStruct(state.shape, F32)],
        input_output_aliases=aliases,
        compiler_params=_cparams("arbitrary"),
        name="mix_sample",
    )(sink, h, h, h, cache_k, cache_v, h, *_s5_params(s5_prm, False), h0re, h0im,
      h, h, h, h, h, *gla_prm, state, *aliased)


def _mix_prompt_kernel(l, n_cast, sink_ref, bias_ref, q_ref, kp_ref, kc_ref, vp_ref, vc_ref, u_ref,
                       gq_ref, gk_ref, gv_ref, gg_ref, ga_ref,
                       bb_ref, cre_ref, cim_ref, d_ref, wg_ref, bg_ref, ab_ref, wa_ref, ba_ref, ng_ref, *rest):
    w_refs, (o_ref, hpre_ref, hpim_ref, s_ref) = rest[:n_cast], rest[n_cast:n_cast + 4]
    c_refs = rest[n_cast + 4:]
    s_blk = pl.program_id(0)

    @pl.when(s_blk == 0)
    def _():
        hpre_ref[...] = jnp.zeros_like(hpre_ref)
        hpim_ref[...] = jnp.zeros_like(hpim_ref)
        s_ref[...] = jnp.zeros_like(s_ref)

    for w_ref, c_ref in zip(w_refs, c_refs):
        c_ref[...] = w_ref[...].astype(BF16)
    _s5_block(u_ref, bb_ref, cre_ref, cim_ref, d_ref, wg_ref, bg_ref, ab_ref, o_ref, hpre_ref, hpim_ref)
    _swa_block(l, sink_ref, bias_ref, q_ref, kp_ref, kc_ref, vp_ref, vc_ref, o_ref)
    _gla_block(gq_ref, gk_ref, gv_ref, gg_ref, ga_ref, wa_ref, ba_ref, ng_ref, o_ref, s_ref)


def _mix_prompt(sink, h, s5_prm, gla_prm, w_casts, mix, l):
    cur = _seq_block
    prev = lambda s: _seq_block(jnp.maximum(s - 1, 0))
    blk = lambda c, w, at=cur: pl.BlockSpec((BLK, w), lambda s: (at(s), c // w))
    const = lambda shape: pl.BlockSpec(shape, lambda s: (0,) * len(shape))
    casts = [_cast_specs(w, rows, layer) for w, rows, layer in w_casts]
    in_specs = ([pl.BlockSpec(memory_space=pltpu.SMEM),
                 pl.BlockSpec((None, BLK, 2 * BLK), lambda s: (jnp.minimum(s, 2), 0, 0)), blk(C_Q, A_WIDTH),
                 blk(C_K, KV_WIDTH, prev), blk(C_K, KV_WIDTH), blk(C_V, KV_WIDTH, prev), blk(C_V, KV_WIDTH),
                 blk(C_U, S5_WIDTH),
                 blk(C_GQ, GLA_KW), blk(C_GK, GLA_KW), blk(C_GV, GLA_WIDTH), blk(C_GG, GLA_WIDTH), blk(C_GA, 128)]
                + _s5_param_specs(l, True) + _gla_param_specs(l) + [c[0] for c in casts])
    n_in = len(in_specs)
    return pl.pallas_call(
        _drop_inputs(functools.partial(_mix_prompt_kernel, l, len(casts)), n_in, 1),
        grid=(N_PBLK,),
        in_specs=in_specs + [_any()],
        out_specs=[pl.BlockSpec((BLK, D_MODEL), lambda s: (cur(s), 0)),
                   const((S5_CHUNKS, 128)), const((S5_CHUNKS, 128)), const((GLA_KW, GLA_DV))] + [c[1] for c in casts],
        out_shape=[jax.ShapeDtypeStruct((ROWS, D_MODEL), BF16),
                   jax.ShapeDtypeStruct((S5_CHUNKS, 128), F32), jax.ShapeDtypeStruct((S5_CHUNKS, 128), F32),
                   jax.ShapeDtypeStruct((GLA_KW, GLA_DV), F32)] + [c[2] for c in casts],
        input_output_aliases={n_in: 0},
        compiler_params=_cparams("arbitrary"),
        name="mix_prompt",
    )(sink, _swa_bias(), *([h] * 11), *_s5_params(s5_prm, True), *gla_prm, *[w for w, _, _ in w_casts], mix)


OUT_TM = 768
OUT_PARTS = 3
LN_ROWS = 16


def _out_proj_kernel(mix_ref, w_ref, x_ref, g_ref, b_ref, o_ref):
    i = pl.program_id(0)
    g, b = g_ref[...], b_ref[...]
    part = OUT_TM // OUT_PARTS
    rows = [slice(k * part, (k + 1) * part) for k in range(OUT_PARTS)]
    def norm(k, acc):
        for c in range(part // LN_ROWS):
            r0 = k * part + c * LN_ROWS
            y = _layer_norm_rows(ALPHA * x_ref[r0:r0 + LN_ROWS, :] + acc[c * LN_ROWS:(c + 1) * LN_ROWS, :], g, b)
            o_ref[r0:r0 + LN_ROWS, :] = _zero_pad_rows(y, i * OUT_TM + r0)

    prev = None
    for k, r in enumerate(rows):
        acc = _dot(mix_ref[r, :], w_ref[...])
        if prev is not None:
            norm(k - 1, prev)
        prev = acc
    norm(OUT_PARTS - 1, prev)


def _out_proj(mix, w, x, g, b, l):
    tm = OUT_TM
    return pl.pallas_call(
        _out_proj_kernel,
        grid=(ROWS // tm,),
        in_specs=[
            pl.BlockSpec((tm, D_MODEL), lambda i: (i, 0)),
            pl.BlockSpec((D_MODEL, D_MODEL), lambda i: (0, 0), pipeline_mode=pl.Buffered(1)),
            pl.BlockSpec((tm, D_MODEL), lambda i: (i, 0)),
            _layer_spec((1, D_MODEL), l), _layer_spec((1, D_MODEL), l),
        ],
        out_specs=pl.BlockSpec((tm, D_MODEL), lambda i: (i, 0)),
        out_shape=jax.ShapeDtypeStruct((ROWS, D_MODEL), F32),
        compiler_params=_cparams("arbitrary"),
        name="out_proj",
    )(mix, w, x, g, b)


FFN_TM = 768
FFN_TF = 512
FFN_HALO = 16
FFN_NI = ROWS // FFN_TM
assert ROWS % FFN_TM == 0 and FFN_TM >= N_SAMPLE + BLK + SUB and META_ROW % FFN_HALO == 0


def _ffn_kernel(final, x_ref, xh_ref, wu_ref, wg_ref, cw_ref, cb_ref, wd_ref, p0_ref, p1_ref, g_ref, b_ref,
                o_ref, *rest):
    if final:
        ys_ref, ut_ref, us_ref, xb_ref, a_ref = rest
    else:
        ut_ref, us_ref, xb_ref, a_ref = rest
    i, f = pl.program_id(0), pl.program_id(1)
    tm, hl = FFN_TM, FFN_HALO
    last_i = FFN_NI - 1

    @pl.when(f == 0)
    def _():
        xb_ref[...] = x_ref[...].astype(BF16)
        o_ref[...] = jnp.zeros_like(o_ref)

    cw0, cw1, cw2, cb = cw_ref[0:1, :], cw_ref[1:2, :], cw_ref[2:3, :], cb_ref[...]

    def act(um2, um1, u0, g0):
        return (jax.nn.gelu(cb + cw0 * um2 + cw1 * um1 + cw2 * u0) * g0).astype(BF16)

    xb = xb_ref[...]
    u = _dot(xb, wu_ref[...])
    gte = _dot(xb, wg_ref[...])
    um2, um1 = pltpu.roll(u, 2, 0), pltpu.roll(u, 1, 0)
    body = slice(hl, tm - N_SAMPLE)
    a_ref[body, :] = act(um2[body, :], um1[body, :], u[body, :], gte[body, :])
    ext = jnp.concatenate([_dot(xh_ref[...].astype(BF16), wu_ref[...]), u[0:hl, :]], axis=0)
    a_ref[0:hl, :] = act(ext[hl - 2:2 * hl - 2, :], ext[hl - 1:2 * hl - 1, :], u[0:hl, :], gte[0:hl, :])
    tail = slice(tm - N_SAMPLE, tm)
    samp = i == last_i
    a_ref[tail, :] = act(jnp.where(samp, p0_ref[...], um2[tail, :]), jnp.where(samp, p1_ref[...], um1[tail, :]),
                         u[tail, :], gte[tail, :])
    o_ref[...] += _dot(a_ref[...], wd_ref[...])
    ut_ref[...] = u[tm - N_SAMPLE - BLK - SUB:tm - N_SAMPLE - BLK, :]
    us_ref[0] = p1_ref[...]
    us_ref[1] = u[tm - N_SAMPLE:, :]

    @pl.when(f == pl.num_programs(1) - 1)
    def _():
        _deepnorm_ln(o_ref, x_ref, g_ref, b_ref, o_ref, i * tm, tm)
        if final:
            ys_ref[...] = o_ref[tm - N_SAMPLE:, :]


def _ffn(x, wup, cw, cb, wd, conv_p0, conv_p1, g, b, l, final):
    tm, tf = FFN_TM, FFN_TF
    nf = D_FF // tf
    tail = lambda i, f: jnp.where(i == FFN_NI - 1, f, 0)
    halo = lambda i, f: (jnp.where(i == 0, ROWS_P // FFN_HALO, i * (tm // FFN_HALO)) - 1, 0)
    y_spec = pl.BlockSpec((tm, D_MODEL), lambda i, f: (i, 0))
    y_specs = [y_spec, pl.BlockSpec((N_SAMPLE, D_MODEL), lambda i, f: (0, 0))] if final else [y_spec]
    y_shapes = ([jax.ShapeDtypeStruct((SEQ, D_MODEL), F32), jax.ShapeDtypeStruct((N_SAMPLE, D_MODEL), F32)]
                if final else [jax.ShapeDtypeStruct((ROWS, D_MODEL), F32)])
    return pl.pallas_call(
        functools.partial(_ffn_kernel, final),
        grid=(FFN_NI, nf),
        in_specs=[
            pl.BlockSpec((tm, D_MODEL), lambda i, f: (i, 0)),
            pl.BlockSpec((FFN_HALO, D_MODEL), halo),
            pl.BlockSpec((D_MODEL, tf), lambda i, f: (0, f)),
            pl.BlockSpec((D_MODEL, tf), lambda i, f: (0, nf + f)),
            pl.BlockSpec((None, CONV_W, tf), lambda i, f: (l, 0, f)),
            pl.BlockSpec((None, 1, tf), lambda i, f: (l, 0, f)),
            pl.BlockSpec((tf, D_MODEL), lambda i, f: (f, 0)),
            pl.BlockSpec((None, N_SAMPLE, tf), lambda i, f: (l, 0, tail(i, f))),
            pl.BlockSpec((None, N_SAMPLE, tf), lambda i, f: (l, 0, tail(i, f))),
            _layer_spec((1, D_MODEL), l), _layer_spec((1, D_MODEL), l),
        ],
        out_specs=y_specs + [
            pl.BlockSpec((SUB, tf), lambda i, f: (0, tail(i, f))),
            pl.BlockSpec((2, N_SAMPLE, tf), lambda i, f: (0, 0, tail(i, f))),
        ],
        out_shape=y_shapes + [
            jax.ShapeDtypeStruct((SUB, D_FF), F32),
            jax.ShapeDtypeStruct((2, N_SAMPLE, D_FF), F32),
        ],
        scratch_shapes=[pltpu.VMEM((tm, D_MODEL), BF16), pltpu.VMEM((tm, tf), BF16)],
        compiler_params=_cparams("arbitrary", "arbitrary"),
        name="conv_ffn",
    )(x, x, wup, wup, cw, cb, wd, conv_p0, conv_p1, g, b)


def _rope_table():
    inv = jnp.power(ROPE_THETA, -jnp.arange(ROPE_HALF, dtype=F32) / ROPE_HALF)
    pos = jnp.concatenate([jnp.arange(SEQ, dtype=jnp.int32) + N_META, jnp.zeros((LEAD,), jnp.int32),
                           jnp.arange(N_META, dtype=jnp.int32), jnp.full((N_SAMPLE,), PAST_LEN, jnp.int32)])
    ang = pos.astype(F32)[:, None] * inv[None, :]
    return jnp.concatenate([jnp.cos(ang), jnp.sin(ang)], axis=1)


def _s5_tables(lam_re, lam_im, log_dt, b_re, b_im, c_re, c_im, d, w_glu, b_glu):
    L = lam_re.shape[0]
    dt = jnp.exp(log_dt)[..., None]
    mag = jnp.exp(lam_re * dt)
    ab_re, ab_im = mag * jnp.cos(lam_im * dt), mag * jnp.sin(lam_im * dt)
    den = lam_re * lam_re + lam_im * lam_im
    nr, ni = ab_re - 1.0, ab_im
    f_re = (nr * lam_re + ni * lam_im) / den
    f_im = (ni * lam_re - nr * lam_im) / den
    bb_re = f_re[..., None] * b_re - f_im[..., None] * b_im
    bb_im = f_re[..., None] * b_im + f_im[..., None] * b_re
    gi, go = S5_GROUPS // S5_IN_PIECES, S5_GROUPS // S5_OUT_PIECES

    def bd_in(t):
        t = t.reshape(L, S5_IN_PIECES, gi, S5_STATE, S5_CH)
        return jnp.einsum('lqgph,gk->lqghkp', t, jnp.eye(gi, dtype=F32)).reshape(L, S5_IN_PIECES, S5_IN_K, S5_IN_N)

    def bd_out(t):
        t = t.reshape(L, S5_OUT_PIECES, go, S5_CH, S5_STATE)
        return jnp.einsum('lqghp,gk->lqgpkh', t, jnp.eye(go, dtype=F32)).reshape(
            L, S5_OUT_PIECES, S5_OUT_K, S5_OUT_N)

    ar, ai = ab_re.reshape(L, 1, S5_LANES), ab_im.reshape(L, 1, S5_LANES)
    pows = [(ar, ai)]
    for _ in range(SUB - 1):
        pr, pi = pows[-1]
        pows.append((pr * ar - pi * ai, pr * ai + pi * ar))
    row = jnp.arange(SUB)[None, :, None]
    parts = []
    for sft in S5_SHIFTS:
        pr, pi = pows[sft - 1]
        parts += [jnp.where(row >= sft, pr, 0.0), jnp.where(row >= sft, pi, 0.0)]
    parts += [jnp.concatenate([p[0] for p in pows], 1), jnp.concatenate([p[1] for p in pows], 1)]
    return dict(bb=jnp.concatenate([bd_in(bb_re), bd_in(bb_im)], axis=3).astype(BF16),
                c_re=bd_out(c_re).astype(BF16), c_im=bd_out(c_im).astype(BF16),
                d=d.reshape(L, 1, S5_WIDTH), w_glu=w_glu.astype(BF16), b_glu=b_glu.reshape(L, 1, S5_WIDTH),
                tab=jnp.concatenate(parts, 1),
                ab=jnp.concatenate([ab_re.reshape(L, S5_CHUNKS, 128), ab_im.reshape(L, S5_CHUNKS, 128)], axis=1))


@jax.jit
def kernel(x_prompt, x_sample, cache_swa_k, cache_swa_v, state_ssm_re, state_ssm_im, state_gla, state_conv,
           meta_tokens, ln_in_g, ln_in_b, w_in, attn_sink, s5_lam_re, s5_lam_im, s5_log_dt, s5_b_re, s5_b_im,
           s5_c_re, s5_c_im, s5_d, s5_w_glu, s5_b_glu, gla_w_a2, gla_b_a, gla_norm_g, w_out, ln1_g, ln1_b,
           ffn_w_up, ffn_conv_w, ffn_conv_b, ffn_w_down, ln2_g, ln2_b):
    L = DEPTH
    row3 = lambda t: t.reshape(L, 1, -1)
    wa_b = jnp.pad(gla_w_a2, ((0, 0), (0, 128 - GLA_LOWRANK), (0, 0))).astype(BF16)
    gla_args = (wa_b, row3(gla_b_a), row3(gla_norm_g))
    ln1, ln2 = (row3(ln1_g), row3(ln1_b)), (row3(ln2_g), row3(ln2_b))
    conv_b3 = row3(ffn_conv_b)
    cs = _rope_table()
    s5p = _s5_tables(s5_lam_re, s5_lam_im, s5_log_dt, s5_b_re, s5_b_im, s5_c_re, s5_c_im, s5_d, s5_w_glu, s5_b_glu)
    cache_t = lambda c: jnp.transpose(c, (0, 1, 3, 4, 2))
    ck, cv = cache_t(cache_swa_k), cache_t(cache_swa_v)
    ck_all = cv_all = s_all = None
    h0re = state_ssm_re.reshape(L, N_SAMPLE, S5_LANES)
    h0im = state_ssm_im.reshape(L, N_SAMPLE, S5_LANES)
    sg = state_gla.reshape(L, N_SAMPLE, GLA_KW, GLA_DV)
    conv_p0, conv_p1 = state_conv[:, :, 0, :], state_conv[:, :, 1, :]
    mix = jnp.zeros((ROWS, D_MODEL), BF16)
    w_in_t = jnp.swapaxes(w_in, 1, 2)

    x, w_in_b = _ln_in(x_prompt.reshape(SEQ, D_MODEL), meta_tokens, x_sample.reshape(N_SAMPLE, D_MODEL),
                       ln_in_g.reshape(1, D_MODEL), ln_in_b.reshape(1, D_MODEL), w_in_t, 240)
    outs = [[] for _ in range(9)]
    for l in range(L):
        h = _in_proj(x, w_in_b, cs, l)
        casts = [(ffn_w_up, 32, l), (ffn_w_down, 176, l), (w_out, 32, l)] + ([(w_in_t, 80, l + 1)] if l + 1 < L else [])
        mix, hp_re, hp_im, s_p, w_up_b, w_down_b, w_out_b, *w_next = _mix_prompt(
            attn_sink, h, s5p, gla_args, casts, mix, l)
        w_in_b = w_next[0] if w_next else None
        mix, ck_all, cv_all, hs_re, hs_im, s_all = _mix_sample(
            attn_sink, h, ck, cv, s5p, h0re, h0im, gla_args, sg, mix, ck_all, cv_all, s_all, l)
        x = _out_proj(mix, w_out_b, x, *ln1, l)
        res = _ffn(x, w_up_b, ffn_conv_w, conv_b3, w_down_b, conv_p0, conv_p1, *ln2, l, l == L - 1)
        if l == L - 1:
            y_prompt, y_sample, u_tail, u_s = res
        else:
            x, u_tail, u_s = res
        kv_p = lambda c: h[SEQ - WINDOW:SEQ, c:c + KV_WIDTH].reshape(1, WINDOW, A_KV_HEADS, HEAD_DIM)
        new = (kv_p(C_K), kv_p(C_V),
               hp_re.reshape(1, S5_GROUPS, S5_STATE), hp_im.reshape(1, S5_GROUPS, S5_STATE),
               s_p.reshape(1, GLA_HEADS, GLA_DK, GLA_DV), u_tail[SUB - 2:SUB][None],
               hs_re.reshape(N_SAMPLE, S5_GROUPS, S5_STATE), hs_im.reshape(N_SAMPLE, S5_GROUPS, S5_STATE),
               jnp.transpose(u_s, (1, 0, 2)))
        for lst, val in zip(outs, new):
            lst.append(val)
    st = [jnp.stack(lst) for lst in outs]
    return (y_prompt.reshape(1, SEQ, D_MODEL), y_sample.reshape(N_SAMPLE, 1, D_MODEL),
            st[0], st[1], st[2], st[3], st[4], st[5],
            jnp.transpose(ck_all, (0, 1, 4, 2, 3)), jnp.transpose(cv_all, (0, 1, 4, 2, 3)),
            st[6], st[7], s_all.reshape(L, N_SAMPLE, GLA_HEADS, GLA_DK, GLA_DV), st[8])
```

```python
import functools

import jax
import jax.numpy as jnp
from jax import lax
from jax.experimental import pallas as pl
from jax.experimental.pallas import tpu as pltpu

F32 = jnp.float32
BF16 = jnp.bfloat16

D_MODEL = 2048
SEQ = 8192
DEPTH = 2
N_SAMPLE = 128
N_META = 16
HEAD_DIM = 64
A_WIDTH = 1024
A_HEADS = 16
A_KV_HEADS = 4
GQA = 4
KV_WIDTH = A_KV_HEADS * HEAD_DIM
WINDOW = 128
ROPE_DIM = 16
ROPE_HALF = ROPE_DIM // 2
ROPE_THETA = 500000.0
PAST_LEN = 8192
S5_WIDTH = 512
S5_CH = 16
S5_GROUPS = 32
S5_STATE = 64
S5_LANES = S5_GROUPS * S5_STATE
GLA_WIDTH = 512
GLA_HEADS = 4
GLA_DV = 128
GLA_DK = 64
GLA_KW = GLA_HEADS * GLA_DK
GLA_LOWRANK = 16
GLA_TAU = 16.0
GLA_CHUNK = 64
D_FF = 5632
CONV_W = 3
LN_EPS = 1e-5
ALPHA = (2 * DEPTH) ** 0.25
NEG_INF = -1e30

BLK = 128
T_PROMPT = N_META + SEQ
LEAD = (-T_PROMPT) % BLK
N_PBLK = (LEAD + T_PROMPT) // BLK
ROWS_P = N_PBLK * BLK
ROWS = ROWS_P + N_SAMPLE
META_ROW = SEQ + LEAD
SUB = 8

C_Q, C_K, C_V, C_U, C_GQ, C_GK, C_GV, C_GG, C_GA = 0, 1024, 1280, 1536, 2048, 2304, 2560, 3072, 3584
IN_COLS = 3600
IN_TILE = 1280
IN_PAD = 3 * IN_TILE
ROPE_COLS = A_WIDTH + KV_WIDTH
M_A, M_B, M_C = 0, A_WIDTH, A_WIDTH + S5_WIDTH

VMEM_LIMIT = 56 * 1024 * 1024


def _cparams(*sem):
    return pltpu.CompilerParams(dimension_semantics=sem, vmem_limit_bytes=VMEM_LIMIT)


def _any():
    return pl.BlockSpec(memory_space=pl.ANY)


def _drop_inputs(kern, start, count):
    def wrapped(*refs):
        return kern(*refs[:start], *refs[start + count:])
    return wrapped


def _seq_block(s):
    return jnp.where(s == 0, N_PBLK - 1, s - 1)


def _layer_spec(shape, l):
    return pl.BlockSpec((None,) + shape, lambda *_: (l,) + (0,) * len(shape))


def _cast_specs(w, rows, l):
    n, cols = w.shape[1] // rows, w.shape[2]
    chunk = lambda s: jnp.minimum(s, n - 1)
    return (pl.BlockSpec((None, rows, cols), lambda s: (l, chunk(s), 0)),
            pl.BlockSpec((rows, cols), lambda s: (chunk(s), 0)),
            jax.ShapeDtypeStruct(w.shape[1:], BF16))


def _layer_norm_rows(x, g, b):
    mu = jnp.mean(x, -1, keepdims=True)
    xc = x - mu
    var = jnp.mean(xc * xc, -1, keepdims=True)
    return xc * lax.rsqrt(var + LN_EPS) * g + b


def _zero_pad_rows(y, first_row):
    row = first_row + lax.broadcasted_iota(jnp.int32, y.shape, 0)
    return jnp.where((row >= SEQ) & (row < META_ROW), 0.0, y)


def _deepnorm_ln(acc_ref, x_ref, g_ref, b_ref, o_ref, first_row, rows):
    g, b = g_ref[...], b_ref[...]
    for c in range(rows // BLK):
        r = slice(c * BLK, (c + 1) * BLK)
        y = _layer_norm_rows(ALPHA * x_ref[r, :] + acc_ref[r, :], g, b)
        o_ref[r, :] = _zero_pad_rows(y, first_row + c * BLK)


def _dot(a, b):
    return jnp.dot(a, b, preferred_element_type=F32)


def _dot_nt(a, b):
    return lax.dot_general(a, b, (((1,), (1,)), ((), ())), preferred_element_type=F32)


def _dot_tn(a, b):
    return lax.dot_general(a, b, (((0,), (0,)), ((), ())), preferred_element_type=F32)


def _log_sigmoid(x):
    return jnp.minimum(x, 0.0) - jnp.log(1.0 + jnp.exp(-jnp.abs(x)))


LN_IN_TM = 512
LN_IN_STEPS = SEQ // LN_IN_TM + 1
assert SEQ % LN_IN_TM == 0 and ROWS - SEQ <= LN_IN_TM


def _ln_in_kernel(xp_ref, meta_ref, xs_ref, g_ref, b_ref, wsrc_ref, o_ref, wdst_ref):
    i = pl.program_id(0)
    g, b = g_ref[...], b_ref[...]
    wdst_ref[...] = wsrc_ref[...].astype(BF16)

    @pl.when(i < LN_IN_STEPS - 1)
    def _():
        for c in range(LN_IN_TM // BLK):
            r = slice(c * BLK, (c + 1) * BLK)
            o_ref[r, :] = _layer_norm_rows(xp_ref[r, :], g, b)

    @pl.when(i == LN_IN_STEPS - 1)
    def _():
        o_ref[0:LEAD, :] = jnp.zeros((LEAD, D_MODEL), F32)
        o_ref[LEAD:BLK, :] = _layer_norm_rows(meta_ref[...], g, b)
        o_ref[BLK:BLK + N_SAMPLE, :] = _layer_norm_rows(xs_ref[...], g, b)


def _ln_in(xp, meta, xs, g, b, w_cast, w_rows):
    tm = LN_IN_TM
    const = lambda shape: pl.BlockSpec(shape, lambda i: (0,) * len(shape))
    w_in_spec, w_out_spec, w_shape = _cast_specs(w_cast, w_rows, 0)
    assert w_cast.shape[1] // w_rows <= LN_IN_STEPS
    return pl.pallas_call(
        _ln_in_kernel,
        grid=(LN_IN_STEPS,),
        in_specs=[
            pl.BlockSpec((tm, D_MODEL), lambda i: (jnp.minimum(i, SEQ // tm - 1), 0)),
            const((N_META, D_MODEL)), const((N_SAMPLE, D_MODEL)), const((1, D_MODEL)), const((1, D_MODEL)),
            w_in_spec,
        ],
        out_specs=[pl.BlockSpec((tm, D_MODEL), lambda i: (i, 0)), w_out_spec],
        out_shape=[jax.ShapeDtypeStruct((ROWS, D_MODEL), F32), w_shape],
        compiler_params=_cparams("arbitrary"),
        name="ln_in",
    )(xp, meta, xs, g, b, w_cast)


IN_TM = 1056


def _in_proj_kernel(x_ref, w_ref, cs_ref, o_ref, xb_ref):
    j = pl.program_id(1)

    last_j = IN_PAD // IN_TILE - 1

    @pl.when(j == 0)
    def _():
        xb = x_ref[...].astype(BF16)
        xb_ref[...] = xb
        c8, s8 = cs_ref[:, 0:ROPE_HALF], cs_ref[:, ROPE_HALF:ROPE_DIM]
        rest = HEAD_DIM - ROPE_DIM
        one, zero = jnp.ones((IN_TM, rest), F32), jnp.zeros((IN_TM, rest), F32)
        z8 = jnp.zeros((IN_TM, ROPE_HALF), F32)
        cos = jnp.concatenate([c8, c8, one] * 2, axis=1)
        sa = jnp.concatenate([-s8, z8, zero] * 2, axis=1)
        sb = jnp.concatenate([z8, s8, zero] * 2, axis=1)
        for c in range(ROPE_COLS // 256):
            acc = _dot_nt(xb, w_ref[c * 256:(c + 1) * 256, :])
            for hlf in range(2):
                blk = acc[:, hlf * 128:(hlf + 1) * 128]
                o_ref[:, c * 256 + hlf * 128:c * 256 + (hlf + 1) * 128] = (
                    blk * cos + pltpu.roll(blk, 128 - ROPE_HALF, 1) * sa + pltpu.roll(blk, ROPE_HALF, 1) * sb)

    @pl.when((j > 0) & (j < last_j))
    def _():
        o_ref[...] = _dot_nt(xb_ref[...], w_ref[...])

    @pl.when(j == last_j)
    def _():
        o_ref[...] = _dot_nt(xb_ref[...], w_ref[...])
        edge = (IN_COLS % IN_TILE) // 128 * 128
        col = edge + lax.broadcasted_iota(jnp.int32, (IN_TM, 128), 1)
        o_ref[:, edge:edge + 128] = jnp.where(col < IN_COLS % IN_TILE, o_ref[:, edge:edge + 128], 0.0)
        o_ref[:, edge + 128:] = jnp.zeros((IN_TM, IN_TILE - edge - 128), F32)


def _in_proj(x, w, cs, l):
    return pl.pallas_call(
        _in_proj_kernel,
        grid=(ROWS // IN_TM, IN_PAD // IN_TILE),
        in_specs=[
            pl.BlockSpec((IN_TM, D_MODEL), lambda i, j: (i, 0)),
            pl.BlockSpec((IN_TILE, D_MODEL), lambda i, j: (j, 0)),
            pl.BlockSpec((IN_TM, ROPE_DIM), lambda i, j: (i, 0)),
        ],
        out_specs=pl.BlockSpec((IN_TM, IN_TILE), lambda i, j: (i, j)),
        out_shape=jax.ShapeDtypeStruct((ROWS, IN_PAD), F32),
        scratch_shapes=[pltpu.VMEM((IN_TM, D_MODEL), BF16)],
        compiler_params=_cparams("arbitrary", "arbitrary"),
        name="in_proj",
    )(x, w, cs)


def _swa_bias():
    row = lax.broadcasted_iota(jnp.int32, (3, BLK, 2 * BLK), 1)
    col = lax.broadcasted_iota(jnp.int32, (3, BLK, 2 * BLK), 2)
    s_blk = lax.broadcasted_iota(jnp.int32, (3, BLK, 2 * BLK), 0)
    diff = row - col + BLK
    kpos = (s_blk - 1) * BLK + col - LEAD
    ok = (diff >= 0) & (diff <= WINDOW) & (kpos >= 0)
    return jnp.where(ok, 0.0, NEG_INF).astype(F32)


def _swa_block(l, sink_ref, bias_ref, q_ref, kp_ref, kc_ref, vp_ref, vc_ref, o_ref):
    k2 = jnp.concatenate([kp_ref[...], kc_ref[...]], axis=0).astype(BF16)
    v2 = jnp.concatenate([vp_ref[...], vc_ref[...]], axis=0).astype(BF16)
    nq = GQA * BLK
    bias = jnp.concatenate([bias_ref[...]] * GQA, axis=0)
    low = lax.broadcasted_iota(jnp.int32, (BLK, 2 * HEAD_DIM), 1) < HEAD_DIM
    g_of_row = lax.broadcasted_iota(jnp.int32, (nq, 1), 0) // BLK
    ones = jnp.ones((2 * BLK, 2 * HEAD_DIM), BF16)
    scores, sinks = [], []
    for kv in range(A_KV_HEADS):
        kh = k2[:, kv * HEAD_DIM:(kv + 1) * HEAD_DIM]
        kk = jnp.concatenate([kh, kh], axis=1)
        parts = []
        for pair in range(GQA // 2):
            c0 = (kv * GQA + 2 * pair) * HEAD_DIM
            qp = q_ref[:, c0:c0 + 2 * HEAD_DIM] * HEAD_DIM ** -0.5
            parts += [jnp.where(low, qp, 0.0).astype(BF16), jnp.where(low, 0.0, qp).astype(BF16)]
        q4 = jnp.concatenate(parts, axis=0)
        sk = jnp.zeros((nq, 1), F32)
        for g in range(GQA):
            sk = jnp.where(g_of_row == g, sink_ref[l, kv * GQA + g], sk)
        scores.append(_dot_nt(q4, kk) + bias)
        sinks.append(sk)
    probs, sink_terms = [], []
    for s, sk in zip(scores, sinks):
        m = jnp.maximum(jnp.max(s, -1, keepdims=True), sk)
        probs.append(jnp.exp(s - m).astype(BF16))
        sink_terms.append(jnp.exp(sk - m))
    for kv in range(A_KV_HEADS):
        vh = v2[:, kv * HEAD_DIM:(kv + 1) * HEAD_DIM]
        v3 = jnp.concatenate([vh, vh, ones], axis=1)
        oa = _dot(probs[kv], v3)
        on = oa[:, 0:2 * HEAD_DIM] / (oa[:, 2 * HEAD_DIM:] + sink_terms[kv])
        for pair in range(GQA // 2):
            c0 = (kv * GQA + 2 * pair) * HEAD_DIM
            lo, hi = on[2 * pair * BLK:(2 * pair + 1) * BLK], on[(2 * pair + 1) * BLK:(2 * pair + 2) * BLK]
            o_ref[:, M_A + c0:M_A + c0 + 2 * HEAD_DIM] = jnp.where(low, lo, hi).astype(BF16)


SWA_BT = 16


def _swa_sample_kernel(l, sink_ref, q_ref, kn_ref, vn_ref, ck_ref, cv_ref, o_ref, cko_ref, cvo_ref):
    bt = SWA_BT
    kn, vn = kn_ref[...], vn_ref[...]

    def cols(x):
        return jnp.transpose(jnp.concatenate([x, jnp.zeros((128 - bt, KV_WIDTH), F32)], axis=0))

    kn_t, vn_t = cols(kn), cols(vn)
    last = lax.broadcasted_iota(jnp.int32, (HEAD_DIM, WINDOW), 1) == WINDOW - 1
    for j in range(bt):
        for kv in range(A_KV_HEADS):
            hd = slice(kv * HEAD_DIM, (kv + 1) * HEAD_DIM)
            cko_ref[j, kv] = jnp.where(last, kn_t[hd, j:j + 1], pltpu.roll(ck_ref[j, kv], WINDOW - 1, 1))
            cvo_ref[j, kv] = jnp.where(last, vn_t[hd, j:j + 1], pltpu.roll(cv_ref[j, kv], WINDOW - 1, 1))
    nq = GQA * bt
    rowb = lax.broadcasted_iota(jnp.int32, (nq, bt * WINDOW), 0) % bt
    colb = lax.broadcasted_iota(jnp.int32, (nq, bt * WINDOW), 1) // WINDOW
    same = rowb == colb
    rg = lax.broadcasted_iota(jnp.int32, (nq, 1), 0) // bt
    for kv in range(A_KV_HEADS):
        sl = slice(kv * HEAD_DIM, (kv + 1) * HEAD_DIM)
        q4 = jnp.concatenate(
            [q_ref[:, (kv * GQA + g) * HEAD_DIM:(kv * GQA + g + 1) * HEAD_DIM] for g in range(GQA)], axis=0)
        q4 = q4 * HEAD_DIM ** -0.5
        kn4 = jnp.concatenate([kn[:, sl]] * GQA, axis=0)
        vn4 = jnp.concatenate([vn[:, sl]] * GQA, axis=0)
        sk = jnp.zeros((nq, 1), F32)
        for g in range(GQA):
            sk = jnp.where(rg == g, sink_ref[l, kv * GQA + g], sk)
        kt = jnp.concatenate([ck_ref[j, kv] for j in range(bt)], axis=1).astype(BF16)
        vt = jnp.concatenate([cv_ref[j, kv] for j in range(bt)], axis=1).astype(BF16)
        s = jnp.where(same, _dot(q4.astype(BF16), kt), NEG_INF)
        s_self = jnp.sum(q4 * kn4, -1, keepdims=True)
        m = jnp.maximum(jnp.maximum(jnp.max(s, -1, keepdims=True), s_self), sk)
        p = jnp.exp(s - m)
        p_self = jnp.exp(s_self - m)
        den = jnp.sum(p, -1, keepdims=True) + p_self + jnp.exp(sk - m)
        o = (_dot_nt(p.astype(BF16), vt) + p_self * vn4) / den
        for g in range(GQA):
            hh = kv * GQA + g
            o_ref[:, hh * HEAD_DIM:(hh + 1) * HEAD_DIM] = o[g * bt:(g + 1) * bt].astype(BF16)


S5_SHIFTS = (1, 2, 4)
S5_NTAB = 2 * (len(S5_SHIFTS) + 1) * SUB
S5_CHUNKS = S5_LANES // 128
S5_IN_PIECES = 4
S5_IN_K = S5_WIDTH // S5_IN_PIECES
S5_IN_N = S5_LANES // S5_IN_PIECES
S5_OUT_PIECES = 2
S5_OUT_K = S5_LANES // S5_OUT_PIECES
S5_OUT_N = S5_WIDTH // S5_OUT_PIECES


def _s5_tab(tab_ref, k, lanes):
    return (tab_ref[2 * k * SUB:(2 * k + 1) * SUB, lanes], tab_ref[(2 * k + 1) * SUB:(2 * k + 2) * SUB, lanes])


def _s5_drive(ub, bb_ref, p):
    return _dot(ub[:, p * S5_IN_K:(p + 1) * S5_IN_K], bb_ref[p])


def _s5_readout(h_piece, u, cre_ref, cim_ref, d_ref, wg_ref, bg_ref):
    ys = []
    for j in range(S5_OUT_PIECES):
        hr, hi = h_piece(j)
        ys.append(_dot(hr.astype(BF16), cre_ref[j]) - _dot(hi.astype(BF16), cim_ref[j]))
    y = jnp.concatenate(ys, axis=1) + d_ref[...] * u
    z = jax.nn.gelu(y)
    gate = jax.nn.sigmoid(_dot(z.astype(BF16), wg_ref[...]) + bg_ref[...])
    return (z * gate).astype(BF16)


def _s5_block(u_ref, bb_ref, cre_ref, cim_ref, d_ref, wg_ref, bg_ref, ab_ref, o_ref, hpre_ref, hpim_ref):
    u = u_ref[...]
    ub = u.astype(BF16)
    drive = [_s5_drive(ub, bb_ref, p) for p in range(S5_IN_PIECES)]
    x3r = jnp.concatenate([d[:, 0:S5_IN_N] for d in drive], axis=1).reshape(BLK, S5_CHUNKS, 128)
    x3i = jnp.concatenate([d[:, S5_IN_N:] for d in drive], axis=1).reshape(BLK, S5_CHUNKS, 128)
    ar, ai = ab_ref[0:S5_CHUNKS, :], ab_ref[S5_CHUNKS:, :]
    hr, hi = hpre_ref[...], hpim_ref[...]
    hs_r, hs_i = [], []
    for t in range(BLK):
        hr, hi = ar * hr - ai * hi + x3r[t], ar * hi + ai * hr + x3i[t]
        hs_r.append(hr)
        hs_i.append(hi)
    hpre_ref[...] = hr
    hpim_ref[...] = hi
    h2r = jnp.stack(hs_r, axis=0).reshape(BLK, S5_LANES)
    h2i = jnp.stack(hs_i, axis=0).reshape(BLK, S5_LANES)

    def piece(j):
        return h2r[:, j * S5_OUT_K:(j + 1) * S5_OUT_K], h2i[:, j * S5_OUT_K:(j + 1) * S5_OUT_K]

    o_ref[:, M_B:M_B + S5_WIDTH] = _s5_readout(piece, u, cre_ref, cim_ref, d_ref, wg_ref, bg_ref)


def _s5_sample_kernel(u_ref, bb_ref, cre_ref, cim_ref, d_ref, wg_ref, bg_ref, tab_ref, h0re_ref, h0im_ref,
                      o_ref, hsre_ref, hsim_ref):
    u = u_ref[...]
    ub = u.astype(BF16)
    for p in range(S5_IN_PIECES):
        lanes = slice(p * S5_IN_N, (p + 1) * S5_IN_N)
        bu = _s5_drive(ub, bb_ref, p)
        pr, pi = _s5_tab(tab_ref, len(S5_SHIFTS), lanes)
        ar, ai = pr[0:1, :], pi[0:1, :]
        h0r, h0i = h0re_ref[:, lanes], h0im_ref[:, lanes]
        hsre_ref[:, lanes] = ar * h0r - ai * h0i + bu[:, 0:S5_IN_N]
        hsim_ref[:, lanes] = ar * h0i + ai * h0r + bu[:, S5_IN_N:]
    piece = lambda j: (hsre_ref[:, j * S5_OUT_K:(j + 1) * S5_OUT_K], hsim_ref[:, j * S5_OUT_K:(j + 1) * S5_OUT_K])
    o_ref[:, M_B:M_B + S5_WIDTH] = _s5_readout(piece, u, cre_ref, cim_ref, d_ref, wg_ref, bg_ref)


def _s5_param_specs(l, prompt):
    return [
        _layer_spec((S5_IN_PIECES, S5_IN_K, 2 * S5_IN_N), l),
        _layer_spec((S5_OUT_PIECES, S5_OUT_K, S5_OUT_N), l), _layer_spec((S5_OUT_PIECES, S5_OUT_K, S5_OUT_N), l),
        _layer_spec((1, S5_WIDTH), l), _layer_spec((S5_WIDTH, S5_WIDTH), l), _layer_spec((1, S5_WIDTH), l),
        _layer_spec((2 * S5_CHUNKS, 128), l) if prompt else _layer_spec((S5_NTAB, S5_LANES), l),
    ]


def _s5_params(prm, prompt):
    return (prm["bb"], prm["c_re"], prm["c_im"], prm["d"], prm["w_glu"], prm["b_glu"],
            prm["ab"] if prompt else prm["tab"])


def _gla_log_gate(ac, wa_ref, ba_ref):
    return _log_sigmoid(_dot(ac.astype(BF16), wa_ref[...]) + ba_ref[...]) * (1.0 / GLA_TAU)


def _gla_finish(o_heads, gate, ng):
    outs = [o * lax.rsqrt(jnp.mean(o * o, -1, keepdims=True) + LN_EPS) for o in o_heads]
    return jnp.concatenate(outs, axis=1) * ng * (gate * jax.nn.sigmoid(gate))


def _gla_block(q_ref, k_ref, v_ref, gate_ref, ac_ref, wa_ref, ba_ref, ng_ref, o_ref, s_ref):
    lg = _gla_log_gate(ac_ref[...], wa_ref, ba_ref)
    L = GLA_CHUNK
    row = lax.broadcasted_iota(jnp.int32, (L, GLA_KW), 0)
    causal = lax.broadcasted_iota(jnp.int32, (L, L), 0) >= lax.broadcasted_iota(jnp.int32, (L, L), 1)
    chunks = range(BLK // L)
    heads = range(GLA_HEADS)
    ks = lambda hh: slice(hh * GLA_DK, (hh + 1) * GLA_DK)
    vs = lambda hh: slice(hh * GLA_DV, (hh + 1) * GLA_DV)
    q_in, k_out, k_dec, dec, vb = [], [], [], [], []
    for c in chunks:
        rows = slice(c * L, (c + 1) * L)
        b = lg[rows, :]
        d = 1
        while d < L:
            b = b + jnp.where(row >= d, pltpu.roll(b, d, 0), 0.0)
            d *= 2
        b_end = b[L - 1:L, :]
        kc = k_ref[rows, :]
        q_in.append((q_ref[rows, :] * GLA_DK ** -0.5 * jnp.exp(b)).astype(BF16))
        k_out.append((kc * jnp.exp(-b)).astype(BF16))
        k_dec.append((kc * jnp.exp(b_end - b)).astype(BF16))
        dec.append(jnp.transpose(jnp.broadcast_to(jnp.exp(b_end), (GLA_DV, GLA_KW))))
        vb.append(v_ref[rows, :].astype(BF16))
    att = [[jnp.where(causal, _dot_nt(q_in[c][:, ks(hh)], k_out[c][:, ks(hh)]), 0.0).astype(BF16) for hh in heads]
           for c in chunks]
    kv = [[_dot_tn(k_dec[c][:, ks(hh)], vb[c][:, vs(hh)]) for hh in heads] for c in chunks]
    state = [s_ref[...]]
    for c in chunks:
        state.append(dec[c] * state[c] + jnp.concatenate(kv[c], axis=0))
    s_ref[...] = state[-1]
    for c in chunks:
        sb = state[c].astype(BF16)
        o_heads = [_dot(q_in[c][:, ks(hh)], sb[ks(hh), :]) + _dot(att[c][hh], vb[c][:, vs(hh)]) for hh in heads]
        o_ref[c * L:(c + 1) * L, M_C:M_C + GLA_WIDTH] = _gla_finish(
            o_heads, gate_ref[c * L:(c + 1) * L, :], ng_ref[...]).astype(BF16)


def _gla_param_specs(l):
    return [_layer_spec((128, GLA_KW), l), _layer_spec((1, GLA_KW), l), _layer_spec((1, GLA_WIDTH), l)]


GLA_BT = 16


def _gla_sample_kernel(q_ref, k_ref, v_ref, gate_ref, ac_ref, wa_ref, ba_ref, ng_ref, s_ref, o_ref, so_ref):
    bt = GLA_BT
    lg = _gla_log_gate(ac_ref[...], wa_ref, ba_ref)
    v = v_ref[...]

    def cols(x):
        return jnp.transpose(jnp.concatenate([x, jnp.zeros((128 - bt, GLA_KW), F32)], axis=0))

    eg_t, k_t, q_t = cols(jnp.exp(lg)), cols(k_ref[...]), cols(q_ref[...] * GLA_DK ** -0.5)
    o_rows = []
    for j in range(bt):
        bc = lambda t: jnp.broadcast_to(t[:, j:j + 1], (GLA_KW, GLA_DV))
        vrow = jnp.concatenate(
            [jnp.broadcast_to(v[j:j + 1, hh * GLA_DV:(hh + 1) * GLA_DV], (GLA_DK, GLA_DV)) for hh in range(GLA_HEADS)],
            axis=0)
        s_new = bc(eg_t) * s_ref[j] + bc(k_t) * vrow
        so_ref[j] = s_new
        qs = bc(q_t) * s_new
        o_rows.append(jnp.concatenate(
            [jnp.sum(qs[hh * GLA_DK:(hh + 1) * GLA_DK, :], axis=0, keepdims=True) for hh in range(GLA_HEADS)], axis=1))
    o = jnp.concatenate(o_rows, axis=0)
    o_heads = [o[:, hh * GLA_DV:(hh + 1) * GLA_DV] for hh in range(GLA_HEADS)]
    o_ref[:, M_C:M_C + GLA_WIDTH] = _gla_finish(o_heads, gate_ref[...], ng_ref[...]).astype(BF16)


assert SWA_BT == GLA_BT
SAMPLE_BT = SWA_BT


def _mix_sample_kernel(l, sink_ref, q_ref, kn_ref, vn_ref, ck_ref, cv_ref,
                       u_ref, bb_ref, cre_ref, cim_ref, d_ref, wg_ref, bg_ref, tab_ref, h0re_ref, h0im_ref,
                       gq_ref, gk_ref, gv_ref, gg_ref, ga_ref, wa_ref, ba_ref, ng_ref, s_ref,
                       o_ref, cko_ref, cvo_ref, hsre_ref, hsim_ref, so_ref):
    if l == 0:
        for r in (cko_ref, cvo_ref, so_ref):
            r[1:] = jnp.zeros((r.shape[0] - 1,) + r.shape[1:], F32)
        cko_ref, cvo_ref, so_ref = cko_ref.at[0], cvo_ref.at[0], so_ref.at[0]
    _swa_sample_kernel(l, sink_ref, q_ref, kn_ref, vn_ref, ck_ref, cv_ref, o_ref, cko_ref, cvo_ref)
    _s5_sample_kernel(u_ref, bb_ref, cre_ref, cim_ref, d_ref, wg_ref, bg_ref, tab_ref, h0re_ref, h0im_ref,
                      o_ref, hsre_ref, hsim_ref)
    _gla_sample_kernel(gq_ref, gk_ref, gv_ref, gg_ref, ga_ref, wa_ref, ba_ref, ng_ref, s_ref, o_ref, so_ref)


def _mix_sample(sink, h, cache_k, cache_v, s5_prm, h0re, h0im, gla_prm, state, mix, ck_all, cv_all, s_all, l):
    bt = SAMPLE_BT
    r0 = ROWS_P // bt
    blk = lambda c, w: pl.BlockSpec((bt, w), lambda s: (r0 + s, c // w))
    cache = pl.BlockSpec((None, bt, A_KV_HEADS, HEAD_DIM, WINDOW), lambda s: (l, s, 0, 0, 0))
    h0 = pl.BlockSpec((None, bt, S5_LANES), lambda s: (l, s, 0))
    hs = pl.BlockSpec((bt, S5_LANES), lambda s: (s, 0))
    st = pl.BlockSpec((None, bt, GLA_KW, GLA_DV), lambda s: (l, s, 0, 0))
    in_specs = ([pl.BlockSpec(memory_space=pltpu.SMEM), blk(C_Q, A_WIDTH), blk(C_K, KV_WIDTH), blk(C_V, KV_WIDTH),
                 cache, cache, blk(C_U, S5_WIDTH)]
                + _s5_param_specs(l, False) + [h0, h0]
                + [blk(C_GQ, GLA_KW), blk(C_GK, GLA_KW), blk(C_GV, GLA_WIDTH), blk(C_GG, GLA_WIDTH), blk(C_GA, 128)]
                + _gla_param_specs(l) + [st])
    n_in = len(in_specs)
    hs_shape = jax.ShapeDtypeStruct((N_SAMPLE, S5_LANES), F32)
    if l == 0:
        all_layers = lambda shape: pl.BlockSpec((DEPTH, bt) + shape, lambda s: (0, s) + (0,) * len(shape))
        cache_o, st_o = all_layers((A_KV_HEADS, HEAD_DIM, WINDOW)), all_layers((GLA_KW, GLA_DV))
        aliased, aliases = [mix], {n_in: 0}
    else:
        cache_o, st_o = cache, st
        aliased, aliases = [mix, ck_all, cv_all, s_all], {n_in: 0, n_in + 1: 1, n_in + 2: 2, n_in + 3: 5}
    return pl.pallas_call(
        _drop_inputs(functools.partial(_mix_sample_kernel, l), n_in, len(aliased)),
        grid=(N_SAMPLE // bt,),
        in_specs=in_specs + [_any()] * len(aliased),
        out_specs=[blk(0, D_MODEL), cache_o, cache_o, hs, hs, st_o],
        out_shape=[jax.ShapeDtypeStruct((ROWS, D_MODEL), BF16),
                   jax.ShapeDtypeStruct(cache_k.shape, F32), jax.ShapeDtypeStruct(cache_v.shape, F32),
                   hs_shape, hs_shape, jax.ShapeDtypeStruct(state.shape, F32)],
        input_output_aliases=aliases,
        compiler_params=_cparams("arbitrary"),
        name="mix_sample",
    )(sink, h, h, h, cache_k, cache_v, h, *_s5_params(s5_prm, False), h0re, h0im,
      h, h, h, h, h, *gla_prm, state, *aliased)


def _mix_prompt_kernel(l, n_cast, sink_ref, bias_ref, q_ref, kp_ref, kc_ref, vp_ref, vc_ref, u_ref,
                       gq_ref, gk_ref, gv_ref, gg_ref, ga_ref,
                       bb_ref, cre_ref, cim_ref, d_ref, wg_ref, bg_ref, ab_ref, wa_ref, ba_ref, ng_ref, *rest):
    w_refs, (o_ref, hpre_ref, hpim_ref, s_ref) = rest[:n_cast], rest[n_cast:n_cast + 4]
    c_refs = rest[n_cast + 4:]
    s_blk = pl.program_id(0)

    @pl.when(s_blk == 0)
    def _():
        hpre_ref[...] = jnp.zeros_like(hpre_ref)
        hpim_ref[...] = jnp.zeros_like(hpim_ref)
        s_ref[...] = jnp.zeros_like(s_ref)

    for w_ref, c_ref in zip(w_refs, c_refs):
        c_ref[...] = w_ref[...].astype(BF16)
    _s5_block(u_ref, bb_ref, cre_ref, cim_ref, d_ref, wg_ref, bg_ref, ab_ref, o_ref, hpre_ref, hpim_ref)
    _swa_block(l, sink_ref, bias_ref, q_ref, kp_ref, kc_ref, vp_ref, vc_ref, o_ref)
    _gla_block(gq_ref, gk_ref, gv_ref, gg_ref, ga_ref, wa_ref, ba_ref, ng_ref, o_ref, s_ref)


def _mix_prompt(sink, h, s5_prm, gla_prm, w_casts, mix, l):
    cur = _seq_block
    prev = lambda s: _seq_block(jnp.maximum(s - 1, 0))
    blk = lambda c, w, at=cur: pl.BlockSpec((BLK, w), lambda s: (at(s), c // w))
    const = lambda shape: pl.BlockSpec(shape, lambda s: (0,) * len(shape))
    casts = [_cast_specs(w, rows, layer) for w, rows, layer in w_casts]
    in_specs = ([pl.BlockSpec(memory_space=pltpu.SMEM),
                 pl.BlockSpec((None, BLK, 2 * BLK), lambda s: (jnp.minimum(s, 2), 0, 0)), blk(C_Q, A_WIDTH),
                 blk(C_K, KV_WIDTH, prev), blk(C_K, KV_WIDTH), blk(C_V, KV_WIDTH, prev), blk(C_V, KV_WIDTH),
                 blk(C_U, S5_WIDTH),
                 blk(C_GQ, GLA_KW), blk(C_GK, GLA_KW), blk(C_GV, GLA_WIDTH), blk(C_GG, GLA_WIDTH), blk(C_GA, 128)]
                + _s5_param_specs(l, True) + _gla_param_specs(l) + [c[0] for c in casts])
    n_in = len(in_specs)
    return pl.pallas_call(
        _drop_inputs(functools.partial(_mix_prompt_kernel, l, len(casts)), n_in, 1),
        grid=(N_PBLK,),
        in_specs=in_specs + [_any()],
        out_specs=[pl.BlockSpec((BLK, D_MODEL), lambda s: (cur(s), 0)),
                   const((S5_CHUNKS, 128)), const((S5_CHUNKS, 128)), const((GLA_KW, GLA_DV))] + [c[1] for c in casts],
        out_shape=[jax.ShapeDtypeStruct((ROWS, D_MODEL), BF16),
                   jax.ShapeDtypeStruct((S5_CHUNKS, 128), F32), jax.ShapeDtypeStruct((S5_CHUNKS, 128), F32),
                   jax.ShapeDtypeStruct((GLA_KW, GLA_DV), F32)] + [c[2] for c in casts],
        input_output_aliases={n_in: 0},
        compiler_params=_cparams("arbitrary"),
        name="mix_prompt",
    )(sink, _swa_bias(), *([h] * 11), *_s5_params(s5_prm, True), *gla_prm, *[w for w, _, _ in w_casts], mix)


OUT_TM = 768
OUT_PARTS = 3
LN_ROWS = 16


def _out_proj_kernel(mix_ref, w_ref, x_ref, g_ref, b_ref, o_ref):
    i = pl.program_id(0)
    g, b = g_ref[...], b_ref[...]
    part = OUT_TM // OUT_PARTS
    rows = [slice(k * part, (k + 1) * part) for k in range(OUT_PARTS)]
    def norm(k, acc):
        for c in range(part // LN_ROWS):
            r0 = k * part + c * LN_ROWS
            y = _layer_norm_rows(ALPHA * x_ref[r0:r0 + LN_ROWS, :] + acc[c * LN_ROWS:(c + 1) * LN_ROWS, :], g, b)
            o_ref[r0:r0 + LN_ROWS, :] = _zero_pad_rows(y, i * OUT_TM + r0)

    prev = None
    for k, r in enumerate(rows):
        acc = _dot(mix_ref[r, :], w_ref[...])
        if prev is not None:
            norm(k - 1, prev)
        prev = acc
    norm(OUT_PARTS - 1, prev)


def _out_proj(mix, w, x, g, b, l):
    tm = OUT_TM
    return pl.pallas_call(
        _out_proj_kernel,
        grid=(ROWS // tm,),
        in_specs=[
            pl.BlockSpec((tm, D_MODEL), lambda i: (i, 0)),
            pl.BlockSpec((D_MODEL, D_MODEL), lambda i: (0, 0), pipeline_mode=pl.Buffered(1)),
            pl.BlockSpec((tm, D_MODEL), lambda i: (i, 0)),
            _layer_spec((1, D_MODEL), l), _layer_spec((1, D_MODEL), l),
        ],
        out_specs=pl.BlockSpec((tm, D_MODEL), lambda i: (i, 0)),
        out_shape=jax.ShapeDtypeStruct((ROWS, D_MODEL), F32),
        compiler_params=_cparams("arbitrary"),
        name="out_proj",
    )(mix, w, x, g, b)


FFN_TM = 768
FFN_TF = 512
FFN_HALO = 16
FFN_NI = ROWS // FFN_TM
assert ROWS % FFN_TM == 0 and FFN_TM >= N_SAMPLE + BLK + SUB and META_ROW % FFN_HALO == 0


def _ffn_kernel(final, x_ref, xh_ref, wu_ref, wg_ref, cw_ref, cb_ref, wd_ref, p0_ref, p1_ref, g_ref, b_ref,
                o_ref, *rest):
    if final:
        ys_ref, ut_ref, us_ref, xb_ref, a_ref = rest
    else:
        ut_ref, us_ref, xb_ref, a_ref = rest
    i, f = pl.program_id(0), pl.program_id(1)
    tm, hl = FFN_TM, FFN_HALO
    last_i = FFN_NI - 1

    @pl.when(f == 0)
    def _():
        xb_ref[...] = x_ref[...].astype(BF16)
        o_ref[...] = jnp.zeros_like(o_ref)

    cw0, cw1, cw2, cb = cw_ref[0:1, :], cw_ref[1:2, :], cw_ref[2:3, :], cb_ref[...]

    def act(um2, um1, u0, g0):
        return (jax.nn.gelu(cb + cw0 * um2 + cw1 * um1 + cw2 * u0) * g0).astype(BF16)

    xb = xb_ref[...]
    u = _dot(xb, wu_ref[...])
    gte = _dot(xb, wg_ref[...])
    um2, um1 = pltpu.roll(u, 2, 0), pltpu.roll(u, 1, 0)
    body = slice(hl, tm - N_SAMPLE)
    a_ref[body, :] = act(um2[body, :], um1[body, :], u[body, :], gte[body, :])
    ext = jnp.concatenate([_dot(xh_ref[...].astype(BF16), wu_ref[...]), u[0:hl, :]], axis=0)
    a_ref[0:hl, :] = act(ext[hl - 2:2 * hl - 2, :], ext[hl - 1:2 * hl - 1, :], u[0:hl, :], gte[0:hl, :])
    tail = slice(tm - N_SAMPLE, tm)
    samp = i == last_i
    a_ref[tail, :] = act(jnp.where(samp, p0_ref[...], um2[tail, :]), jnp.where(samp, p1_ref[...], um1[tail, :]),
                         u[tail, :], gte[tail, :])
    o_ref[...] += _dot(a_ref[...], wd_ref[...])
    ut_ref[...] = u[tm - N_SAMPLE - BLK - SUB:tm - N_SAMPLE - BLK, :]
    us_ref[0] = p1_ref[...]
    us_ref[1] = u[tm - N_SAMPLE:, :]

    @pl.when(f == pl.num_programs(1) - 1)
    def _():
        _deepnorm_ln(o_ref, x_ref, g_ref, b_ref, o_ref, i * tm, tm)
        if final:
            ys_ref[...] = o_ref[tm - N_SAMPLE:, :]


def _ffn(x, wup, cw, cb, wd, conv_p0, conv_p1, g, b, l, final):
    tm, tf = FFN_TM, FFN_TF
    nf = D_FF // tf
    tail = lambda i, f: jnp.where(i == FFN_NI - 1, f, 0)
    halo = lambda i, f: (jnp.where(i == 0, ROWS_P // FFN_HALO, i * (tm // FFN_HALO)) - 1, 0)
    y_spec = pl.BlockSpec((tm, D_MODEL), lambda i, f: (i, 0))
    y_specs = [y_spec, pl.BlockSpec((N_SAMPLE, D_MODEL), lambda i, f: (0, 0))] if final else [y_spec]
    y_shapes = ([jax.ShapeDtypeStruct((SEQ, D_MODEL), F32), jax.ShapeDtypeStruct((N_SAMPLE, D_MODEL), F32)]
                if final else [jax.ShapeDtypeStruct((ROWS, D_MODEL), F32)])
    return pl.pallas_call(
        functools.partial(_ffn_kernel, final),
        grid=(FFN_NI, nf),
        in_specs=[
            pl.BlockSpec((tm, D_MODEL), lambda i, f: (i, 0)),
            pl.BlockSpec((FFN_HALO, D_MODEL), halo),
            pl.BlockSpec((D_MODEL, tf), lambda i, f: (0, f)),
            pl.BlockSpec((D_MODEL, tf), lambda i, f: (0, nf + f)),
            pl.BlockSpec((None, CONV_W, tf), lambda i, f: (l, 0, f)),
            pl.BlockSpec((None, 1, tf), lambda i, f: (l, 0, f)),
            pl.BlockSpec((tf, D_MODEL), lambda i, f: (f, 0)),
            pl.BlockSpec((None, None, N_SAMPLE, tf), lambda i, f: (l, 0, 0, tail(i, f))),
            pl.BlockSpec((None, None, N_SAMPLE, tf), lambda i, f: (l, 1, 0, tail(i, f))),
            _layer_spec((1, D_MODEL), l), _layer_spec((1, D_MODEL), l),
        ],
        out_specs=y_specs + [
            pl.BlockSpec((SUB, tf), lambda i, f: (0, tail(i, f))),
            pl.BlockSpec((2, N_SAMPLE, tf), lambda i, f: (0, 0, tail(i, f))),
        ],
        out_shape=y_shapes + [
            jax.ShapeDtypeStruct((SUB, D_FF), F32),
            jax.ShapeDtypeStruct((2, N_SAMPLE, D_FF), F32),
        ],
        scratch_shapes=[pltpu.VMEM((tm, D_MODEL), BF16), pltpu.VMEM((tm, tf), BF16)],
        compiler_params=_cparams("arbitrary", "arbitrary"),
        name="conv_ffn",
    )(x, x, wup, wup, cw, cb, wd, conv_p0, conv_p1, g, b)


def _rope_table():
    inv = jnp.power(ROPE_THETA, -jnp.arange(ROPE_HALF, dtype=F32) / ROPE_HALF)
    pos = jnp.concatenate([jnp.arange(SEQ, dtype=jnp.int32) + N_META, jnp.zeros((LEAD,), jnp.int32),
                           jnp.arange(N_META, dtype=jnp.int32), jnp.full((N_SAMPLE,), PAST_LEN, jnp.int32)])
    ang = pos.astype(F32)[:, None] * inv[None, :]
    return jnp.concatenate([jnp.cos(ang), jnp.sin(ang)], axis=1)


def _s5_tables(lam_re, lam_im, log_dt, b_re, b_im, c_re, c_im, d, w_glu, b_glu):
    L = lam_re.shape[0]
    dt = jnp.exp(log_dt)[..., None]
    mag = jnp.exp(lam_re * dt)
    ab_re, ab_im = mag * jnp.cos(lam_im * dt), mag * jnp.sin(lam_im * dt)
    den = lam_re * lam_re + lam_im * lam_im
    nr, ni = ab_re - 1.0, ab_im
    f_re = (nr * lam_re + ni * lam_im) / den
    f_im = (ni * lam_re - nr * lam_im) / den
    bb_re = f_re[..., None] * b_re - f_im[..., None] * b_im
    bb_im = f_re[..., None] * b_im + f_im[..., None] * b_re
    gi, go = S5_GROUPS // S5_IN_PIECES, S5_GROUPS // S5_OUT_PIECES

    def bd_in(t):
        t = t.reshape(L, S5_IN_PIECES, gi, S5_STATE, S5_CH)
        return jnp.einsum('lqgph,gk->lqghkp', t, jnp.eye(gi, dtype=F32)).reshape(L, S5_IN_PIECES, S5_IN_K, S5_IN_N)

    def bd_out(t):
        t = t.reshape(L, S5_OUT_PIECES, go, S5_CH, S5_STATE)
        return jnp.einsum('lqghp,gk->lqgpkh', t, jnp.eye(go, dtype=F32)).reshape(
            L, S5_OUT_PIECES, S5_OUT_K, S5_OUT_N)

    ar, ai = ab_re.reshape(L, 1, S5_LANES), ab_im.reshape(L, 1, S5_LANES)
    pows = [(ar, ai)]
    for _ in range(SUB - 1):
        pr, pi = pows[-1]
        pows.append((pr * ar - pi * ai, pr * ai + pi * ar))
    row = jnp.arange(SUB)[None, :, None]
    parts = []
    for sft in S5_SHIFTS:
        pr, pi = pows[sft - 1]
        parts += [jnp.where(row >= sft, pr, 0.0), jnp.where(row >= sft, pi, 0.0)]
    parts += [jnp.concatenate([p[0] for p in pows], 1), jnp.concatenate([p[1] for p in pows], 1)]
    return dict(bb=jnp.concatenate([bd_in(bb_re), bd_in(bb_im)], axis=3).astype(BF16),
                c_re=bd_out(c_re).astype(BF16), c_im=bd_out(c_im).astype(BF16),
                d=d.reshape(L, 1, S5_WIDTH), w_glu=w_glu.astype(BF16), b_glu=b_glu.reshape(L, 1, S5_WIDTH),
                tab=jnp.concatenate(parts, 1),
                ab=jnp.concatenate([ab_re.reshape(L, S5_CHUNKS, 128), ab_im.reshape(L, S5_CHUNKS, 128)], axis=1))


@jax.jit
def kernel(x_prompt, x_sample, cache_swa_k, cache_swa_v, state_ssm_re, state_ssm_im, state_gla, state_conv,
           meta_tokens, ln_in_g, ln_in_b, w_in, attn_sink, s5_lam_re, s5_lam_im, s5_log_dt, s5_b_re, s5_b_im,
           s5_c_re, s5_c_im, s5_d, s5_w_glu, s5_b_glu, gla_w_a2, gla_b_a, gla_norm_g, w_out, ln1_g, ln1_b,
           ffn_w_up, ffn_conv_w, ffn_conv_b, ffn_w_down, ln2_g, ln2_b):
    L = DEPTH
    row3 = lambda t: t.reshape(L, 1, -1)
    wa_b = jnp.pad(gla_w_a2, ((0, 0), (0, 128 - GLA_LOWRANK), (0, 0))).astype(BF16)
    gla_args = (wa_b, row3(gla_b_a), row3(gla_norm_g))
    ln1, ln2 = (row3(ln1_g), row3(ln1_b)), (row3(ln2_g), row3(ln2_b))
    conv_b3 = row3(ffn_conv_b)
    cs = _rope_table()
    s5p = _s5_tables(s5_lam_re, s5_lam_im, s5_log_dt, s5_b_re, s5_b_im, s5_c_re, s5_c_im, s5_d, s5_w_glu, s5_b_glu)
    cache_t = lambda c: jnp.transpose(c, (0, 1, 3, 4, 2))
    ck, cv = cache_t(cache_swa_k), cache_t(cache_swa_v)
    ck_all = cv_all = s_all = None
    h0re = state_ssm_re.reshape(L, N_SAMPLE, S5_LANES)
    h0im = state_ssm_im.reshape(L, N_SAMPLE, S5_LANES)
    sg = state_gla.reshape(L, N_SAMPLE, GLA_KW, GLA_DV)
    conv_p0 = conv_p1 = jnp.transpose(state_conv, (0, 2, 1, 3))
    mix = jnp.zeros((ROWS, D_MODEL), BF16)
    w_in_t = jnp.swapaxes(w_in, 1, 2)

    x, w_in_b = _ln_in(x_prompt.reshape(SEQ, D_MODEL), meta_tokens, x_sample.reshape(N_SAMPLE, D_MODEL),
                       ln_in_g.reshape(1, D_MODEL), ln_in_b.reshape(1, D_MODEL), w_in_t, 240)
    outs = [[] for _ in range(9)]
    for l in range(L):
        h = _in_proj(x, w_in_b, cs, l)
        casts = [(ffn_w_up, 32, l), (ffn_w_down, 176, l), (w_out, 32, l)] + ([(w_in_t, 80, l + 1)] if l + 1 < L else [])
        mix, hp_re, hp_im, s_p, w_up_b, w_down_b, w_out_b, *w_next = _mix_prompt(
            attn_sink, h, s5p, gla_args, casts, mix, l)
        w_in_b = w_next[0] if w_next else None
        mix, ck_all, cv_all, hs_re, hs_im, s_all = _mix_sample(
            attn_sink, h, ck, cv, s5p, h0re, h0im, gla_args, sg, mix, ck_all, cv_all, s_all, l)
        x = _out_proj(mix, w_out_b, x, *ln1, l)
        res = _ffn(x, w_up_b, ffn_conv_w, conv_b3, w_down_b, conv_p0, conv_p1, *ln2, l, l == L - 1)
        if l == L - 1:
            y_prompt, y_sample, u_tail, u_s = res
        else:
            x, u_tail, u_s = res
        kv_p = lambda c: h[SEQ - WINDOW:SEQ, c:c + KV_WIDTH].reshape(1, WINDOW, A_KV_HEADS, HEAD_DIM)
        new = (kv_p(C_K), kv_p(C_V),
               hp_re.reshape(1, S5_GROUPS, S5_STATE), hp_im.reshape(1, S5_GROUPS, S5_STATE),
               s_p.reshape(1, GLA_HEADS, GLA_DK, GLA_DV), u_tail[SUB - 2:SUB][None],
               hs_re.reshape(N_SAMPLE, S5_GROUPS, S5_STATE), hs_im.reshape(N_SAMPLE, S5_GROUPS, S5_STATE),
               u_s)
        for lst, val in zip(outs, new):
            lst.append(val)
    st = [jnp.stack(lst) for lst in outs]
    return (y_prompt.reshape(1, SEQ, D_MODEL), y_sample.reshape(N_SAMPLE, 1, D_MODEL),
            st[0], st[1], st[2], st[3], st[4], st[5],
            jnp.transpose(ck_all, (0, 1, 4, 2, 3)), jnp.transpose(cv_all, (0, 1, 4, 2, 3)),
            st[6], st[7], s_all.reshape(L, N_SAMPLE, GLA_HEADS, GLA_DK, GLA_DV),
            jnp.transpose(st[8], (0, 2, 1, 3)))
```

```python
import functools

import jax
import jax.numpy as jnp
from jax import lax
from jax.experimental import pallas as pl
from jax.experimental.pallas import tpu as pltpu

F32 = jnp.float32
BF16 = jnp.bfloat16

D_MODEL = 2048
SEQ = 8192
DEPTH = 2
N_SAMPLE = 128
N_META = 16
HEAD_DIM = 64
A_WIDTH = 1024
A_HEADS = 16
A_KV_HEADS = 4
GQA = 4
KV_WIDTH = A_KV_HEADS * HEAD_DIM
WINDOW = 128
ROPE_DIM = 16
ROPE_HALF = ROPE_DIM // 2
ROPE_THETA = 500000.0
PAST_LEN = 8192
S5_WIDTH = 512
S5_CH = 16
S5_GROUPS = 32
S5_STATE = 64
S5_LANES = S5_GROUPS * S5_STATE
GLA_WIDTH = 512
GLA_HEADS = 4
GLA_DV = 128
GLA_DK = 64
GLA_KW = GLA_HEADS * GLA_DK
GLA_LOWRANK = 16
GLA_TAU = 16.0
GLA_CHUNK = 64
D_FF = 5632
CONV_W = 3
LN_EPS = 1e-5
ALPHA = (2 * DEPTH) ** 0.25
NEG_INF = -1e30

BLK = 128
T_PROMPT = N_META + SEQ
LEAD = (-T_PROMPT) % BLK
N_PBLK = (LEAD + T_PROMPT) // BLK
ROWS_P = N_PBLK * BLK
ROWS = ROWS_P + N_SAMPLE
META_ROW = SEQ + LEAD
SUB = 8
BF16_ROWS = 16
MXU_COLS = 256

C_Q, C_K, C_V, C_U, C_GQ, C_GK, C_GV, C_GG, C_GA = 0, 1024, 1280, 1536, 2048, 2304, 2560, 3072, 3584
IN_COLS = 3600
IN_TILE = 1280
IN_PAD = 3 * IN_TILE
ROPE_COLS = A_WIDTH + KV_WIDTH
M_A, M_B, M_C = 0, A_WIDTH, A_WIDTH + S5_WIDTH

VMEM_LIMIT = 56 * 1024 * 1024


def _cparams(*sem):
    return pltpu.CompilerParams(dimension_semantics=sem, vmem_limit_bytes=VMEM_LIMIT)


def _any():
    return pl.BlockSpec(memory_space=pl.ANY)


def _drop_inputs(kern, start, count):
    def wrapped(*refs):
        return kern(*refs[:start], *refs[start + count:])
    return wrapped


def _seq_block(s):
    return jnp.where(s == 0, N_PBLK - 1, s - 1)


def _layer_spec(shape, l):
    return pl.BlockSpec((None,) + shape, lambda *_: (l,) + (0,) * len(shape))


def _cast_specs(w, steps, l):
    n_rows, cols = w.shape[1], w.shape[2]
    n = next(n for n in range(steps, 0, -1) if n_rows % n == 0 and (n_rows // n) % BF16_ROWS == 0)
    rows = n_rows // n
    chunk = lambda s: jnp.minimum(s, n - 1)
    return (pl.BlockSpec((None, rows, cols), lambda s: (l, chunk(s), 0)),
            pl.BlockSpec((rows, cols), lambda s: (chunk(s), 0)),
            jax.ShapeDtypeStruct(w.shape[1:], BF16))


def _layer_norm_rows(x, g, b):
    mu = jnp.mean(x, -1, keepdims=True)
    xc = x - mu
    var = jnp.mean(xc * xc, -1, keepdims=True)
    return xc * lax.rsqrt(var + LN_EPS) * g + b


def _zero_pad_rows(y, first_row):
    row = first_row + lax.broadcasted_iota(jnp.int32, y.shape, 0)
    return jnp.where((row >= SEQ) & (row < META_ROW), 0.0, y)


def _deepnorm_ln(acc_ref, x_ref, g_ref, b_ref, o_ref, first_row, rows):
    g, b = g_ref[...], b_ref[...]
    for c in range(rows // BLK):
        r = slice(c * BLK, (c + 1) * BLK)
        y = _layer_norm_rows(ALPHA * x_ref[r, :] + acc_ref[r, :], g, b)
        o_ref[r, :] = _zero_pad_rows(y, first_row + c * BLK)


def _dot(a, b):
    return jnp.dot(a, b, preferred_element_type=F32)


def _dot_nt(a, b):
    return lax.dot_general(a, b, (((1,), (1,)), ((), ())), preferred_element_type=F32)


def _dot_tn(a, b):
    return lax.dot_general(a, b, (((0,), (0,)), ((), ())), preferred_element_type=F32)


def _log_sigmoid(x):
    return jnp.minimum(x, 0.0) - jnp.log(1.0 + jnp.exp(-jnp.abs(x)))


LN_IN_TM = 512
LN_IN_STEPS = SEQ // LN_IN_TM + 1
assert SEQ % LN_IN_TM == 0 and ROWS - SEQ <= LN_IN_TM


def _ln_in_kernel(xp_ref, meta_ref, xs_ref, g_ref, b_ref, wsrc_ref, o_ref, wdst_ref):
    i = pl.program_id(0)
    g, b = g_ref[...], b_ref[...]
    wdst_ref[...] = wsrc_ref[...].astype(BF16)

    @pl.when(i < LN_IN_STEPS - 1)
    def _():
        for c in range(LN_IN_TM // BLK):
            r = slice(c * BLK, (c + 1) * BLK)
            o_ref[r, :] = _layer_norm_rows(xp_ref[r, :], g, b)

    @pl.when(i == LN_IN_STEPS - 1)
    def _():
        o_ref[0:LEAD, :] = jnp.zeros((LEAD, D_MODEL), F32)
        o_ref[LEAD:BLK, :] = _layer_norm_rows(meta_ref[...], g, b)
        o_ref[BLK:BLK + N_SAMPLE, :] = _layer_norm_rows(xs_ref[...], g, b)


def _ln_in(xp, meta, xs, g, b, w_cast):
    tm = LN_IN_TM
    const = lambda shape: pl.BlockSpec(shape, lambda i: (0,) * len(shape))
    w_in_spec, w_out_spec, w_shape = _cast_specs(w_cast, LN_IN_STEPS, 0)
    return pl.pallas_call(
        _ln_in_kernel,
        grid=(LN_IN_STEPS,),
        in_specs=[
            pl.BlockSpec((tm, D_MODEL), lambda i: (jnp.minimum(i, SEQ // tm - 1), 0)),
            const((N_META, D_MODEL)), const((N_SAMPLE, D_MODEL)), const((1, D_MODEL)), const((1, D_MODEL)),
            w_in_spec,
        ],
        out_specs=[pl.BlockSpec((tm, D_MODEL), lambda i: (i, 0)), w_out_spec],
        out_shape=[jax.ShapeDtypeStruct((ROWS, D_MODEL), F32), w_shape],
        compiler_params=_cparams("arbitrary"),
        name="ln_in",
    )(xp, meta, xs, g, b, w_cast)


IN_TM = 1056


def _in_proj_kernel(x_ref, w_ref, cs_ref, o_ref, xb_ref):
    j = pl.program_id(1)

    last_j = IN_PAD // IN_TILE - 1

    @pl.when(j == 0)
    def _():
        xb = x_ref[...].astype(BF16)
        xb_ref[...] = xb
        c8, s8 = cs_ref[:, 0:ROPE_HALF], cs_ref[:, ROPE_HALF:ROPE_DIM]
        rest = HEAD_DIM - ROPE_DIM
        one, zero = jnp.ones((IN_TM, rest), F32), jnp.zeros((IN_TM, rest), F32)
        z8 = jnp.zeros((IN_TM, ROPE_HALF), F32)
        cos = jnp.concatenate([c8, c8, one] * 2, axis=1)
        sa = jnp.concatenate([-s8, z8, zero] * 2, axis=1)
        sb = jnp.concatenate([z8, s8, zero] * 2, axis=1)
        for c in range(ROPE_COLS // MXU_COLS):
            acc = _dot_nt(xb, w_ref[c * MXU_COLS:(c + 1) * MXU_COLS, :])
            for hlf in range(MXU_COLS // 128):
                blk = acc[:, hlf * 128:(hlf + 1) * 128]
                o_ref[:, c * MXU_COLS + hlf * 128:c * MXU_COLS + (hlf + 1) * 128] = (
                    blk * cos + pltpu.roll(blk, 128 - ROPE_HALF, 1) * sa + pltpu.roll(blk, ROPE_HALF, 1) * sb)

    @pl.when((j > 0) & (j < last_j))
    def _():
        o_ref[...] = _dot_nt(xb_ref[...], w_ref[...])

    @pl.when(j == last_j)
    def _():
        o_ref[...] = _dot_nt(xb_ref[...], w_ref[...])
        edge = (IN_COLS % IN_TILE) // 128 * 128
        col = edge + lax.broadcasted_iota(jnp.int32, (IN_TM, 128), 1)
        o_ref[:, edge:edge + 128] = jnp.where(col < IN_COLS % IN_TILE, o_ref[:, edge:edge + 128], 0.0)
        o_ref[:, edge + 128:] = jnp.zeros((IN_TM, IN_TILE - edge - 128), F32)


def _in_proj(x, w, cs, l):
    return pl.pallas_call(
        _in_proj_kernel,
        grid=(ROWS // IN_TM, IN_PAD // IN_TILE),
        in_specs=[
            pl.BlockSpec((IN_TM, D_MODEL), lambda i, j: (i, 0)),
            pl.BlockSpec((IN_TILE, D_MODEL), lambda i, j: (j, 0)),
            pl.BlockSpec((IN_TM, ROPE_DIM), lambda i, j: (i, 0)),
        ],
        out_specs=pl.BlockSpec((IN_TM, IN_TILE), lambda i, j: (i, j)),
        out_shape=jax.ShapeDtypeStruct((ROWS, IN_PAD), F32),
        scratch_shapes=[pltpu.VMEM((IN_TM, D_MODEL), BF16)],
        compiler_params=_cparams("arbitrary", "arbitrary"),
        name="in_proj",
    )(x, w, cs)


def _swa_bias():
    row = lax.broadcasted_iota(jnp.int32, (3, BLK, 2 * BLK), 1)
    col = lax.broadcasted_iota(jnp.int32, (3, BLK, 2 * BLK), 2)
    s_blk = lax.broadcasted_iota(jnp.int32, (3, BLK, 2 * BLK), 0)
    diff = row - col + BLK
    kpos = (s_blk - 1) * BLK + col - LEAD
    ok = (diff >= 0) & (diff <= WINDOW) & (kpos >= 0)
    return jnp.where(ok, 0.0, NEG_INF).astype(F32)


def _swa_block(l, sink_ref, bias_ref, q_ref, kp_ref, kc_ref, vp_ref, vc_ref, o_ref):
    k2 = jnp.concatenate([kp_ref[...], kc_ref[...]], axis=0).astype(BF16)
    v2 = jnp.concatenate([vp_ref[...], vc_ref[...]], axis=0).astype(BF16)
    nq = GQA * BLK
    bias = jnp.concatenate([bias_ref[...]] * GQA, axis=0)
    low = lax.broadcasted_iota(jnp.int32, (BLK, 2 * HEAD_DIM), 1) < HEAD_DIM
    g_of_row = lax.broadcasted_iota(jnp.int32, (nq, 1), 0) // BLK
    ones = jnp.ones((2 * BLK, 2 * HEAD_DIM), BF16)
    scores, sinks = [], []
    for kv in range(A_KV_HEADS):
        kh = k2[:, kv * HEAD_DIM:(kv + 1) * HEAD_DIM]
        kk = jnp.concatenate([kh, kh], axis=1)
        parts = []
        for pair in range(GQA // 2):
            c0 = (kv * GQA + 2 * pair) * HEAD_DIM
            qp = q_ref[:, c0:c0 + 2 * HEAD_DIM] * HEAD_DIM ** -0.5
            parts += [jnp.where(low, qp, 0.0).astype(BF16), jnp.where(low, 0.0, qp).astype(BF16)]
        q4 = jnp.concatenate(parts, axis=0)
        sk = jnp.zeros((nq, 1), F32)
        for g in range(GQA):
            sk = jnp.where(g_of_row == g, sink_ref[l, kv * GQA + g], sk)
        scores.append(_dot_nt(q4, kk) + bias)
        sinks.append(sk)
    probs, sink_terms = [], []
    for s, sk in zip(scores, sinks):
        m = jnp.maximum(jnp.max(s, -1, keepdims=True), sk)
        probs.append(jnp.exp(s - m).astype(BF16))
        sink_terms.append(jnp.exp(sk - m))
    for kv in range(A_KV_HEADS):
        vh = v2[:, kv * HEAD_DIM:(kv + 1) * HEAD_DIM]
        v3 = jnp.concatenate([vh, vh, ones], axis=1)
        oa = _dot(probs[kv], v3)
        on = oa[:, 0:2 * HEAD_DIM] / (oa[:, 2 * HEAD_DIM:] + sink_terms[kv])
        for pair in range(GQA // 2):
            c0 = (kv * GQA + 2 * pair) * HEAD_DIM
            lo, hi = on[2 * pair * BLK:(2 * pair + 1) * BLK], on[(2 * pair + 1) * BLK:(2 * pair + 2) * BLK]
            o_ref[:, M_A + c0:M_A + c0 + 2 * HEAD_DIM] = jnp.where(low, lo, hi).astype(BF16)


SWA_BT = 16


def _swa_sample_kernel(l, sink_ref, q_ref, kn_ref, vn_ref, ck_ref, cv_ref, o_ref, cko_ref, cvo_ref):
    bt = SWA_BT
    kn, vn = kn_ref[...], vn_ref[...]

    def cols(x):
        return jnp.transpose(jnp.concatenate([x, jnp.zeros((128 - bt, KV_WIDTH), F32)], axis=0))

    kn_t, vn_t = cols(kn), cols(vn)
    last = lax.broadcasted_iota(jnp.int32, (HEAD_DIM, WINDOW), 1) == WINDOW - 1
    for j in range(bt):
        for kv in range(A_KV_HEADS):
            hd = slice(kv * HEAD_DIM, (kv + 1) * HEAD_DIM)
            cko_ref[j, kv] = jnp.where(last, kn_t[hd, j:j + 1], pltpu.roll(ck_ref[j, kv], WINDOW - 1, 1))
            cvo_ref[j, kv] = jnp.where(last, vn_t[hd, j:j + 1], pltpu.roll(cv_ref[j, kv], WINDOW - 1, 1))
    nq = GQA * bt
    rowb = lax.broadcasted_iota(jnp.int32, (nq, bt * WINDOW), 0) % bt
    colb = lax.broadcasted_iota(jnp.int32, (nq, bt * WINDOW), 1) // WINDOW
    same = rowb == colb
    rg = lax.broadcasted_iota(jnp.int32, (nq, 1), 0) // bt
    for kv in range(A_KV_HEADS):
        sl = slice(kv * HEAD_DIM, (kv + 1) * HEAD_DIM)
        q4 = jnp.concatenate(
            [q_ref[:, (kv * GQA + g) * HEAD_DIM:(kv * GQA + g + 1) * HEAD_DIM] for g in range(GQA)], axis=0)
        q4 = q4 * HEAD_DIM ** -0.5
        kn4 = jnp.concatenate([kn[:, sl]] * GQA, axis=0)
        vn4 = jnp.concatenate([vn[:, sl]] * GQA, axis=0)
        sk = jnp.zeros((nq, 1), F32)
        for g in range(GQA):
            sk = jnp.where(rg == g, sink_ref[l, kv * GQA + g], sk)
        kt = jnp.concatenate([ck_ref[j, kv] for j in range(bt)], axis=1).astype(BF16)
        vt = jnp.concatenate([cv_ref[j, kv] for j in range(bt)], axis=1).astype(BF16)
        s = jnp.where(same, _dot(q4.astype(BF16), kt), NEG_INF)
        s_self = jnp.sum(q4 * kn4, -1, keepdims=True)
        m = jnp.maximum(jnp.maximum(jnp.max(s, -1, keepdims=True), s_self), sk)
        p = jnp.exp(s - m)
        p_self = jnp.exp(s_self - m)
        den = jnp.sum(p, -1, keepdims=True) + p_self + jnp.exp(sk - m)
        o = (_dot_nt(p.astype(BF16), vt) + p_self * vn4) / den
        for g in range(GQA):
            hh = kv * GQA + g
            o_ref[:, hh * HEAD_DIM:(hh + 1) * HEAD_DIM] = o[g * bt:(g + 1) * bt].astype(BF16)


S5_SHIFTS = (1, 2, 4)
S5_NTAB = 2 * (len(S5_SHIFTS) + 1) * SUB
S5_CHUNKS = S5_LANES // 128
S5_IN_PIECES = 4
S5_IN_K = S5_WIDTH // S5_IN_PIECES
S5_IN_N = S5_LANES // S5_IN_PIECES
S5_OUT_PIECES = 2
S5_OUT_K = S5_LANES // S5_OUT_PIECES
S5_OUT_N = S5_WIDTH // S5_OUT_PIECES


def _s5_tab(tab_ref, k, lanes):
    return (tab_ref[2 * k * SUB:(2 * k + 1) * SUB, lanes], tab_ref[(2 * k + 1) * SUB:(2 * k + 2) * SUB, lanes])


def _s5_drive(ub, bb_ref, p):
    return _dot(ub[:, p * S5_IN_K:(p + 1) * S5_IN_K], bb_ref[p])


def _s5_readout(h_piece, u, cre_ref, cim_ref, d_ref, wg_ref, bg_ref):
    ys = []
    for j in range(S5_OUT_PIECES):
        hr, hi = h_piece(j)
        ys.append(_dot(hr.astype(BF16), cre_ref[j]) - _dot(hi.astype(BF16), cim_ref[j]))
    y = jnp.concatenate(ys, axis=1) + d_ref[...] * u
    z = jax.nn.gelu(y)
    gate = jax.nn.sigmoid(_dot(z.astype(BF16), wg_ref[...]) + bg_ref[...])
    return (z * gate).astype(BF16)


def _s5_block(u_ref, bb_ref, cre_ref, cim_ref, d_ref, wg_ref, bg_ref, ab_ref, o_ref, hpre_ref, hpim_ref):
    u = u_ref[...]
    ub = u.astype(BF16)
    drive = [_s5_drive(ub, bb_ref, p) for p in range(S5_IN_PIECES)]
    x3r = jnp.concatenate([d[:, 0:S5_IN_N] for d in drive], axis=1).reshape(BLK, S5_CHUNKS, 128)
    x3i = jnp.concatenate([d[:, S5_IN_N:] for d in drive], axis=1).reshape(BLK, S5_CHUNKS, 128)
    ar, ai = ab_ref[0:S5_CHUNKS, :], ab_ref[S5_CHUNKS:, :]
    hr, hi = hpre_ref[...], hpim_ref[...]
    hs_r, hs_i = [], []
    for t in range(BLK):
        hr, hi = ar * hr - ai * hi + x3r[t], ar * hi + ai * hr + x3i[t]
        hs_r.append(hr)
        hs_i.append(hi)
    hpre_ref[...] = hr
    hpim_ref[...] = hi
    h2r = jnp.stack(hs_r, axis=0).reshape(BLK, S5_LANES)
    h2i = jnp.stack(hs_i, axis=0).reshape(BLK, S5_LANES)

    def piece(j):
        return h2r[:, j * S5_OUT_K:(j + 1) * S5_OUT_K], h2i[:, j * S5_OUT_K:(j + 1) * S5_OUT_K]

    o_ref[:, M_B:M_B + S5_WIDTH] = _s5_readout(piece, u, cre_ref, cim_ref, d_ref, wg_ref, bg_ref)


def _s5_sample_kernel(u_ref, bb_ref, cre_ref, cim_ref, d_ref, wg_ref, bg_ref, tab_ref, h0re_ref, h0im_ref,
                      o_ref, hsre_ref, hsim_ref):
    u = u_ref[...]
    ub = u.astype(BF16)
    for p in range(S5_IN_PIECES):
        lanes = slice(p * S5_IN_N, (p + 1) * S5_IN_N)
        bu = _s5_drive(ub, bb_ref, p)
        pr, pi = _s5_tab(tab_ref, len(S5_SHIFTS), lanes)
        ar, ai = pr[0:1, :], pi[0:1, :]
        h0r, h0i = h0re_ref[:, lanes], h0im_ref[:, lanes]
        hsre_ref[:, lanes] = ar * h0r - ai * h0i + bu[:, 0:S5_IN_N]
        hsim_ref[:, lanes] = ar * h0i + ai * h0r + bu[:, S5_IN_N:]
    piece = lambda j: (hsre_ref[:, j * S5_OUT_K:(j + 1) * S5_OUT_K], hsim_ref[:, j * S5_OUT_K:(j + 1) * S5_OUT_K])
    o_ref[:, M_B:M_B + S5_WIDTH] = _s5_readout(piece, u, cre_ref, cim_ref, d_ref, wg_ref, bg_ref)


def _s5_param_specs(l, prompt):
    return [
        _layer_spec((S5_IN_PIECES, S5_IN_K, 2 * S5_IN_N), l),
        _layer_spec((S5_OUT_PIECES, S5_OUT_K, S5_OUT_N), l), _layer_spec((S5_OUT_PIECES, S5_OUT_K, S5_OUT_N), l),
        _layer_spec((1, S5_WIDTH), l), _layer_spec((S5_WIDTH, S5_WIDTH), l), _layer_spec((1, S5_WIDTH), l),
        _layer_spec((2 * S5_CHUNKS, 128), l) if prompt else _layer_spec((S5_NTAB, S5_LANES), l),
    ]


def _s5_params(prm, prompt):
    return (prm["bb"], prm["c_re"], prm["c_im"], prm["d"], prm["w_glu"], prm["b_glu"],
            prm["ab"] if prompt else prm["tab"])


def _gla_log_gate(ac, wa_ref, ba_ref):
    return _log_sigmoid(_dot(ac.astype(BF16), wa_ref[...]) + ba_ref[...]) * (1.0 / GLA_TAU)


def _gla_finish(o_heads, gate, ng):
    outs = [o * lax.rsqrt(jnp.mean(o * o, -1, keepdims=True) + LN_EPS) for o in o_heads]
    return jnp.concatenate(outs, axis=1) * ng * (gate * jax.nn.sigmoid(gate))


def _gla_block(q_ref, k_ref, v_ref, gate_ref, ac_ref, wa_ref, ba_ref, ng_ref, o_ref, s_ref):
    lg = _gla_log_gate(ac_ref[...], wa_ref, ba_ref)
    L = GLA_CHUNK
    row = lax.broadcasted_iota(jnp.int32, (L, GLA_KW), 0)
    causal = lax.broadcasted_iota(jnp.int32, (L, L), 0) >= lax.broadcasted_iota(jnp.int32, (L, L), 1)
    chunks = range(BLK // L)
    heads = range(GLA_HEADS)
    ks = lambda hh: slice(hh * GLA_DK, (hh + 1) * GLA_DK)
    vs = lambda hh: slice(hh * GLA_DV, (hh + 1) * GLA_DV)
    q_in, k_out, k_dec, dec, vb = [], [], [], [], []
    for c in chunks:
        rows = slice(c * L, (c + 1) * L)
        b = lg[rows, :]
        d = 1
        while d < L:
            b = b + jnp.where(row >= d, pltpu.roll(b, d, 0), 0.0)
            d *= 2
        b_end = b[L - 1:L, :]
        kc = k_ref[rows, :]
        q_in.append((q_ref[rows, :] * GLA_DK ** -0.5 * jnp.exp(b)).astype(BF16))
        k_out.append((kc * jnp.exp(-b)).astype(BF16))
        k_dec.append((kc * jnp.exp(b_end - b)).astype(BF16))
        dec.append(jnp.transpose(jnp.broadcast_to(jnp.exp(b_end), (GLA_DV, GLA_KW))))
        vb.append(v_ref[rows, :].astype(BF16))
    att = [[jnp.where(causal, _dot_nt(q_in[c][:, ks(hh)], k_out[c][:, ks(hh)]), 0.0).astype(BF16) for hh in heads]
           for c in chunks]
    kv = [[_dot_tn(k_dec[c][:, ks(hh)], vb[c][:, vs(hh)]) for hh in heads] for c in chunks]
    state = [s_ref[...]]
    for c in chunks:
        state.append(dec[c] * state[c] + jnp.concatenate(kv[c], axis=0))
    s_ref[...] = state[-1]
    for c in chunks:
        sb = state[c].astype(BF16)
        o_heads = [_dot(q_in[c][:, ks(hh)], sb[ks(hh), :]) + _dot(att[c][hh], vb[c][:, vs(hh)]) for hh in heads]
        o_ref[c * L:(c + 1) * L, M_C:M_C + GLA_WIDTH] = _gla_finish(
            o_heads, gate_ref[c * L:(c + 1) * L, :], ng_ref[...]).astype(BF16)


def _gla_param_specs(l):
    return [_layer_spec((128, GLA_KW), l), _layer_spec((1, GLA_KW), l), _layer_spec((1, GLA_WIDTH), l)]


GLA_BT = 16


def _gla_sample_kernel(q_ref, k_ref, v_ref, gate_ref, ac_ref, wa_ref, ba_ref, ng_ref, s_ref, o_ref, so_ref):
    bt = GLA_BT
    lg = _gla_log_gate(ac_ref[...], wa_ref, ba_ref)
    v = v_ref[...]

    def cols(x):
        return jnp.transpose(jnp.concatenate([x, jnp.zeros((128 - bt, GLA_KW), F32)], axis=0))

    eg_t, k_t, q_t = cols(jnp.exp(lg)), cols(k_ref[...]), cols(q_ref[...] * GLA_DK ** -0.5)
    o_rows = []
    for j in range(bt):
        bc = lambda t: jnp.broadcast_to(t[:, j:j + 1], (GLA_KW, GLA_DV))
        vrow = jnp.concatenate(
            [jnp.broadcast_to(v[j:j + 1, hh * GLA_DV:(hh + 1) * GLA_DV], (GLA_DK, GLA_DV)) for hh in range(GLA_HEADS)],
            axis=0)
        s_new = bc(eg_t) * s_ref[j] + bc(k_t) * vrow
        so_ref[j] = s_new
        qs = bc(q_t) * s_new
        o_rows.append(jnp.concatenate(
            [jnp.sum(qs[hh * GLA_DK:(hh + 1) * GLA_DK, :], axis=0, keepdims=True) for hh in range(GLA_HEADS)], axis=1))
    o = jnp.concatenate(o_rows, axis=0)
    o_heads = [o[:, hh * GLA_DV:(hh + 1) * GLA_DV] for hh in range(GLA_HEADS)]
    o_ref[:, M_C:M_C + GLA_WIDTH] = _gla_finish(o_heads, gate_ref[...], ng_ref[...]).astype(BF16)


assert SWA_BT == GLA_BT
SAMPLE_BT = SWA_BT


def _mix_sample_kernel(l, sink_ref, q_ref, kn_ref, vn_ref, ck_ref, cv_ref,
                       u_ref, bb_ref, cre_ref, cim_ref, d_ref, wg_ref, bg_ref, tab_ref, h0re_ref, h0im_ref,
                       gq_ref, gk_ref, gv_ref, gg_ref, ga_ref, wa_ref, ba_ref, ng_ref, s_ref,
                       o_ref, cko_ref, cvo_ref, hsre_ref, hsim_ref, so_ref):
    if l == 0:
        for r in (cko_ref, cvo_ref, so_ref):
            r[1:] = jnp.zeros((r.shape[0] - 1,) + r.shape[1:], F32)
        cko_ref, cvo_ref, so_ref = cko_ref.at[0], cvo_ref.at[0], so_ref.at[0]
    _swa_sample_kernel(l, sink_ref, q_ref, kn_ref, vn_ref, ck_ref, cv_ref, o_ref, cko_ref, cvo_ref)
    _s5_sample_kernel(u_ref, bb_ref, cre_ref, cim_ref, d_ref, wg_ref, bg_ref, tab_ref, h0re_ref, h0im_ref,
                      o_ref, hsre_ref, hsim_ref)
    _gla_sample_kernel(gq_ref, gk_ref, gv_ref, gg_ref, ga_ref, wa_ref, ba_ref, ng_ref, s_ref, o_ref, so_ref)


def _mix_sample(sink, h, cache_k, cache_v, s5_prm, h0re, h0im, gla_prm, state, mix, ck_all, cv_all, s_all, l):
    bt = SAMPLE_BT
    r0 = ROWS_P // bt
    blk = lambda c, w: pl.BlockSpec((bt, w), lambda s: (r0 + s, c // w))
    cache = pl.BlockSpec((None, bt, A_KV_HEADS, HEAD_DIM, WINDOW), lambda s: (l, s, 0, 0, 0))
    h0 = pl.BlockSpec((None, bt, S5_LANES), lambda s: (l, s, 0))
    hs = pl.BlockSpec((bt, S5_LANES), lambda s: (s, 0))
    st = pl.BlockSpec((None, bt, GLA_KW, GLA_DV), lambda s: (l, s, 0, 0))
    in_specs = ([pl.BlockSpec(memory_space=pltpu.SMEM), blk(C_Q, A_WIDTH), blk(C_K, KV_WIDTH), blk(C_V, KV_WIDTH),
                 cache, cache, blk(C_U, S5_WIDTH)]
                + _s5_param_specs(l, False) + [h0, h0]
                + [blk(C_GQ, GLA_KW), blk(C_GK, GLA_KW), blk(C_GV, GLA_WIDTH), blk(C_GG, GLA_WIDTH), blk(C_GA, 128)]
                + _gla_param_specs(l) + [st])
    n_in = len(in_specs)
    hs_shape = jax.ShapeDtypeStruct((N_SAMPLE, S5_LANES), F32)
    if l == 0:
        all_layers = lambda shape: pl.BlockSpec((DEPTH, bt) + shape, lambda s: (0, s) + (0,) * len(shape))
        cache_o, st_o = all_layers((A_KV_HEADS, HEAD_DIM, WINDOW)), all_layers((GLA_KW, GLA_DV))
        aliased, aliases = [mix], {n_in: 0}
    else:
        cache_o, st_o = cache, st
        aliased, aliases = [mix, ck_all, cv_all, s_all], {n_in: 0, n_in + 1: 1, n_in + 2: 2, n_in + 3: 5}
    return pl.pallas_call(
        _drop_inputs(functools.partial(_mix_sample_kernel, l), n_in, len(aliased)),
        grid=(N_SAMPLE // bt,),
        in_specs=in_specs + [_any()] * len(aliased),
        out_specs=[blk(0, D_MODEL), cache_o, cache_o, hs, hs, st_o],
        out_shape=[jax.ShapeDtypeStruct((ROWS, D_MODEL), BF16),
                   jax.ShapeDtypeStruct(cache_k.shape, F32), jax.ShapeDtypeStruct(cache_v.shape, F32),
                   hs_shape, hs_shape, jax.ShapeDtypeStruct(state.shape, F32)],
        input_output_aliases=aliases,
        compiler_params=_cparams("arbitrary"),
        name="mix_sample",
    )(sink, h, h, h, cache_k, cache_v, h, *_s5_params(s5_prm, False), h0re, h0im,
      h, h, h, h, h, *gla_prm, state, *aliased)


def _mix_prompt_kernel(l, n_cast, sink_ref, bias_ref, q_ref, kp_ref, kc_ref, vp_ref, vc_ref, u_ref,
                       gq_ref, gk_ref, gv_ref, gg_ref, ga_ref,
                       bb_ref, cre_ref, cim_ref, d_ref, wg_ref, bg_ref, ab_ref, wa_ref, ba_ref, ng_ref, *rest):
    w_refs, (o_ref, hpre_ref, hpim_ref, s_ref) = rest[:n_cast], rest[n_cast:n_cast + 4]
    c_refs = rest[n_cast + 4:]
    s_blk = pl.program_id(0)

    @pl.when(s_blk == 0)
    def _():
        hpre_ref[...] = jnp.zeros_like(hpre_ref)
        hpim_ref[...] = jnp.zeros_like(hpim_ref)
        s_ref[...] = jnp.zeros_like(s_ref)

    for w_ref, c_ref in zip(w_refs, c_refs):
        c_ref[...] = w_ref[...].astype(BF16)
    _s5_block(u_ref, bb_ref, cre_ref, cim_ref, d_ref, wg_ref, bg_ref, ab_ref, o_ref, hpre_ref, hpim_ref)
    _swa_block(l, sink_ref, bias_ref, q_ref, kp_ref, kc_ref, vp_ref, vc_ref, o_ref)
    _gla_block(gq_ref, gk_ref, gv_ref, gg_ref, ga_ref, wa_ref, ba_ref, ng_ref, o_ref, s_ref)


def _mix_prompt(sink, h, s5_prm, gla_prm, w_casts, mix, l):
    cur = _seq_block
    prev = lambda s: _seq_block(jnp.maximum(s - 1, 0))
    blk = lambda c, w, at=cur: pl.BlockSpec((BLK, w), lambda s: (at(s), c // w))
    const = lambda shape: pl.BlockSpec(shape, lambda s: (0,) * len(shape))
    casts = [_cast_specs(w, N_PBLK, layer) for w, layer in w_casts]
    in_specs = ([pl.BlockSpec(memory_space=pltpu.SMEM),
                 pl.BlockSpec((None, BLK, 2 * BLK), lambda s: (jnp.minimum(s, 2), 0, 0)), blk(C_Q, A_WIDTH),
                 blk(C_K, KV_WIDTH, prev), blk(C_K, KV_WIDTH), blk(C_V, KV_WIDTH, prev), blk(C_V, KV_WIDTH),
                 blk(C_U, S5_WIDTH),
                 blk(C_GQ, GLA_KW), blk(C_GK, GLA_KW), blk(C_GV, GLA_WIDTH), blk(C_GG, GLA_WIDTH), blk(C_GA, 128)]
                + _s5_param_specs(l, True) + _gla_param_specs(l) + [c[0] for c in casts])
    n_in = len(in_specs)
    return pl.pallas_call(
        _drop_inputs(functools.partial(_mix_prompt_kernel, l, len(casts)), n_in, 1),
        grid=(N_PBLK,),
        in_specs=in_specs + [_any()],
        out_specs=[pl.BlockSpec((BLK, D_MODEL), lambda s: (cur(s), 0)),
                   const((S5_CHUNKS, 128)), const((S5_CHUNKS, 128)), const((GLA_KW, GLA_DV))] + [c[1] for c in casts],
        out_shape=[jax.ShapeDtypeStruct((ROWS, D_MODEL), BF16),
                   jax.ShapeDtypeStruct((S5_CHUNKS, 128), F32), jax.ShapeDtypeStruct((S5_CHUNKS, 128), F32),
                   jax.ShapeDtypeStruct((GLA_KW, GLA_DV), F32)] + [c[2] for c in casts],
        input_output_aliases={n_in: 0},
        compiler_params=_cparams("arbitrary"),
        name="mix_prompt",
    )(sink, _swa_bias(), *([h] * 11), *_s5_params(s5_prm, True), *gla_prm, *[w for w, _ in w_casts], mix)


OUT_TM = 768
OUT_PARTS = 3
LN_ROWS = 16


def _out_proj_kernel(mix_ref, w_ref, x_ref, g_ref, b_ref, o_ref):
    i = pl.program_id(0)
    g, b = g_ref[...], b_ref[...]
    part = OUT_TM // OUT_PARTS
    rows = [slice(k * part, (k + 1) * part) for k in range(OUT_PARTS)]
    def norm(k, acc):
        for c in range(part // LN_ROWS):
            r0 = k * part + c * LN_ROWS
            y = _layer_norm_rows(ALPHA * x_ref[r0:r0 + LN_ROWS, :] + acc[c * LN_ROWS:(c + 1) * LN_ROWS, :], g, b)
            o_ref[r0:r0 + LN_ROWS, :] = _zero_pad_rows(y, i * OUT_TM + r0)

    prev = None
    for k, r in enumerate(rows):
        acc = _dot(mix_ref[r, :], w_ref[...])
        if prev is not None:
            norm(k - 1, prev)
        prev = acc
    norm(OUT_PARTS - 1, prev)


def _out_proj(mix, w, x, g, b, l):
    tm = OUT_TM
    return pl.pallas_call(
        _out_proj_kernel,
        grid=(ROWS // tm,),
        in_specs=[
            pl.BlockSpec((tm, D_MODEL), lambda i: (i, 0)),
            pl.BlockSpec((D_MODEL, D_MODEL), lambda i: (0, 0), pipeline_mode=pl.Buffered(1)),
            pl.BlockSpec((tm, D_MODEL), lambda i: (i, 0)),
            _layer_spec((1, D_MODEL), l), _layer_spec((1, D_MODEL), l),
        ],
        out_specs=pl.BlockSpec((tm, D_MODEL), lambda i: (i, 0)),
        out_shape=jax.ShapeDtypeStruct((ROWS, D_MODEL), F32),
        compiler_params=_cparams("arbitrary"),
        name="out_proj",
    )(mix, w, x, g, b)


FFN_TM = 768
FFN_TF = 512
FFN_HALO = 16
FFN_NI = ROWS // FFN_TM
assert ROWS % FFN_TM == 0 and FFN_TM >= N_SAMPLE + BLK + SUB and META_ROW % FFN_HALO == 0


def _ffn_kernel(final, x_ref, xh_ref, wu_ref, wg_ref, cw_ref, cb_ref, wd_ref, p0_ref, p1_ref, g_ref, b_ref,
                o_ref, *rest):
    if final:
        ys_ref, ut_ref, us_ref, xb_ref, a_ref = rest
    else:
        ut_ref, us_ref, xb_ref, a_ref = rest
    i, f = pl.program_id(0), pl.program_id(1)
    tm, hl = FFN_TM, FFN_HALO
    last_i = FFN_NI - 1

    @pl.when(f == 0)
    def _():
        xb_ref[...] = x_ref[...].astype(BF16)
        o_ref[...] = jnp.zeros_like(o_ref)

    cw0, cw1, cw2, cb = cw_ref[0:1, :], cw_ref[1:2, :], cw_ref[2:3, :], cb_ref[...]

    def act(um2, um1, u0, g0):
        return (jax.nn.gelu(cb + cw0 * um2 + cw1 * um1 + cw2 * u0) * g0).astype(BF16)

    xb = xb_ref[...]
    u = _dot(xb, wu_ref[...])
    gte = _dot(xb, wg_ref[...])
    um2, um1 = pltpu.roll(u, 2, 0), pltpu.roll(u, 1, 0)
    body = slice(hl, tm - N_SAMPLE)
    a_ref[body, :] = act(um2[body, :], um1[body, :], u[body, :], gte[body, :])
    ext = jnp.concatenate([_dot(xh_ref[...].astype(BF16), wu_ref[...]), u[0:hl, :]], axis=0)
    a_ref[0:hl, :] = act(ext[hl - 2:2 * hl - 2, :], ext[hl - 1:2 * hl - 1, :], u[0:hl, :], gte[0:hl, :])
    tail = slice(tm - N_SAMPLE, tm)
    samp = i == last_i
    a_ref[tail, :] = act(jnp.where(samp, p0_ref[...], um2[tail, :]), jnp.where(samp, p1_ref[...], um1[tail, :]),
                         u[tail, :], gte[tail, :])
    o_ref[...] += _dot(a_ref[...], wd_ref[...])
    ut_ref[...] = u[tm - N_SAMPLE - BLK - SUB:tm - N_SAMPLE - BLK, :]
    us_ref[0] = p1_ref[...]
    us_ref[1] = u[tm - N_SAMPLE:, :]

    @pl.when(f == pl.num_programs(1) - 1)
    def _():
        _deepnorm_ln(o_ref, x_ref, g_ref, b_ref, o_ref, i * tm, tm)
        if final:
            ys_ref[...] = o_ref[tm - N_SAMPLE:, :]


def _ffn(x, wup, cw, cb, wd, conv_p0, conv_p1, g, b, l, final):
    tm, tf = FFN_TM, FFN_TF
    nf = D_FF // tf
    tail = lambda i, f: jnp.where(i == FFN_NI - 1, f, 0)
    halo = lambda i, f: (jnp.where(i == 0, ROWS_P // FFN_HALO, i * (tm // FFN_HALO)) - 1, 0)
    y_spec = pl.BlockSpec((tm, D_MODEL), lambda i, f: (i, 0))
    y_specs = [y_spec, pl.BlockSpec((N_SAMPLE, D_MODEL), lambda i, f: (0, 0))] if final else [y_spec]
    y_shapes = ([jax.ShapeDtypeStruct((SEQ, D_MODEL), F32), jax.ShapeDtypeStruct((N_SAMPLE, D_MODEL), F32)]
                if final else [jax.ShapeDtypeStruct((ROWS, D_MODEL), F32)])
    return pl.pallas_call(
        functools.partial(_ffn_kernel, final),
        grid=(FFN_NI, nf),
        in_specs=[
            pl.BlockSpec((tm, D_MODEL), lambda i, f: (i, 0)),
            pl.BlockSpec((FFN_HALO, D_MODEL), halo),
            pl.BlockSpec((D_MODEL, tf), lambda i, f: (0, f)),
            pl.BlockSpec((D_MODEL, tf), lambda i, f: (0, nf + f)),
            pl.BlockSpec((None, CONV_W, tf), lambda i, f: (l, 0, f)),
            pl.BlockSpec((None, 1, tf), lambda i, f: (l, 0, f)),
            pl.BlockSpec((tf, D_MODEL), lambda i, f: (f, 0)),
            pl.BlockSpec((None, None, N_SAMPLE, tf), lambda i, f: (l, 0, 0, tail(i, f))),
            pl.BlockSpec((None, None, N_SAMPLE, tf), lambda i, f: (l, 1, 0, tail(i, f))),
            _layer_spec((1, D_MODEL), l), _layer_spec((1, D_MODEL), l),
        ],
        out_specs=y_specs + [
            pl.BlockSpec((SUB, tf), lambda i, f: (0, tail(i, f))),
            pl.BlockSpec((2, N_SAMPLE, tf), lambda i, f: (0, 0, tail(i, f))),
        ],
        out_shape=y_shapes + [
            jax.ShapeDtypeStruct((SUB, D_FF), F32),
            jax.ShapeDtypeStruct((2, N_SAMPLE, D_FF), F32),
        ],
        scratch_shapes=[pltpu.VMEM((tm, D_MODEL), BF16), pltpu.VMEM((tm, tf), BF16)],
        compiler_params=_cparams("arbitrary", "arbitrary"),
        name="conv_ffn",
    )(x, x, wup, wup, cw, cb, wd, conv_p0, conv_p1, g, b)


def _rope_table():
    inv = jnp.power(ROPE_THETA, -jnp.arange(ROPE_HALF, dtype=F32) / ROPE_HALF)
    pos = jnp.concatenate([jnp.arange(SEQ, dtype=jnp.int32) + N_META, jnp.zeros((LEAD,), jnp.int32),
                           jnp.arange(N_META, dtype=jnp.int32), jnp.full((N_SAMPLE,), PAST_LEN, jnp.int32)])
    ang = pos.astype(F32)[:, None] * inv[None, :]
    return jnp.concatenate([jnp.cos(ang), jnp.sin(ang)], axis=1)


def _s5_tables(lam_re, lam_im, log_dt, b_re, b_im, c_re, c_im, d, w_glu, b_glu):
    L = lam_re.shape[0]
    dt = jnp.exp(log_dt)[..., None]
    mag = jnp.exp(lam_re * dt)
    ab_re, ab_im = mag * jnp.cos(lam_im * dt), mag * jnp.sin(lam_im * dt)
    den = lam_re * lam_re + lam_im * lam_im
    nr, ni = ab_re - 1.0, ab_im
    f_re = (nr * lam_re + ni * lam_im) / den
    f_im = (ni * lam_re - nr * lam_im) / den
    bb_re = f_re[..., None] * b_re - f_im[..., None] * b_im
    bb_im = f_re[..., None] * b_im + f_im[..., None] * b_re
    gi, go = S5_GROUPS // S5_IN_PIECES, S5_GROUPS // S5_OUT_PIECES

    def bd_in(t):
        t = t.reshape(L, S5_IN_PIECES, gi, S5_STATE, S5_CH)
        return jnp.einsum('lqgph,gk->lqghkp', t, jnp.eye(gi, dtype=F32)).reshape(L, S5_IN_PIECES, S5_IN_K, S5_IN_N)

    def bd_out(t):
        t = t.reshape(L, S5_OUT_PIECES, go, S5_CH, S5_STATE)
        return jnp.einsum('lqghp,gk->lqgpkh', t, jnp.eye(go, dtype=F32)).reshape(
            L, S5_OUT_PIECES, S5_OUT_K, S5_OUT_N)

    ar, ai = ab_re.reshape(L, 1, S5_LANES), ab_im.reshape(L, 1, S5_LANES)
    pows = [(ar, ai)]
    for _ in range(SUB - 1):
        pr, pi = pows[-1]
        pows.append((pr * ar - pi * ai, pr * ai + pi * ar))
    row = jnp.arange(SUB)[None, :, None]
    parts = []
    for sft in S5_SHIFTS:
        pr, pi = pows[sft - 1]
        parts += [jnp.where(row >= sft, pr, 0.0), jnp.where(row >= sft, pi, 0.0)]
    parts += [jnp.concatenate([p[0] for p in pows], 1), jnp.concatenate([p[1] for p in pows], 1)]
    return dict(bb=jnp.concatenate([bd_in(bb_re), bd_in(bb_im)], axis=3).astype(BF16),
                c_re=bd_out(c_re).astype(BF16), c_im=bd_out(c_im).astype(BF16),
                d=d.reshape(L, 1, S5_WIDTH), w_glu=w_glu.astype(BF16), b_glu=b_glu.reshape(L, 1, S5_WIDTH),
                tab=jnp.concatenate(parts, 1),
                ab=jnp.concatenate([ab_re.reshape(L, S5_CHUNKS, 128), ab_im.reshape(L, S5_CHUNKS, 128)], axis=1))


@jax.jit
def kernel(x_prompt, x_sample, cache_swa_k, cache_swa_v, state_ssm_re, state_ssm_im, state_gla, state_conv,
           meta_tokens, ln_in_g, ln_in_b, w_in, attn_sink, s5_lam_re, s5_lam_im, s5_log_dt, s5_b_re, s5_b_im,
           s5_c_re, s5_c_im, s5_d, s5_w_glu, s5_b_glu, gla_w_a2, gla_b_a, gla_norm_g, w_out, ln1_g, ln1_b,
           ffn_w_up, ffn_conv_w, ffn_conv_b, ffn_w_down, ln2_g, ln2_b):
    L = DEPTH
    row3 = lambda t: t.reshape(L, 1, -1)
    wa_b = jnp.pad(gla_w_a2, ((0, 0), (0, 128 - GLA_LOWRANK), (0, 0))).astype(BF16)
    gla_args = (wa_b, row3(gla_b_a), row3(gla_norm_g))
    ln1, ln2 = (row3(ln1_g), row3(ln1_b)), (row3(ln2_g), row3(ln2_b))
    conv_b3 = row3(ffn_conv_b)
    cs = _rope_table()
    s5p = _s5_tables(s5_lam_re, s5_lam_im, s5_log_dt, s5_b_re, s5_b_im, s5_c_re, s5_c_im, s5_d, s5_w_glu, s5_b_glu)
    cache_t = lambda c: jnp.transpose(c, (0, 1, 3, 4, 2))
    ck, cv = cache_t(cache_swa_k), cache_t(cache_swa_v)
    ck_all = cv_all = s_all = None
    h0re = state_ssm_re.reshape(L, N_SAMPLE, S5_LANES)
    h0im = state_ssm_im.reshape(L, N_SAMPLE, S5_LANES)
    sg = state_gla.reshape(L, N_SAMPLE, GLA_KW, GLA_DV)
    conv_p0 = conv_p1 = jnp.transpose(state_conv, (0, 2, 1, 3))
    mix = jnp.zeros((ROWS, D_MODEL), BF16)
    w_in_t = jnp.swapaxes(w_in, 1, 2)

    x, w_in_b = _ln_in(x_prompt.reshape(SEQ, D_MODEL), meta_tokens, x_sample.reshape(N_SAMPLE, D_MODEL),
                       ln_in_g.reshape(1, D_MODEL), ln_in_b.reshape(1, D_MODEL), w_in_t)
    outs = [[] for _ in range(9)]
    for l in range(L):
        h = _in_proj(x, w_in_b, cs, l)
        casts = [(ffn_w_up, l), (ffn_w_down, l), (w_out, l)] + ([(w_in_t, l + 1)] if l + 1 < L else [])
        mix, hp_re, hp_im, s_p, w_up_b, w_down_b, w_out_b, *w_next = _mix_prompt(
            attn_sink, h, s5p, gla_args, casts, mix, l)
        w_in_b = w_next[0] if w_next else None
        mix, ck_all, cv_all, hs_re, hs_im, s_all = _mix_sample(
            attn_sink, h, ck, cv, s5p, h0re, h0im, gla_args, sg, mix, ck_all, cv_all, s_all, l)
        x = _out_proj(mix, w_out_b, x, *ln1, l)
        res = _ffn(x, w_up_b, ffn_conv_w, conv_b3, w_down_b, conv_p0, conv_p1, *ln2, l, l == L - 1)
        if l == L - 1:
            y_prompt, y_sample, u_tail, u_s = res
        else:
            x, u_tail, u_s = res
        kv_p = lambda c: h[SEQ - WINDOW:SEQ, c:c + KV_WIDTH].reshape(1, WINDOW, A_KV_HEADS, HEAD_DIM)
        new = (kv_p(C_K), kv_p(C_V),
               hp_re.reshape(1, S5_GROUPS, S5_STATE), hp_im.reshape(1, S5_GROUPS, S5_STATE),
               s_p.reshape(1, GLA_HEADS, GLA_DK, GLA_DV), u_tail[SUB - 2:SUB][None],
               hs_re.reshape(N_SAMPLE, S5_GROUPS, S5_STATE), hs_im.reshape(N_SAMPLE, S5_GROUPS, S5_STATE),
               u_s)
        for lst, val in zip(outs, new):
            lst.append(val)
    st = [jnp.stack(lst) for lst in outs]
    return (y_prompt.reshape(1, SEQ, D_MODEL), y_sample.reshape(N_SAMPLE, 1, D_MODEL),
            st[0], st[1], st[2], st[3], st[4], st[5],
            jnp.transpose(ck_all, (0, 1, 4, 2, 3)), jnp.transpose(cv_all, (0, 1, 4, 2, 3)),
            st[6], st[7], s_all.reshape(L, N_SAMPLE, GLA_HEADS, GLA_DK, GLA_DV),
            jnp.transpose(st[8], (0, 2, 1, 3)))
```
